```python
import math
import jax, jax.numpy as jnp
from jax import lax
import numpy as np

D_MODEL = 1024
BATCH = 8
SEQ = 4096
DEPTH = 2

DSWA_PATTERNS = ((128, 1), (512, 4), (2048, 16))
DSWA_GROUPS = 3
DSWA_HEADS = 4
DSWA_HEAD_DIM = 64
DSWA_BLOCK = 128
CONV_CH = 256
CONV_WIDTH = 31
GLA_HEADS = 4
GLA_DK = 64
GLA_DV = 128
GLA_GATE_RANK = 16
GLA_TAU = 16.0
GLA_CHUNK = 64
SB_HEADS = 4
SB_HEAD_DIM = 64
SB_BLOCK = 128
N_BRANCH = 4
D_FF = 4 * D_MODEL
NORM_EPS = 1e-6

A_QK = DSWA_GROUPS * DSWA_HEADS * DSWA_HEAD_DIM
A_OUT = DSWA_HEADS * DSWA_HEAD_DIM
C_QK = GLA_HEADS * GLA_DK
C_V = GLA_HEADS * GLA_DV
D_W = SB_HEADS * SB_HEAD_DIM
IN_SPLITS = (A_QK, A_QK, A_QK, 2 * CONV_CH, C_QK, C_QK, C_V, C_V, GLA_GATE_RANK, D_W, D_W, D_W, N_BRANCH * D_MODEL)
IN_WIDTH = sum(IN_SPLITS)

kernel_name = "hybrid_gated_parallel_mixer_block"


def rms_norm(x, g):
    xf = x.astype(jnp.float32)
    y = xf * lax.rsqrt(jnp.mean(xf * xf, axis=-1, keepdims=True) + NORM_EPS)
    return (y * g.astype(jnp.float32)).astype(x.dtype)


def layer_norm(x, g, b):
    xf = x.astype(jnp.float32)
    mu = jnp.mean(xf, axis=-1, keepdims=True)
    var = jnp.mean(jnp.square(xf - mu), axis=-1, keepdims=True)
    y = (xf - mu) * lax.rsqrt(var + NORM_EPS)
    return (y * g.astype(jnp.float32) + b.astype(jnp.float32)).astype(x.dtype)


def alibi_slopes(n):
    return jnp.asarray(2.0 ** (-8.0 * np.arange(1, n + 1) / n), dtype=jnp.float32)


def dilated_window_group(q, k, v, dilation, window, slopes):
    bsz, seq, nh, hd = q.shape
    span = dilation * DSWA_BLOCK
    seq_pad = -(-seq // span) * span
    pad = seq_pad - seq
    m = seq_pad // dilation
    nb = m // DSWA_BLOCK

    def to_blocks(t):
        t = jnp.pad(t, ((0, 0), (0, pad), (0, 0), (0, 0)))
        t = t.reshape(bsz, m, dilation, nh, hd).transpose(0, 2, 1, 3, 4)
        return t.reshape(bsz, dilation, nb, DSWA_BLOCK, nh, hd)

    def band(t):
        prev = jnp.pad(t, ((0, 0), (0, 0), (1, 0), (0, 0), (0, 0), (0, 0)))[:, :, :-1]
        return jnp.concatenate([prev, t], axis=3)

    qb, kb, vb = to_blocks(q), to_blocks(k), to_blocks(v)
    kk, vv = band(kb), band(vb)
    scores = jnp.einsum('bdnqhe,bdnkhe->bdnhqk', qb, kk).astype(jnp.float32) * (hd ** -0.5)
    qi = jnp.arange(DSWA_BLOCK)
    ki = jnp.arange(2 * DSWA_BLOCK) - DSWA_BLOCK
    rel = qi[:, None] - ki[None, :]
    blk = jnp.arange(nb)
    valid = ((rel >= 0) & (rel <= window // dilation))[None] & (
        (blk[:, None, None] * DSWA_BLOCK + ki[None, None, :]) >= 0)
    bias = -slopes[:, None, None] * (dilation * rel).astype(jnp.float32)
    scores = jnp.where(valid[None, None, :, None], scores + bias, -jnp.inf)
    mx = jnp.max(scores, axis=-1, keepdims=True)
    p = jnp.exp(scores - mx)
    den = jnp.sum(p, axis=-1)
    o = jnp.einsum('bdnhqk,bdnkhe->bdnqhe', p, vv.astype(jnp.float32))
    o = o / jnp.swapaxes(den, 3, 4)[..., None]
    lse = jnp.swapaxes(mx[..., 0] + jnp.log(den), 3, 4)

    def from_blocks(t):
        rest = t.shape[4:]
        t = jnp.swapaxes(t.reshape((bsz, dilation, m) + rest), 1, 2)
        return t.reshape((bsz, seq_pad) + rest)[:, :seq]

    return from_blocks(o), from_blocks(lse)


def dilated_attention_mixer(q, k, v, q_gain, k_gain):
    b, s, _ = q.shape
    shp = (b, s, DSWA_GROUPS, DSWA_HEADS, DSWA_HEAD_DIM)
    q = rms_norm(q.reshape(shp), q_gain)
    k = rms_norm(k.reshape(shp), k_gain)
    v = v.reshape(shp)
    slopes = alibi_slopes(DSWA_GROUPS * DSWA_HEADS).reshape(DSWA_GROUPS, DSWA_HEADS)
    outs, lses = [], []
    for g, (window, dilation) in enumerate(DSWA_PATTERNS):
        o, lse = dilated_window_group(q[:, :, g], k[:, :, g], v[:, :, g], dilation, window, slopes[g])
        outs.append(o)
        lses.append(lse)
    wts = jax.nn.softmax(jnp.stack(lses), axis=0)
    o = jnp.sum(wts[..., None] * jnp.stack(outs), axis=0)
    return o.reshape(b, s, A_OUT).astype(q.dtype)


def conformer_conv_mixer(u, conv_w, conv_b, ln_g, ln_b):
    a, gate = jnp.split(u, 2, axis=-1)
    y = a * jax.nn.sigmoid(gate)
    y = lax.conv_general_dilated(y, conv_w[:, None, :].astype(y.dtype), window_strides=(1,),
                                 padding=((CONV_WIDTH - 1, 0),),
                                 dimension_numbers=('NWC', 'WIO', 'NWC'),
                                 feature_group_count=CONV_CH) + conv_b
    return jax.nn.silu(layer_norm(y, ln_g, ln_b))


def gla_mixer(q, k, v, r, g_low, w_gate, b_gate, o_gain):
    f32 = jnp.float32
    b, s, _ = q.shape
    n = s // GLA_CHUNK
    log_a = jax.nn.log_sigmoid((g_low @ w_gate + b_gate).astype(f32)) / GLA_TAU

    def chunks(t, e):
        return t.reshape(b, n, GLA_CHUNK, GLA_HEADS, e).transpose(1, 0, 3, 2, 4)

    qc = chunks(q.astype(f32) * (GLA_DK ** -0.5), GLA_DK)
    kc = chunks(k.astype(f32), GLA_DK)
    vc = chunks(v.astype(f32), GLA_DV)
    gc = chunks(log_a, GLA_DK)
    causal = jnp.tril(jnp.ones((GLA_CHUNK, GLA_CHUNK), dtype=bool))

    def step(state, xs):
        qt, kt, vt, gt = xs
        cum = jnp.cumsum(gt, axis=2)
        o_inter = jnp.einsum('bhtk,bhkv->bhtv', qt * jnp.exp(cum), state)
        diff = cum[:, :, :, None, :] - cum[:, :, None, :, :]
        decay = jnp.exp(jnp.where(causal[:, :, None], diff, -jnp.inf))
        att = jnp.einsum('bhtk,bhsk,bhtsk->bhts', qt, kt, decay)
        o = o_inter + jnp.einsum('bhts,bhsv->bhtv', att, vt)
        last = cum[:, :, -1:, :]
        state = jnp.exp(last[:, :, 0, :, None]) * state + jnp.einsum(
            'bhsk,bhsv->bhkv', kt * jnp.exp(last - cum), vt)
        return state, o

    state0 = jnp.zeros((b, GLA_HEADS, GLA_DK, GLA_DV), f32)
    _, o = lax.scan(step, state0, (qc, kc, vc, gc))
    o = o.transpose(1, 0, 3, 2, 4).reshape(b, s, GLA_HEADS, GLA_DV)
    o = rms_norm(o, o_gain).reshape(b, s, C_V)
    return (jax.nn.silu(r.astype(f32)) * o).astype(q.dtype)


def stick_breaking_mixer(q, k, v):
    f32 = jnp.float32
    b, s, _ = q.shape

    def heads(t):
        return t.reshape(b, s, SB_HEADS, SB_HEAD_DIM).transpose(0, 2, 1, 3)

    q, k, v = heads(q), heads(k), heads(v)
    vf = v.astype(f32)
    nb = s // SB_BLOCK
    q_blocks = q.reshape(b, SB_HEADS, nb, SB_BLOCK, SB_HEAD_DIM).transpose(2, 0, 1, 3, 4)
    key_pos = jnp.arange(s)
    scale = SB_HEAD_DIM ** -0.5

    def block(args):
        qb, i = args
        z = jnp.einsum('bhqe,bhke->bhqk', qb, k).astype(f32) * scale
        q_pos = i * SB_BLOCK + jnp.arange(SB_BLOCK)
        mask = key_pos[None, :] < q_pos[:, None]
        log_keep = jnp.where(mask, jax.nn.log_sigmoid(-z), 0.0)
        after = lax.cumsum(log_keep, axis=3, reverse=True) - log_keep
        w = jnp.where(mask, jnp.exp(jax.nn.log_sigmoid(z) + after), 0.0)
        return jnp.einsum('bhqk,bhke->bhqe', w, vf)

    o = lax.map(block, (q_blocks, jnp.arange(nb)))
    return o.transpose(1, 0, 3, 2, 4).reshape(b, s, D_W).astype(q.dtype)


def setup_inputs(seed: int = 0) -> dict:
    key = jax.random.key(seed)
    ks = jax.random.split(key, 26)
    L, D = DEPTH, D_MODEL

    def nrm(k, shape, scale):
        return jax.random.normal(k, shape, jnp.float32) * scale

    def gain(k, shape):
        return 1.0 + nrm(k, shape, 0.05)

    return {
        "x": nrm(ks[0], (BATCH, SEQ, D), 1.0),
        "c": nrm(ks[1], (BATCH, D), 1.0),
        "w_ada": nrm(ks[2], (L, D, 6 * D), 0.5 * D ** -0.5),
        "b_ada": nrm(ks[3], (L, 6 * D), 0.02),
        "g_mix": gain(ks[4], (L, D)),
        "w_in": nrm(ks[5], (L, D, IN_WIDTH), D ** -0.5),
        "q_gain": gain(ks[6], (L, DSWA_HEAD_DIM)),
        "k_gain": gain(ks[7], (L, DSWA_HEAD_DIM)),
        "conv_w": nrm(ks[8], (L, CONV_WIDTH, CONV_CH), CONV_WIDTH ** -0.5),
        "conv_b": nrm(ks[9], (L, CONV_CH), 0.02),
        "conv_ln_g": gain(ks[10], (L, CONV_CH)),
        "conv_ln_b": nrm(ks[11], (L, CONV_CH), 0.02),
        "gla_w_gate": nrm(ks[12], (L, GLA_GATE_RANK, C_QK), GLA_GATE_RANK ** -0.5),
        "gla_b_gate": nrm(ks[13], (L, C_QK), 0.1),
        "gla_o_gain": gain(ks[14], (L, GLA_DV)),
        "w_br_a": nrm(ks[15], (L, A_OUT, D), A_OUT ** -0.5),
        "w_br_b": nrm(ks[16], (L, CONV_CH, D), CONV_CH ** -0.5),
        "w_br_c": nrm(ks[17], (L, C_V, D), C_V ** -0.5),
        "w_br_d": nrm(ks[18], (L, D_W, D), D_W ** -0.5),
        "w_out": nrm(ks[19], (L, D, D), D ** -0.5),
        "g_mlp": gain(ks[20], (L, D)),
        "w_up": nrm(ks[21], (L, D, D_FF), D ** -0.5),
        "w_down": nrm(ks[22], (L, D_FF, D), D_FF ** -0.5),
    }


def reference(x, c, w_ada, b_ada, g_mix, w_in, q_gain, k_gain, conv_w, conv_b, conv_ln_g, conv_ln_b,
              gla_w_gate, gla_b_gate, gla_o_gain, w_br_a, w_br_b, w_br_c, w_br_d, w_out,
              g_mlp, w_up, w_down):
    bsz, seq, _ = x.shape
    split_points = np.cumsum(IN_SPLITS)[:-1].tolist()
    for l in range(DEPTH):
        mod = (c @ w_ada[l] + b_ada[l])[:, None, :]
        sh1, sc1, gt1, sh2, sc2, gt2 = jnp.split(mod, 6, axis=-1)

        h = rms_norm(x, g_mix[l]) * (1.0 + sc1) + sh1
        (a_q, a_k, a_v, conv_in, c_q, c_k, c_v, c_r, c_low,
         d_q, d_k, d_v, gate_logits) = jnp.split(h @ w_in[l], split_points, axis=-1)

        y_a = dilated_attention_mixer(a_q, a_k, a_v, q_gain[l], k_gain[l]) @ w_br_a[l]
        y_b = conformer_conv_mixer(conv_in, conv_w[l], conv_b[l], conv_ln_g[l], conv_ln_b[l]) @ w_br_b[l]
        y_c = gla_mixer(c_q, c_k, c_v, c_r, c_low, gla_w_gate[l], gla_b_gate[l], gla_o_gain[l]) @ w_br_c[l]
        y_d = stick_breaking_mixer(d_q, d_k, d_v) @ w_br_d[l]

        gates = jax.nn.sigmoid(gate_logits).reshape(bsz, seq, N_BRANCH, D_MODEL)
        merged = (gates[:, :, 0] * y_a + gates[:, :, 1] * y_b
                  + gates[:, :, 2] * y_c + gates[:, :, 3] * y_d)
        x = x + gt1 * (merged @ w_out[l])

        h = rms_norm(x, g_mlp[l]) * (1.0 + sc2) + sh2
        x = x + gt2 * (jnp.square(jax.nn.relu(h @ w_up[l])) @ w_down[l])
    return x
```

```python
import functools

import numpy as np
import jax
import jax.numpy as jnp
from jax import lax
from jax.experimental import pallas as pl
from jax.experimental.pallas import tpu as pltpu

F32 = jnp.float32
BF16 = jnp.bfloat16

D_MODEL = 1024
D_FF = 4 * D_MODEL
NORM_EPS = 1e-6
HEAD_LANES = 64
N_HEADS = 4
QK_W = N_HEADS * HEAD_LANES
DSWA_PATTERNS = ((128, 1), (512, 4), (2048, 16))
DSWA_BLOCK = 128
CONV_CH = 256
CONV_WIDTH = 31
CONV_HALO = 32
GLA_DV = 128
GLA_V_W = N_HEADS * GLA_DV
GLA_RANK = 16
GLA_LOW_PAD = 128
GLA_TAU = 16.0
GLA_CHUNK = 64
GLA_SUB = 16
GLA_CHUNKS_PER_STEP = 4
SB_TQ = 128
SB_TK = 256
TOKEN_TILE = 512
MASK_VALUE = -1e30
VMEM_LIMIT = 56 * 1024 * 1024

A_W = 3 * 3 * QK_W
C_W = 2 * QK_W + 2 * GLA_V_W + GLA_LOW_PAD
D_W = 3 * QK_W


def _dot(a, b):
    return jnp.dot(a, b, preferred_element_type=F32)


def _dot_nt(a, b):
    return lax.dot_general(a, b, (((1,), (1,)), ((), ())), preferred_element_type=F32)


def _dot_tn(a, b):
    return lax.dot_general(a, b, (((0,), (0,)), ((), ())), preferred_element_type=F32)


def _split_bf16(x):
    hi = x.astype(BF16)
    lo = (x - hi.astype(F32)).astype(BF16)
    return hi, lo


def _sigmoid(x):
    return 1.0 / (1.0 + jnp.exp(-x))


def _neg_softplus(x):
    return -(jnp.maximum(x, 0.0) + jnp.log(1.0 + jnp.exp(-jnp.abs(x))))


def _iota(shape, dim):
    return lax.broadcasted_iota(jnp.int32, shape, dim)


def _head_of_lane(width=QK_W, lanes_per_head=HEAD_LANES):
    return jnp.right_shift(_iota((1, width), 1), int(np.log2(lanes_per_head)))


def _ones_where(cond):
    return jnp.where(cond, 1.0, 0.0).astype(BF16)


def _const_spec(shape):
    nd = len(shape)
    return pl.BlockSpec(shape, lambda *_: (0,) * nd, pipeline_mode=pl.Buffered(1))


def _params(*sem):
    return pltpu.CompilerParams(dimension_semantics=sem, vmem_limit_bytes=VMEM_LIMIT)


def _modulated_norm(x, gain, scale, shift):
    ms = jnp.mean(x * x, axis=-1, keepdims=True)
    y = x * lax.rsqrt(ms + NORM_EPS) * gain
    return y * (1.0 + scale) + shift


def _mod_kernel(c_ref, w_ref, b_ref, o_ref):
    c_hi, c_lo = _split_bf16(c_ref[...])
    w_hi, w_lo = _split_bf16(w_ref[...])
    o_ref[...] = _dot(c_hi, w_hi) + _dot(c_hi, w_lo) + _dot(c_lo, w_hi) + b_ref[...]


def _modulation(c, w_ada, b_ada):
    depth, d, six_d = w_ada.shape
    bsz = c.shape[0]
    out = pl.pallas_call(
        _mod_kernel,
        grid=(depth, six_d // d),
        in_specs=[
            pl.BlockSpec((bsz, d), lambda l, j: (0, 0)),
            pl.BlockSpec((None, d, d), lambda l, j: (l, 0, j)),
            pl.BlockSpec((None, 1, d), lambda l, j: (l, 0, j)),
        ],
        out_specs=pl.BlockSpec((None, bsz, d), lambda l, j: (l, 0, j)),
        out_shape=jax.ShapeDtypeStruct((depth, bsz, six_d), F32),
        compiler_params=_params("parallel", "parallel"),
        name="adaln_modulation",
    )(c, w_ada, b_ada.reshape(depth, 1, six_d))
    return out.reshape(depth, bsz, six_d // d, d)


def _inproj_kernel(x_ref, mod_ref, g_ref, wa_ref, wb_ref, wc_ref, wd_ref, qg_ref, kg_ref, bd_ref,
                   a_ref, y_ref, c_ref, d_ref):
    h = _modulated_norm(x_ref[...], g_ref[...], mod_ref[1:2, :], mod_ref[0:1, :]).astype(BF16)
    bd = bd_ref[...]
    for j in range(A_W // QK_W):
        cols = slice(j * QK_W, (j + 1) * QK_W)
        acc = _dot(h, wa_ref[:, cols])
        if j < 6:
            sq_hi, sq_lo = _split_bf16(acc * acc)
            ms = _dot(sq_hi, bd) + _dot(sq_lo, bd)
            gain = qg_ref[...] if j < 3 else kg_ref[...]
            acc = acc * lax.rsqrt(ms + NORM_EPS) * gain
        a_ref[:, cols] = acc.astype(BF16)
    u = _dot(h, wb_ref[...])
    y_ref[...] = u[:, :CONV_CH] * _sigmoid(u[:, CONV_CH:])
    c_ref[...] = _dot(h, wc_ref[...])
    for j in range(3):
        cols = slice(j * QK_W, (j + 1) * QK_W)
        acc = _dot(h, wd_ref[:, cols])
        if j == 0:
            acc = acc * (HEAD_LANES ** -0.5)
        d_ref[:, cols] = acc.astype(BF16)


def _in_projection(x2d, mod, g_mix, wa, wb, wc, wd, qg, kg, bd, seq):
    t = x2d.shape[0]
    tm = TOKEN_TILE
    per_b = seq // tm
    row = lambda w: pl.BlockSpec((tm, w), lambda i: (i, 0))
    return pl.pallas_call(
        _inproj_kernel,
        grid=(t // tm,),
        in_specs=[
            row(D_MODEL),
            pl.BlockSpec((None, 6, D_MODEL), lambda i: (i // per_b, 0, 0)),
            _const_spec((1, D_MODEL)),
            _const_spec(wa.shape), _const_spec(wb.shape), _const_spec(wc.shape), _const_spec(wd.shape),
            _const_spec((1, QK_W)), _const_spec((1, QK_W)), _const_spec((QK_W, QK_W)),
        ],
        out_specs=[row(A_W), row(CONV_CH), row(C_W), row(D_W)],
        out_shape=[
            jax.ShapeDtypeStruct((t, A_W), BF16),
            jax.ShapeDtypeStruct((t, CONV_CH), F32),
            jax.ShapeDtypeStruct((t, C_W), F32),
            jax.ShapeDtypeStruct((t, D_W), BF16),
        ],
        compiler_params=_params("parallel"),
        name="in_projection",
    )(x2d, mod, g_mix, wa, wb, wc, wd, qg, kg, bd)


def _dswa_kernel(q_ref, kp_ref, kc_ref, vp_ref, vc_ref, o_ref, lse_ref, *, slopes, dilation, max_rel):
    n = pl.program_id(1)
    blk = DSWA_BLOCK
    q = q_ref[...]
    k = jnp.concatenate([kp_ref[...], kc_ref[...]], axis=0)
    v = jnp.concatenate([vp_ref[...], vc_ref[...]], axis=0)
    head = _head_of_lane()
    qi = _iota((blk, 2 * blk), 0)
    kj = _iota((blk, 2 * blk), 1)
    rel = qi - kj + blk
    valid = (rel >= 0) & (rel <= max_rel) & ((kj >= blk) | (n > 0))
    dist = (dilation * rel).astype(F32)
    o_acc = jnp.zeros((blk, QK_W), F32)
    lse_acc = jnp.zeros((blk, QK_W), F32)
    for h in range(N_HEADS):
        in_head = head == h
        s = _dot_nt(jnp.where(in_head, q, jnp.zeros_like(q)), k)
        s = jnp.where(valid, s - slopes[h] * dist, MASK_VALUE)
        mx = jnp.max(s, axis=1, keepdims=True)
        p = jnp.exp(s - mx)
        den = jnp.sum(p, axis=1, keepdims=True)
        o_h = _dot(p.astype(BF16), jnp.where(in_head, v, jnp.zeros_like(v)))
        o_acc = o_acc + o_h * (1.0 / den)
        lse_acc = jnp.where(in_head, mx + jnp.log(den), lse_acc)
    o_ref[...] = o_acc.astype(BF16)
    lse_ref[...] = lse_acc


def _dilated_attention_group(q, k, v, slopes, window, dilation):
    nsub, m, _ = q.shape
    blk = DSWA_BLOCK
    cur = pl.BlockSpec((None, blk, QK_W), lambda s, n: (s, n, 0))
    prev = pl.BlockSpec((None, blk, QK_W), lambda s, n: (s, jnp.maximum(n - 1, 0), 0))
    kern = functools.partial(_dswa_kernel, slopes=slopes, dilation=dilation, max_rel=window // dilation)
    return pl.pallas_call(
        kern,
        grid=(nsub, m // blk),
        in_specs=[cur, prev, cur, prev, cur],
        out_specs=[cur, cur],
        out_shape=[jax.ShapeDtypeStruct((nsub, m, QK_W), BF16), jax.ShapeDtypeStruct((nsub, m, QK_W), F32)],
        compiler_params=_params("parallel", "parallel"),
        name=f"dilated_attention_d{dilation}",
    )(q, k, k, v, v)


def _alibi_slopes(n):
    return [float(np.float32(2.0 ** (-8.0 * i / n))) for i in range(1, n + 1)]


def _dilated_attention(a, bsz, seq):
    slopes = _alibi_slopes(len(DSWA_PATTERNS) * N_HEADS)
    a4 = a.reshape(bsz, seq, 9, QK_W)
    outs, lses = [], []
    for g, (window, dilation) in enumerate(DSWA_PATTERNS):
        m = seq // dilation

        def to_sub(t):
            return t.reshape(bsz, m, dilation, QK_W).transpose(0, 2, 1, 3).reshape(bsz * dilation, m, QK_W)

        def from_sub(t):
            return t.reshape(bsz, dilation, m, QK_W).transpose(0, 2, 1, 3).reshape(bsz * seq, QK_W)

        o, lse = _dilated_attention_group(to_sub(a4[:, :, g]), to_sub(a4[:, :, 3 + g]), to_sub(a4[:, :, 6 + g]),
                                          slopes[g * N_HEADS:(g + 1) * N_HEADS], window, dilation)
        outs.append(from_sub(o))
        lses.append(from_sub(lse))
    return outs, lses


def _conv_kernel(prev_ref, cur_ref, w_ref, b_ref, lg_ref, lb_ref, o_ref, buf_ref):
    ts = cur_ref.shape[0]
    first = pl.program_id(1) == 0
    buf_ref[0:CONV_HALO, :] = jnp.where(first, 0.0, prev_ref[...])
    buf_ref[CONV_HALO:, :] = cur_ref[...]
    off = CONV_HALO - (CONV_WIDTH - 1)
    acc = jnp.zeros((ts, CONV_CH), F32) + b_ref[...]
    for w in range(CONV_WIDTH):
        acc = acc + buf_ref[off + w:off + w + ts, :] * w_ref[w:w + 1, :]
    mu = jnp.mean(acc, axis=-1, keepdims=True)
    ctr = acc - mu
    var = jnp.mean(ctr * ctr, axis=-1, keepdims=True)
    y = ctr * lax.rsqrt(var + NORM_EPS) * lg_ref[...] + lb_ref[...]
    o_ref[...] = (y * _sigmoid(y)).astype(BF16)


def _conv_mixer(y, conv_w, conv_b, ln_g, ln_b, bsz, seq):
    ts = TOKEN_TILE
    per_halo = ts // CONV_HALO
    y3 = y.reshape(bsz, seq, CONV_CH)
    w_pad = jnp.zeros((CONV_HALO, CONV_CH), F32).at[:CONV_WIDTH].set(conv_w)
    out = pl.pallas_call(
        _conv_kernel,
        grid=(bsz, seq // ts),
        in_specs=[
            pl.BlockSpec((None, CONV_HALO, CONV_CH), lambda b, i: (b, jnp.maximum(i * per_halo - 1, 0), 0)),
            pl.BlockSpec((None, ts, CONV_CH), lambda b, i: (b, i, 0)),
            _const_spec((CONV_HALO, CONV_CH)), _const_spec((1, CONV_CH)), _const_spec((1, CONV_CH)),
            _const_spec((1, CONV_CH)),
        ],
        out_specs=pl.BlockSpec((None, ts, CONV_CH), lambda b, i: (b, i, 0)),
        out_shape=jax.ShapeDtypeStruct((bsz, seq, CONV_CH), BF16),
        scratch_shapes=[pltpu.VMEM((ts + CONV_HALO, CONV_CH), F32)],
        compiler_params=_params("parallel", "parallel"),
        name="conv_mixer",
    )(y3, y3, w_pad, conv_b.reshape(1, -1), ln_g.reshape(1, -1), ln_b.reshape(1, -1))
    return out.reshape(bsz * seq, CONV_CH)


def _gla_kernel(q_ref, k_ref, v_ref, r_ref, low_ref, wg_ref, bg_ref, og_ref, o_ref, st_ref):
    ch, sub = GLA_CHUNK, GLA_SUB

    @pl.when(pl.program_id(1) == 0)
    def _():
        st_ref[...] = jnp.zeros_like(st_ref)

    head = _head_of_lane()
    tri = _ones_where(_iota((ch, ch), 0) >= _iota((ch, ch), 1))
    st_mask = (jnp.right_shift(_iota((GLA_V_W, QK_W), 0), int(np.log2(GLA_DV)))
               == jnp.right_shift(_iota((GLA_V_W, QK_W), 1), int(np.log2(HEAD_LANES))))
    row_in_sub = jnp.bitwise_and(_iota((N_HEADS * sub, ch), 0), sub - 1)
    key_idx = _iota((N_HEADS * sub, ch), 1)
    key_row = _iota((ch, QK_W), 0)

    for c in range(q_ref.shape[0] // ch):
        rows = slice(c * ch, (c + 1) * ch)
        logit = _dot(low_ref[rows, :].astype(BF16), wg_ref[...]) + bg_ref[...]
        log_a = _neg_softplus(-logit) * (1.0 / GLA_TAU)
        la_hi, la_lo = _split_bf16(log_a)
        cum = _dot(tri, la_hi) + _dot(tri, la_lo)
        total = cum[ch - 1:ch, :]
        q = q_ref[rows, :] * (HEAD_LANES ** -0.5)
        k = k_ref[rows, :]
        v = v_ref[rows, :].astype(BF16)
        state = st_ref[...]

        o = _dot_nt((q * jnp.exp(cum)).astype(BF16), state.astype(BF16))

        att_blocks = []
        for i in range(ch // sub):
            base = jnp.zeros((1, QK_W), F32) if i == 0 else cum[i * sub - 1:i * sub, :]
            q_i = q[i * sub:(i + 1) * sub, :] * jnp.exp(cum[i * sub:(i + 1) * sub, :] - base)
            q_stack = jnp.concatenate([jnp.where(head == h, q_i, 0.0) for h in range(N_HEADS)], axis=0)
            k_i = jnp.where(key_row < (i + 1) * sub, k * jnp.exp(base - cum), 0.0)
            att = _dot_nt(q_stack.astype(BF16), k_i.astype(BF16))
            att_blocks.append(jnp.where(key_idx <= row_in_sub + i * sub, att, 0.0))
        intra = []
        for h in range(N_HEADS):
            att_h = jnp.concatenate([blk[h * sub:(h + 1) * sub, :] for blk in att_blocks], axis=0)
            intra.append(_dot(att_h.astype(BF16), v[:, h * GLA_DV:(h + 1) * GLA_DV]))
        o = o + jnp.concatenate(intra, axis=1)

        k_dec = (k * jnp.exp(total - cum)).astype(BF16)
        st_ref[...] = state * jnp.exp(total) + jnp.where(st_mask, _dot_tn(v, k_dec), 0.0)

        r = r_ref[rows, :]
        outs = []
        for h in range(N_HEADS):
            cols = slice(h * GLA_DV, (h + 1) * GLA_DV)
            o_h = o[:, cols]
            ms = jnp.mean(o_h * o_h, axis=-1, keepdims=True)
            outs.append(o_h * lax.rsqrt(ms + NORM_EPS) * og_ref[...])
        o_n = jnp.concatenate(outs, axis=1)
        o_ref[rows, :] = (r * _sigmoid(r) * o_n).astype(BF16)


def _gla_mixer(cbuf, w_gate, b_gate, o_gain, bsz, seq):
    tc = GLA_CHUNK * GLA_CHUNKS_PER_STEP
    c3 = cbuf.reshape(bsz, seq, C_W)
    wg = jnp.zeros((GLA_LOW_PAD, QK_W), F32).at[:GLA_RANK].set(w_gate).astype(BF16)
    spec = lambda w, j: pl.BlockSpec((None, tc, w), lambda b, i: (b, i, j))
    out = pl.pallas_call(
        _gla_kernel,
        grid=(bsz, seq // tc),
        in_specs=[
            spec(QK_W, 0), spec(QK_W, 1), spec(GLA_V_W, 1), spec(GLA_V_W, 2),
            spec(GLA_LOW_PAD, (2 * QK_W + 2 * GLA_V_W) // GLA_LOW_PAD),
            _const_spec((GLA_LOW_PAD, QK_W)), _const_spec((1, QK_W)), _const_spec((1, GLA_DV)),
        ],
        out_specs=pl.BlockSpec((None, tc, GLA_V_W), lambda b, i: (b, i, 0)),
        out_shape=jax.ShapeDtypeStruct((bsz, seq, GLA_V_W), BF16),
        scratch_shapes=[pltpu.VMEM((GLA_V_W, QK_W), F32)],
        compiler_params=_params("parallel", "arbitrary"),
        name="gla_mixer",
    )(c3, c3, c3, c3, c3, wg, b_gate.reshape(1, -1), o_gain.reshape(1, -1))
    return out.reshape(bsz * seq, GLA_V_W)


def _sb_kernel(q_ref, k_ref, v_ref, o_ref):
    tq, tk = SB_TQ, SB_TK
    i = pl.program_id(1)
    q = q_ref[...]
    head = _head_of_lane()
    q_heads = [jnp.where(head == h, q, jnp.zeros_like(q)) for h in range(N_HEADS)]
    tri = _ones_where(_iota((tk, tk), 0) >= _iota((tk, tk), 1))
    q_pos = i * tq + _iota((tq, tk), 0)
    diag = (i * tq) // tk

    def block(j, carry, masked):
        acc, run = carry
        start = pl.multiple_of(j * tk, tk)
        k_blk = k_ref[pl.ds(start, tk), :]
        v_blk = v_ref[pl.ds(start, tk), :]
        if masked:
            keep = (j * tk + _iota((tq, tk), 1)) < q_pos
        new_run = []
        for h in range(N_HEADS):
            z = _dot_nt(q_heads[h], k_blk)
            log_keep = _neg_softplus(z)
            if masked:
                log_keep = jnp.where(keep, log_keep, 0.0)
            lk_hi, lk_lo = _split_bf16(log_keep)
            suffix = _dot(lk_hi, tri) + _dot(lk_lo, tri)
            w = jnp.exp(z + suffix + run[h])
            if masked:
                w = jnp.where(keep, w, 0.0)
            acc = acc + _dot(w.astype(BF16), jnp.where(head == h, v_blk, jnp.zeros_like(v_blk)))
            new_run.append(run[h] + jnp.sum(log_keep, axis=1, keepdims=True))
        return acc, tuple(new_run)

    carry = (jnp.zeros((tq, QK_W), F32), tuple(jnp.zeros((tq, 1), F32) for _ in range(N_HEADS)))
    carry = block(diag, carry, True)
    acc, _ = lax.fori_loop(0, diag, lambda it, c: block(diag - 1 - it, c, False), carry)
    o_ref[...] = acc.astype(BF16)


def _stick_breaking_mixer(dbuf, bsz, seq):
    d3 = dbuf.reshape(bsz, seq, D_W)
    full = lambda j: pl.BlockSpec((None, seq, QK_W), lambda b, i: (b, 0, j))
    out = pl.pallas_call(
        _sb_kernel,
        grid=(bsz, seq // SB_TQ),
        in_specs=[pl.BlockSpec((None, SB_TQ, QK_W), lambda b, i: (b, i, 0)), full(1), full(2)],
        out_specs=pl.BlockSpec((None, SB_TQ, QK_W), lambda b, i: (b, i, 0)),
        out_shape=jax.ShapeDtypeStruct((bsz, seq, QK_W), BF16),
        compiler_params=_params("parallel", "arbitrary"),
        name="stick_breaking_mixer",
    )(d3, d3, d3)
    return out.reshape(bsz * seq, QK_W)


def _merge_kernel(x_ref, mod_ref, g_ref, oa0_ref, oa1_ref, oa2_ref, l0_ref, l1_ref, l2_ref, ob_ref, oc_ref, od_ref,
                  wg_ref, wa_ref, wb_ref, wc_ref, wd_ref, wo_ref, out_ref):
    x = x_ref[...]
    h = _modulated_norm(x, g_ref[...], mod_ref[1:2, :], mod_ref[0:1, :]).astype(BF16)
    l0, l1, l2 = l0_ref[...], l1_ref[...], l2_ref[...]
    mx = jnp.maximum(jnp.maximum(l0, l1), l2)
    e0, e1, e2 = jnp.exp(l0 - mx), jnp.exp(l1 - mx), jnp.exp(l2 - mx)
    o_a = (e0 * oa0_ref[...].astype(F32) + e1 * oa1_ref[...].astype(F32) + e2 * oa2_ref[...].astype(F32)) / (e0 + e1 + e2)
    branches = ((o_a.astype(BF16), wa_ref), (ob_ref[...], wb_ref), (oc_ref[...], wc_ref), (od_ref[...], wd_ref))
    merged = jnp.zeros(x.shape, F32)
    for j, (o_j, w_ref) in enumerate(branches):
        gate = _sigmoid(_dot(h, wg_ref[:, j * D_MODEL:(j + 1) * D_MODEL]))
        merged = merged + gate * _dot(o_j, w_ref[...])
    out_ref[...] = x + mod_ref[2:3, :] * _dot(merged.astype(BF16), wo_ref[...])


def _merge(x2d, mod, g_mix, oa, lse, ob, oc, od, wg, wa, wb, wc, wd, wo, seq):
    t = x2d.shape[0]
    tm = TOKEN_TILE
    per_b = seq // tm
    row = lambda w: pl.BlockSpec((tm, w), lambda i: (i, 0))
    return pl.pallas_call(
        _merge_kernel,
        grid=(t // tm,),
        in_specs=[
            row(D_MODEL),
            pl.BlockSpec((None, 6, D_MODEL), lambda i: (i // per_b, 0, 0)),
            _const_spec((1, D_MODEL)),
            row(QK_W), row(QK_W), row(QK_W), row(QK_W), row(QK_W), row(QK_W),
            row(CONV_CH), row(GLA_V_W), row(QK_W),
            _const_spec(wg.shape), _const_spec(wa.shape), _const_spec(wb.shape), _const_spec(wc.shape),
            _const_spec(wd.shape), _const_spec(wo.shape),
        ],
        out_specs=row(D_MODEL),
        out_shape=jax.ShapeDtypeStruct((t, D_MODEL), F32),
        compiler_params=_params("parallel"),
        name="gated_merge",
    )(x2d, mod, g_mix, *oa, *lse, ob, oc, od, wg, wa, wb, wc, wd, wo)


def _mlp_kernel(x_ref, mod_ref, g_ref, wu_ref, wd_ref, out_ref):
    x = x_ref[...]
    h = _modulated_norm(x, g_ref[...], mod_ref[4:5, :], mod_ref[3:4, :]).astype(BF16)
    acc = jnp.zeros(x.shape, F32)
    for j in range(D_FF // D_MODEL):
        cols = slice(j * D_MODEL, (j + 1) * D_MODEL)
        u = jnp.maximum(_dot(h, wu_ref[:, cols]), 0.0)
        acc = acc + _dot((u * u).astype(BF16), wd_ref[cols, :])
    out_ref[...] = x + mod_ref[5:6, :] * acc


def _mlp(x2d, mod, g_mlp, wu, wd, seq):
    t = x2d.shape[0]
    tm = TOKEN_TILE
    per_b = seq // tm
    row = pl.BlockSpec((tm, D_MODEL), lambda i: (i, 0))
    return pl.pallas_call(
        _mlp_kernel,
        grid=(t // tm,),
        in_specs=[
            row,
            pl.BlockSpec((None, 6, D_MODEL), lambda i: (i // per_b, 0, 0)),
            _const_spec((1, D_MODEL)), _const_spec(wu.shape), _const_spec(wd.shape),
        ],
        out_specs=row,
        out_shape=jax.ShapeDtypeStruct((t, D_MODEL), F32),
        compiler_params=_params("parallel"),
        name="relu2_mlp",
    )(x2d, mod, g_mlp, wu, wd)


def _head_block_diag():
    idx = np.arange(QK_W) // HEAD_LANES
    return jnp.asarray((idx[:, None] == idx[None, :]).astype(np.float32) / HEAD_LANES, dtype=BF16)


def kernel(x, c, w_ada, b_ada, g_mix, w_in, q_gain, k_gain, conv_w, conv_b, conv_ln_g, conv_ln_b,
           gla_w_gate, gla_b_gate, gla_o_gain, w_br_a, w_br_b, w_br_c, w_br_d, w_out, g_mlp, w_up, w_down):
    bsz, seq, d = x.shape
    depth = w_ada.shape[0]
    assert d == D_MODEL and seq % (DSWA_BLOCK * DSWA_PATTERNS[-1][1]) == 0 and seq % TOKEN_TILE == 0
    mods = _modulation(c, w_ada, b_ada)
    bd = _head_block_diag()
    x2d = x.reshape(bsz * seq, d)
    qk_scale = HEAD_LANES ** -0.5
    for l in range(depth):
        w = w_in[l]
        o_conv = A_W
        o_c = o_conv + 2 * CONV_CH
        o_low = o_c + 2 * QK_W + 2 * GLA_V_W
        o_d = o_low + GLA_RANK
        o_gate = o_d + D_W
        wa = w[:, :A_W].astype(BF16)
        wb = w[:, o_conv:o_c].astype(BF16)
        wc = jnp.concatenate([w[:, o_c:o_low + GLA_RANK], jnp.zeros((d, GLA_LOW_PAD - GLA_RANK), F32)], axis=1).astype(BF16)
        wd = w[:, o_d:o_gate].astype(BF16)
        wg = w[:, o_gate:].astype(BF16)
        qg = jnp.tile(q_gain[l], N_HEADS).reshape(1, QK_W) * qk_scale
        kg = jnp.tile(k_gain[l], N_HEADS).reshape(1, QK_W)
        mod = mods[l]
        gm = g_mix[l].reshape(1, d)

        a, y, cbuf, dbuf = _in_projection(x2d, mod, gm, wa, wb, wc, wd, qg, kg, bd, seq)
        oa, lse = _dilated_attention(a, bsz, seq)
        ob = _conv_mixer(y, conv_w[l], conv_b[l], conv_ln_g[l], conv_ln_b[l], bsz, seq)
        oc = _gla_mixer(cbuf, gla_w_gate[l], gla_b_gate[l], gla_o_gain[l], bsz, seq)
        od = _stick_breaking_mixer(dbuf, bsz, seq)
        x2d = _merge(x2d, mod, gm, oa, lse, ob, oc, od, wg, w_br_a[l].astype(BF16), w_br_b[l].astype(BF16),
                     w_br_c[l].astype(BF16), w_br_d[l].astype(BF16), w_out[l].astype(BF16), seq)
        x2d = _mlp(x2d, mod, g_mlp[l].reshape(1, d), w_up[l].astype(BF16), w_down[l].astype(BF16), seq)
    return x2d.reshape(bsz, seq, d)
```

```python
import functools

import numpy as np
import jax
import jax.numpy as jnp
from jax import lax
from jax.experimental import pallas as pl
from jax.experimental.pallas import tpu as pltpu

F32 = jnp.float32
BF16 = jnp.bfloat16

D_MODEL = 1024
D_FF = 4 * D_MODEL
NORM_EPS = 1e-6
HEAD_LANES = 64
N_HEADS = 4
QK_W = N_HEADS * HEAD_LANES
DSWA_PATTERNS = ((128, 1), (512, 4), (2048, 16))
DSWA_BLOCK = 128
CONV_CH = 256
CONV_WIDTH = 31
CONV_HALO = 32
GLA_DV = 128
GLA_V_W = N_HEADS * GLA_DV
GLA_RANK = 16
GLA_LOW_PAD = 128
GLA_TAU = 16.0
GLA_CHUNK = 64
GLA_SUB = 16
GLA_CHUNKS_PER_STEP = 4
SB_T = 256
SB_NORM_ROWS = 512
SB_BOUND_SLACK = 1.02
SB_ZERO_EXPONENT = -110.0
TOKEN_TILE = 512
MASK_VALUE = -1e30
VMEM_LIMIT = 56 * 1024 * 1024

A_W = 3 * 3 * QK_W
C_W = 2 * QK_W + 2 * GLA_V_W + GLA_LOW_PAD
D_W = 3 * QK_W


def _dot(a, b):
    return jnp.dot(a, b, preferred_element_type=F32)


def _dot_nt(a, b):
    return lax.dot_general(a, b, (((1,), (1,)), ((), ())), preferred_element_type=F32)


def _dot_tn(a, b):
    return lax.dot_general(a, b, (((0,), (0,)), ((), ())), preferred_element_type=F32)


def _split_bf16(x):
    hi = x.astype(BF16)
    lo = (x - hi.astype(F32)).astype(BF16)
    return hi, lo


def _sigmoid(x):
    return 1.0 / (1.0 + jnp.exp(-x))


def _neg_softplus(x):
    return -(jnp.maximum(x, 0.0) + jnp.log(1.0 + jnp.exp(-jnp.abs(x))))


def _iota(shape, dim):
    return lax.broadcasted_iota(jnp.int32, shape, dim)


def _head_of_lane(width=QK_W, lanes_per_head=HEAD_LANES):
    return jnp.right_shift(_iota((1, width), 1), int(np.log2(lanes_per_head)))


def _ones_where(cond):
    return jnp.where(cond, 1.0, 0.0).astype(BF16)


def _const_spec(shape):
    nd = len(shape)
    return pl.BlockSpec(shape, lambda *_: (0,) * nd, pipeline_mode=pl.Buffered(1))


def _params(*sem):
    return pltpu.CompilerParams(dimension_semantics=sem, vmem_limit_bytes=VMEM_LIMIT)


def _modulated_norm(x, gain, scale, shift):
    ms = jnp.mean(x * x, axis=-1, keepdims=True)
    y = x * lax.rsqrt(ms + NORM_EPS) * gain
    return y * (1.0 + scale) + shift


def _mod_kernel(c_ref, w_ref, b_ref, o_ref):
    c_hi, c_lo = _split_bf16(c_ref[...])
    w_hi, w_lo = _split_bf16(w_ref[...])
    o_ref[...] = _dot(c_hi, w_hi) + _dot(c_hi, w_lo) + _dot(c_lo, w_hi) + b_ref[...]


def _modulation(c, w_ada, b_ada):
    depth, d, six_d = w_ada.shape
    bsz = c.shape[0]
    out = pl.pallas_call(
        _mod_kernel,
        grid=(depth, six_d // d),
        in_specs=[
            pl.BlockSpec((bsz, d), lambda l, j: (0, 0)),
            pl.BlockSpec((None, d, d), lambda l, j: (l, 0, j)),
            pl.BlockSpec((None, 1, d), lambda l, j: (l, 0, j)),
        ],
        out_specs=pl.BlockSpec((None, bsz, d), lambda l, j: (l, 0, j)),
        out_shape=jax.ShapeDtypeStruct((depth, bsz, six_d), F32),
        compiler_params=_params("parallel", "parallel"),
        name="adaln_modulation",
    )(c, w_ada, b_ada.reshape(depth, 1, six_d))
    return out.reshape(depth, bsz, six_d // d, d)


def _inproj_kernel(x_ref, mod_ref, g_ref, wa_ref, wb_ref, wc_ref, wd_ref, qg_ref, kg_ref, bd_ref,
                   a_ref, y_ref, c_ref, d_ref):
    h = _modulated_norm(x_ref[...], g_ref[...], mod_ref[1:2, :], mod_ref[0:1, :]).astype(BF16)
    bd = bd_ref[...]
    for j in range(A_W // QK_W):
        cols = slice(j * QK_W, (j + 1) * QK_W)
        acc = _dot(h, wa_ref[:, cols])
        if j < 6:
            sq_hi, sq_lo = _split_bf16(acc * acc)
            ms = _dot(sq_hi, bd) + _dot(sq_lo, bd)
            gain = qg_ref[...] if j < 3 else kg_ref[...]
            acc = acc * lax.rsqrt(ms + NORM_EPS) * gain
        a_ref[:, cols] = acc.astype(BF16)
    u = _dot(h, wb_ref[...])
    y_ref[...] = u[:, :CONV_CH] * _sigmoid(u[:, CONV_CH:])
    c_ref[...] = _dot(h, wc_ref[...])
    for j in range(3):
        cols = slice(j * QK_W, (j + 1) * QK_W)
        acc = _dot(h, wd_ref[:, cols])
        if j == 0:
            acc = acc * (HEAD_LANES ** -0.5)
        d_ref[:, cols] = acc.astype(BF16)


def _in_projection(x2d, mod, g_mix, wa, wb, wc, wd, qg, kg, bd, seq):
    t = x2d.shape[0]
    tm = TOKEN_TILE
    per_b = seq // tm
    row = lambda w: pl.BlockSpec((tm, w), lambda i: (i, 0))
    return pl.pallas_call(
        _inproj_kernel,
        grid=(t // tm,),
        in_specs=[
            row(D_MODEL),
            pl.BlockSpec((None, 6, D_MODEL), lambda i: (i // per_b, 0, 0)),
            _const_spec((1, D_MODEL)),
            _const_spec(wa.shape), _const_spec(wb.shape), _const_spec(wc.shape), _const_spec(wd.shape),
            _const_spec((1, QK_W)), _const_spec((1, QK_W)), _const_spec((QK_W, QK_W)),
        ],
        out_specs=[row(A_W), row(CONV_CH), row(C_W), row(D_W)],
        out_shape=[
            jax.ShapeDtypeStruct((t, A_W), BF16),
            jax.ShapeDtypeStruct((t, CONV_CH), F32),
            jax.ShapeDtypeStruct((t, C_W), F32),
            jax.ShapeDtypeStruct((t, D_W), BF16),
        ],
        compiler_params=_params("parallel"),
        name="in_projection",
    )(x2d, mod, g_mix, wa, wb, wc, wd, qg, kg, bd)


def _dswa_kernel(q_ref, kp_ref, kc_ref, vp_ref, vc_ref, o_ref, lse_ref, *, slopes, dilation, max_rel):
    n = pl.program_id(1)
    blk = DSWA_BLOCK
    q = q_ref[...]
    k = jnp.concatenate([kp_ref[...], kc_ref[...]], axis=0)
    v = jnp.concatenate([vp_ref[...], vc_ref[...]], axis=0)
    head = _head_of_lane()
    qi = _iota((blk, 2 * blk), 0)
    kj = _iota((blk, 2 * blk), 1)
    rel = qi - kj + blk
    valid = (rel >= 0) & (rel <= max_rel) & ((kj >= blk) | (n > 0))
    dist = (dilation * rel).astype(F32)
    o_acc = jnp.zeros((blk, QK_W), F32)
    lse_acc = jnp.zeros((blk, QK_W), F32)
    for h in range(N_HEADS):
        in_head = head == h
        s = _dot_nt(jnp.where(in_head, q, jnp.zeros_like(q)), k)
        s = jnp.where(valid, s - slopes[h] * dist, MASK_VALUE)
        mx = jnp.max(s, axis=1, keepdims=True)
        p = jnp.exp(s - mx)
        den = jnp.sum(p, axis=1, keepdims=True)
        o_h = _dot(p.astype(BF16), jnp.where(in_head, v, jnp.zeros_like(v)))
        o_acc = o_acc + o_h * (1.0 / den)
        lse_acc = jnp.where(in_head, mx + jnp.log(den), lse_acc)
    o_ref[...] = o_acc.astype(BF16)
    lse_ref[...] = lse_acc


def _dilated_attention_group(q, k, v, slopes, window, dilation):
    nsub, m, _ = q.shape
    blk = DSWA_BLOCK
    cur = pl.BlockSpec((None, blk, QK_W), lambda s, n: (s, n, 0))
    prev = pl.BlockSpec((None, blk, QK_W), lambda s, n: (s, jnp.maximum(n - 1, 0), 0))
    kern = functools.partial(_dswa_kernel, slopes=slopes, dilation=dilation, max_rel=window // dilation)
    return pl.pallas_call(
        kern,
        grid=(nsub, m // blk),
        in_specs=[cur, prev, cur, prev, cur],
        out_specs=[cur, cur],
        out_shape=[jax.ShapeDtypeStruct((nsub, m, QK_W), BF16), jax.ShapeDtypeStruct((nsub, m, QK_W), F32)],
        compiler_params=_params("parallel", "parallel"),
        name=f"dilated_attention_d{dilation}",
    )(q, k, k, v, v)


def _alibi_slopes(n):
    return [float(np.float32(2.0 ** (-8.0 * i / n))) for i in range(1, n + 1)]


def _dilated_attention(a, bsz, seq):
    slopes = _alibi_slopes(len(DSWA_PATTERNS) * N_HEADS)
    a4 = a.reshape(bsz, seq, 9, QK_W)
    outs, lses = [], []
    for g, (window, dilation) in enumerate(DSWA_PATTERNS):
        m = seq // dilation

        def to_sub(t):
            return t.reshape(bsz, m, dilation, QK_W).transpose(0, 2, 1, 3).reshape(bsz * dilation, m, QK_W)

        def from_sub(t):
            return t.reshape(bsz, dilation, m, QK_W).transpose(0, 2, 1, 3).reshape(bsz * seq, QK_W)

        o, lse = _dilated_attention_group(to_sub(a4[:, :, g]), to_sub(a4[:, :, 3 + g]), to_sub(a4[:, :, 6 + g]),
                                          slopes[g * N_HEADS:(g + 1) * N_HEADS], window, dilation)
        outs.append(from_sub(o))
        lses.append(from_sub(lse))
    return outs, lses


def _conv_kernel(prev_ref, cur_ref, w_ref, b_ref, lg_ref, lb_ref, o_ref, buf_ref):
    ts = cur_ref.shape[0]
    first = pl.program_id(1) == 0
    buf_ref[0:CONV_HALO, :] = jnp.where(first, 0.0, prev_ref[...])
    buf_ref[CONV_HALO:, :] = cur_ref[...]
    off = CONV_HALO - (CONV_WIDTH - 1)
    acc = jnp.zeros((ts, CONV_CH), F32) + b_ref[...]
    for w in range(CONV_WIDTH):
        acc = acc + buf_ref[off + w:off + w + ts, :] * w_ref[w:w + 1, :]
    mu = jnp.mean(acc, axis=-1, keepdims=True)
    ctr = acc - mu
    var = jnp.mean(ctr * ctr, axis=-1, keepdims=True)
    y = ctr * lax.rsqrt(var + NORM_EPS) * lg_ref[...] + lb_ref[...]
    o_ref[...] = (y * _sigmoid(y)).astype(BF16)


def _conv_mixer(y, conv_w, conv_b, ln_g, ln_b, bsz, seq):
    ts = TOKEN_TILE
    per_halo = ts // CONV_HALO
    y3 = y.reshape(bsz, seq, CONV_CH)
    w_pad = jnp.zeros((CONV_HALO, CONV_CH), F32).at[:CONV_WIDTH].set(conv_w)
    out = pl.pallas_call(
        _conv_kernel,
        grid=(bsz, seq // ts),
        in_specs=[
            pl.BlockSpec((None, CONV_HALO, CONV_CH), lambda b, i: (b, jnp.maximum(i * per_halo - 1, 0), 0)),
            pl.BlockSpec((None, ts, CONV_CH), lambda b, i: (b, i, 0)),
            _const_spec((CONV_HALO, CONV_CH)), _const_spec((1, CONV_CH)), _const_spec((1, CONV_CH)),
            _const_spec((1, CONV_CH)),
        ],
        out_specs=pl.BlockSpec((None, ts, CONV_CH), lambda b, i: (b, i, 0)),
        out_shape=jax.ShapeDtypeStruct((bsz, seq, CONV_CH), BF16),
        scratch_shapes=[pltpu.VMEM((ts + CONV_HALO, CONV_CH), F32)],
        compiler_params=_params("parallel", "parallel"),
        name="conv_mixer",
    )(y3, y3, w_pad, conv_b.reshape(1, -1), ln_g.reshape(1, -1), ln_b.reshape(1, -1))
    return out.reshape(bsz * seq, CONV_CH)


def _gla_kernel(q_ref, k_ref, v_ref, r_ref, low_ref, wg_ref, bg_ref, og_ref, o_ref, st_ref):
    ch, sub = GLA_CHUNK, GLA_SUB

    @pl.when(pl.program_id(1) == 0)
    def _():
        st_ref[...] = jnp.zeros_like(st_ref)

    head = _head_of_lane()
    tri = _ones_where(_iota((ch, ch), 0) >= _iota((ch, ch), 1))
    st_mask = (jnp.right_shift(_iota((GLA_V_W, QK_W), 0), int(np.log2(GLA_DV)))
               == jnp.right_shift(_iota((GLA_V_W, QK_W), 1), int(np.log2(HEAD_LANES))))
    row_in_sub = jnp.bitwise_and(_iota((N_HEADS * sub, ch), 0), sub - 1)
    key_idx = _iota((N_HEADS * sub, ch), 1)
    key_row = _iota((ch, QK_W), 0)

    for c in range(q_ref.shape[0] // ch):
        rows = slice(c * ch, (c + 1) * ch)
        logit = _dot(low_ref[rows, :].astype(BF16), wg_ref[...]) + bg_ref[...]
        log_a = _neg_softplus(-logit) * (1.0 / GLA_TAU)
        la_hi, la_lo = _split_bf16(log_a)
        cum = _dot(tri, la_hi) + _dot(tri, la_lo)
        total = cum[ch - 1:ch, :]
        q = q_ref[rows, :] * (HEAD_LANES ** -0.5)
        k = k_ref[rows, :]
        v = v_ref[rows, :].astype(BF16)
        state = st_ref[...]

        o = _dot_nt((q * jnp.exp(cum)).astype(BF16), state.astype(BF16))

        att_blocks = []
        for i in range(ch // sub):
            base = jnp.zeros((1, QK_W), F32) if i == 0 else cum[i * sub - 1:i * sub, :]
            q_i = q[i * sub:(i + 1) * sub, :] * jnp.exp(cum[i * sub:(i + 1) * sub, :] - base)
            q_stack = jnp.concatenate([jnp.where(head == h, q_i, 0.0) for h in range(N_HEADS)], axis=0)
            k_i = jnp.where(key_row < (i + 1) * sub, k * jnp.exp(base - cum), 0.0)
            att = _dot_nt(q_stack.astype(BF16), k_i.astype(BF16))
            att_blocks.append(jnp.where(key_idx <= row_in_sub + i * sub, att, 0.0))
        intra = []
        for h in range(N_HEADS):
            att_h = jnp.concatenate([blk[h * sub:(h + 1) * sub, :] for blk in att_blocks], axis=0)
            intra.append(_dot(att_h.astype(BF16), v[:, h * GLA_DV:(h + 1) * GLA_DV]))
        o = o + jnp.concatenate(intra, axis=1)

        k_dec = (k * jnp.exp(total - cum)).astype(BF16)
        st_ref[...] = state * jnp.exp(total) + jnp.where(st_mask, _dot_tn(v, k_dec), 0.0)

        r = r_ref[rows, :]
        outs = []
        for h in range(N_HEADS):
            cols = slice(h * GLA_DV, (h + 1) * GLA_DV)
            o_h = o[:, cols]
            ms = jnp.mean(o_h * o_h, axis=-1, keepdims=True)
            outs.append(o_h * lax.rsqrt(ms + NORM_EPS) * og_ref[...])
        o_n = jnp.concatenate(outs, axis=1)
        o_ref[rows, :] = (r * _sigmoid(r) * o_n).astype(BF16)


def _gla_mixer(cbuf, w_gate, b_gate, o_gain, bsz, seq):
    tc = GLA_CHUNK * GLA_CHUNKS_PER_STEP
    c3 = cbuf.reshape(bsz, seq, C_W)
    wg = jnp.zeros((GLA_LOW_PAD, QK_W), F32).at[:GLA_RANK].set(w_gate).astype(BF16)
    spec = lambda w, j: pl.BlockSpec((None, tc, w), lambda b, i: (b, i, j))
    out = pl.pallas_call(
        _gla_kernel,
        grid=(bsz, seq // tc),
        in_specs=[
            spec(QK_W, 0), spec(QK_W, 1), spec(GLA_V_W, 1), spec(GLA_V_W, 2),
            spec(GLA_LOW_PAD, (2 * QK_W + 2 * GLA_V_W) // GLA_LOW_PAD),
            _const_spec((GLA_LOW_PAD, QK_W)), _const_spec((1, QK_W)), _const_spec((1, GLA_DV)),
        ],
        out_specs=pl.BlockSpec((None, tc, GLA_V_W), lambda b, i: (b, i, 0)),
        out_shape=jax.ShapeDtypeStruct((bsz, seq, GLA_V_W), BF16),
        scratch_shapes=[pltpu.VMEM((GLA_V_W, QK_W), F32)],
        compiler_params=_params("parallel", "arbitrary"),
        name="gla_mixer",
    )(c3, c3, c3, c3, c3, wg, b_gate.reshape(1, -1), o_gain.reshape(1, -1))
    return out.reshape(bsz * seq, GLA_V_W)


def _sb_kernel(q_ref, k_ref, v_ref, o_ref, kmax_ref):
    t = SB_T
    seq = k_ref.shape[0]
    i = pl.program_id(1)
    head = _head_of_lane()
    head_ones = _ones_where(jnp.right_shift(_iota((QK_W, QK_W), 0), int(np.log2(HEAD_LANES))) == head)

    @pl.when(i == 0)
    def _():
        best = jnp.zeros((SB_NORM_ROWS, QK_W), F32)
        for c in range(seq // SB_NORM_ROWS):
            kf = k_ref[c * SB_NORM_ROWS:(c + 1) * SB_NORM_ROWS, :].astype(F32)
            sq_hi, sq_lo = _split_bf16(kf * kf)
            best = jnp.maximum(best, _dot(sq_hi, head_ones) + _dot(sq_lo, head_ones))
        kmax_ref[...] = jnp.zeros_like(kmax_ref) + jnp.max(best)

    q = q_ref[...]
    q_stack = jnp.concatenate([jnp.where(head == h, q, jnp.zeros_like(q)) for h in range(N_HEADS)], axis=0)
    qf = q_stack.astype(F32)
    z_bound = jnp.sqrt(jnp.sum(qf * qf, axis=1, keepdims=True) * kmax_ref[0:1, 0:1]) * SB_BOUND_SLACK
    tri = _ones_where(_iota((t, t), 0) >= _iota((t, t), 1))
    keep = _iota((N_HEADS * t, t), 1) < jnp.bitwise_and(_iota((N_HEADS * t, t), 0), t - 1)

    def block(j, acc, run, masked):
        start = pl.multiple_of(j * t, t)
        k_blk = k_ref[pl.ds(start, t), :]
        v_blk = v_ref[pl.ds(start, t), :]
        z = _dot_nt(q_stack, k_blk)
        log_keep = _neg_softplus(z)
        if masked:
            log_keep = jnp.where(keep, log_keep, 0.0)
        lk_hi, lk_lo = _split_bf16(log_keep)
        suffix = _dot(lk_hi, tri) + _dot(lk_lo, tri)
        w = jnp.exp(z + suffix + run)
        if masked:
            w = jnp.where(keep, w, 0.0)
        w = w.astype(BF16)
        w_wide = jnp.concatenate([w[h * t:(h + 1) * t, :] for h in range(N_HEADS)], axis=1)
        v_stack = jnp.concatenate([jnp.where(head == h, v_blk, jnp.zeros_like(v_blk)) for h in range(N_HEADS)], axis=0)
        acc = acc + _dot(w_wide, v_stack)
        run = run + jnp.sum(log_keep, axis=1, keepdims=True)
        done = (jnp.max(run + z_bound) < SB_ZERO_EXPONENT).astype(jnp.int32)
        return acc, run, done

    acc, run, done = block(i, jnp.zeros((t, QK_W), F32), jnp.zeros((N_HEADS * t, 1), F32), True)

    def body(c):
        j, _, acc, run = c
        acc, run, done = block(j, acc, run, False)
        return j - 1, done, acc, run

    _, _, acc, _ = lax.while_loop(lambda c: jnp.logical_and(c[0] >= 0, c[1] == 0), body, (i - 1, done, acc, run))
    o_ref[...] = acc.astype(BF16)


def _stick_breaking_mixer(dbuf, bsz, seq):
    d3 = dbuf.reshape(bsz, seq, D_W)
    full = lambda j: pl.BlockSpec((None, seq, QK_W), lambda b, i: (b, 0, j))
    out = pl.pallas_call(
        _sb_kernel,
        grid=(bsz, seq // SB_T),
        in_specs=[pl.BlockSpec((None, SB_T, QK_W), lambda b, i: (b, i, 0)), full(1), full(2)],
        out_specs=pl.BlockSpec((None, SB_T, QK_W), lambda b, i: (b, i, 0)),
        out_shape=jax.ShapeDtypeStruct((bsz, seq, QK_W), BF16),
        scratch_shapes=[pltpu.VMEM((8, 128), F32)],
        compiler_params=_params("parallel", "arbitrary"),
        name="stick_breaking_mixer",
    )(d3, d3, d3)
    return out.reshape(bsz * seq, QK_W)


def _merge_kernel(x_ref, mod_ref, g_ref, oa0_ref, oa1_ref, oa2_ref, l0_ref, l1_ref, l2_ref, ob_ref, oc_ref, od_ref,
                  wg_ref, wa_ref, wb_ref, wc_ref, wd_ref, wo_ref, out_ref):
    x = x_ref[...]
    h = _modulated_norm(x, g_ref[...], mod_ref[1:2, :], mod_ref[0:1, :]).astype(BF16)
    l0, l1, l2 = l0_ref[...], l1_ref[...], l2_ref[...]
    mx = jnp.maximum(jnp.maximum(l0, l1), l2)
    e0, e1, e2 = jnp.exp(l0 - mx), jnp.exp(l1 - mx), jnp.exp(l2 - mx)
    o_a = (e0 * oa0_ref[...].astype(F32) + e1 * oa1_ref[...].astype(F32) + e2 * oa2_ref[...].astype(F32)) / (e0 + e1 + e2)
    branches = ((o_a.astype(BF16), wa_ref), (ob_ref[...], wb_ref), (oc_ref[...], wc_ref), (od_ref[...], wd_ref))
    merged = jnp.zeros(x.shape, F32)
    for j, (o_j, w_ref) in enumerate(branches):
        gate = _sigmoid(_dot(h, wg_ref[:, j * D_MODEL:(j + 1) * D_MODEL]))
        merged = merged + gate * _dot(o_j, w_ref[...])
    out_ref[...] = x + mod_ref[2:3, :] * _dot(merged.astype(BF16), wo_ref[...])


def _merge(x2d, mod, g_mix, oa, lse, ob, oc, od, wg, wa, wb, wc, wd, wo, seq):
    t = x2d.shape[0]
    tm = TOKEN_TILE
    per_b = seq // tm
    row = lambda w: pl.BlockSpec((tm, w), lambda i: (i, 0))
    return pl.pallas_call(
        _merge_kernel,
        grid=(t // tm,),
        in_specs=[
            row(D_MODEL),
            pl.BlockSpec((None, 6, D_MODEL), lambda i: (i // per_b, 0, 0)),
            _const_spec((1, D_MODEL)),
            row(QK_W), row(QK_W), row(QK_W), row(QK_W), row(QK_W), row(QK_W),
            row(CONV_CH), row(GLA_V_W), row(QK_W),
            _const_spec(wg.shape), _const_spec(wa.shape), _const_spec(wb.shape), _const_spec(wc.shape),
            _const_spec(wd.shape), _const_spec(wo.shape),
        ],
        out_specs=row(D_MODEL),
        out_shape=jax.ShapeDtypeStruct((t, D_MODEL), F32),
        compiler_params=_params("parallel"),
        name="gated_merge",
    )(x2d, mod, g_mix, *oa, *lse, ob, oc, od, wg, wa, wb, wc, wd, wo)


def _mlp_kernel(x_ref, mod_ref, g_ref, wu_ref, wd_ref, out_ref):
    x = x_ref[...]
    h = _modulated_norm(x, g_ref[...], mod_ref[4:5, :], mod_ref[3:4, :]).astype(BF16)
    acc = jnp.zeros(x.shape, F32)
    for j in range(D_FF // D_MODEL):
        cols = slice(j * D_MODEL, (j + 1) * D_MODEL)
        u = jnp.maximum(_dot(h, wu_ref[:, cols]), 0.0)
        acc = acc + _dot((u * u).astype(BF16), wd_ref[cols, :])
    out_ref[...] = x + mod_ref[5:6, :] * acc


def _mlp(x2d, mod, g_mlp, wu, wd, seq):
    t = x2d.shape[0]
    tm = TOKEN_TILE
    per_b = seq // tm
    row = pl.BlockSpec((tm, D_MODEL), lambda i: (i, 0))
    return pl.pallas_call(
        _mlp_kernel,
        grid=(t // tm,),
        in_specs=[
            row,
            pl.BlockSpec((None, 6, D_MODEL), lambda i: (i // per_b, 0, 0)),
            _const_spec((1, D_MODEL)), _const_spec(wu.shape), _const_spec(wd.shape),
        ],
        out_specs=row,
        out_shape=jax.ShapeDtypeStruct((t, D_MODEL), F32),
        compiler_params=_params("parallel"),
        name="relu2_mlp",
    )(x2d, mod, g_mlp, wu, wd)


def _head_block_diag():
    idx = np.arange(QK_W) // HEAD_LANES
    return jnp.asarray((idx[:, None] == idx[None, :]).astype(np.float32) / HEAD_LANES, dtype=BF16)


def kernel(x, c, w_ada, b_ada, g_mix, w_in, q_gain, k_gain, conv_w, conv_b, conv_ln_g, conv_ln_b,
           gla_w_gate, gla_b_gate, gla_o_gain, w_br_a, w_br_b, w_br_c, w_br_d, w_out, g_mlp, w_up, w_down):
    bsz, seq, d = x.shape
    depth = w_ada.shape[0]
    assert d == D_MODEL and seq % (DSWA_BLOCK * DSWA_PATTERNS[-1][1]) == 0 and seq % TOKEN_TILE == 0
    mods = _modulation(c, w_ada, b_ada)
    bd = _head_block_diag()
    x2d = x.reshape(bsz * seq, d)
    qk_scale = HEAD_LANES ** -0.5
    for l in range(depth):
        w = w_in[l]
        o_conv = A_W
        o_c = o_conv + 2 * CONV_CH
        o_low = o_c + 2 * QK_W + 2 * GLA_V_W
        o_d = o_low + GLA_RANK
        o_gate = o_d + D_W
        wa = w[:, :A_W].astype(BF16)
        wb = w[:, o_conv:o_c].astype(BF16)
        wc = jnp.concatenate([w[:, o_c:o_low + GLA_RANK], jnp.zeros((d, GLA_LOW_PAD - GLA_RANK), F32)], axis=1).astype(BF16)
        wd = w[:, o_d:o_gate].astype(BF16)
        wg = w[:, o_gate:].astype(BF16)
        qg = jnp.tile(q_gain[l], N_HEADS).reshape(1, QK_W) * qk_scale
        kg = jnp.tile(k_gain[l], N_HEADS).reshape(1, QK_W)
        mod = mods[l]
        gm = g_mix[l].reshape(1, d)

        a, y, cbuf, dbuf = _in_projection(x2d, mod, gm, wa, wb, wc, wd, qg, kg, bd, seq)
        oa, lse = _dilated_attention(a, bsz, seq)
        ob = _conv_mixer(y, conv_w[l], conv_b[l], conv_ln_g[l], conv_ln_b[l], bsz, seq)
        oc = _gla_mixer(cbuf, gla_w_gate[l], gla_b_gate[l], gla_o_gain[l], bsz, seq)
        od = _stick_breaking_mixer(dbuf, bsz, seq)
        x2d = _merge(x2d, mod, gm, oa, lse, ob, oc, od, wg, w_br_a[l].astype(BF16), w_br_b[l].astype(BF16),
                     w_br_c[l].astype(BF16), w_br_d[l].astype(BF16), w_out[l].astype(BF16), seq)
        x2d = _mlp(x2d, mod, g_mlp[l].reshape(1, d), w_up[l].astype(BF16), w_down[l].astype(BF16), seq)
    return x2d.reshape(bsz, seq, d)
```

```python
import functools

import numpy as np
import jax
import jax.numpy as jnp
from jax import lax
from jax.experimental import pallas as pl
from jax.experimental.pallas import tpu as pltpu

F32 = jnp.float32
BF16 = jnp.bfloat16

D_MODEL = 1024
D_FF = 4 * D_MODEL
NORM_EPS = 1e-6
HEAD_LANES = 64
N_HEADS = 4
QK_W = N_HEADS * HEAD_LANES
DSWA_PATTERNS = ((128, 1), (512, 4), (2048, 16))
DSWA_BLOCK = 128
DSWA_SPAN = DSWA_BLOCK * DSWA_PATTERNS[-1][1]
LANE_TILE = 128
CONV_CH = 256
CONV_WIDTH = 31
CONV_HALO = 32
GLA_DV = 128
GLA_V_W = N_HEADS * GLA_DV
GLA_RANK = 16
GLA_LOW_PAD = 128
GLA_TAU = 16.0
GLA_CHUNK = 64
GLA_SUB = 16
GLA_CHUNKS_PER_STEP = 4
SB_T = 256
SB_NORM_ROWS = 512
SB_BOUND_SLACK = 1.02
SB_ZERO_EXPONENT = -110.0
TOKEN_TILE = 512
MASK_VALUE = -1e30
VMEM_LIMIT = 56 * 1024 * 1024

A_W = 3 * 3 * QK_W
C_W = 2 * QK_W + 2 * GLA_V_W + GLA_LOW_PAD
D_W = 3 * QK_W


def _dot(a, b):
    return jnp.dot(a, b, preferred_element_type=F32)


def _dot_nt(a, b):
    return lax.dot_general(a, b, (((1,), (1,)), ((), ())), preferred_element_type=F32)


def _dot_tn(a, b):
    return lax.dot_general(a, b, (((0,), (0,)), ((), ())), preferred_element_type=F32)


def _split_bf16(x):
    hi = x.astype(BF16)
    lo = (x - hi.astype(F32)).astype(BF16)
    return hi, lo


def _sigmoid(x):
    return 1.0 / (1.0 + jnp.exp(-x))


def _neg_softplus(x):
    return -(jnp.maximum(x, 0.0) + jnp.log(1.0 + jnp.exp(-jnp.abs(x))))


def _iota(shape, dim):
    return lax.broadcasted_iota(jnp.int32, shape, dim)


def _head_of_lane(width=QK_W, lanes_per_head=HEAD_LANES):
    return jnp.right_shift(_iota((1, width), 1), int(np.log2(lanes_per_head)))


def _ones_where(cond):
    return jnp.where(cond, 1.0, 0.0).astype(BF16)


def _const_spec(shape):
    nd = len(shape)
    return pl.BlockSpec(shape, lambda *_: (0,) * nd, pipeline_mode=pl.Buffered(1))


def _params(*sem):
    return pltpu.CompilerParams(dimension_semantics=sem, vmem_limit_bytes=VMEM_LIMIT)


def _modulated_norm(x, gain, scale, shift):
    ms = jnp.mean(x * x, axis=-1, keepdims=True)
    y = x * lax.rsqrt(ms + NORM_EPS) * gain
    return y * (1.0 + scale) + shift


def _mod_kernel(c_ref, w_ref, b_ref, o_ref):
    c_hi, c_lo = _split_bf16(c_ref[...])
    w_hi, w_lo = _split_bf16(w_ref[...])
    o_ref[...] = _dot(c_hi, w_hi) + _dot(c_hi, w_lo) + _dot(c_lo, w_hi) + b_ref[...]


def _modulation(c, w_ada, b_ada):
    depth, d, six_d = w_ada.shape
    bsz = c.shape[0]
    out = pl.pallas_call(
        _mod_kernel,
        grid=(depth, six_d // d),
        in_specs=[
            pl.BlockSpec((bsz, d), lambda l, j: (0, 0)),
            pl.BlockSpec((None, d, d), lambda l, j: (l, 0, j)),
            pl.BlockSpec((None, 1, d), lambda l, j: (l, 0, j)),
        ],
        out_specs=pl.BlockSpec((None, bsz, d), lambda l, j: (l, 0, j)),
        out_shape=jax.ShapeDtypeStruct((depth, bsz, six_d), F32),
        compiler_params=_params("parallel", "parallel"),
        name="adaln_modulation",
    )(c, w_ada, b_ada.reshape(depth, 1, six_d))
    return out.reshape(depth, bsz, six_d // d, d)


def _inproj_kernel(x_ref, mod_ref, g_ref, wa_ref, wb_ref, wc_ref, wd_ref, qg_ref, kg_ref, bd_ref,
                   a_ref, y_ref, c_ref, d_ref):
    h = _modulated_norm(x_ref[...], g_ref[...], mod_ref[1:2, :], mod_ref[0:1, :]).astype(BF16)
    bd = bd_ref[...]
    for j in range(A_W // QK_W):
        cols = slice(j * QK_W, (j + 1) * QK_W)
        acc = _dot(h, wa_ref[:, cols])
        if j < 6:
            sq_hi, sq_lo = _split_bf16(acc * acc)
            ms = _dot(sq_hi, bd) + _dot(sq_lo, bd)
            gain = qg_ref[...] if j < 3 else kg_ref[...]
            acc = acc * lax.rsqrt(ms + NORM_EPS) * gain
        a_ref[:, cols] = acc.astype(BF16)
    u = _dot(h, wb_ref[...])
    y_ref[...] = u[:, :CONV_CH] * _sigmoid(u[:, CONV_CH:])
    c_ref[...] = _dot(h, wc_ref[...])
    for j in range(3):
        cols = slice(j * QK_W, (j + 1) * QK_W)
        acc = _dot(h, wd_ref[:, cols])
        if j == 0:
            acc = acc * (HEAD_LANES ** -0.5)
        d_ref[:, cols] = acc.astype(BF16)


def _in_projection(x2d, mod, g_mix, wa, wb, wc, wd, qg, kg, bd, seq):
    t = x2d.shape[0]
    tm = TOKEN_TILE
    per_b = seq // tm
    row = lambda w: pl.BlockSpec((tm, w), lambda i: (i, 0))
    return pl.pallas_call(
        _inproj_kernel,
        grid=(t // tm,),
        in_specs=[
            row(D_MODEL),
            pl.BlockSpec((None, 6, D_MODEL), lambda i: (i // per_b, 0, 0)),
            _const_spec((1, D_MODEL)),
            _const_spec(wa.shape), _const_spec(wb.shape), _const_spec(wc.shape), _const_spec(wd.shape),
            _const_spec((1, QK_W)), _const_spec((1, QK_W)), _const_spec((QK_W, QK_W)),
        ],
        out_specs=[row(A_W), row(CONV_CH), row(C_W), row(D_W)],
        out_shape=[
            jax.ShapeDtypeStruct((t, A_W), BF16),
            jax.ShapeDtypeStruct((t, CONV_CH), F32),
            jax.ShapeDtypeStruct((t, C_W), F32),
            jax.ShapeDtypeStruct((t, D_W), BF16),
        ],
        compiler_params=_params("parallel"),
        name="in_projection",
    )(x2d, mod, g_mix, wa, wb, wc, wd, qg, kg, bd)


def _dswa_group(group, span_idx, q_ref, kp_ref, kc_ref, vp_ref, vc_ref, o_ref, q_st, k_st, v_st, m_st, den_st, acc_st,
                *, slopes, window, dilation):
    blk, span = DSWA_BLOCK, DSWA_SPAN
    halo = blk * dilation
    shift = int(np.log2(dilation))
    for s in range(QK_W // LANE_TILE):
        lanes = slice(s * LANE_TILE, (s + 1) * LANE_TILE)
        q_st[s, :, :] = q_ref[:, lanes].astype(F32)
        k_st[s, 0:halo, :] = kp_ref[span - halo:span, lanes].astype(F32)
        k_st[s, halo:halo + span, :] = kc_ref[:, lanes].astype(F32)
        v_st[s, 0:halo, :] = vp_ref[span - halo:span, lanes].astype(F32)
        v_st[s, halo:halo + span, :] = vc_ref[:, lanes].astype(F32)
    head = _head_of_lane()
    qi = _iota((blk, 2 * blk), 0)
    kj = _iota((blk, 2 * blk), 1)
    rel = qi - kj + blk
    in_window = (rel >= 0) & (rel <= window // dilation)
    dist = (dilation * rel).astype(F32)

    def rows_of(ref, start, n):
        return jnp.concatenate([ref[s, pl.ds(start, n, stride=dilation), :] for s in range(QK_W // LANE_TILE)], axis=1)

    def put_rows(ref, start, val):
        for s in range(QK_W // LANE_TILE):
            ref[s, pl.ds(start, blk, stride=dilation), :] = val[:, s * LANE_TILE:(s + 1) * LANE_TILE]

    def step(it, carry):
        n = jnp.right_shift(it, shift)
        start = n * halo + jnp.bitwise_and(it, dilation - 1)
        q = rows_of(q_st, start, blk).astype(BF16)
        k = rows_of(k_st, start, 2 * blk).astype(BF16)
        v = rows_of(v_st, start, 2 * blk).astype(BF16)
        valid = in_window & ((kj >= blk) | (span_idx > 0) | (n > 0))
        q_stack = jnp.concatenate([jnp.where(head == h, q, jnp.zeros_like(q)) for h in range(N_HEADS)], axis=0)
        s = _dot_nt(q_stack, k)
        s = jnp.concatenate([jnp.where(valid, s[h * blk:(h + 1) * blk] - slopes[h] * dist, MASK_VALUE)
                             for h in range(N_HEADS)], axis=0)
        mx = jnp.max(s, axis=1, keepdims=True)
        p = jnp.exp(s - mx)
        den = jnp.sum(p, axis=1, keepdims=True)
        o_all = _dot(p.astype(BF16), v) * (1.0 / den)
        lse_all = mx + jnp.log(den)
        o_new = jnp.zeros((blk, QK_W), F32)
        lse = jnp.zeros((blk, QK_W), F32)
        for h in range(N_HEADS):
            in_head = head == h
            o_new = jnp.where(in_head, o_all[h * blk:(h + 1) * blk], o_new)
            lse = jnp.where(in_head, lse_all[h * blk:(h + 1) * blk], lse)
        if group == 0:
            m_new, den_new, acc_new = lse, jnp.ones_like(lse), o_new
        else:
            m_old = rows_of(m_st, start, blk)
            m_new = jnp.maximum(m_old, lse)
            w_old, w_new = jnp.exp(m_old - m_new), jnp.exp(lse - m_new)
            den_new = rows_of(den_st, start, blk) * w_old + w_new
            acc_new = rows_of(acc_st, start, blk) * w_old + o_new * w_new
        put_rows(m_st, start, m_new)
        put_rows(den_st, start, den_new)
        put_rows(acc_st, start, acc_new)
        return carry

    lax.fori_loop(0, span // blk, step, 0, unroll=2)
    if group == len(DSWA_PATTERNS) - 1:
        for s in range(QK_W // LANE_TILE):
            o_ref[:, s * LANE_TILE:(s + 1) * LANE_TILE] = (acc_st[s, :, :] / den_st[s, :, :]).astype(BF16)


def _dswa_kernel(q_ref, kp_ref, kc_ref, vp_ref, vc_ref, o_ref, *scratch, slopes):
    span_idx, group = pl.program_id(1), pl.program_id(2)
    for g, (window, dilation) in enumerate(DSWA_PATTERNS):
        pl.when(group == g)(functools.partial(
            _dswa_group, g, span_idx, q_ref, kp_ref, kc_ref, vp_ref, vc_ref, o_ref, *scratch,
            slopes=slopes[g * N_HEADS:(g + 1) * N_HEADS], window=window, dilation=dilation))


def _alibi_slopes(n):
    return [float(np.float32(2.0 ** (-8.0 * i / n))) for i in range(1, n + 1)]


def _dilated_attention(a, bsz, seq):
    span = DSWA_SPAN
    a3 = a.reshape(bsz, seq, A_W)
    n_groups = len(DSWA_PATTERNS)
    cur = lambda part: pl.BlockSpec((None, span, QK_W), lambda b, i, g: (b, i, part * n_groups + g))
    prev = lambda part: pl.BlockSpec((None, span, QK_W), lambda b, i, g: (b, jnp.maximum(i - 1, 0), part * n_groups + g))
    slab = lambda rows: pltpu.VMEM((QK_W // LANE_TILE, rows, LANE_TILE), F32)
    out = pl.pallas_call(
        functools.partial(_dswa_kernel, slopes=_alibi_slopes(n_groups * N_HEADS)),
        grid=(bsz, seq // span, n_groups),
        in_specs=[cur(0), prev(1), cur(1), prev(2), cur(2)],
        out_specs=pl.BlockSpec((None, span, QK_W), lambda b, i, g: (b, i, 0)),
        out_shape=jax.ShapeDtypeStruct((bsz, seq, QK_W), BF16),
        scratch_shapes=[slab(span), slab(2 * span), slab(2 * span), slab(span), slab(span), slab(span)],
        compiler_params=_params("parallel", "arbitrary", "arbitrary"),
        name="dilated_attention",
    )(a3, a3, a3, a3, a3)
    return out.reshape(bsz * seq, QK_W)


def _conv_kernel(prev_ref, cur_ref, w_ref, b_ref, lg_ref, lb_ref, o_ref, buf_ref):
    ts = cur_ref.shape[0]
    first = pl.program_id(1) == 0
    buf_ref[0:CONV_HALO, :] = jnp.where(first, 0.0, prev_ref[...])
    buf_ref[CONV_HALO:, :] = cur_ref[...]
    off = CONV_HALO - (CONV_WIDTH - 1)
    acc = jnp.zeros((ts, CONV_CH), F32) + b_ref[...]
    for w in range(CONV_WIDTH):
        acc = acc + buf_ref[off + w:off + w + ts, :] * w_ref[w:w + 1, :]
    mu = jnp.mean(acc, axis=-1, keepdims=True)
    ctr = acc - mu
    var = jnp.mean(ctr * ctr, axis=-1, keepdims=True)
    y = ctr * lax.rsqrt(var + NORM_EPS) * lg_ref[...] + lb_ref[...]
    o_ref[...] = (y * _sigmoid(y)).astype(BF16)


def _conv_mixer(y, conv_w, conv_b, ln_g, ln_b, bsz, seq):
    ts = TOKEN_TILE
    per_halo = ts // CONV_HALO
    y3 = y.reshape(bsz, seq, CONV_CH)
    w_pad = jnp.zeros((CONV_HALO, CONV_CH), F32).at[:CONV_WIDTH].set(conv_w)
    out = pl.pallas_call(
        _conv_kernel,
        grid=(bsz, seq // ts),
        in_specs=[
            pl.BlockSpec((None, CONV_HALO, CONV_CH), lambda b, i: (b, jnp.maximum(i * per_halo - 1, 0), 0)),
            pl.BlockSpec((None, ts, CONV_CH), lambda b, i: (b, i, 0)),
            _const_spec((CONV_HALO, CONV_CH)), _const_spec((1, CONV_CH)), _const_spec((1, CONV_CH)),
            _const_spec((1, CONV_CH)),
        ],
        out_specs=pl.BlockSpec((None, ts, CONV_CH), lambda b, i: (b, i, 0)),
        out_shape=jax.ShapeDtypeStruct((bsz, seq, CONV_CH), BF16),
        scratch_shapes=[pltpu.VMEM((ts + CONV_HALO, CONV_CH), F32)],
        compiler_params=_params("parallel", "parallel"),
        name="conv_mixer",
    )(y3, y3, w_pad, conv_b.reshape(1, -1), ln_g.reshape(1, -1), ln_b.reshape(1, -1))
    return out.reshape(bsz * seq, CONV_CH)


def _gla_kernel(q_ref, k_ref, v_ref, r_ref, low_ref, wg_ref, bg_ref, og_ref, o_ref, st_ref):
    ch, sub = GLA_CHUNK, GLA_SUB

    @pl.when(pl.program_id(1) == 0)
    def _():
        st_ref[...] = jnp.zeros_like(st_ref)

    head = _head_of_lane()
    tri = _ones_where(_iota((ch, ch), 0) >= _iota((ch, ch), 1))
    st_mask = (jnp.right_shift(_iota((GLA_V_W, QK_W), 0), int(np.log2(GLA_DV)))
               == jnp.right_shift(_iota((GLA_V_W, QK_W), 1), int(np.log2(HEAD_LANES))))
    row_in_sub = jnp.bitwise_and(_iota((N_HEADS * sub, ch), 0), sub - 1)
    key_idx = _iota((N_HEADS * sub, ch), 1)
    key_row = _iota((ch, QK_W), 0)

    for c in range(q_ref.shape[0] // ch):
        rows = slice(c * ch, (c + 1) * ch)
        logit = _dot(low_ref[rows, :].astype(BF16), wg_ref[...]) + bg_ref[...]
        log_a = _neg_softplus(-logit) * (1.0 / GLA_TAU)
        la_hi, la_lo = _split_bf16(log_a)
        cum = _dot(tri, la_hi) + _dot(tri, la_lo)
        total = cum[ch - 1:ch, :]
        q = q_ref[rows, :] * (HEAD_LANES ** -0.5)
        k = k_ref[rows, :]
        v = v_ref[rows, :].astype(BF16)
        state = st_ref[...]

        o = _dot_nt((q * jnp.exp(cum)).astype(BF16), state.astype(BF16))

        att_blocks = []
        for i in range(ch // sub):
            base = jnp.zeros((1, QK_W), F32) if i == 0 else cum[i * sub - 1:i * sub, :]
            q_i = q[i * sub:(i + 1) * sub, :] * jnp.exp(cum[i * sub:(i + 1) * sub, :] - base)
            q_stack = jnp.concatenate([jnp.where(head == h, q_i, 0.0) for h in range(N_HEADS)], axis=0)
            k_i = jnp.where(key_row < (i + 1) * sub, k * jnp.exp(base - cum), 0.0)
            att = _dot_nt(q_stack.astype(BF16), k_i.astype(BF16))
            att_blocks.append(jnp.where(key_idx <= row_in_sub + i * sub, att, 0.0))
        intra = []
        for h in range(N_HEADS):
            att_h = jnp.concatenate([blk[h * sub:(h + 1) * sub, :] for blk in att_blocks], axis=0)
            intra.append(_dot(att_h.astype(BF16), v[:, h * GLA_DV:(h + 1) * GLA_DV]))
        o = o + jnp.concatenate(intra, axis=1)

        k_dec = (k * jnp.exp(total - cum)).astype(BF16)
        st_ref[...] = state * jnp.exp(total) + jnp.where(st_mask, _dot_tn(v, k_dec), 0.0)

        r = r_ref[rows, :]
        outs = []
        for h in range(N_HEADS):
            cols = slice(h * GLA_DV, (h + 1) * GLA_DV)
            o_h = o[:, cols]
            ms = jnp.mean(o_h * o_h, axis=-1, keepdims=True)
            outs.append(o_h * lax.rsqrt(ms + NORM_EPS) * og_ref[...])
        o_n = jnp.concatenate(outs, axis=1)
        o_ref[rows, :] = (r * _sigmoid(r) * o_n).astype(BF16)


def _gla_mixer(cbuf, w_gate, b_gate, o_gain, bsz, seq):
    tc = GLA_CHUNK * GLA_CHUNKS_PER_STEP
    c3 = cbuf.reshape(bsz, seq, C_W)
    wg = jnp.zeros((GLA_LOW_PAD, QK_W), F32).at[:GLA_RANK].set(w_gate).astype(BF16)
    spec = lambda w, j: pl.BlockSpec((None, tc, w), lambda b, i: (b, i, j))
    out = pl.pallas_call(
        _gla_kernel,
        grid=(bsz, seq // tc),
        in_specs=[
            spec(QK_W, 0), spec(QK_W, 1), spec(GLA_V_W, 1), spec(GLA_V_W, 2),
            spec(GLA_LOW_PAD, (2 * QK_W + 2 * GLA_V_W) // GLA_LOW_PAD),
            _const_spec((GLA_LOW_PAD, QK_W)), _const_spec((1, QK_W)), _const_spec((1, GLA_DV)),
        ],
        out_specs=pl.BlockSpec((None, tc, GLA_V_W), lambda b, i: (b, i, 0)),
        out_shape=jax.ShapeDtypeStruct((bsz, seq, GLA_V_W), BF16),
        scratch_shapes=[pltpu.VMEM((GLA_V_W, QK_W), F32)],
        compiler_params=_params("parallel", "arbitrary"),
        name="gla_mixer",
    )(c3, c3, c3, c3, c3, wg, b_gate.reshape(1, -1), o_gain.reshape(1, -1))
    return out.reshape(bsz * seq, GLA_V_W)


def _sb_kernel(q_ref, k_ref, v_ref, o_ref, kmax_ref):
    t = SB_T
    seq = k_ref.shape[0]
    i = pl.program_id(1)
    head = _head_of_lane()
    head_ones = _ones_where(jnp.right_shift(_iota((QK_W, QK_W), 0), int(np.log2(HEAD_LANES))) == head)

    @pl.when(i == 0)
    def _():
        best = jnp.zeros((SB_NORM_ROWS, QK_W), F32)
        for c in range(seq // SB_NORM_ROWS):
            kf = k_ref[c * SB_NORM_ROWS:(c + 1) * SB_NORM_ROWS, :].astype(F32)
            sq_hi, sq_lo = _split_bf16(kf * kf)
            best = jnp.maximum(best, _dot(sq_hi, head_ones) + _dot(sq_lo, head_ones))
        kmax_ref[...] = jnp.zeros_like(kmax_ref) + jnp.max(best)

    q = q_ref[...]
    q_stack = jnp.concatenate([jnp.where(head == h, q, jnp.zeros_like(q)) for h in range(N_HEADS)], axis=0)
    qf = q_stack.astype(F32)
    z_bound = jnp.sqrt(jnp.sum(qf * qf, axis=1, keepdims=True) * kmax_ref[0:1, 0:1]) * SB_BOUND_SLACK
    tri = _ones_where(_iota((t, t), 0) >= _iota((t, t), 1))
    keep = _iota((N_HEADS * t, t), 1) < jnp.bitwise_and(_iota((N_HEADS * t, t), 0), t - 1)

    def block(j, acc, run, masked):
        start = pl.multiple_of(j * t, t)
        k_blk = k_ref[pl.ds(start, t), :]
        v_blk = v_ref[pl.ds(start, t), :]
        z = _dot_nt(q_stack, k_blk)
        log_keep = _neg_softplus(z)
        if masked:
            log_keep = jnp.where(keep, log_keep, 0.0)
        lk_hi, lk_lo = _split_bf16(log_keep)
        suffix = _dot(lk_hi, tri) + _dot(lk_lo, tri)
        w = jnp.exp(z + suffix + run)
        if masked:
            w = jnp.where(keep, w, 0.0)
        w = w.astype(BF16)
        w_wide = jnp.concatenate([w[h * t:(h + 1) * t, :] for h in range(N_HEADS)], axis=1)
        v_stack = jnp.concatenate([jnp.where(head == h, v_blk, jnp.zeros_like(v_blk)) for h in range(N_HEADS)], axis=0)
        acc = acc + _dot(w_wide, v_stack)
        run = run + jnp.sum(log_keep, axis=1, keepdims=True)
        done = (jnp.max(run + z_bound) < SB_ZERO_EXPONENT).astype(jnp.int32)
        return acc, run, done

    acc, run, done = block(i, jnp.zeros((t, QK_W), F32), jnp.zeros((N_HEADS * t, 1), F32), True)

    def body(c):
        j, _, acc, run = c
        acc, run, done = block(j, acc, run, False)
        return j - 1, done, acc, run

    _, _, acc, _ = lax.while_loop(lambda c: jnp.logical_and(c[0] >= 0, c[1] == 0), body, (i - 1, done, acc, run))
    o_ref[...] = acc.astype(BF16)


def _stick_breaking_mixer(dbuf, bsz, seq):
    d3 = dbuf.reshape(bsz, seq, D_W)
    full = lambda j: pl.BlockSpec((None, seq, QK_W), lambda b, i: (b, 0, j))
    out = pl.pallas_call(
        _sb_kernel,
        grid=(bsz, seq // SB_T),
        in_specs=[pl.BlockSpec((None, SB_T, QK_W), lambda b, i: (b, i, 0)), full(1), full(2)],
        out_specs=pl.BlockSpec((None, SB_T, QK_W), lambda b, i: (b, i, 0)),
        out_shape=jax.ShapeDtypeStruct((bsz, seq, QK_W), BF16),
        scratch_shapes=[pltpu.VMEM((8, 128), F32)],
        compiler_params=_params("parallel", "arbitrary"),
        name="stick_breaking_mixer",
    )(d3, d3, d3)
    return out.reshape(bsz * seq, QK_W)


def _merge_kernel(x_ref, mod_ref, g_ref, oa_ref, ob_ref, oc_ref, od_ref,
                  wg_ref, wa_ref, wb_ref, wc_ref, wd_ref, wo_ref, out_ref):
    x = x_ref[...]
    h = _modulated_norm(x, g_ref[...], mod_ref[1:2, :], mod_ref[0:1, :]).astype(BF16)
    branches = ((oa_ref[...], wa_ref), (ob_ref[...], wb_ref), (oc_ref[...], wc_ref), (od_ref[...], wd_ref))
    merged = jnp.zeros(x.shape, F32)
    for j, (o_j, w_ref) in enumerate(branches):
        gate = _sigmoid(_dot(h, wg_ref[:, j * D_MODEL:(j + 1) * D_MODEL]))
        merged = merged + gate * _dot(o_j, w_ref[...])
    out_ref[...] = x + mod_ref[2:3, :] * _dot(merged.astype(BF16), wo_ref[...])


def _merge(x2d, mod, g_mix, oa, ob, oc, od, wg, wa, wb, wc, wd, wo, seq):
    t = x2d.shape[0]
    tm = TOKEN_TILE
    per_b = seq // tm
    row = lambda w: pl.BlockSpec((tm, w), lambda i: (i, 0))
    return pl.pallas_call(
        _merge_kernel,
        grid=(t // tm,),
        in_specs=[
            row(D_MODEL),
            pl.BlockSpec((None, 6, D_MODEL), lambda i: (i // per_b, 0, 0)),
            _const_spec((1, D_MODEL)),
            row(QK_W), row(CONV_CH), row(GLA_V_W), row(QK_W),
            _const_spec(wg.shape), _const_spec(wa.shape), _const_spec(wb.shape), _const_spec(wc.shape),
            _const_spec(wd.shape), _const_spec(wo.shape),
        ],
        out_specs=row(D_MODEL),
        out_shape=jax.ShapeDtypeStruct((t, D_MODEL), F32),
        compiler_params=_params("parallel"),
        name="gated_merge",
    )(x2d, mod, g_mix, oa, ob, oc, od, wg, wa, wb, wc, wd, wo)


def _mlp_kernel(x_ref, mod_ref, g_ref, wu_ref, wd_ref, out_ref):
    x = x_ref[...]
    h = _modulated_norm(x, g_ref[...], mod_ref[4:5, :], mod_ref[3:4, :]).astype(BF16)
    acc = jnp.zeros(x.shape, F32)
    for j in range(D_FF // D_MODEL):
        cols = slice(j * D_MODEL, (j + 1) * D_MODEL)
        u = jnp.maximum(_dot(h, wu_ref[:, cols]), 0.0)
        acc = acc + _dot((u * u).astype(BF16), wd_ref[cols, :])
    out_ref[...] = x + mod_ref[5:6, :] * acc


def _mlp(x2d, mod, g_mlp, wu, wd, seq):
    t = x2d.shape[0]
    tm = TOKEN_TILE
    per_b = seq // tm
    row = pl.BlockSpec((tm, D_MODEL), lambda i: (i, 0))
    return pl.pallas_call(
        _mlp_kernel,
        grid=(t // tm,),
        in_specs=[
            row,
            pl.BlockSpec((None, 6, D_MODEL), lambda i: (i // per_b, 0, 0)),
            _const_spec((1, D_MODEL)), _const_spec(wu.shape), _const_spec(wd.shape),
        ],
        out_specs=row,
        out_shape=jax.ShapeDtypeStruct((t, D_MODEL), F32),
        compiler_params=_params("parallel"),
        name="relu2_mlp",
    )(x2d, mod, g_mlp, wu, wd)


def _head_block_diag():
    idx = np.arange(QK_W) // HEAD_LANES
    return jnp.asarray((idx[:, None] == idx[None, :]).astype(np.float32) / HEAD_LANES, dtype=BF16)


def kernel(x, c, w_ada, b_ada, g_mix, w_in, q_gain, k_gain, conv_w, conv_b, conv_ln_g, conv_ln_b,
           gla_w_gate, gla_b_gate, gla_o_gain, w_br_a, w_br_b, w_br_c, w_br_d, w_out, g_mlp, w_up, w_down):
    bsz, seq, d = x.shape
    depth = w_ada.shape[0]
    assert d == D_MODEL and seq % (DSWA_BLOCK * DSWA_PATTERNS[-1][1]) == 0 and seq % TOKEN_TILE == 0
    mods = _modulation(c, w_ada, b_ada)
    bd = _head_block_diag()
    x2d = x.reshape(bsz * seq, d)
    qk_scale = HEAD_LANES ** -0.5
    for l in range(depth):
        w = w_in[l]
        o_conv = A_W
        o_c = o_conv + 2 * CONV_CH
        o_low = o_c + 2 * QK_W + 2 * GLA_V_W
        o_d = o_low + GLA_RANK
        o_gate = o_d + D_W
        wa = w[:, :A_W].astype(BF16)
        wb = w[:, o_conv:o_c].astype(BF16)
        wc = jnp.concatenate([w[:, o_c:o_low + GLA_RANK], jnp.zeros((d, GLA_LOW_PAD - GLA_RANK), F32)], axis=1).astype(BF16)
        wd = w[:, o_d:o_gate].astype(BF16)
        wg = w[:, o_gate:].astype(BF16)
        qg = jnp.tile(q_gain[l], N_HEADS).reshape(1, QK_W) * qk_scale
        kg = jnp.tile(k_gain[l], N_HEADS).reshape(1, QK_W)
        mod = mods[l]
        gm = g_mix[l].reshape(1, d)

        a, y, cbuf, dbuf = _in_projection(x2d, mod, gm, wa, wb, wc, wd, qg, kg, bd, seq)
        oa = _dilated_attention(a, bsz, seq)
        ob = _conv_mixer(y, conv_w[l], conv_b[l], conv_ln_g[l], conv_ln_b[l], bsz, seq)
        oc = _gla_mixer(cbuf, gla_w_gate[l], gla_b_gate[l], gla_o_gain[l], bsz, seq)
        od = _stick_breaking_mixer(dbuf, bsz, seq)
        x2d = _merge(x2d, mod, gm, oa, ob, oc, od, wg, w_br_a[l].astype(BF16), w_br_b[l].astype(BF16),
                     w_br_c[l].astype(BF16), w_br_d[l].astype(BF16), w_out[l].astype(BF16), seq)
        x2d = _mlp(x2d, mod, g_mlp[l].reshape(1, d), w_up[l].astype(BF16), w_down[l].astype(BF16), seq)
    return x2d.reshape(bsz, seq, d)
```

```python
import functools

import numpy as np
import jax
import jax.numpy as jnp
from jax import lax
from jax.experimental import pallas as pl
from jax.experimental.pallas import tpu as pltpu

F32 = jnp.float32
BF16 = jnp.bfloat16

D_MODEL = 1024
D_FF = 4 * D_MODEL
NORM_EPS = 1e-6
LOG2_E = 1.4426950408889634
HEAD_LANES = 64
N_HEADS = 4
QK_W = N_HEADS * HEAD_LANES
DSWA_PATTERNS = ((128, 1), (512, 4), (2048, 16))
DSWA_BLOCK = 128
DSWA_SPAN = DSWA_BLOCK * DSWA_PATTERNS[-1][1]
LANE_TILE = 128
SUBLANES = 8
CONV_CH = 256
CONV_WIDTH = 31
CONV_HALO = 32
GLA_DV = 128
GLA_V_W = N_HEADS * GLA_DV
GLA_RANK = 16
GLA_LOW_PAD = 128
GLA_TAU = 16.0
GLA_CHUNK = 64
GLA_SUB = 16
GLA_CHUNKS_PER_STEP = 4
SB_T = 256
SB_NORM_ROWS = 512
SB_BOUND_SLACK = 1.02
SB_ZERO_EXPONENT = -110.0
TOKEN_TILE = 512
MASK_VALUE = -1e30
VMEM_LIMIT = 56 * 1024 * 1024

A_W = 3 * 3 * QK_W
C_W = 2 * QK_W + 2 * GLA_V_W + GLA_LOW_PAD
D_W = 3 * QK_W


def _dot(a, b):
    return jnp.dot(a, b, preferred_element_type=F32)


def _dot_nt(a, b):
    return lax.dot_general(a, b, (((1,), (1,)), ((), ())), preferred_element_type=F32)


def _dot_tn(a, b):
    return lax.dot_general(a, b, (((0,), (0,)), ((), ())), preferred_element_type=F32)


def _split_bf16(x):
    hi = x.astype(BF16)
    lo = (x - hi.astype(F32)).astype(BF16)
    return hi, lo


def _sigmoid(x):
    return 1.0 / (1.0 + jnp.exp(-x))


def _neg_softplus(x):
    return -(jnp.maximum(x, 0.0) + jnp.log(1.0 + jnp.exp(-jnp.abs(x))))


def _iota(shape, dim):
    return lax.broadcasted_iota(jnp.int32, shape, dim)


def _head_of_lane(width=QK_W, lanes_per_head=HEAD_LANES):
    return jnp.right_shift(_iota((1, width), 1), int(np.log2(lanes_per_head)))


def _ones_where(cond):
    return jnp.where(cond, 1.0, 0.0).astype(BF16)


def _const_spec(shape):
    nd = len(shape)
    return pl.BlockSpec(shape, lambda *_: (0,) * nd, pipeline_mode=pl.Buffered(1))


def _params(*sem):
    return pltpu.CompilerParams(dimension_semantics=sem, vmem_limit_bytes=VMEM_LIMIT)


def _modulated_norm(x, gain, scale, shift):
    ms = jnp.mean(x * x, axis=-1, keepdims=True)
    y = x * lax.rsqrt(ms + NORM_EPS) * gain
    return y * (1.0 + scale) + shift


def _mod_kernel(c_ref, w_ref, b_ref, o_ref):
    c_hi, c_lo = _split_bf16(c_ref[...])
    w_hi, w_lo = _split_bf16(w_ref[...])
    o_ref[...] = _dot(c_hi, w_hi) + _dot(c_hi, w_lo) + _dot(c_lo, w_hi) + b_ref[...]


def _modulation(c, w_ada, b_ada):
    depth, d, six_d = w_ada.shape
    bsz = c.shape[0]
    out = pl.pallas_call(
        _mod_kernel,
        grid=(depth, six_d // d),
        in_specs=[
            pl.BlockSpec((bsz, d), lambda l, j: (0, 0)),
            pl.BlockSpec((None, d, d), lambda l, j: (l, 0, j)),
            pl.BlockSpec((None, 1, d), lambda l, j: (l, 0, j)),
        ],
        out_specs=pl.BlockSpec((None, bsz, d), lambda l, j: (l, 0, j)),
        out_shape=jax.ShapeDtypeStruct((depth, bsz, six_d), F32),
        compiler_params=_params("parallel", "parallel"),
        name="adaln_modulation",
    )(c, w_ada, b_ada.reshape(depth, 1, six_d))
    return out.reshape(depth, bsz, six_d // d, d)


def _inproj_kernel(x_ref, mod_ref, g_ref, wa_ref, wb_ref, wc_ref, wd_ref, qg_ref, kg_ref, bd_ref,
                   a_ref, y_ref, c_ref, d_ref):
    h = _modulated_norm(x_ref[...], g_ref[...], mod_ref[1:2, :], mod_ref[0:1, :]).astype(BF16)
    bd = bd_ref[...]
    a_all = _dot(h, wa_ref[...])
    for j in range(A_W // QK_W):
        cols = slice(j * QK_W, (j + 1) * QK_W)
        acc = a_all[:, cols]
        if j < 6:
            sq_hi, sq_lo = _split_bf16(acc * acc)
            ms = _dot(sq_hi, bd) + _dot(sq_lo, bd)
            gain = qg_ref[...] if j < 3 else kg_ref[...]
            acc = acc * lax.rsqrt(ms + NORM_EPS) * gain
        a_ref[:, cols] = acc.astype(BF16)
    u = _dot(h, wb_ref[...])
    y_ref[...] = u[:, :CONV_CH] * _sigmoid(u[:, CONV_CH:])
    c_ref[...] = _dot(h, wc_ref[...])
    d_all = _dot(h, wd_ref[...])
    d_ref[:, :QK_W] = (d_all[:, :QK_W] * (HEAD_LANES ** -0.5)).astype(BF16)
    d_ref[:, QK_W:] = d_all[:, QK_W:].astype(BF16)


def _in_projection(x2d, mod, g_mix, wa, wb, wc, wd, qg, kg, bd, seq):
    t = x2d.shape[0]
    tm = TOKEN_TILE
    per_b = seq // tm
    row = lambda w: pl.BlockSpec((tm, w), lambda i: (i, 0))
    return pl.pallas_call(
        _inproj_kernel,
        grid=(t // tm,),
        in_specs=[
            row(D_MODEL),
            pl.BlockSpec((None, 6, D_MODEL), lambda i: (i // per_b, 0, 0)),
            _const_spec((1, D_MODEL)),
            _const_spec(wa.shape), _const_spec(wb.shape), _const_spec(wc.shape), _const_spec(wd.shape),
            _const_spec((1, QK_W)), _const_spec((1, QK_W)), _const_spec((QK_W, QK_W)),
        ],
        out_specs=[row(A_W), row(CONV_CH), row(C_W), row(D_W)],
        out_shape=[
            jax.ShapeDtypeStruct((t, A_W), BF16),
            jax.ShapeDtypeStruct((t, CONV_CH), F32),
            jax.ShapeDtypeStruct((t, C_W), F32),
            jax.ShapeDtypeStruct((t, D_W), BF16),
        ],
        compiler_params=_params("parallel"),
        name="in_projection",
    )(x2d, mod, g_mix, wa, wb, wc, wd, qg, kg, bd)


def _dswa_group(group, span_idx, q_ref, kp_ref, kc_ref, vp_ref, vc_ref, o_ref, q_st, k_st, v_st, m_st, den_st, acc_st,
                *, slopes, window, dilation):
    blk, span = DSWA_BLOCK, DSWA_SPAN
    halo = blk * dilation
    shift = int(np.log2(dilation))
    for s in range(QK_W // LANE_TILE):
        lanes = slice(s * LANE_TILE, (s + 1) * LANE_TILE)
        q_st[s, :, :] = q_ref[:, lanes].astype(F32)
        k_st[s, 0:halo, :] = kp_ref[span - halo:span, lanes].astype(F32)
        k_st[s, halo:halo + span, :] = kc_ref[:, lanes].astype(F32)
        v_st[s, 0:halo, :] = vp_ref[span - halo:span, lanes].astype(F32)
        v_st[s, halo:halo + span, :] = vc_ref[:, lanes].astype(F32)
    head = _head_of_lane()
    qi = _iota((blk, 2 * blk), 0)
    kj = _iota((blk, 2 * blk), 1)
    rel = qi - kj + blk
    in_window = (rel >= 0) & (rel <= window // dilation)
    dist = (dilation * rel).astype(F32)

    def rows_of(ref, start, n):
        return jnp.concatenate([ref[s, pl.ds(start, n, stride=dilation), :] for s in range(QK_W // LANE_TILE)], axis=1)

    def put_rows(ref, start, val):
        for s in range(QK_W // LANE_TILE):
            ref[s, pl.ds(start, blk, stride=dilation), :] = val[:, s * LANE_TILE:(s + 1) * LANE_TILE]

    def step(it, carry):
        n = jnp.right_shift(it, shift)
        start = n * halo + jnp.bitwise_and(it, dilation - 1)
        q = rows_of(q_st, start, blk).astype(BF16)
        k = rows_of(k_st, start, 2 * blk).astype(BF16)
        v = rows_of(v_st, start, 2 * blk).astype(BF16)
        valid = in_window & ((kj >= blk) | (span_idx > 0) | (n > 0))
        q_stack = jnp.concatenate([jnp.where(head == h, q, jnp.zeros_like(q)) for h in range(N_HEADS)], axis=0)
        s = _dot_nt(q_stack, k)
        s = jnp.concatenate([jnp.where(valid, s[h * blk:(h + 1) * blk] - slopes[h] * dist, MASK_VALUE)
                             for h in range(N_HEADS)], axis=0)
        mx = jnp.max(s, axis=1, keepdims=True)
        p = jnp.exp(s - mx)
        den = jnp.sum(p, axis=1, keepdims=True)
        o_all = _dot(p.astype(BF16), v) * (1.0 / den)
        lse_all = mx + jnp.log(den)
        o_new = jnp.zeros((blk, QK_W), F32)
        lse = jnp.zeros((blk, QK_W), F32)
        for h in range(N_HEADS):
            in_head = head == h
            o_new = jnp.where(in_head, o_all[h * blk:(h + 1) * blk], o_new)
            lse = jnp.where(in_head, lse_all[h * blk:(h + 1) * blk], lse)
        if group == 0:
            m_new, den_new, acc_new = lse, jnp.ones_like(lse), o_new
        else:
            m_old = rows_of(m_st, start, blk)
            m_new = jnp.maximum(m_old, lse)
            w_old, w_new = jnp.exp(m_old - m_new), jnp.exp(lse - m_new)
            den_new = rows_of(den_st, start, blk) * w_old + w_new
            acc_new = rows_of(acc_st, start, blk) * w_old + o_new * w_new
        put_rows(m_st, start, m_new)
        put_rows(den_st, start, den_new)
        put_rows(acc_st, start, acc_new)
        return carry

    lax.fori_loop(0, span // blk, step, 0, unroll=2)
    if group == len(DSWA_PATTERNS) - 1:
        for s in range(QK_W // LANE_TILE):
            o_ref[:, s * LANE_TILE:(s + 1) * LANE_TILE] = (acc_st[s, :, :] / den_st[s, :, :]).astype(BF16)


def _dswa_kernel(q_ref, kp_ref, kc_ref, vp_ref, vc_ref, o_ref, *scratch, slopes):
    span_idx, group = pl.program_id(1), pl.program_id(2)
    for g, (window, dilation) in enumerate(DSWA_PATTERNS):
        pl.when(group == g)(functools.partial(
            _dswa_group, g, span_idx, q_ref, kp_ref, kc_ref, vp_ref, vc_ref, o_ref, *scratch,
            slopes=slopes[g * N_HEADS:(g + 1) * N_HEADS], window=window, dilation=dilation))


def _alibi_slopes(n):
    return [float(np.float32(2.0 ** (-8.0 * i / n))) for i in range(1, n + 1)]


def _dilated_attention(a, bsz, seq):
    span = DSWA_SPAN
    a3 = a.reshape(bsz, seq, A_W)
    n_groups = len(DSWA_PATTERNS)
    cur = lambda part: pl.BlockSpec((None, span, QK_W), lambda b, i, g: (b, i, part * n_groups + g))
    prev = lambda part: pl.BlockSpec((None, span, QK_W), lambda b, i, g: (b, jnp.maximum(i - 1, 0), part * n_groups + g))
    slab = lambda rows: pltpu.VMEM((QK_W // LANE_TILE, rows, LANE_TILE), F32)
    out = pl.pallas_call(
        functools.partial(_dswa_kernel, slopes=_alibi_slopes(n_groups * N_HEADS)),
        grid=(bsz, seq // span, n_groups),
        in_specs=[cur(0), prev(1), cur(1), prev(2), cur(2)],
        out_specs=pl.BlockSpec((None, span, QK_W), lambda b, i, g: (b, i, 0)),
        out_shape=jax.ShapeDtypeStruct((bsz, seq, QK_W), BF16),
        scratch_shapes=[slab(span), slab(2 * span), slab(2 * span), slab(span), slab(span), slab(span)],
        compiler_params=_params("parallel", "arbitrary", "arbitrary"),
        name="dilated_attention",
    )(a3, a3, a3, a3, a3)
    return out.reshape(bsz * seq, QK_W)


def _conv_kernel(prev_ref, cur_ref, w_ref, b_ref, lg_ref, lb_ref, o_ref, buf_ref):
    ts = cur_ref.shape[0]
    first = pl.program_id(1) == 0
    buf_ref[0, 0:CONV_HALO, :] = jnp.where(first, 0.0, prev_ref[...])
    buf_ref[0, CONV_HALO:, :] = cur_ref[...]
    rows = ts + CONV_HALO - SUBLANES
    for r in range(1, SUBLANES):
        buf_ref[r, 0:rows, :] = buf_ref[0, r:r + rows, :]
    off = CONV_HALO - (CONV_WIDTH - 1)
    acc = jnp.zeros((ts, CONV_CH), F32) + b_ref[...]
    for w in range(CONV_WIDTH):
        r, base = (off + w) % SUBLANES, (off + w) // SUBLANES * SUBLANES
        acc = acc + buf_ref[r, base:base + ts, :] * w_ref[w:w + 1, :]
    mu = jnp.mean(acc, axis=-1, keepdims=True)
    ctr = acc - mu
    var = jnp.mean(ctr * ctr, axis=-1, keepdims=True)
    y = ctr * lax.rsqrt(var + NORM_EPS) * lg_ref[...] + lb_ref[...]
    o_ref[...] = (y * _sigmoid(y)).astype(BF16)


def _conv_mixer(y, conv_w, conv_b, ln_g, ln_b, bsz, seq):
    ts = TOKEN_TILE
    per_halo = ts // CONV_HALO
    y3 = y.reshape(bsz, seq, CONV_CH)
    w_pad = jnp.zeros((CONV_HALO, CONV_CH), F32).at[:CONV_WIDTH].set(conv_w)
    out = pl.pallas_call(
        _conv_kernel,
        grid=(bsz, seq // ts),
        in_specs=[
            pl.BlockSpec((None, CONV_HALO, CONV_CH), lambda b, i: (b, jnp.maximum(i * per_halo - 1, 0), 0)),
            pl.BlockSpec((None, ts, CONV_CH), lambda b, i: (b, i, 0)),
            _const_spec((CONV_HALO, CONV_CH)), _const_spec((1, CONV_CH)), _const_spec((1, CONV_CH)),
            _const_spec((1, CONV_CH)),
        ],
        out_specs=pl.BlockSpec((None, ts, CONV_CH), lambda b, i: (b, i, 0)),
        out_shape=jax.ShapeDtypeStruct((bsz, seq, CONV_CH), BF16),
        scratch_shapes=[pltpu.VMEM((SUBLANES, ts + CONV_HALO, CONV_CH), F32)],
        compiler_params=_params("parallel", "parallel"),
        name="conv_mixer",
    )(y3, y3, w_pad, conv_b.reshape(1, -1), ln_g.reshape(1, -1), ln_b.reshape(1, -1))
    return out.reshape(bsz * seq, CONV_CH)


def _gla_kernel(q_ref, k_ref, v_ref, r_ref, low_ref, wg_ref, bg_ref, og_ref, o_ref, st_ref):
    ch, sub = GLA_CHUNK, GLA_SUB

    @pl.when(pl.program_id(1) == 0)
    def _():
        st_ref[...] = jnp.zeros_like(st_ref)

    head = _head_of_lane()
    tri = _ones_where(_iota((ch, ch), 0) >= _iota((ch, ch), 1))
    st_mask = (jnp.right_shift(_iota((GLA_V_W, QK_W), 0), int(np.log2(GLA_DV)))
               == jnp.right_shift(_iota((GLA_V_W, QK_W), 1), int(np.log2(HEAD_LANES))))
    row_in_sub = jnp.bitwise_and(_iota((N_HEADS * sub, ch), 0), sub - 1)
    key_idx = _iota((N_HEADS * sub, ch), 1)
    key_row = _iota((ch, QK_W), 0)

    for c in range(q_ref.shape[0] // ch):
        rows = slice(c * ch, (c + 1) * ch)
        logit = _dot(low_ref[rows, :].astype(BF16), wg_ref[...]) + bg_ref[...]
        log_a = _neg_softplus(-logit) * (1.0 / GLA_TAU)
        la_hi, la_lo = _split_bf16(log_a)
        cum = _dot(tri, la_hi) + _dot(tri, la_lo)
        total = cum[ch - 1:ch, :]
        q = q_ref[rows, :] * (HEAD_LANES ** -0.5)
        k = k_ref[rows, :]
        v = v_ref[rows, :].astype(BF16)
        state = st_ref[...]

        o = _dot_nt((q * jnp.exp(cum)).astype(BF16), state.astype(BF16))

        att_blocks = []
        for i in range(ch // sub):
            base = jnp.zeros((1, QK_W), F32) if i == 0 else cum[i * sub - 1:i * sub, :]
            q_i = q[i * sub:(i + 1) * sub, :] * jnp.exp(cum[i * sub:(i + 1) * sub, :] - base)
            q_stack = jnp.concatenate([jnp.where(head == h, q_i, 0.0) for h in range(N_HEADS)], axis=0)
            k_i = jnp.where(key_row < (i + 1) * sub, k * jnp.exp(base - cum), 0.0)
            att = _dot_nt(q_stack.astype(BF16), k_i.astype(BF16))
            att_blocks.append(jnp.where(key_idx <= row_in_sub + i * sub, att, 0.0))
        intra = []
        for h in range(N_HEADS):
            att_h = jnp.concatenate([blk[h * sub:(h + 1) * sub, :] for blk in att_blocks], axis=0)
            intra.append(_dot(att_h.astype(BF16), v[:, h * GLA_DV:(h + 1) * GLA_DV]))
        o = o + jnp.concatenate(intra, axis=1)

        k_dec = (k * jnp.exp(total - cum)).astype(BF16)
        st_ref[...] = state * jnp.exp(total) + jnp.where(st_mask, _dot_tn(v, k_dec), 0.0)

        r = r_ref[rows, :]
        outs = []
        for h in range(N_HEADS):
            cols = slice(h * GLA_DV, (h + 1) * GLA_DV)
            o_h = o[:, cols]
            ms = jnp.mean(o_h * o_h, axis=-1, keepdims=True)
            outs.append(o_h * lax.rsqrt(ms + NORM_EPS) * og_ref[...])
        o_n = jnp.concatenate(outs, axis=1)
        o_ref[rows, :] = (r * _sigmoid(r) * o_n).astype(BF16)


def _gla_mixer(cbuf, w_gate, b_gate, o_gain, bsz, seq):
    tc = GLA_CHUNK * GLA_CHUNKS_PER_STEP
    c3 = cbuf.reshape(bsz, seq, C_W)
    wg = jnp.zeros((GLA_LOW_PAD, QK_W), F32).at[:GLA_RANK].set(w_gate).astype(BF16)
    spec = lambda w, j: pl.BlockSpec((None, tc, w), lambda b, i: (b, i, j))
    out = pl.pallas_call(
        _gla_kernel,
        grid=(bsz, seq // tc),
        in_specs=[
            spec(QK_W, 0), spec(QK_W, 1), spec(GLA_V_W, 1), spec(GLA_V_W, 2),
            spec(GLA_LOW_PAD, (2 * QK_W + 2 * GLA_V_W) // GLA_LOW_PAD),
            _const_spec((GLA_LOW_PAD, QK_W)), _const_spec((1, QK_W)), _const_spec((1, GLA_DV)),
        ],
        out_specs=pl.BlockSpec((None, tc, GLA_V_W), lambda b, i: (b, i, 0)),
        out_shape=jax.ShapeDtypeStruct((bsz, seq, GLA_V_W), BF16),
        scratch_shapes=[pltpu.VMEM((GLA_V_W, QK_W), F32)],
        compiler_params=_params("parallel", "arbitrary"),
        name="gla_mixer",
    )(c3, c3, c3, c3, c3, wg, b_gate.reshape(1, -1), o_gain.reshape(1, -1))
    return out.reshape(bsz * seq, GLA_V_W)


def _sb_kernel(q_ref, k_ref, v_ref, o_ref, kmax_ref):
    t = SB_T
    seq = k_ref.shape[0]
    i = pl.program_id(1)
    head = _head_of_lane()
    head_ones = _ones_where(jnp.right_shift(_iota((QK_W, QK_W), 0), int(np.log2(HEAD_LANES))) == head)

    @pl.when(i == 0)
    def _():
        best = jnp.zeros((SB_NORM_ROWS, QK_W), F32)
        for c in range(seq // SB_NORM_ROWS):
            kf = k_ref[c * SB_NORM_ROWS:(c + 1) * SB_NORM_ROWS, :].astype(F32)
            sq_hi, sq_lo = _split_bf16(kf * kf)
            best = jnp.maximum(best, _dot(sq_hi, head_ones) + _dot(sq_lo, head_ones))
        kmax_ref[...] = jnp.zeros_like(kmax_ref) + jnp.max(best)

    q = q_ref[...]
    q_stack = jnp.concatenate([jnp.where(head == h, q, jnp.zeros_like(q)) for h in range(N_HEADS)], axis=0)
    qf = q_stack.astype(F32)
    z_bound = jnp.sqrt(jnp.sum(qf * qf, axis=1, keepdims=True) * kmax_ref[0:1, 0:1]) * SB_BOUND_SLACK
    tri2 = _ones_where(jnp.bitwise_and(_iota((2 * t, t), 0), t - 1) >= _iota((2 * t, t), 1))
    keep = _iota((N_HEADS * t, t), 1) < jnp.bitwise_and(_iota((N_HEADS * t, t), 0), t - 1)

    def block(j, acc, run, masked):
        start = pl.multiple_of(j * t, t)
        k_blk = k_ref[pl.ds(start, t), :]
        v_blk = v_ref[pl.ds(start, t), :]
        z = _dot_nt(q_stack, k_blk)
        drop = jnp.maximum(z, 0.0) + jnp.log(1.0 + jnp.exp2(jnp.abs(z) * (-LOG2_E)))
        if masked:
            drop = jnp.where(keep, drop, 0.0)
        d_hi, d_lo = _split_bf16(drop)
        suffix = _dot(jnp.concatenate([d_hi, d_lo], axis=1), tri2)
        w = jnp.exp(z - suffix - run)
        if masked:
            w = jnp.where(keep, w, 0.0)
        w = w.astype(BF16)
        w_wide = jnp.concatenate([w[h * t:(h + 1) * t, :] for h in range(N_HEADS)], axis=1)
        v_stack = jnp.concatenate([jnp.where(head == h, v_blk, jnp.zeros_like(v_blk)) for h in range(N_HEADS)], axis=0)
        acc = acc + _dot(w_wide, v_stack)
        run = run + jnp.sum(drop, axis=1, keepdims=True)
        done = (jnp.max(z_bound - run) < SB_ZERO_EXPONENT).astype(jnp.int32)
        return acc, run, done

    acc, run, done = block(i, jnp.zeros((t, QK_W), F32), jnp.zeros((N_HEADS * t, 1), F32), True)

    def body(c):
        j, _, acc, run = c
        acc, run, done = block(j, acc, run, False)
        return j - 1, done, acc, run

    _, _, acc, _ = lax.while_loop(lambda c: jnp.logical_and(c[0] >= 0, c[1] == 0), body, (i - 1, done, acc, run))
    o_ref[...] = acc.astype(BF16)


def _stick_breaking_mixer(dbuf, bsz, seq):
    d3 = dbuf.reshape(bsz, seq, D_W)
    full = lambda j: pl.BlockSpec((None, seq, QK_W), lambda b, i: (b, 0, j))
    out = pl.pallas_call(
        _sb_kernel,
        grid=(bsz, seq // SB_T),
        in_specs=[pl.BlockSpec((None, SB_T, QK_W), lambda b, i: (b, i, 0)), full(1), full(2)],
        out_specs=pl.BlockSpec((None, SB_T, QK_W), lambda b, i: (b, i, 0)),
        out_shape=jax.ShapeDtypeStruct((bsz, seq, QK_W), BF16),
        scratch_shapes=[pltpu.VMEM((8, 128), F32)],
        compiler_params=_params("parallel", "arbitrary"),
        name="stick_breaking_mixer",
    )(d3, d3, d3)
    return out.reshape(bsz * seq, QK_W)


def _merge_kernel(x_ref, mod_ref, g_ref, oa_ref, ob_ref, oc_ref, od_ref,
                  wg_ref, wa_ref, wb_ref, wc_ref, wd_ref, wo_ref, out_ref):
    x = x_ref[...]
    h = _modulated_norm(x, g_ref[...], mod_ref[1:2, :], mod_ref[0:1, :]).astype(BF16)
    branches = ((oa_ref[...], wa_ref), (ob_ref[...], wb_ref), (oc_ref[...], wc_ref), (od_ref[...], wd_ref))
    merged = jnp.zeros(x.shape, F32)
    for j, (o_j, w_ref) in enumerate(branches):
        gate = _sigmoid(_dot(h, wg_ref[:, j * D_MODEL:(j + 1) * D_MODEL]))
        merged = merged + gate * _dot(o_j, w_ref[...])
    out_ref[...] = x + mod_ref[2:3, :] * _dot(merged.astype(BF16), wo_ref[...])


def _merge(x2d, mod, g_mix, oa, ob, oc, od, wg, wa, wb, wc, wd, wo, seq):
    t = x2d.shape[0]
    tm = TOKEN_TILE
    per_b = seq // tm
    row = lambda w: pl.BlockSpec((tm, w), lambda i: (i, 0))
    return pl.pallas_call(
        _merge_kernel,
        grid=(t // tm,),
        in_specs=[
            row(D_MODEL),
            pl.BlockSpec((None, 6, D_MODEL), lambda i: (i // per_b, 0, 0)),
            _const_spec((1, D_MODEL)),
            row(QK_W), row(CONV_CH), row(GLA_V_W), row(QK_W),
            _const_spec(wg.shape), _const_spec(wa.shape), _const_spec(wb.shape), _const_spec(wc.shape),
            _const_spec(wd.shape), _const_spec(wo.shape),
        ],
        out_specs=row(D_MODEL),
        out_shape=jax.ShapeDtypeStruct((t, D_MODEL), F32),
        compiler_params=_params("parallel"),
        name="gated_merge",
    )(x2d, mod, g_mix, oa, ob, oc, od, wg, wa, wb, wc, wd, wo)


def _mlp_kernel(x_ref, mod_ref, g_ref, wu_ref, wd_ref, out_ref):
    x = x_ref[...]
    h = _modulated_norm(x, g_ref[...], mod_ref[4:5, :], mod_ref[3:4, :]).astype(BF16)
    acc = jnp.zeros(x.shape, F32)
    for j in range(D_FF // D_MODEL):
        cols = slice(j * D_MODEL, (j + 1) * D_MODEL)
        u = jnp.maximum(_dot(h, wu_ref[:, cols]), 0.0)
        acc = acc + _dot((u * u).astype(BF16), wd_ref[cols, :])
    out_ref[...] = x + mod_ref[5:6, :] * acc


def _mlp(x2d, mod, g_mlp, wu, wd, seq):
    t = x2d.shape[0]
    tm = TOKEN_TILE
    per_b = seq // tm
    row = pl.BlockSpec((tm, D_MODEL), lambda i: (i, 0))
    return pl.pallas_call(
        _mlp_kernel,
        grid=(t // tm,),
        in_specs=[
            row,
            pl.BlockSpec((None, 6, D_MODEL), lambda i: (i // per_b, 0, 0)),
            _const_spec((1, D_MODEL)), _const_spec(wu.shape), _const_spec(wd.shape),
        ],
        out_specs=row,
        out_shape=jax.ShapeDtypeStruct((t, D_MODEL), F32),
        compiler_params=_params("parallel"),
        name="relu2_mlp",
    )(x2d, mod, g_mlp, wu, wd)


def _head_block_diag():
    idx = np.arange(QK_W) // HEAD_LANES
    return jnp.asarray((idx[:, None] == idx[None, :]).astype(np.float32) / HEAD_LANES, dtype=BF16)


def kernel(x, c, w_ada, b_ada, g_mix, w_in, q_gain, k_gain, conv_w, conv_b, conv_ln_g, conv_ln_b,
           gla_w_gate, gla_b_gate, gla_o_gain, w_br_a, w_br_b, w_br_c, w_br_d, w_out, g_mlp, w_up, w_down):
    bsz, seq, d = x.shape
    depth = w_ada.shape[0]
    assert d == D_MODEL and seq % (DSWA_BLOCK * DSWA_PATTERNS[-1][1]) == 0 and seq % TOKEN_TILE == 0
    mods = _modulation(c, w_ada, b_ada)
    bd = _head_block_diag()
    x2d = x.reshape(bsz * seq, d)
    qk_scale = HEAD_LANES ** -0.5
    for l in range(depth):
        w = w_in[l]
        o_conv = A_W
        o_c = o_conv + 2 * CONV_CH
        o_low = o_c + 2 * QK_W + 2 * GLA_V_W
        o_d = o_low + GLA_RANK
        o_gate = o_d + D_W
        wa = w[:, :A_W].astype(BF16)
        wb = w[:, o_conv:o_c].astype(BF16)
        wc = jnp.concatenate([w[:, o_c:o_low + GLA_RANK], jnp.zeros((d, GLA_LOW_PAD - GLA_RANK), F32)], axis=1).astype(BF16)
        wd = w[:, o_d:o_gate].astype(BF16)
        wg = w[:, o_gate:].astype(BF16)
        qg = jnp.tile(q_gain[l], N_HEADS).reshape(1, QK_W) * qk_scale
        kg = jnp.tile(k_gain[l], N_HEADS).reshape(1, QK_W)
        mod = mods[l]
        gm = g_mix[l].reshape(1, d)

        a, y, cbuf, dbuf = _in_projection(x2d, mod, gm, wa, wb, wc, wd, qg, kg, bd, seq)
        oa = _dilated_attention(a, bsz, seq)
        ob = _conv_mixer(y, conv_w[l], conv_b[l], conv_ln_g[l], conv_ln_b[l], bsz, seq)
        oc = _gla_mixer(cbuf, gla_w_gate[l], gla_b_gate[l], gla_o_gain[l], bsz, seq)
        od = _stick_breaking_mixer(dbuf, bsz, seq)
        x2d = _merge(x2d, mod, gm, oa, ob, oc, od, wg, w_br_a[l].astype(BF16), w_br_b[l].astype(BF16),
                     w_br_c[l].astype(BF16), w_br_d[l].astype(BF16), w_out[l].astype(BF16), seq)
        x2d = _mlp(x2d, mod, g_mlp[l].reshape(1, d), w_up[l].astype(BF16), w_down[l].astype(BF16), seq)
    return x2d.reshape(bsz, seq, d)
```

```python
import functools

import numpy as np
import jax
import jax.numpy as jnp
from jax import lax
from jax.experimental import pallas as pl
from jax.experimental.pallas import tpu as pltpu

F32 = jnp.float32
BF16 = jnp.bfloat16

D_MODEL = 1024
D_FF = 4 * D_MODEL
NORM_EPS = 1e-6
LOG2_E = 1.4426950408889634
HEAD_LANES = 64
N_HEADS = 4
QK_W = N_HEADS * HEAD_LANES
DSWA_PATTERNS = ((128, 1), (512, 4), (2048, 16))
DSWA_BLOCK = 128
DSWA_SPAN = DSWA_BLOCK * DSWA_PATTERNS[-1][1]
LANE_TILE = 128
SUBLANES = 8
CONV_CH = 256
CONV_WIDTH = 31
CONV_HALO = 32
GLA_DV = 128
GLA_V_W = N_HEADS * GLA_DV
GLA_RANK = 16
GLA_LOW_PAD = 128
GLA_TAU = 16.0
GLA_CHUNK = 64
GLA_SUB = 16
GLA_CHUNKS_PER_STEP = 8
SB_T = 256
SB_NORM_ROWS = 512
SB_BOUND_SLACK = 1.02
SB_ZERO_EXPONENT = -110.0
TOKEN_TILE = 512
MASK_VALUE = -1e30
VMEM_LIMIT = 56 * 1024 * 1024

A_W = 3 * 3 * QK_W
C_W = 2 * QK_W + 2 * GLA_V_W + GLA_LOW_PAD
D_W = 3 * QK_W


def _dot(a, b):
    return jnp.dot(a, b, preferred_element_type=F32)


def _dot_nt(a, b):
    return lax.dot_general(a, b, (((1,), (1,)), ((), ())), preferred_element_type=F32)


def _dot_tn(a, b):
    return lax.dot_general(a, b, (((0,), (0,)), ((), ())), preferred_element_type=F32)


def _split_bf16(x):
    hi = x.astype(BF16)
    lo = (x - hi.astype(F32)).astype(BF16)
    return hi, lo


def _sigmoid(x):
    return 1.0 / (1.0 + jnp.exp(-x))


def _neg_softplus(x):
    return -(jnp.maximum(x, 0.0) + jnp.log(1.0 + jnp.exp(-jnp.abs(x))))


def _iota(shape, dim):
    return lax.broadcasted_iota(jnp.int32, shape, dim)


def _head_of_lane(width=QK_W, lanes_per_head=HEAD_LANES):
    return jnp.right_shift(_iota((1, width), 1), int(np.log2(lanes_per_head)))


def _ones_where(cond):
    return jnp.where(cond, 1.0, 0.0).astype(BF16)


def _const_spec(shape):
    nd = len(shape)
    return pl.BlockSpec(shape, lambda *_: (0,) * nd, pipeline_mode=pl.Buffered(1))


def _params(*sem):
    return pltpu.CompilerParams(dimension_semantics=sem, vmem_limit_bytes=VMEM_LIMIT)


def _modulated_norm(x, gain, scale, shift):
    ms = jnp.mean(x * x, axis=-1, keepdims=True)
    y = x * lax.rsqrt(ms + NORM_EPS) * gain
    return y * (1.0 + scale) + shift


def _mod_kernel(c_ref, w_ref, b_ref, o_ref):
    c_hi, c_lo = _split_bf16(c_ref[...])
    w_hi, w_lo = _split_bf16(w_ref[...])
    o_ref[...] = _dot(c_hi, w_hi) + _dot(c_hi, w_lo) + _dot(c_lo, w_hi) + b_ref[...]


def _modulation(c, w_ada, b_ada):
    depth, d, six_d = w_ada.shape
    bsz = c.shape[0]
    out = pl.pallas_call(
        _mod_kernel,
        grid=(depth, six_d // d),
        in_specs=[
            pl.BlockSpec((bsz, d), lambda l, j: (0, 0)),
            pl.BlockSpec((None, d, d), lambda l, j: (l, 0, j)),
            pl.BlockSpec((None, 1, d), lambda l, j: (l, 0, j)),
        ],
        out_specs=pl.BlockSpec((None, bsz, d), lambda l, j: (l, 0, j)),
        out_shape=jax.ShapeDtypeStruct((depth, bsz, six_d), F32),
        compiler_params=_params("parallel", "parallel"),
        name="adaln_modulation",
    )(c, w_ada, b_ada.reshape(depth, 1, six_d))
    return out.reshape(depth, bsz, six_d // d, d)


def _inproj_kernel(x_ref, mod_ref, g_ref, wa_ref, wb_ref, wc_ref, wd_ref, qg_ref, kg_ref, bd_ref,
                   a_ref, y_ref, c_ref, d_ref):
    h = _modulated_norm(x_ref[...], g_ref[...], mod_ref[1:2, :], mod_ref[0:1, :]).astype(BF16)
    bd = bd_ref[...]
    a_all = _dot(h, wa_ref[...])
    for j in range(A_W // QK_W):
        cols = slice(j * QK_W, (j + 1) * QK_W)
        acc = a_all[:, cols]
        if j < 6:
            sq_hi, sq_lo = _split_bf16(acc * acc)
            ms = _dot(sq_hi, bd) + _dot(sq_lo, bd)
            gain = qg_ref[...] if j < 3 else kg_ref[...]
            acc = acc * lax.rsqrt(ms + NORM_EPS) * gain
        a_ref[:, cols] = acc.astype(BF16)
    u = _dot(h, wb_ref[...])
    y_ref[...] = u[:, :CONV_CH] * _sigmoid(u[:, CONV_CH:])
    c_ref[...] = _dot(h, wc_ref[...])
    d_all = _dot(h, wd_ref[...])
    d_ref[:, :QK_W] = (d_all[:, :QK_W] * (HEAD_LANES ** -0.5)).astype(BF16)
    d_ref[:, QK_W:] = d_all[:, QK_W:].astype(BF16)


def _in_projection(x2d, mod, g_mix, wa, wb, wc, wd, qg, kg, bd, seq):
    t = x2d.shape[0]
    tm = TOKEN_TILE
    per_b = seq // tm
    row = lambda w: pl.BlockSpec((tm, w), lambda i: (i, 0))
    return pl.pallas_call(
        _inproj_kernel,
        grid=(t // tm,),
        in_specs=[
            row(D_MODEL),
            pl.BlockSpec((None, 6, D_MODEL), lambda i: (i // per_b, 0, 0)),
            _const_spec((1, D_MODEL)),
            _const_spec(wa.shape), _const_spec(wb.shape), _const_spec(wc.shape), _const_spec(wd.shape),
            _const_spec((1, QK_W)), _const_spec((1, QK_W)), _const_spec((QK_W, QK_W)),
        ],
        out_specs=[row(A_W), row(CONV_CH), row(C_W), row(D_W)],
        out_shape=[
            jax.ShapeDtypeStruct((t, A_W), BF16),
            jax.ShapeDtypeStruct((t, CONV_CH), F32),
            jax.ShapeDtypeStruct((t, C_W), F32),
            jax.ShapeDtypeStruct((t, D_W), BF16),
        ],
        compiler_params=_params("parallel"),
        name="in_projection",
    )(x2d, mod, g_mix, wa, wb, wc, wd, qg, kg, bd)


def _dswa_group(position, span_idx, q_ref, kp_ref, kc_ref, vp_ref, vc_ref, o_ref, q_st, k_st, v_st, m_st, den_st, acc_st,
                *, slopes, window, dilation):
    blk, span = DSWA_BLOCK, DSWA_SPAN
    halo = blk * dilation
    shift = int(np.log2(dilation))
    for s in range(QK_W // LANE_TILE):
        lanes = slice(s * LANE_TILE, (s + 1) * LANE_TILE)
        q_st[s, :, :] = q_ref[:, lanes].astype(F32)
        k_st[s, 0:halo, :] = kp_ref[span - halo:span, lanes].astype(F32)
        k_st[s, halo:halo + span, :] = kc_ref[:, lanes].astype(F32)
        v_st[s, 0:halo, :] = vp_ref[span - halo:span, lanes].astype(F32)
        v_st[s, halo:halo + span, :] = vc_ref[:, lanes].astype(F32)
    head = _head_of_lane()
    qi = _iota((blk, 2 * blk), 0)
    kj = _iota((blk, 2 * blk), 1)
    rel = qi - kj + blk
    in_window = (rel >= 0) & (rel <= window // dilation)
    dist = (dilation * rel).astype(F32)
    bias = jnp.concatenate([jnp.where(in_window, -slopes[h] * dist, MASK_VALUE) for h in range(N_HEADS)], axis=0)
    first_block_penalty = jnp.where(_iota((1, 2 * blk), 1) < blk, MASK_VALUE, 0.0)

    def rows_of(ref, start, n):
        return jnp.concatenate([ref[s, pl.ds(start, n, stride=dilation), :] for s in range(QK_W // LANE_TILE)], axis=1)

    def put_rows(ref, start, val):
        for s in range(QK_W // LANE_TILE):
            ref[s, pl.ds(start, blk, stride=dilation), :] = val[:, s * LANE_TILE:(s + 1) * LANE_TILE]

    def step(it, carry):
        n = jnp.right_shift(it, shift)
        start = n * halo + jnp.bitwise_and(it, dilation - 1)
        q = rows_of(q_st, start, blk).astype(BF16)
        k = rows_of(k_st, start, 2 * blk).astype(BF16)
        v = rows_of(v_st, start, 2 * blk).astype(BF16)
        no_prev = jnp.where((span_idx == 0) & (n == 0), first_block_penalty, 0.0)
        q_stack = jnp.concatenate([jnp.where(head == h, q, jnp.zeros_like(q)) for h in range(N_HEADS)], axis=0)
        s = _dot_nt(q_stack, k) + bias + no_prev
        mx = jnp.max(s, axis=1, keepdims=True)
        p = jnp.exp(s - mx)
        den = jnp.sum(p, axis=1, keepdims=True)
        o_all = _dot(p.astype(BF16), v) * (1.0 / den)
        lse_all = mx + jnp.log(den)
        o_new = jnp.zeros((blk, QK_W), F32)
        lse = jnp.zeros((blk, QK_W), F32)
        for h in range(N_HEADS):
            in_head = head == h
            o_new = jnp.where(in_head, o_all[h * blk:(h + 1) * blk], o_new)
            lse = jnp.where(in_head, lse_all[h * blk:(h + 1) * blk], lse)
        if position == 0:
            m_new, den_new, acc_new = lse, jnp.ones_like(lse), o_new
        else:
            m_old = rows_of(m_st, start, blk)
            m_new = jnp.maximum(m_old, lse)
            w_old, w_new = jnp.exp(m_old - m_new), jnp.exp(lse - m_new)
            den_new = rows_of(den_st, start, blk) * w_old + w_new
            acc_new = rows_of(acc_st, start, blk) * w_old + o_new * w_new
        put_rows(m_st, start, m_new)
        put_rows(den_st, start, den_new)
        put_rows(acc_st, start, acc_new)
        return carry

    lax.fori_loop(0, span // blk, step, 0, unroll=2)
    if position == len(DSWA_PATTERNS) - 1:
        for s in range(QK_W // LANE_TILE):
            o_ref[:, s * LANE_TILE:(s + 1) * LANE_TILE] = (acc_st[s, :, :] / den_st[s, :, :]).astype(BF16)


def _dswa_kernel(q_ref, kp_ref, kc_ref, vp_ref, vc_ref, o_ref, *scratch, slopes):
    span_idx, position = pl.program_id(1), pl.program_id(2)
    for pos in range(len(DSWA_PATTERNS)):
        g = _dswa_group_at(pos)
        window, dilation = DSWA_PATTERNS[g]
        pl.when(position == pos)(functools.partial(
            _dswa_group, pos, span_idx, q_ref, kp_ref, kc_ref, vp_ref, vc_ref, o_ref, *scratch,
            slopes=slopes[g * N_HEADS:(g + 1) * N_HEADS], window=window, dilation=dilation))


def _dswa_group_at(position):
    return len(DSWA_PATTERNS) - 1 - position


def _alibi_slopes(n):
    return [float(np.float32(2.0 ** (-8.0 * i / n))) for i in range(1, n + 1)]


def _dilated_attention(a, bsz, seq):
    span = DSWA_SPAN
    a3 = a.reshape(bsz, seq, A_W)
    n_groups = len(DSWA_PATTERNS)
    col = lambda part, pos: part * n_groups + _dswa_group_at(pos)
    cur = lambda part: pl.BlockSpec((None, span, QK_W), lambda b, i, p: (b, i, col(part, p)))
    prev = lambda part: pl.BlockSpec((None, span, QK_W), lambda b, i, p: (b, jnp.maximum(i - 1, 0), col(part, p)))
    slab = lambda rows: pltpu.VMEM((QK_W // LANE_TILE, rows, LANE_TILE), F32)
    out = pl.pallas_call(
        functools.partial(_dswa_kernel, slopes=_alibi_slopes(n_groups * N_HEADS)),
        grid=(bsz, seq // span, n_groups),
        in_specs=[cur(0), prev(1), cur(1), prev(2), cur(2)],
        out_specs=pl.BlockSpec((None, span, QK_W), lambda b, i, g: (b, i, 0)),
        out_shape=jax.ShapeDtypeStruct((bsz, seq, QK_W), BF16),
        scratch_shapes=[slab(span), slab(2 * span), slab(2 * span), slab(span), slab(span), slab(span)],
        compiler_params=_params("parallel", "arbitrary", "arbitrary"),
        name="dilated_attention",
    )(a3, a3, a3, a3, a3)
    return out.reshape(bsz * seq, QK_W)


def _conv_kernel(prev_ref, cur_ref, w_ref, b_ref, lg_ref, lb_ref, o_ref, buf_ref):
    ts = cur_ref.shape[0]
    first = pl.program_id(1) == 0
    buf_ref[0, 0:CONV_HALO, :] = jnp.where(first, 0.0, prev_ref[...])
    buf_ref[0, CONV_HALO:, :] = cur_ref[...]
    rows = ts + CONV_HALO - SUBLANES
    for r in range(1, SUBLANES):
        buf_ref[r, 0:rows, :] = buf_ref[0, r:r + rows, :]
    off = CONV_HALO - (CONV_WIDTH - 1)
    acc = jnp.zeros((ts, CONV_CH), F32) + b_ref[...]
    for w in range(CONV_WIDTH):
        r, base = (off + w) % SUBLANES, (off + w) // SUBLANES * SUBLANES
        acc = acc + buf_ref[r, base:base + ts, :] * w_ref[w:w + 1, :]
    mu = jnp.mean(acc, axis=-1, keepdims=True)
    ctr = acc - mu
    var = jnp.mean(ctr * ctr, axis=-1, keepdims=True)
    y = ctr * lax.rsqrt(var + NORM_EPS) * lg_ref[...] + lb_ref[...]
    o_ref[...] = (y * _sigmoid(y)).astype(BF16)


def _conv_mixer(y, conv_w, conv_b, ln_g, ln_b, bsz, seq):
    ts = TOKEN_TILE
    per_halo = ts // CONV_HALO
    y3 = y.reshape(bsz, seq, CONV_CH)
    w_pad = jnp.zeros((CONV_HALO, CONV_CH), F32).at[:CONV_WIDTH].set(conv_w)
    out = pl.pallas_call(
        _conv_kernel,
        grid=(bsz, seq // ts),
        in_specs=[
            pl.BlockSpec((None, CONV_HALO, CONV_CH), lambda b, i: (b, jnp.maximum(i * per_halo - 1, 0), 0)),
            pl.BlockSpec((None, ts, CONV_CH), lambda b, i: (b, i, 0)),
            _const_spec((CONV_HALO, CONV_CH)), _const_spec((1, CONV_CH)), _const_spec((1, CONV_CH)),
            _const_spec((1, CONV_CH)),
        ],
        out_specs=pl.BlockSpec((None, ts, CONV_CH), lambda b, i: (b, i, 0)),
        out_shape=jax.ShapeDtypeStruct((bsz, seq, CONV_CH), BF16),
        scratch_shapes=[pltpu.VMEM((SUBLANES, ts + CONV_HALO, CONV_CH), F32)],
        compiler_params=_params("parallel", "parallel"),
        name="conv_mixer",
    )(y3, y3, w_pad, conv_b.reshape(1, -1), ln_g.reshape(1, -1), ln_b.reshape(1, -1))
    return out.reshape(bsz * seq, CONV_CH)


def _gla_kernel(q_ref, k_ref, v_ref, r_ref, low_ref, wg_ref, bg_ref, og_ref, tri_ref, am_ref, sm_ref, o_ref, st_ref):
    ch, sub = GLA_CHUNK, GLA_SUB
    tc = q_ref.shape[0]
    n_sub = ch // sub

    @pl.when(pl.program_id(1) == 0)
    def _():
        st_ref[...] = jnp.zeros_like(st_ref)

    head = _head_of_lane()
    key_row = _iota((ch, QK_W), 0)
    att_mask = am_ref[...]
    st_mask = sm_ref[...]

    logit = _dot(low_ref[...].astype(BF16), wg_ref[...]) + bg_ref[...]
    log_a = _neg_softplus(-logit) * (1.0 / GLA_TAU)
    la_hi, la_lo = _split_bf16(log_a)
    cum_all = _dot(tri_ref[...], jnp.concatenate([la_hi, la_lo], axis=0))

    state = st_ref[...]
    for c in range(tc // ch):
        rows = slice(c * ch, (c + 1) * ch)
        cum = cum_all[rows, :]
        total = cum[ch - 1:ch, :]
        q = q_ref[rows, :] * (HEAD_LANES ** -0.5)
        k = k_ref[rows, :]
        v = v_ref[rows, :].astype(BF16)

        bases = [jnp.zeros((1, QK_W), F32)] + [cum[i * sub - 1:i * sub, :] for i in range(1, n_sub)]
        base_rows = jnp.concatenate([jnp.broadcast_to(b, (sub, QK_W)) for b in bases], axis=0)
        q_rel = q * jnp.exp(cum - base_rows)
        q_all = jnp.concatenate([jnp.where(head == h, q_rel, 0.0) for h in range(N_HEADS)], axis=0).astype(BF16)
        k_all = jnp.concatenate([jnp.where(key_row < (i + 1) * sub, k * jnp.exp(bases[i] - cum), 0.0)
                                 for i in range(n_sub)], axis=0).astype(BF16)
        att = (_dot_nt(q_all, k_all) * att_mask).astype(BF16)
        intra = _dot(att, jnp.concatenate([v] * n_sub, axis=0))
        o = jnp.concatenate([intra[h * ch:(h + 1) * ch, h * GLA_DV:(h + 1) * GLA_DV] for h in range(N_HEADS)], axis=1)

        o = o + _dot_nt((q * jnp.exp(cum)).astype(BF16), state.astype(BF16))
        k_dec = (k * jnp.exp(total - cum)).astype(BF16)
        state = state * jnp.exp(total) + _dot_tn(v, k_dec) * st_mask

        r = r_ref[rows, :]
        outs = []
        for h in range(N_HEADS):
            o_h = o[:, h * GLA_DV:(h + 1) * GLA_DV]
            ms = jnp.mean(o_h * o_h, axis=-1, keepdims=True)
            outs.append(o_h * lax.rsqrt(ms + NORM_EPS) * og_ref[...])
        o_ref[rows, :] = (r * _sigmoid(r) * jnp.concatenate(outs, axis=1)).astype(BF16)
    st_ref[...] = state


def _gla_masks(tc):
    ch, sub = GLA_CHUNK, GLA_SUB
    t = np.arange(tc)
    tri = (t[:, None] >= t[None, :]) & (t[:, None] // ch == t[None, :] // ch)
    a_t = np.arange(N_HEADS * ch) % ch
    a_col = np.arange((ch // sub) * ch)
    att = (a_col[None, :] // ch == a_t[:, None] // sub) & (a_col[None, :] % ch <= a_t[:, None])
    state = np.arange(GLA_V_W)[:, None] // GLA_DV == np.arange(QK_W)[None, :] // HEAD_LANES
    return (jnp.asarray(np.concatenate([tri, tri], axis=1), BF16), jnp.asarray(att, F32), jnp.asarray(state, F32))


def _gla_mixer(cbuf, w_gate, b_gate, o_gain, bsz, seq):
    tc = GLA_CHUNK * GLA_CHUNKS_PER_STEP
    c3 = cbuf.reshape(bsz, seq, C_W)
    wg = jnp.zeros((GLA_LOW_PAD, QK_W), F32).at[:GLA_RANK].set(w_gate).astype(BF16)
    tri2, att_mask, st_mask = _gla_masks(tc)
    spec = lambda w, j: pl.BlockSpec((None, tc, w), lambda b, i: (b, i, j))
    out = pl.pallas_call(
        _gla_kernel,
        grid=(bsz, seq // tc),
        in_specs=[
            spec(QK_W, 0), spec(QK_W, 1), spec(GLA_V_W, 1), spec(GLA_V_W, 2),
            spec(GLA_LOW_PAD, (2 * QK_W + 2 * GLA_V_W) // GLA_LOW_PAD),
            _const_spec((GLA_LOW_PAD, QK_W)), _const_spec((1, QK_W)), _const_spec((1, GLA_DV)),
            _const_spec(tri2.shape), _const_spec(att_mask.shape), _const_spec(st_mask.shape),
        ],
        out_specs=pl.BlockSpec((None, tc, GLA_V_W), lambda b, i: (b, i, 0)),
        out_shape=jax.ShapeDtypeStruct((bsz, seq, GLA_V_W), BF16),
        scratch_shapes=[pltpu.VMEM((GLA_V_W, QK_W), F32)],
        compiler_params=_params("parallel", "arbitrary"),
        name="gla_mixer",
    )(c3, c3, c3, c3, c3, wg, b_gate.reshape(1, -1), o_gain.reshape(1, -1), tri2, att_mask, st_mask)
    return out.reshape(bsz * seq, GLA_V_W)


def _sb_kernel(q_ref, k_ref, v_ref, o_ref, kmax_ref):
    t = SB_T
    seq = k_ref.shape[0]
    i = pl.program_id(1)
    head = _head_of_lane()
    head_ones = _ones_where(jnp.right_shift(_iota((QK_W, QK_W), 0), int(np.log2(HEAD_LANES))) == head)

    @pl.when(i == 0)
    def _():
        best = jnp.zeros((SB_NORM_ROWS, QK_W), F32)
        for c in range(seq // SB_NORM_ROWS):
            kf = k_ref[c * SB_NORM_ROWS:(c + 1) * SB_NORM_ROWS, :].astype(F32)
            sq_hi, sq_lo = _split_bf16(kf * kf)
            best = jnp.maximum(best, _dot(sq_hi, head_ones) + _dot(sq_lo, head_ones))
        kmax_ref[...] = jnp.zeros_like(kmax_ref) + jnp.max(best)

    q = q_ref[...]
    q_stack = jnp.concatenate([jnp.where(head == h, q, jnp.zeros_like(q)) for h in range(N_HEADS)], axis=0)
    qf = q_stack.astype(F32)
    z_bound = jnp.sqrt(jnp.sum(qf * qf, axis=1, keepdims=True) * kmax_ref[0:1, 0:1]) * SB_BOUND_SLACK
    tri2 = _ones_where(jnp.bitwise_and(_iota((2 * t, t), 0), t - 1) >= _iota((2 * t, t), 1))
    keep = _iota((N_HEADS * t, t), 1) < jnp.bitwise_and(_iota((N_HEADS * t, t), 0), t - 1)

    def block(j, acc, run, masked):
        start = pl.multiple_of(j * t, t)
        k_blk = k_ref[pl.ds(start, t), :]
        v_blk = v_ref[pl.ds(start, t), :]
        z = _dot_nt(q_stack, k_blk)
        drop = jnp.maximum(z, 0.0) + jnp.log(1.0 + jnp.exp2(jnp.abs(z) * (-LOG2_E)))
        if masked:
            drop = jnp.where(keep, drop, 0.0)
        d_hi, d_lo = _split_bf16(drop)
        suffix = _dot(jnp.concatenate([d_hi, d_lo], axis=1), tri2)
        w = jnp.exp(z - suffix - run)
        if masked:
            w = jnp.where(keep, w, 0.0)
        w = w.astype(BF16)
        w_wide = jnp.concatenate([w[h * t:(h + 1) * t, :] for h in range(N_HEADS)], axis=1)
        v_stack = jnp.concatenate([jnp.where(head == h, v_blk, jnp.zeros_like(v_blk)) for h in range(N_HEADS)], axis=0)
        acc = acc + _dot(w_wide, v_stack)
        run = run + jnp.sum(drop, axis=1, keepdims=True)
        done = (jnp.max(z_bound - run) < SB_ZERO_EXPONENT).astype(jnp.int32)
        return acc, run, done

    acc, run, done = block(i, jnp.zeros((t, QK_W), F32), jnp.zeros((N_HEADS * t, 1), F32), True)

    def body(c):
        j, _, acc, run = c
        acc, run, done = block(j, acc, run, False)
        return j - 1, done, acc, run

    _, _, acc, _ = lax.while_loop(lambda c: jnp.logical_and(c[0] >= 0, c[1] == 0), body, (i - 1, done, acc, run))
    o_ref[...] = acc.astype(BF16)


def _stick_breaking_mixer(dbuf, bsz, seq):
    d3 = dbuf.reshape(bsz, seq, D_W)
    full = lambda j: pl.BlockSpec((None, seq, QK_W), lambda b, i: (b, 0, j))
    out = pl.pallas_call(
        _sb_kernel,
        grid=(bsz, seq // SB_T),
        in_specs=[pl.BlockSpec((None, SB_T, QK_W), lambda b, i: (b, i, 0)), full(1), full(2)],
        out_specs=pl.BlockSpec((None, SB_T, QK_W), lambda b, i: (b, i, 0)),
        out_shape=jax.ShapeDtypeStruct((bsz, seq, QK_W), BF16),
        scratch_shapes=[pltpu.VMEM((8, 128), F32)],
        compiler_params=_params("parallel", "arbitrary"),
        name="stick_breaking_mixer",
    )(d3, d3, d3)
    return out.reshape(bsz * seq, QK_W)


def _merge_kernel(x_ref, mod_ref, g_ref, oa_ref, ob_ref, oc_ref, od_ref,
                  wg_ref, wa_ref, wb_ref, wc_ref, wd_ref, wo_ref, out_ref):
    x = x_ref[...]
    h = _modulated_norm(x, g_ref[...], mod_ref[1:2, :], mod_ref[0:1, :]).astype(BF16)
    branches = ((oa_ref[...], wa_ref), (ob_ref[...], wb_ref), (oc_ref[...], wc_ref), (od_ref[...], wd_ref))
    merged = jnp.zeros(x.shape, F32)
    for j, (o_j, w_ref) in enumerate(branches):
        gate = _sigmoid(_dot(h, wg_ref[:, j * D_MODEL:(j + 1) * D_MODEL]))
        merged = merged + gate * _dot(o_j, w_ref[...])
    out_ref[...] = x + mod_ref[2:3, :] * _dot(merged.astype(BF16), wo_ref[...])


def _merge(x2d, mod, g_mix, oa, ob, oc, od, wg, wa, wb, wc, wd, wo, seq):
    t = x2d.shape[0]
    tm = TOKEN_TILE
    per_b = seq // tm
    row = lambda w: pl.BlockSpec((tm, w), lambda i: (i, 0))
    return pl.pallas_call(
        _merge_kernel,
        grid=(t // tm,),
        in_specs=[
            row(D_MODEL),
            pl.BlockSpec((None, 6, D_MODEL), lambda i: (i // per_b, 0, 0)),
            _const_spec((1, D_MODEL)),
            row(QK_W), row(CONV_CH), row(GLA_V_W), row(QK_W),
            _const_spec(wg.shape), _const_spec(wa.shape), _const_spec(wb.shape), _const_spec(wc.shape),
            _const_spec(wd.shape), _const_spec(wo.shape),
        ],
        out_specs=row(D_MODEL),
        out_shape=jax.ShapeDtypeStruct((t, D_MODEL), F32),
        compiler_params=_params("parallel"),
        name="gated_merge",
    )(x2d, mod, g_mix, oa, ob, oc, od, wg, wa, wb, wc, wd, wo)


def _mlp_kernel(x_ref, mod_ref, g_ref, wu_ref, wd_ref, out_ref):
    x = x_ref[...]
    h = _modulated_norm(x, g_ref[...], mod_ref[4:5, :], mod_ref[3:4, :]).astype(BF16)
    acc = jnp.zeros(x.shape, F32)
    for j in range(D_FF // D_MODEL):
        cols = slice(j * D_MODEL, (j + 1) * D_MODEL)
        u = jnp.maximum(_dot(h, wu_ref[:, cols]), 0.0)
        acc = acc + _dot((u * u).astype(BF16), wd_ref[cols, :])
    out_ref[...] = x + mod_ref[5:6, :] * acc


def _mlp(x2d, mod, g_mlp, wu, wd, seq):
    t = x2d.shape[0]
    tm = TOKEN_TILE
    per_b = seq // tm
    row = pl.BlockSpec((tm, D_MODEL), lambda i: (i, 0))
    return pl.pallas_call(
        _mlp_kernel,
        grid=(t // tm,),
        in_specs=[
            row,
            pl.BlockSpec((None, 6, D_MODEL), lambda i: (i // per_b, 0, 0)),
            _const_spec((1, D_MODEL)), _const_spec(wu.shape), _const_spec(wd.shape),
        ],
        out_specs=row,
        out_shape=jax.ShapeDtypeStruct((t, D_MODEL), F32),
        compiler_params=_params("parallel"),
        name="relu2_mlp",
    )(x2d, mod, g_mlp, wu, wd)


def _head_block_diag():
    idx = np.arange(QK_W) // HEAD_LANES
    return jnp.asarray((idx[:, None] == idx[None, :]).astype(np.float32) / HEAD_LANES, dtype=BF16)


def kernel(x, c, w_ada, b_ada, g_mix, w_in, q_gain, k_gain, conv_w, conv_b, conv_ln_g, conv_ln_b,
           gla_w_gate, gla_b_gate, gla_o_gain, w_br_a, w_br_b, w_br_c, w_br_d, w_out, g_mlp, w_up, w_down):
    bsz, seq, d = x.shape
    depth = w_ada.shape[0]
    assert d == D_MODEL and seq % (DSWA_BLOCK * DSWA_PATTERNS[-1][1]) == 0 and seq % TOKEN_TILE == 0
    mods = _modulation(c, w_ada, b_ada)
    bd = _head_block_diag()
    x2d = x.reshape(bsz * seq, d)
    qk_scale = HEAD_LANES ** -0.5
    for l in range(depth):
        w = w_in[l]
        o_conv = A_W
        o_c = o_conv + 2 * CONV_CH
        o_low = o_c + 2 * QK_W + 2 * GLA_V_W
        o_d = o_low + GLA_RANK
        o_gate = o_d + D_W
        wa = w[:, :A_W].astype(BF16)
        wb = w[:, o_conv:o_c].astype(BF16)
        wc = jnp.concatenate([w[:, o_c:o_low + GLA_RANK], jnp.zeros((d, GLA_LOW_PAD - GLA_RANK), F32)], axis=1).astype(BF16)
        wd = w[:, o_d:o_gate].astype(BF16)
        wg = w[:, o_gate:].astype(BF16)
        qg = jnp.tile(q_gain[l], N_HEADS).reshape(1, QK_W) * qk_scale
        kg = jnp.tile(k_gain[l], N_HEADS).reshape(1, QK_W)
        mod = mods[l]
        gm = g_mix[l].reshape(1, d)

        a, y, cbuf, dbuf = _in_projection(x2d, mod, gm, wa, wb, wc, wd, qg, kg, bd, seq)
        oa = _dilated_attention(a, bsz, seq)
        ob = _conv_mixer(y, conv_w[l], conv_b[l], conv_ln_g[l], conv_ln_b[l], bsz, seq)
        oc = _gla_mixer(cbuf, gla_w_gate[l], gla_b_gate[l], gla_o_gain[l], bsz, seq)
        od = _stick_breaking_mixer(dbuf, bsz, seq)
        x2d = _merge(x2d, mod, gm, oa, ob, oc, od, wg, w_br_a[l].astype(BF16), w_br_b[l].astype(BF16),
                     w_br_c[l].astype(BF16), w_br_d[l].astype(BF16), w_out[l].astype(BF16), seq)
        x2d = _mlp(x2d, mod, g_mlp[l].reshape(1, d), w_up[l].astype(BF16), w_down[l].astype(BF16), seq)
    return x2d.reshape(bsz, seq, d)
```

```python
import functools

import numpy as np
import jax
import jax.numpy as jnp
from jax import lax
from jax.experimental import pallas as pl
from jax.experimental.pallas import tpu as pltpu

F32 = jnp.float32
BF16 = jnp.bfloat16

D_MODEL = 1024
D_FF = 4 * D_MODEL
NORM_EPS = 1e-6
LOG2_E = 1.4426950408889634
HEAD_LANES = 64
N_HEADS = 4
QK_W = N_HEADS * HEAD_LANES
DSWA_PATTERNS = ((128, 1), (512, 4), (2048, 16))
DSWA_BLOCK = 128
DSWA_SPAN = DSWA_BLOCK * DSWA_PATTERNS[-1][1]
LANE_TILE = 128
SUBLANES = 8
CONV_CH = 256
CONV_WIDTH = 31
CONV_HALO = 32
GLA_DV = 128
GLA_V_W = N_HEADS * GLA_DV
GLA_RANK = 16
GLA_LOW_PAD = 128
GLA_TAU = 16.0
GLA_CHUNK = 64
GLA_SUB = 16
GLA_CHUNKS_PER_STEP = 16
GLA_CUM_ROWS = 256
SB_T = 256
SB_NORM_ROWS = 512
SB_BOUND_SLACK = 1.02
SB_ZERO_EXPONENT = -110.0
TOKEN_TILE = 512
MASK_VALUE = -1e30
VMEM_LIMIT = 56 * 1024 * 1024

A_W = 3 * 3 * QK_W
C_W = 2 * QK_W + 2 * GLA_V_W + GLA_LOW_PAD
D_W = 3 * QK_W


def _dot(a, b):
    return jnp.dot(a, b, preferred_element_type=F32)


def _dot_nt(a, b):
    return lax.dot_general(a, b, (((1,), (1,)), ((), ())), preferred_element_type=F32)


def _dot_tn(a, b):
    return lax.dot_general(a, b, (((0,), (0,)), ((), ())), preferred_element_type=F32)


def _split_bf16(x):
    hi = x.astype(BF16)
    lo = (x - hi.astype(F32)).astype(BF16)
    return hi, lo


def _sigmoid(x):
    return 1.0 / (1.0 + jnp.exp(-x))


def _neg_softplus(x):
    return -(jnp.maximum(x, 0.0) + jnp.log(1.0 + jnp.exp(-jnp.abs(x))))


def _iota(shape, dim):
    return lax.broadcasted_iota(jnp.int32, shape, dim)


def _head_of_lane(width=QK_W, lanes_per_head=HEAD_LANES):
    return jnp.right_shift(_iota((1, width), 1), int(np.log2(lanes_per_head)))


def _ones_where(cond):
    return jnp.where(cond, 1.0, 0.0).astype(BF16)


def _const_spec(shape):
    nd = len(shape)
    return pl.BlockSpec(shape, lambda *_: (0,) * nd, pipeline_mode=pl.Buffered(1))


def _params(*sem):
    return pltpu.CompilerParams(dimension_semantics=sem, vmem_limit_bytes=VMEM_LIMIT)


def _modulated_norm(x, gain, scale, shift):
    ms = jnp.mean(x * x, axis=-1, keepdims=True)
    y = x * lax.rsqrt(ms + NORM_EPS) * gain
    return y * (1.0 + scale) + shift


def _mod_kernel(c_ref, w_ref, b_ref, o_ref):
    c_hi, c_lo = _split_bf16(c_ref[...])
    w_hi, w_lo = _split_bf16(w_ref[...])
    o_ref[...] = _dot(c_hi, w_hi) + _dot(c_hi, w_lo) + _dot(c_lo, w_hi) + b_ref[...]


def _modulation(c, w_ada, b_ada):
    depth, d, six_d = w_ada.shape
    bsz = c.shape[0]
    out = pl.pallas_call(
        _mod_kernel,
        grid=(depth, six_d // d),
        in_specs=[
            pl.BlockSpec((bsz, d), lambda l, j: (0, 0)),
            pl.BlockSpec((None, d, d), lambda l, j: (l, 0, j)),
            pl.BlockSpec((None, 1, d), lambda l, j: (l, 0, j)),
        ],
        out_specs=pl.BlockSpec((None, bsz, d), lambda l, j: (l, 0, j)),
        out_shape=jax.ShapeDtypeStruct((depth, bsz, six_d), F32),
        compiler_params=_params("parallel", "parallel"),
        name="adaln_modulation",
    )(c, w_ada, b_ada.reshape(depth, 1, six_d))
    return out.reshape(depth, bsz, six_d // d, d)


def _inproj_kernel(x_ref, mod_ref, g_ref, wa_ref, wb_ref, wc_ref, wd_ref, qg_ref, kg_ref, bd_ref,
                   a_ref, y_ref, c_ref, d_ref):
    h = _modulated_norm(x_ref[...], g_ref[...], mod_ref[1:2, :], mod_ref[0:1, :]).astype(BF16)
    bd = bd_ref[...]
    a_all = _dot(h, wa_ref[...])
    for j in range(A_W // QK_W):
        cols = slice(j * QK_W, (j + 1) * QK_W)
        acc = a_all[:, cols]
        if j < 6:
            ms = _dot((acc * acc).astype(BF16), bd)
            gain = qg_ref[...] if j < 3 else kg_ref[...]
            acc = acc * lax.rsqrt(ms + NORM_EPS) * gain
        a_ref[:, cols] = acc.astype(BF16)
    u = _dot(h, wb_ref[...])
    y_ref[...] = u[:, :CONV_CH] * _sigmoid(u[:, CONV_CH:])
    c_ref[...] = _dot(h, wc_ref[...])
    d_all = _dot(h, wd_ref[...])
    d_ref[:, :QK_W] = (d_all[:, :QK_W] * (HEAD_LANES ** -0.5)).astype(BF16)
    d_ref[:, QK_W:] = d_all[:, QK_W:].astype(BF16)


def _in_projection(x2d, mod, g_mix, wa, wb, wc, wd, qg, kg, bd, seq):
    t = x2d.shape[0]
    tm = TOKEN_TILE
    per_b = seq // tm
    row = lambda w: pl.BlockSpec((tm, w), lambda i: (i, 0))
    return pl.pallas_call(
        _inproj_kernel,
        grid=(t // tm,),
        in_specs=[
            row(D_MODEL),
            pl.BlockSpec((None, 6, D_MODEL), lambda i: (i // per_b, 0, 0)),
            _const_spec((1, D_MODEL)),
            _const_spec(wa.shape), _const_spec(wb.shape), _const_spec(wc.shape), _const_spec(wd.shape),
            _const_spec((1, QK_W)), _const_spec((1, QK_W)), _const_spec((QK_W, QK_W)),
        ],
        out_specs=[row(A_W), row(CONV_CH), row(C_W), row(D_W)],
        out_shape=[
            jax.ShapeDtypeStruct((t, A_W), BF16),
            jax.ShapeDtypeStruct((t, CONV_CH), F32),
            jax.ShapeDtypeStruct((t, C_W), F32),
            jax.ShapeDtypeStruct((t, D_W), BF16),
        ],
        compiler_params=_params("parallel"),
        name="in_projection",
    )(x2d, mod, g_mix, wa, wb, wc, wd, qg, kg, bd)


def _dswa_group(position, span_idx, q_ref, kp_ref, kc_ref, vp_ref, vc_ref, o_ref, q_st, k_st, v_st, m_st, den_st, acc_st,
                *, slopes, window, dilation):
    blk, span = DSWA_BLOCK, DSWA_SPAN
    halo = blk * dilation
    shift = int(np.log2(dilation))
    for s in range(QK_W // LANE_TILE):
        lanes = slice(s * LANE_TILE, (s + 1) * LANE_TILE)
        q_st[s, :, :] = q_ref[:, lanes].astype(F32)
        k_st[s, 0:halo, :] = kp_ref[span - halo:span, lanes].astype(F32)
        k_st[s, halo:halo + span, :] = kc_ref[:, lanes].astype(F32)
        v_st[s, 0:halo, :] = vp_ref[span - halo:span, lanes].astype(F32)
        v_st[s, halo:halo + span, :] = vc_ref[:, lanes].astype(F32)
    head = _head_of_lane()
    qi = _iota((blk, 2 * blk), 0)
    kj = _iota((blk, 2 * blk), 1)
    rel = qi - kj + blk
    in_window = (rel >= 0) & (rel <= window // dilation)
    dist = (dilation * rel).astype(F32)
    bias = jnp.concatenate([jnp.where(in_window, -slopes[h] * dist, MASK_VALUE) for h in range(N_HEADS)], axis=0)
    first_block_penalty = jnp.where(_iota((1, 2 * blk), 1) < blk, MASK_VALUE, 0.0)

    def rows_of(ref, start, n):
        return jnp.concatenate([ref[s, pl.ds(start, n, stride=dilation), :] for s in range(QK_W // LANE_TILE)], axis=1)

    def put_rows(ref, start, val):
        for s in range(QK_W // LANE_TILE):
            ref[s, pl.ds(start, blk, stride=dilation), :] = val[:, s * LANE_TILE:(s + 1) * LANE_TILE]

    def step(it, carry):
        n = jnp.right_shift(it, shift)
        start = n * halo + jnp.bitwise_and(it, dilation - 1)
        q = rows_of(q_st, start, blk).astype(BF16)
        k = rows_of(k_st, start, 2 * blk).astype(BF16)
        v = rows_of(v_st, start, 2 * blk).astype(BF16)
        no_prev = jnp.where((span_idx == 0) & (n == 0), first_block_penalty, 0.0)
        q_stack = jnp.concatenate([jnp.where(head == h, q, jnp.zeros_like(q)) for h in range(N_HEADS)], axis=0)
        s = _dot_nt(q_stack, k) + bias + no_prev
        mx = jnp.max(s, axis=1, keepdims=True)
        p = jnp.exp(s - mx)
        den = jnp.sum(p, axis=1, keepdims=True)
        o_all = _dot(p.astype(BF16), v) * (1.0 / den)
        lse_all = mx + jnp.log(den)
        o_new = jnp.zeros((blk, QK_W), F32)
        lse = jnp.zeros((blk, QK_W), F32)
        for h in range(N_HEADS):
            in_head = head == h
            o_new = jnp.where(in_head, o_all[h * blk:(h + 1) * blk], o_new)
            lse = jnp.where(in_head, lse_all[h * blk:(h + 1) * blk], lse)
        if position == 0:
            m_new, den_new, acc_new = lse, jnp.ones_like(lse), o_new
        else:
            m_old = rows_of(m_st, start, blk)
            m_new = jnp.maximum(m_old, lse)
            w_old, w_new = jnp.exp(m_old - m_new), jnp.exp(lse - m_new)
            den_new = rows_of(den_st, start, blk) * w_old + w_new
            acc_new = rows_of(acc_st, start, blk) * w_old + o_new * w_new
        put_rows(m_st, start, m_new)
        put_rows(den_st, start, den_new)
        put_rows(acc_st, start, acc_new)
        return carry

    lax.fori_loop(0, span // blk, step, 0, unroll=2)
    if position == len(DSWA_PATTERNS) - 1:
        for s in range(QK_W // LANE_TILE):
            o_ref[:, s * LANE_TILE:(s + 1) * LANE_TILE] = (acc_st[s, :, :] / den_st[s, :, :]).astype(BF16)


def _dswa_kernel(q_ref, kp_ref, kc_ref, vp_ref, vc_ref, o_ref, *scratch, slopes):
    span_idx, position = pl.program_id(1), pl.program_id(2)
    for pos in range(len(DSWA_PATTERNS)):
        g = _dswa_group_at(pos)
        window, dilation = DSWA_PATTERNS[g]
        pl.when(position == pos)(functools.partial(
            _dswa_group, pos, span_idx, q_ref, kp_ref, kc_ref, vp_ref, vc_ref, o_ref, *scratch,
            slopes=slopes[g * N_HEADS:(g + 1) * N_HEADS], window=window, dilation=dilation))


def _dswa_group_at(position):
    return len(DSWA_PATTERNS) - 1 - position


def _alibi_slopes(n):
    return [float(np.float32(2.0 ** (-8.0 * i / n))) for i in range(1, n + 1)]


def _dilated_attention(a, bsz, seq):
    span = DSWA_SPAN
    a3 = a.reshape(bsz, seq, A_W)
    n_groups = len(DSWA_PATTERNS)
    col = lambda part, pos: part * n_groups + _dswa_group_at(pos)
    cur = lambda part: pl.BlockSpec((None, span, QK_W), lambda b, i, p: (b, i, col(part, p)))
    prev = lambda part: pl.BlockSpec((None, span, QK_W), lambda b, i, p: (b, jnp.maximum(i - 1, 0), col(part, p)))
    slab = lambda rows: pltpu.VMEM((QK_W // LANE_TILE, rows, LANE_TILE), F32)
    out = pl.pallas_call(
        functools.partial(_dswa_kernel, slopes=_alibi_slopes(n_groups * N_HEADS)),
        grid=(bsz, seq // span, n_groups),
        in_specs=[cur(0), prev(1), cur(1), prev(2), cur(2)],
        out_specs=pl.BlockSpec((None, span, QK_W), lambda b, i, g: (b, i, 0)),
        out_shape=jax.ShapeDtypeStruct((bsz, seq, QK_W), BF16),
        scratch_shapes=[slab(span), slab(2 * span), slab(2 * span), slab(span), slab(span), slab(span)],
        compiler_params=_params("parallel", "arbitrary", "arbitrary"),
        name="dilated_attention",
    )(a3, a3, a3, a3, a3)
    return out.reshape(bsz * seq, QK_W)


def _conv_kernel(prev_ref, cur_ref, w_ref, b_ref, lg_ref, lb_ref, o_ref, buf_ref):
    ts = cur_ref.shape[0]
    first = pl.program_id(1) == 0
    buf_ref[0, 0:CONV_HALO, :] = jnp.where(first, 0.0, prev_ref[...])
    buf_ref[0, CONV_HALO:, :] = cur_ref[...]
    rows = ts + CONV_HALO - SUBLANES
    for r in range(1, SUBLANES):
        buf_ref[r, 0:rows, :] = buf_ref[0, r:r + rows, :]
    off = CONV_HALO - (CONV_WIDTH - 1)
    acc = jnp.zeros((ts, CONV_CH), F32) + b_ref[...]
    for w in range(CONV_WIDTH):
        r, base = (off + w) % SUBLANES, (off + w) // SUBLANES * SUBLANES
        acc = acc + buf_ref[r, base:base + ts, :] * w_ref[w:w + 1, :]
    mu = jnp.mean(acc, axis=-1, keepdims=True)
    ctr = acc - mu
    var = jnp.mean(ctr * ctr, axis=-1, keepdims=True)
    y = ctr * lax.rsqrt(var + NORM_EPS) * lg_ref[...] + lb_ref[...]
    o_ref[...] = (y * _sigmoid(y)).astype(BF16)


def _conv_mixer(y, conv_w, conv_b, ln_g, ln_b, bsz, seq):
    ts = TOKEN_TILE
    per_halo = ts // CONV_HALO
    y3 = y.reshape(bsz, seq, CONV_CH)
    w_pad = jnp.zeros((CONV_HALO, CONV_CH), F32).at[:CONV_WIDTH].set(conv_w)
    out = pl.pallas_call(
        _conv_kernel,
        grid=(bsz, seq // ts),
        in_specs=[
            pl.BlockSpec((None, CONV_HALO, CONV_CH), lambda b, i: (b, jnp.maximum(i * per_halo - 1, 0), 0)),
            pl.BlockSpec((None, ts, CONV_CH), lambda b, i: (b, i, 0)),
            _const_spec((CONV_HALO, CONV_CH)), _const_spec((1, CONV_CH)), _const_spec((1, CONV_CH)),
            _const_spec((1, CONV_CH)),
        ],
        out_specs=pl.BlockSpec((None, ts, CONV_CH), lambda b, i: (b, i, 0)),
        out_shape=jax.ShapeDtypeStruct((bsz, seq, CONV_CH), BF16),
        scratch_shapes=[pltpu.VMEM((SUBLANES, ts + CONV_HALO, CONV_CH), F32)],
        compiler_params=_params("parallel", "parallel"),
        name="conv_mixer",
    )(y3, y3, w_pad, conv_b.reshape(1, -1), ln_g.reshape(1, -1), ln_b.reshape(1, -1))
    return out.reshape(bsz * seq, CONV_CH)


def _gla_kernel(q_ref, k_ref, v_ref, r_ref, low_ref, wg_ref, bg_ref, og_ref, tri_ref, am_ref, sm_ref, o_ref, st_ref):
    ch, sub = GLA_CHUNK, GLA_SUB
    tc = q_ref.shape[0]
    n_sub = ch // sub

    @pl.when(pl.program_id(1) == 0)
    def _():
        st_ref[...] = jnp.zeros_like(st_ref)

    head = _head_of_lane()
    key_row = _iota((ch, QK_W), 0)
    att_mask = am_ref[...]
    st_mask = sm_ref[...]

    logit = _dot(low_ref[...].astype(BF16), wg_ref[...]) + bg_ref[...]
    log_a = _neg_softplus(-logit) * (1.0 / GLA_TAU)
    la_hi, la_lo = _split_bf16(log_a)
    cr = GLA_CUM_ROWS
    cum_all = jnp.concatenate(
        [_dot(tri_ref[...], jnp.concatenate([la_hi[g * cr:(g + 1) * cr], la_lo[g * cr:(g + 1) * cr]], axis=0))
         for g in range(tc // cr)], axis=0)

    state = st_ref[...]
    for c in range(tc // ch):
        rows = slice(c * ch, (c + 1) * ch)
        cum = cum_all[rows, :]
        total = cum[ch - 1:ch, :]
        q = q_ref[rows, :] * (HEAD_LANES ** -0.5)
        k = k_ref[rows, :]
        v = v_ref[rows, :].astype(BF16)

        bases = [jnp.zeros((1, QK_W), F32)] + [cum[i * sub - 1:i * sub, :] for i in range(1, n_sub)]
        base_rows = jnp.concatenate([jnp.broadcast_to(b, (sub, QK_W)) for b in bases], axis=0)
        q_rel = q * jnp.exp(cum - base_rows)
        q_all = jnp.concatenate([jnp.where(head == h, q_rel, 0.0) for h in range(N_HEADS)], axis=0).astype(BF16)
        k_all = jnp.concatenate([jnp.where(key_row < (i + 1) * sub, k * jnp.exp(bases[i] - cum), 0.0)
                                 for i in range(n_sub)], axis=0).astype(BF16)
        att = (_dot_nt(q_all, k_all) * att_mask).astype(BF16)
        intra = _dot(att, jnp.concatenate([v] * n_sub, axis=0))
        o = jnp.concatenate([intra[h * ch:(h + 1) * ch, h * GLA_DV:(h + 1) * GLA_DV] for h in range(N_HEADS)], axis=1)

        o = o + _dot_nt((q * jnp.exp(cum)).astype(BF16), state.astype(BF16))
        k_dec = (k * jnp.exp(total - cum)).astype(BF16)
        state = state * jnp.exp(total) + _dot_tn(v, k_dec) * st_mask

        r = r_ref[rows, :]
        outs = []
        for h in range(N_HEADS):
            o_h = o[:, h * GLA_DV:(h + 1) * GLA_DV]
            ms = jnp.mean(o_h * o_h, axis=-1, keepdims=True)
            outs.append(o_h * lax.rsqrt(ms + NORM_EPS) * og_ref[...])
        o_ref[rows, :] = (r * _sigmoid(r) * jnp.concatenate(outs, axis=1)).astype(BF16)
    st_ref[...] = state


def _gla_masks(tc):
    ch, sub = GLA_CHUNK, GLA_SUB
    t = np.arange(tc)
    tri = (t[:, None] >= t[None, :]) & (t[:, None] // ch == t[None, :] // ch)
    a_t = np.arange(N_HEADS * ch) % ch
    a_col = np.arange((ch // sub) * ch)
    att = (a_col[None, :] // ch == a_t[:, None] // sub) & (a_col[None, :] % ch <= a_t[:, None])
    state = np.arange(GLA_V_W)[:, None] // GLA_DV == np.arange(QK_W)[None, :] // HEAD_LANES
    return (jnp.asarray(np.concatenate([tri, tri], axis=1), BF16), jnp.asarray(att, F32), jnp.asarray(state, F32))


def _gla_mixer(cbuf, w_gate, b_gate, o_gain, bsz, seq):
    tc = GLA_CHUNK * GLA_CHUNKS_PER_STEP
    c3 = cbuf.reshape(bsz, seq, C_W)
    wg = jnp.zeros((GLA_LOW_PAD, QK_W), F32).at[:GLA_RANK].set(w_gate).astype(BF16)
    tri2, att_mask, st_mask = _gla_masks(GLA_CUM_ROWS)
    spec = lambda w, j: pl.BlockSpec((None, tc, w), lambda b, i: (b, i, j))
    out = pl.pallas_call(
        _gla_kernel,
        grid=(bsz, seq // tc),
        in_specs=[
            spec(QK_W, 0), spec(QK_W, 1), spec(GLA_V_W, 1), spec(GLA_V_W, 2),
            spec(GLA_LOW_PAD, (2 * QK_W + 2 * GLA_V_W) // GLA_LOW_PAD),
            _const_spec((GLA_LOW_PAD, QK_W)), _const_spec((1, QK_W)), _const_spec((1, GLA_DV)),
            _const_spec(tri2.shape), _const_spec(att_mask.shape), _const_spec(st_mask.shape),
        ],
        out_specs=pl.BlockSpec((None, tc, GLA_V_W), lambda b, i: (b, i, 0)),
        out_shape=jax.ShapeDtypeStruct((bsz, seq, GLA_V_W), BF16),
        scratch_shapes=[pltpu.VMEM((GLA_V_W, QK_W), F32)],
        compiler_params=_params("parallel", "arbitrary"),
        name="gla_mixer",
    )(c3, c3, c3, c3, c3, wg, b_gate.reshape(1, -1), o_gain.reshape(1, -1), tri2, att_mask, st_mask)
    return out.reshape(bsz * seq, GLA_V_W)


def _sb_kernel(q_ref, k_ref, v_ref, o_ref, kmax_ref):
    t = SB_T
    seq = k_ref.shape[0]
    i = pl.program_id(1)
    head = _head_of_lane()
    head_ones = _ones_where(jnp.right_shift(_iota((QK_W, QK_W), 0), int(np.log2(HEAD_LANES))) == head)

    @pl.when(i == 0)
    def _():
        best = jnp.zeros((SB_NORM_ROWS, QK_W), F32)
        for c in range(seq // SB_NORM_ROWS):
            kf = k_ref[c * SB_NORM_ROWS:(c + 1) * SB_NORM_ROWS, :].astype(F32)
            sq_hi, sq_lo = _split_bf16(kf * kf)
            best = jnp.maximum(best, _dot(sq_hi, head_ones) + _dot(sq_lo, head_ones))
        kmax_ref[...] = jnp.zeros_like(kmax_ref) + jnp.max(best)

    q = q_ref[...]
    q_stack = jnp.concatenate([jnp.where(head == h, q, jnp.zeros_like(q)) for h in range(N_HEADS)], axis=0)
    qf = q_stack.astype(F32)
    z_bound = jnp.sqrt(jnp.sum(qf * qf, axis=1, keepdims=True) * kmax_ref[0:1, 0:1]) * SB_BOUND_SLACK
    tri2 = _ones_where(jnp.bitwise_and(_iota((2 * t, t), 0), t - 1) >= _iota((2 * t, t), 1))
    keep = _iota((N_HEADS * t, t), 1) < jnp.bitwise_and(_iota((N_HEADS * t, t), 0), t - 1)

    def block(j, acc, run, masked):
        start = pl.multiple_of(j * t, t)
        k_blk = k_ref[pl.ds(start, t), :]
        v_blk = v_ref[pl.ds(start, t), :]
        z = _dot_nt(q_stack, k_blk)
        drop = jnp.maximum(z, 0.0) + jnp.log(1.0 + jnp.exp2(jnp.abs(z) * (-LOG2_E)))
        if masked:
            drop = jnp.where(keep, drop, 0.0)
        d_hi, d_lo = _split_bf16(drop)
        suffix = _dot(jnp.concatenate([d_hi, d_lo], axis=1), tri2)
        w = jnp.exp(z - suffix - run)
        if masked:
            w = jnp.where(keep, w, 0.0)
        w = w.astype(BF16)
        w_wide = jnp.concatenate([w[h * t:(h + 1) * t, :] for h in range(N_HEADS)], axis=1)
        v_stack = jnp.concatenate([jnp.where(head == h, v_blk, jnp.zeros_like(v_blk)) for h in range(N_HEADS)], axis=0)
        acc = acc + _dot(w_wide, v_stack)
        run = run + jnp.sum(drop, axis=1, keepdims=True)
        done = (jnp.max(z_bound - run) < SB_ZERO_EXPONENT).astype(jnp.int32)
        return acc, run, done

    acc, run, done = block(i, jnp.zeros((t, QK_W), F32), jnp.zeros((N_HEADS * t, 1), F32), True)

    def body(c):
        j, _, acc, run = c
        acc, run, done = block(j, acc, run, False)
        return j - 1, done, acc, run

    _, _, acc, _ = lax.while_loop(lambda c: jnp.logical_and(c[0] >= 0, c[1] == 0), body, (i - 1, done, acc, run))
    o_ref[...] = acc.astype(BF16)


def _stick_breaking_mixer(dbuf, bsz, seq):
    d3 = dbuf.reshape(bsz, seq, D_W)
    full = lambda j: pl.BlockSpec((None, seq, QK_W), lambda b, i: (b, 0, j))
    out = pl.pallas_call(
        _sb_kernel,
        grid=(bsz, seq // SB_T),
        in_specs=[pl.BlockSpec((None, SB_T, QK_W), lambda b, i: (b, i, 0)), full(1), full(2)],
        out_specs=pl.BlockSpec((None, SB_T, QK_W), lambda b, i: (b, i, 0)),
        out_shape=jax.ShapeDtypeStruct((bsz, seq, QK_W), BF16),
        scratch_shapes=[pltpu.VMEM((8, 128), F32)],
        compiler_params=_params("parallel", "arbitrary"),
        name="stick_breaking_mixer",
    )(d3, d3, d3)
    return out.reshape(bsz * seq, QK_W)


def _merge_kernel(x_ref, mod_ref, g_ref, oa_ref, ob_ref, oc_ref, od_ref,
                  wg_ref, wa_ref, wb_ref, wc_ref, wd_ref, wo_ref, out_ref):
    x = x_ref[...]
    h = _modulated_norm(x, g_ref[...], mod_ref[1:2, :], mod_ref[0:1, :]).astype(BF16)
    branches = ((oa_ref[...], wa_ref), (ob_ref[...], wb_ref), (oc_ref[...], wc_ref), (od_ref[...], wd_ref))
    merged = jnp.zeros(x.shape, F32)
    for j, (o_j, w_ref) in enumerate(branches):
        gate = _sigmoid(_dot(h, wg_ref[:, j * D_MODEL:(j + 1) * D_MODEL]))
        merged = merged + gate * _dot(o_j, w_ref[...])
    out_ref[...] = x + mod_ref[2:3, :] * _dot(merged.astype(BF16), wo_ref[...])


def _merge(x2d, mod, g_mix, oa, ob, oc, od, wg, wa, wb, wc, wd, wo, seq):
    t = x2d.shape[0]
    tm = TOKEN_TILE
    per_b = seq // tm
    row = lambda w: pl.BlockSpec((tm, w), lambda i: (i, 0))
    return pl.pallas_call(
        _merge_kernel,
        grid=(t // tm,),
        in_specs=[
            row(D_MODEL),
            pl.BlockSpec((None, 6, D_MODEL), lambda i: (i // per_b, 0, 0)),
            _const_spec((1, D_MODEL)),
            row(QK_W), row(CONV_CH), row(GLA_V_W), row(QK_W),
            _const_spec(wg.shape), _const_spec(wa.shape), _const_spec(wb.shape), _const_spec(wc.shape),
            _const_spec(wd.shape), _const_spec(wo.shape),
        ],
        out_specs=row(D_MODEL),
        out_shape=jax.ShapeDtypeStruct((t, D_MODEL), F32),
        compiler_params=_params("parallel"),
        name="gated_merge",
    )(x2d, mod, g_mix, oa, ob, oc, od, wg, wa, wb, wc, wd, wo)


def _mlp_kernel(x_ref, mod_ref, g_ref, wu_ref, wd_ref, out_ref):
    x = x_ref[...]
    h = _modulated_norm(x, g_ref[...], mod_ref[4:5, :], mod_ref[3:4, :]).astype(BF16)
    acc = jnp.zeros(x.shape, F32)
    for j in range(D_FF // D_MODEL):
        cols = slice(j * D_MODEL, (j + 1) * D_MODEL)
        u = jnp.maximum(_dot(h, wu_ref[:, cols]), 0.0)
        acc = acc + _dot((u * u).astype(BF16), wd_ref[cols, :])
    out_ref[...] = x + mod_ref[5:6, :] * acc


def _mlp(x2d, mod, g_mlp, wu, wd, seq):
    t = x2d.shape[0]
    tm = TOKEN_TILE
    per_b = seq // tm
    row = pl.BlockSpec((tm, D_MODEL), lambda i: (i, 0))
    return pl.pallas_call(
        _mlp_kernel,
        grid=(t // tm,),
        in_specs=[
            row,
            pl.BlockSpec((None, 6, D_MODEL), lambda i: (i // per_b, 0, 0)),
            _const_spec((1, D_MODEL)), _const_spec(wu.shape), _const_spec(wd.shape),
        ],
        out_specs=row,
        out_shape=jax.ShapeDtypeStruct((t, D_MODEL), F32),
        compiler_params=_params("parallel"),
        name="relu2_mlp",
    )(x2d, mod, g_mlp, wu, wd)


def _head_block_diag():
    idx = np.arange(QK_W) // HEAD_LANES
    return jnp.asarray((idx[:, None] == idx[None, :]).astype(np.float32) / HEAD_LANES, dtype=BF16)


def kernel(x, c, w_ada, b_ada, g_mix, w_in, q_gain, k_gain, conv_w, conv_b, conv_ln_g, conv_ln_b,
           gla_w_gate, gla_b_gate, gla_o_gain, w_br_a, w_br_b, w_br_c, w_br_d, w_out, g_mlp, w_up, w_down):
    bsz, seq, d = x.shape
    depth = w_ada.shape[0]
    assert d == D_MODEL and seq % (DSWA_BLOCK * DSWA_PATTERNS[-1][1]) == 0 and seq % TOKEN_TILE == 0
    mods = _modulation(c, w_ada, b_ada)
    bd = _head_block_diag()
    x2d = x.reshape(bsz * seq, d)
    qk_scale = HEAD_LANES ** -0.5
    for l in range(depth):
        w = w_in[l]
        o_conv = A_W
        o_c = o_conv + 2 * CONV_CH
        o_low = o_c + 2 * QK_W + 2 * GLA_V_W
        o_d = o_low + GLA_RANK
        o_gate = o_d + D_W
        wa = w[:, :A_W].astype(BF16)
        wb = w[:, o_conv:o_c].astype(BF16)
        wc = jnp.concatenate([w[:, o_c:o_low + GLA_RANK], jnp.zeros((d, GLA_LOW_PAD - GLA_RANK), F32)], axis=1).astype(BF16)
        wd = w[:, o_d:o_gate].astype(BF16)
        wg = w[:, o_gate:].astype(BF16)
        qg = jnp.tile(q_gain[l], N_HEADS).reshape(1, QK_W) * qk_scale
        kg = jnp.tile(k_gain[l], N_HEADS).reshape(1, QK_W)
        mod = mods[l]
        gm = g_mix[l].reshape(1, d)

        a, y, cbuf, dbuf = _in_projection(x2d, mod, gm, wa, wb, wc, wd, qg, kg, bd, seq)
        oa = _dilated_attention(a, bsz, seq)
        ob = _conv_mixer(y, conv_w[l], conv_b[l], conv_ln_g[l], conv_ln_b[l], bsz, seq)
        oc = _gla_mixer(cbuf, gla_w_gate[l], gla_b_gate[l], gla_o_gain[l], bsz, seq)
        od = _stick_breaking_mixer(dbuf, bsz, seq)
        x2d = _merge(x2d, mod, gm, oa, ob, oc, od, wg, w_br_a[l].astype(BF16), w_br_b[l].astype(BF16),
                     w_br_c[l].astype(BF16), w_br_d[l].astype(BF16), w_out[l].astype(BF16), seq)
        x2d = _mlp(x2d, mod, g_mlp[l].reshape(1, d), w_up[l].astype(BF16), w_down[l].astype(BF16), seq)
    return x2d.reshape(bsz, seq, d)
```

```python
import functools

import numpy as np
import jax
import jax.numpy as jnp
from jax import lax
from jax.experimental import pallas as pl
from jax.experimental.pallas import tpu as pltpu

F32 = jnp.float32
BF16 = jnp.bfloat16

D_MODEL = 1024
D_FF = 4 * D_MODEL
NORM_EPS = 1e-6
LOG2_E = 1.4426950408889634
HEAD_LANES = 64
N_HEADS = 4
QK_W = N_HEADS * HEAD_LANES
DSWA_PATTERNS = ((128, 1), (512, 4), (2048, 16))
DSWA_BLOCK = 128
DSWA_SPAN = DSWA_BLOCK * DSWA_PATTERNS[-1][1]
LANE_TILE = 128
SUBLANES = 8
CONV_CH = 256
CONV_WIDTH = 31
CONV_HALO = 32
GLA_DV = 128
GLA_V_W = N_HEADS * GLA_DV
GLA_RANK = 16
GLA_LOW_PAD = 128
GLA_TAU = 16.0
GLA_CHUNK = 64
GLA_SUB = 16
GLA_CHUNKS_PER_STEP = 16
GLA_CUM_ROWS = 256
SB_T = 256
SB_STRAIGHT_BLOCKS = 4
SB_NORM_ROWS = 512
SB_BOUND_SLACK = 1.02
SB_ZERO_EXPONENT = -110.0
TOKEN_TILE = 512
MASK_VALUE = -1e30
VMEM_LIMIT = 56 * 1024 * 1024

A_W = 3 * 3 * QK_W
C_W = 2 * QK_W + 2 * GLA_V_W + GLA_LOW_PAD
D_W = 3 * QK_W


def _dot(a, b):
    return jnp.dot(a, b, preferred_element_type=F32)


def _dot_nt(a, b):
    return lax.dot_general(a, b, (((1,), (1,)), ((), ())), preferred_element_type=F32)


def _dot_tn(a, b):
    return lax.dot_general(a, b, (((0,), (0,)), ((), ())), preferred_element_type=F32)


def _split_bf16(x):
    hi = x.astype(BF16)
    lo = (x - hi.astype(F32)).astype(BF16)
    return hi, lo


def _sigmoid(x):
    return 1.0 / (1.0 + jnp.exp(-x))


def _neg_softplus(x):
    return -(jnp.maximum(x, 0.0) + jnp.log(1.0 + jnp.exp(-jnp.abs(x))))


def _iota(shape, dim):
    return lax.broadcasted_iota(jnp.int32, shape, dim)


def _head_of_lane(width=QK_W, lanes_per_head=HEAD_LANES):
    return jnp.right_shift(_iota((1, width), 1), int(np.log2(lanes_per_head)))


def _ones_where(cond):
    return jnp.where(cond, 1.0, 0.0).astype(BF16)


def _const_spec(shape):
    nd = len(shape)
    return pl.BlockSpec(shape, lambda *_: (0,) * nd, pipeline_mode=pl.Buffered(1))


def _params(*sem):
    return pltpu.CompilerParams(dimension_semantics=sem, vmem_limit_bytes=VMEM_LIMIT)


def _modulated_norm(x, gain, scale, shift):
    ms = jnp.mean(x * x, axis=-1, keepdims=True)
    y = x * lax.rsqrt(ms + NORM_EPS) * gain
    return y * (1.0 + scale) + shift


def _mod_kernel(c_ref, w_ref, b_ref, o_ref):
    c_hi, c_lo = _split_bf16(c_ref[...])
    w_hi, w_lo = _split_bf16(w_ref[...])
    o_ref[...] = _dot(c_hi, w_hi) + _dot(c_hi, w_lo) + _dot(c_lo, w_hi) + b_ref[...]


def _modulation(c, w_ada, b_ada):
    depth, d, six_d = w_ada.shape
    bsz = c.shape[0]
    out = pl.pallas_call(
        _mod_kernel,
        grid=(depth, six_d // d),
        in_specs=[
            pl.BlockSpec((bsz, d), lambda l, j: (0, 0)),
            pl.BlockSpec((None, d, d), lambda l, j: (l, 0, j)),
            pl.BlockSpec((None, 1, d), lambda l, j: (l, 0, j)),
        ],
        out_specs=pl.BlockSpec((None, bsz, d), lambda l, j: (l, 0, j)),
        out_shape=jax.ShapeDtypeStruct((depth, bsz, six_d), F32),
        compiler_params=_params("parallel", "parallel"),
        name="adaln_modulation",
    )(c, w_ada, b_ada.reshape(depth, 1, six_d))
    return out.reshape(depth, bsz, six_d // d, d)


def _inproj_kernel(x_ref, mod_ref, g_ref, wa_ref, wb_ref, wc_ref, wd_ref, qg_ref, kg_ref, bd_ref,
                   a_ref, y_ref, c_ref, d_ref):
    h = _modulated_norm(x_ref[...], g_ref[...], mod_ref[1:2, :], mod_ref[0:1, :]).astype(BF16)
    bd = bd_ref[...]
    a_all = _dot(h, wa_ref[...])
    for j in range(A_W // QK_W):
        cols = slice(j * QK_W, (j + 1) * QK_W)
        acc = a_all[:, cols]
        if j < 6:
            ms = _dot((acc * acc).astype(BF16), bd)
            gain = qg_ref[...] if j < 3 else kg_ref[...]
            acc = acc * lax.rsqrt(ms + NORM_EPS) * gain
        a_ref[:, cols] = acc.astype(BF16)
    u = _dot(h, wb_ref[...])
    y_ref[...] = u[:, :CONV_CH] * _sigmoid(u[:, CONV_CH:])
    c_ref[...] = _dot(h, wc_ref[...])
    d_all = _dot(h, wd_ref[...])
    d_ref[:, :QK_W] = (d_all[:, :QK_W] * (HEAD_LANES ** -0.5)).astype(BF16)
    d_ref[:, QK_W:] = d_all[:, QK_W:].astype(BF16)


def _in_projection(x2d, mod, g_mix, wa, wb, wc, wd, qg, kg, bd, seq):
    t = x2d.shape[0]
    tm = TOKEN_TILE
    per_b = seq // tm
    row = lambda w: pl.BlockSpec((tm, w), lambda i: (i, 0))
    return pl.pallas_call(
        _inproj_kernel,
        grid=(t // tm,),
        in_specs=[
            row(D_MODEL),
            pl.BlockSpec((None, 6, D_MODEL), lambda i: (i // per_b, 0, 0)),
            _const_spec((1, D_MODEL)),
            _const_spec(wa.shape), _const_spec(wb.shape), _const_spec(wc.shape), _const_spec(wd.shape),
            _const_spec((1, QK_W)), _const_spec((1, QK_W)), _const_spec((QK_W, QK_W)),
        ],
        out_specs=[row(A_W), row(CONV_CH), row(C_W), row(D_W)],
        out_shape=[
            jax.ShapeDtypeStruct((t, A_W), BF16),
            jax.ShapeDtypeStruct((t, CONV_CH), F32),
            jax.ShapeDtypeStruct((t, C_W), F32),
            jax.ShapeDtypeStruct((t, D_W), BF16),
        ],
        compiler_params=_params("parallel"),
        name="in_projection",
    )(x2d, mod, g_mix, wa, wb, wc, wd, qg, kg, bd)


def _dswa_group(position, span_idx, q_ref, kp_ref, kc_ref, vp_ref, vc_ref, o_ref, q_st, k_st, v_st, m_st, den_st, acc_st,
                *, slopes, window, dilation):
    blk, span = DSWA_BLOCK, DSWA_SPAN
    halo = blk * dilation
    shift = int(np.log2(dilation))
    for s in range(QK_W // LANE_TILE):
        lanes = slice(s * LANE_TILE, (s + 1) * LANE_TILE)
        q_st[s, :, :] = q_ref[:, lanes].astype(F32)
        k_st[s, 0:halo, :] = kp_ref[span - halo:span, lanes].astype(F32)
        k_st[s, halo:halo + span, :] = kc_ref[:, lanes].astype(F32)
        v_st[s, 0:halo, :] = vp_ref[span - halo:span, lanes].astype(F32)
        v_st[s, halo:halo + span, :] = vc_ref[:, lanes].astype(F32)
    head = _head_of_lane()
    qi = _iota((blk, 2 * blk), 0)
    kj = _iota((blk, 2 * blk), 1)
    rel = qi - kj + blk
    in_window = (rel >= 0) & (rel <= window // dilation)
    dist = (dilation * rel).astype(F32)
    bias = jnp.concatenate([jnp.where(in_window, -slopes[h] * dist, MASK_VALUE) for h in range(N_HEADS)], axis=0)
    first_block_penalty = jnp.where(_iota((1, 2 * blk), 1) < blk, MASK_VALUE, 0.0)

    def rows_of(ref, start, n):
        return jnp.concatenate([ref[s, pl.ds(start, n, stride=dilation), :] for s in range(QK_W // LANE_TILE)], axis=1)

    def put_rows(ref, start, val):
        for s in range(QK_W // LANE_TILE):
            ref[s, pl.ds(start, blk, stride=dilation), :] = val[:, s * LANE_TILE:(s + 1) * LANE_TILE]

    def step(it, carry):
        n = jnp.right_shift(it, shift)
        start = n * halo + jnp.bitwise_and(it, dilation - 1)
        q = rows_of(q_st, start, blk).astype(BF16)
        k = rows_of(k_st, start, 2 * blk).astype(BF16)
        v = rows_of(v_st, start, 2 * blk).astype(BF16)
        no_prev = jnp.where((span_idx == 0) & (n == 0), first_block_penalty, 0.0)
        q_stack = jnp.concatenate([jnp.where(head == h, q, jnp.zeros_like(q)) for h in range(N_HEADS)], axis=0)
        s = _dot_nt(q_stack, k) + bias + no_prev
        mx = jnp.max(s, axis=1, keepdims=True)
        p = jnp.exp(s - mx)
        den = jnp.sum(p, axis=1, keepdims=True)
        o_all = _dot(p.astype(BF16), v) * (1.0 / den)
        lse_all = mx + jnp.log(den)
        o_new = jnp.zeros((blk, QK_W), F32)
        lse = jnp.zeros((blk, QK_W), F32)
        for h in range(N_HEADS):
            in_head = head == h
            o_new = jnp.where(in_head, o_all[h * blk:(h + 1) * blk], o_new)
            lse = jnp.where(in_head, lse_all[h * blk:(h + 1) * blk], lse)
        if position == 0:
            m_new, den_new, acc_new = lse, jnp.ones_like(lse), o_new
        else:
            m_old = rows_of(m_st, start, blk)
            m_new = jnp.maximum(m_old, lse)
            w_old, w_new = jnp.exp(m_old - m_new), jnp.exp(lse - m_new)
            den_new = rows_of(den_st, start, blk) * w_old + w_new
            acc_new = rows_of(acc_st, start, blk) * w_old + o_new * w_new
        put_rows(m_st, start, m_new)
        put_rows(den_st, start, den_new)
        put_rows(acc_st, start, acc_new)
        return carry

    lax.fori_loop(0, span // blk, step, 0, unroll=8)
    if position == len(DSWA_PATTERNS) - 1:
        for s in range(QK_W // LANE_TILE):
            o_ref[:, s * LANE_TILE:(s + 1) * LANE_TILE] = (acc_st[s, :, :] / den_st[s, :, :]).astype(BF16)


def _dswa_kernel(q_ref, kp_ref, kc_ref, vp_ref, vc_ref, o_ref, *scratch, slopes):
    span_idx, position = pl.program_id(1), pl.program_id(2)
    for pos in range(len(DSWA_PATTERNS)):
        g = _dswa_group_at(pos)
        window, dilation = DSWA_PATTERNS[g]
        pl.when(position == pos)(functools.partial(
            _dswa_group, pos, span_idx, q_ref, kp_ref, kc_ref, vp_ref, vc_ref, o_ref, *scratch,
            slopes=slopes[g * N_HEADS:(g + 1) * N_HEADS], window=window, dilation=dilation))


def _dswa_group_at(position):
    return len(DSWA_PATTERNS) - 1 - position


def _alibi_slopes(n):
    return [float(np.float32(2.0 ** (-8.0 * i / n))) for i in range(1, n + 1)]


def _dilated_attention(a, bsz, seq):
    span = DSWA_SPAN
    a3 = a.reshape(bsz, seq, A_W)
    n_groups = len(DSWA_PATTERNS)
    col = lambda part, pos: part * n_groups + _dswa_group_at(pos)
    cur = lambda part: pl.BlockSpec((None, span, QK_W), lambda b, i, p: (b, i, col(part, p)))
    prev = lambda part: pl.BlockSpec((None, span, QK_W), lambda b, i, p: (b, jnp.maximum(i - 1, 0), col(part, p)))
    slab = lambda rows: pltpu.VMEM((QK_W // LANE_TILE, rows, LANE_TILE), F32)
    out = pl.pallas_call(
        functools.partial(_dswa_kernel, slopes=_alibi_slopes(n_groups * N_HEADS)),
        grid=(bsz, seq // span, n_groups),
        in_specs=[cur(0), prev(1), cur(1), prev(2), cur(2)],
        out_specs=pl.BlockSpec((None, span, QK_W), lambda b, i, g: (b, i, 0)),
        out_shape=jax.ShapeDtypeStruct((bsz, seq, QK_W), BF16),
        scratch_shapes=[slab(span), slab(2 * span), slab(2 * span), slab(span), slab(span), slab(span)],
        compiler_params=_params("parallel", "arbitrary", "arbitrary"),
        name="dilated_attention",
    )(a3, a3, a3, a3, a3)
    return out.reshape(bsz * seq, QK_W)


def _conv_kernel(prev_ref, cur_ref, w_ref, b_ref, lg_ref, lb_ref, o_ref, buf_ref):
    ts = cur_ref.shape[0]
    first = pl.program_id(1) == 0
    buf_ref[0, 0:CONV_HALO, :] = jnp.where(first, 0.0, prev_ref[...])
    buf_ref[0, CONV_HALO:, :] = cur_ref[...]
    rows = ts + CONV_HALO - SUBLANES
    for r in range(1, SUBLANES):
        buf_ref[r, 0:rows, :] = buf_ref[0, r:r + rows, :]
    off = CONV_HALO - (CONV_WIDTH - 1)
    acc = jnp.zeros((ts, CONV_CH), F32) + b_ref[...]
    for w in range(CONV_WIDTH):
        r, base = (off + w) % SUBLANES, (off + w) // SUBLANES * SUBLANES
        acc = acc + buf_ref[r, base:base + ts, :] * w_ref[w:w + 1, :]
    mu = jnp.mean(acc, axis=-1, keepdims=True)
    ctr = acc - mu
    var = jnp.mean(ctr * ctr, axis=-1, keepdims=True)
    y = ctr * lax.rsqrt(var + NORM_EPS) * lg_ref[...] + lb_ref[...]
    o_ref[...] = (y * _sigmoid(y)).astype(BF16)


def _conv_mixer(y, conv_w, conv_b, ln_g, ln_b, bsz, seq):
    ts = TOKEN_TILE
    per_halo = ts // CONV_HALO
    y3 = y.reshape(bsz, seq, CONV_CH)
    w_pad = jnp.zeros((CONV_HALO, CONV_CH), F32).at[:CONV_WIDTH].set(conv_w)
    out = pl.pallas_call(
        _conv_kernel,
        grid=(bsz, seq // ts),
        in_specs=[
            pl.BlockSpec((None, CONV_HALO, CONV_CH), lambda b, i: (b, jnp.maximum(i * per_halo - 1, 0), 0)),
            pl.BlockSpec((None, ts, CONV_CH), lambda b, i: (b, i, 0)),
            _const_spec((CONV_HALO, CONV_CH)), _const_spec((1, CONV_CH)), _const_spec((1, CONV_CH)),
            _const_spec((1, CONV_CH)),
        ],
        out_specs=pl.BlockSpec((None, ts, CONV_CH), lambda b, i: (b, i, 0)),
        out_shape=jax.ShapeDtypeStruct((bsz, seq, CONV_CH), BF16),
        scratch_shapes=[pltpu.VMEM((SUBLANES, ts + CONV_HALO, CONV_CH), F32)],
        compiler_params=_params("parallel", "parallel"),
        name="conv_mixer",
    )(y3, y3, w_pad, conv_b.reshape(1, -1), ln_g.reshape(1, -1), ln_b.reshape(1, -1))
    return out.reshape(bsz * seq, CONV_CH)


def _gla_kernel(q_ref, k_ref, v_ref, r_ref, low_ref, wg_ref, bg_ref, og_ref, tri_ref, am_ref, sm_ref, o_ref, st_ref):
    ch, sub = GLA_CHUNK, GLA_SUB
    tc = q_ref.shape[0]
    n_sub = ch // sub

    @pl.when(pl.program_id(1) == 0)
    def _():
        st_ref[...] = jnp.zeros_like(st_ref)

    head = _head_of_lane()
    key_row = _iota((ch, QK_W), 0)
    att_mask = am_ref[...]
    st_mask = sm_ref[...]

    logit = _dot(low_ref[...].astype(BF16), wg_ref[...]) + bg_ref[...]
    log_a = _neg_softplus(-logit) * (1.0 / GLA_TAU)
    la_hi, la_lo = _split_bf16(log_a)
    cr = GLA_CUM_ROWS
    cum_all = jnp.concatenate(
        [_dot(tri_ref[...], jnp.concatenate([la_hi[g * cr:(g + 1) * cr], la_lo[g * cr:(g + 1) * cr]], axis=0))
         for g in range(tc // cr)], axis=0)

    state = st_ref[...]
    for c in range(tc // ch):
        rows = slice(c * ch, (c + 1) * ch)
        cum = cum_all[rows, :]
        total = cum[ch - 1:ch, :]
        q = q_ref[rows, :] * (HEAD_LANES ** -0.5)
        k = k_ref[rows, :]
        v = v_ref[rows, :].astype(BF16)

        bases = [jnp.zeros((1, QK_W), F32)] + [cum[i * sub - 1:i * sub, :] for i in range(1, n_sub)]
        base_rows = jnp.concatenate([jnp.broadcast_to(b, (sub, QK_W)) for b in bases], axis=0)
        q_rel = q * jnp.exp(cum - base_rows)
        q_all = jnp.concatenate([jnp.where(head == h, q_rel, 0.0) for h in range(N_HEADS)], axis=0).astype(BF16)
        k_all = jnp.concatenate([jnp.where(key_row < (i + 1) * sub, k * jnp.exp(bases[i] - cum), 0.0)
                                 for i in range(n_sub)], axis=0).astype(BF16)
        att = (_dot_nt(q_all, k_all) * att_mask).astype(BF16)
        intra = _dot(att, jnp.concatenate([v] * n_sub, axis=0))
        o = jnp.concatenate([intra[h * ch:(h + 1) * ch, h * GLA_DV:(h + 1) * GLA_DV] for h in range(N_HEADS)], axis=1)

        o = o + _dot_nt((q * jnp.exp(cum)).astype(BF16), state.astype(BF16))
        k_dec = (k * jnp.exp(total - cum)).astype(BF16)
        state = state * jnp.exp(total) + _dot_tn(v, k_dec) * st_mask

        r = r_ref[rows, :]
        outs = []
        for h in range(N_HEADS):
            o_h = o[:, h * GLA_DV:(h + 1) * GLA_DV]
            ms = jnp.mean(o_h * o_h, axis=-1, keepdims=True)
            outs.append(o_h * lax.rsqrt(ms + NORM_EPS) * og_ref[...])
        o_ref[rows, :] = (r * _sigmoid(r) * jnp.concatenate(outs, axis=1)).astype(BF16)
    st_ref[...] = state


def _gla_masks(tc):
    ch, sub = GLA_CHUNK, GLA_SUB
    t = np.arange(tc)
    tri = (t[:, None] >= t[None, :]) & (t[:, None] // ch == t[None, :] // ch)
    a_t = np.arange(N_HEADS * ch) % ch
    a_col = np.arange((ch // sub) * ch)
    att = (a_col[None, :] // ch == a_t[:, None] // sub) & (a_col[None, :] % ch <= a_t[:, None])
    state = np.arange(GLA_V_W)[:, None] // GLA_DV == np.arange(QK_W)[None, :] // HEAD_LANES
    return (jnp.asarray(np.concatenate([tri, tri], axis=1), BF16), jnp.asarray(att, F32), jnp.asarray(state, F32))


def _gla_mixer(cbuf, w_gate, b_gate, o_gain, bsz, seq):
    tc = GLA_CHUNK * GLA_CHUNKS_PER_STEP
    c3 = cbuf.reshape(bsz, seq, C_W)
    wg = jnp.zeros((GLA_LOW_PAD, QK_W), F32).at[:GLA_RANK].set(w_gate).astype(BF16)
    tri2, att_mask, st_mask = _gla_masks(GLA_CUM_ROWS)
    spec = lambda w, j: pl.BlockSpec((None, tc, w), lambda b, i: (b, i, j))
    out = pl.pallas_call(
        _gla_kernel,
        grid=(bsz, seq // tc),
        in_specs=[
            spec(QK_W, 0), spec(QK_W, 1), spec(GLA_V_W, 1), spec(GLA_V_W, 2),
            spec(GLA_LOW_PAD, (2 * QK_W + 2 * GLA_V_W) // GLA_LOW_PAD),
            _const_spec((GLA_LOW_PAD, QK_W)), _const_spec((1, QK_W)), _const_spec((1, GLA_DV)),
            _const_spec(tri2.shape), _const_spec(att_mask.shape), _const_spec(st_mask.shape),
        ],
        out_specs=pl.BlockSpec((None, tc, GLA_V_W), lambda b, i: (b, i, 0)),
        out_shape=jax.ShapeDtypeStruct((bsz, seq, GLA_V_W), BF16),
        scratch_shapes=[pltpu.VMEM((GLA_V_W, QK_W), F32)],
        compiler_params=_params("parallel", "arbitrary"),
        name="gla_mixer",
    )(c3, c3, c3, c3, c3, wg, b_gate.reshape(1, -1), o_gain.reshape(1, -1), tri2, att_mask, st_mask)
    return out.reshape(bsz * seq, GLA_V_W)


def _sb_kernel(q_ref, k_ref, v_ref, o_ref, kmax_ref):
    t = SB_T
    seq = k_ref.shape[0]
    i = pl.program_id(1)
    head = _head_of_lane()
    head_ones = _ones_where(jnp.right_shift(_iota((QK_W, QK_W), 0), int(np.log2(HEAD_LANES))) == head)

    @pl.when(i == 0)
    def _():
        best = jnp.zeros((SB_NORM_ROWS, QK_W), F32)
        for c in range(seq // SB_NORM_ROWS):
            kf = k_ref[c * SB_NORM_ROWS:(c + 1) * SB_NORM_ROWS, :].astype(F32)
            sq_hi, sq_lo = _split_bf16(kf * kf)
            best = jnp.maximum(best, _dot(sq_hi, head_ones) + _dot(sq_lo, head_ones))
        kmax_ref[...] = jnp.zeros_like(kmax_ref) + jnp.max(best)

    q = q_ref[...]
    q_stack = jnp.concatenate([jnp.where(head == h, q, jnp.zeros_like(q)) for h in range(N_HEADS)], axis=0)
    qf = q_stack.astype(F32)
    z_bound = jnp.sqrt(jnp.sum(qf * qf, axis=1, keepdims=True) * kmax_ref[0:1, 0:1]) * SB_BOUND_SLACK
    tri2 = _ones_where(jnp.bitwise_and(_iota((2 * t, t), 0), t - 1) >= _iota((2 * t, t), 1))
    keep = _iota((N_HEADS * t, t), 1) < jnp.bitwise_and(_iota((N_HEADS * t, t), 0), t - 1)

    def block(j, acc, run, masked):
        start = pl.multiple_of(j * t, t)
        k_blk = k_ref[pl.ds(start, t), :]
        v_blk = v_ref[pl.ds(start, t), :]
        z = _dot_nt(q_stack, k_blk)
        drop = jnp.maximum(z, 0.0) + jnp.log(1.0 + jnp.exp2(jnp.abs(z) * (-LOG2_E)))
        if masked:
            drop = jnp.where(keep, drop, 0.0)
        d_hi, d_lo = _split_bf16(drop)
        suffix = _dot(jnp.concatenate([d_hi, d_lo], axis=1), tri2)
        w = jnp.exp(z - suffix - run)
        if masked:
            w = jnp.where(keep, w, 0.0)
        w = w.astype(BF16)
        w_wide = jnp.concatenate([w[h * t:(h + 1) * t, :] for h in range(N_HEADS)], axis=1)
        v_stack = jnp.concatenate([jnp.where(head == h, v_blk, jnp.zeros_like(v_blk)) for h in range(N_HEADS)], axis=0)
        acc = acc + _dot(w_wide, v_stack)
        run = run + jnp.sum(drop, axis=1, keepdims=True)
        done = (jnp.max(z_bound - run) < SB_ZERO_EXPONENT).astype(jnp.int32)
        return acc, run, done

    def body(c):
        j, _, acc, run = c
        acc, run, done = block(j, acc, run, False)
        return j - 1, done, acc, run

    def walk(first_blocks):
        acc, run, done = block(i, jnp.zeros((t, QK_W), F32), jnp.zeros((N_HEADS * t, 1), F32), True)
        for n in range(1, first_blocks):
            acc, run, done = block(i - n, acc, run, False)
        carry = lax.while_loop(lambda c: jnp.logical_and(c[0] >= 0, c[1] == 0), body, (i - first_blocks, done, acc, run))
        o_ref[...] = carry[2].astype(BF16)

    pl.when(i >= SB_STRAIGHT_BLOCKS - 1)(functools.partial(walk, SB_STRAIGHT_BLOCKS))
    pl.when(i < SB_STRAIGHT_BLOCKS - 1)(functools.partial(walk, 1))


def _stick_breaking_mixer(dbuf, bsz, seq):
    d3 = dbuf.reshape(bsz, seq, D_W)
    full = lambda j: pl.BlockSpec((None, seq, QK_W), lambda b, i: (b, 0, j))
    out = pl.pallas_call(
        _sb_kernel,
        grid=(bsz, seq // SB_T),
        in_specs=[pl.BlockSpec((None, SB_T, QK_W), lambda b, i: (b, i, 0)), full(1), full(2)],
        out_specs=pl.BlockSpec((None, SB_T, QK_W), lambda b, i: (b, i, 0)),
        out_shape=jax.ShapeDtypeStruct((bsz, seq, QK_W), BF16),
        scratch_shapes=[pltpu.VMEM((8, 128), F32)],
        compiler_params=_params("parallel", "arbitrary"),
        name="stick_breaking_mixer",
    )(d3, d3, d3)
    return out.reshape(bsz * seq, QK_W)


def _merge_kernel(x_ref, mod_ref, g_ref, oa_ref, ob_ref, oc_ref, od_ref,
                  wg_ref, wa_ref, wb_ref, wc_ref, wd_ref, wo_ref, out_ref):
    x = x_ref[...]
    h = _modulated_norm(x, g_ref[...], mod_ref[1:2, :], mod_ref[0:1, :]).astype(BF16)
    branches = ((oa_ref[...], wa_ref), (ob_ref[...], wb_ref), (oc_ref[...], wc_ref), (od_ref[...], wd_ref))
    merged = jnp.zeros(x.shape, F32)
    for j, (o_j, w_ref) in enumerate(branches):
        gate = _sigmoid(_dot(h, wg_ref[:, j * D_MODEL:(j + 1) * D_MODEL]))
        merged = merged + gate * _dot(o_j, w_ref[...])
    out_ref[...] = x + mod_ref[2:3, :] * _dot(merged.astype(BF16), wo_ref[...])


def _merge(x2d, mod, g_mix, oa, ob, oc, od, wg, wa, wb, wc, wd, wo, seq):
    t = x2d.shape[0]
    tm = TOKEN_TILE
    per_b = seq // tm
    row = lambda w: pl.BlockSpec((tm, w), lambda i: (i, 0))
    return pl.pallas_call(
        _merge_kernel,
        grid=(t // tm,),
        in_specs=[
            row(D_MODEL),
            pl.BlockSpec((None, 6, D_MODEL), lambda i: (i // per_b, 0, 0)),
            _const_spec((1, D_MODEL)),
            row(QK_W), row(CONV_CH), row(GLA_V_W), row(QK_W),
            _const_spec(wg.shape), _const_spec(wa.shape), _const_spec(wb.shape), _const_spec(wc.shape),
            _const_spec(wd.shape), _const_spec(wo.shape),
        ],
        out_specs=row(D_MODEL),
        out_shape=jax.ShapeDtypeStruct((t, D_MODEL), F32),
        compiler_params=_params("parallel"),
        name="gated_merge",
    )(x2d, mod, g_mix, oa, ob, oc, od, wg, wa, wb, wc, wd, wo)


def _mlp_kernel(x_ref, mod_ref, g_ref, wu_ref, wd_ref, out_ref):
    x = x_ref[...]
    h = _modulated_norm(x, g_ref[...], mod_ref[4:5, :], mod_ref[3:4, :]).astype(BF16)
    acc = jnp.zeros(x.shape, F32)
    for j in range(D_FF // D_MODEL):
        cols = slice(j * D_MODEL, (j + 1) * D_MODEL)
        u = jnp.maximum(_dot(h, wu_ref[:, cols]), 0.0)
        acc = acc + _dot((u * u).astype(BF16), wd_ref[cols, :])
    out_ref[...] = x + mod_ref[5:6, :] * acc


def _mlp(x2d, mod, g_mlp, wu, wd, seq):
    t = x2d.shape[0]
    tm = TOKEN_TILE
    per_b = seq // tm
    row = pl.BlockSpec((tm, D_MODEL), lambda i: (i, 0))
    return pl.pallas_call(
        _mlp_kernel,
        grid=(t // tm,),
        in_specs=[
            row,
            pl.BlockSpec((None, 6, D_MODEL), lambda i: (i // per_b, 0, 0)),
            _const_spec((1, D_MODEL)), _const_spec(wu.shape), _const_spec(wd.shape),
        ],
        out_specs=row,
        out_shape=jax.ShapeDtypeStruct((t, D_MODEL), F32),
        compiler_params=_params("parallel"),
        name="relu2_mlp",
    )(x2d, mod, g_mlp, wu, wd)


def _head_block_diag():
    idx = np.arange(QK_W) // HEAD_LANES
    return jnp.asarray((idx[:, None] == idx[None, :]).astype(np.float32) / HEAD_LANES, dtype=BF16)


def kernel(x, c, w_ada, b_ada, g_mix, w_in, q_gain, k_gain, conv_w, conv_b, conv_ln_g, conv_ln_b,
           gla_w_gate, gla_b_gate, gla_o_gain, w_br_a, w_br_b, w_br_c, w_br_d, w_out, g_mlp, w_up, w_down):
    bsz, seq, d = x.shape
    depth = w_ada.shape[0]
    assert d == D_MODEL and seq % (DSWA_BLOCK * DSWA_PATTERNS[-1][1]) == 0 and seq % TOKEN_TILE == 0
    mods = _modulation(c, w_ada, b_ada)
    bd = _head_block_diag()
    x2d = x.reshape(bsz * seq, d)
    qk_scale = HEAD_LANES ** -0.5
    for l in range(depth):
        w = w_in[l]
        o_conv = A_W
        o_c = o_conv + 2 * CONV_CH
        o_low = o_c + 2 * QK_W + 2 * GLA_V_W
        o_d = o_low + GLA_RANK
        o_gate = o_d + D_W
        wa = w[:, :A_W].astype(BF16)
        wb = w[:, o_conv:o_c].astype(BF16)
        wc = jnp.concatenate([w[:, o_c:o_low + GLA_RANK], jnp.zeros((d, GLA_LOW_PAD - GLA_RANK), F32)], axis=1).astype(BF16)
        wd = w[:, o_d:o_gate].astype(BF16)
        wg = w[:, o_gate:].astype(BF16)
        qg = jnp.tile(q_gain[l], N_HEADS).reshape(1, QK_W) * qk_scale
        kg = jnp.tile(k_gain[l], N_HEADS).reshape(1, QK_W)
        mod = mods[l]
        gm = g_mix[l].reshape(1, d)

        a, y, cbuf, dbuf = _in_projection(x2d, mod, gm, wa, wb, wc, wd, qg, kg, bd, seq)
        oa = _dilated_attention(a, bsz, seq)
        ob = _conv_mixer(y, conv_w[l], conv_b[l], conv_ln_g[l], conv_ln_b[l], bsz, seq)
        oc = _gla_mixer(cbuf, gla_w_gate[l], gla_b_gate[l], gla_o_gain[l], bsz, seq)
        od = _stick_breaking_mixer(dbuf, bsz, seq)
        x2d = _merge(x2d, mod, gm, oa, ob, oc, od, wg, w_br_a[l].astype(BF16), w_br_b[l].astype(BF16),
                     w_br_c[l].astype(BF16), w_br_d[l].astype(BF16), w_out[l].astype(BF16), seq)
        x2d = _mlp(x2d, mod, g_mlp[l].reshape(1, d), w_up[l].astype(BF16), w_down[l].astype(BF16), seq)
    return x2d.reshape(bsz, seq, d)
```

```python
import functools

import numpy as np
import jax
import jax.numpy as jnp
from jax import lax
from jax.experimental import pallas as pl
from jax.experimental.pallas import tpu as pltpu

F32 = jnp.float32
BF16 = jnp.bfloat16

D_MODEL = 1024
D_FF = 4 * D_MODEL
NORM_EPS = 1e-6
LOG2_E = 1.4426950408889634
HEAD_LANES = 64
N_HEADS = 4
QK_W = N_HEADS * HEAD_LANES
DSWA_PATTERNS = ((128, 1), (512, 4), (2048, 16))
DSWA_BLOCK = 128
DSWA_SPAN = DSWA_BLOCK * DSWA_PATTERNS[-1][1]
LANE_TILE = 128
SUBLANES = 8
CONV_CH = 256
CONV_WIDTH = 31
CONV_HALO = 32
GLA_DV = 128
GLA_V_W = N_HEADS * GLA_DV
GLA_RANK = 16
GLA_LOW_PAD = 128
GLA_TAU = 16.0
GLA_CHUNK = 64
GLA_SUB = 16
GLA_CHUNKS_PER_STEP = 16
GLA_CUM_ROWS = 256
SB_T = 256
SB_STRAIGHT_BLOCKS = 4
SB_NORM_ROWS = 512
SB_BOUND_SLACK = 1.02
SB_ZERO_EXPONENT = -110.0
TOKEN_TILE = 512
MASK_VALUE = -1e30
VMEM_LIMIT = 56 * 1024 * 1024

A_W = 3 * 3 * QK_W
C_W = 2 * QK_W + 2 * GLA_V_W + GLA_LOW_PAD
D_W = 3 * QK_W


def _dot(a, b):
    return jnp.dot(a, b, preferred_element_type=F32)


def _dot_nt(a, b):
    return lax.dot_general(a, b, (((1,), (1,)), ((), ())), preferred_element_type=F32)


def _dot_tn(a, b):
    return lax.dot_general(a, b, (((0,), (0,)), ((), ())), preferred_element_type=F32)


def _split_bf16(x):
    hi = x.astype(BF16)
    lo = (x - hi.astype(F32)).astype(BF16)
    return hi, lo


def _sigmoid(x):
    return 1.0 / (1.0 + jnp.exp(-x))


def _neg_softplus(x):
    return -(jnp.maximum(x, 0.0) + jnp.log(1.0 + jnp.exp(-jnp.abs(x))))


def _iota(shape, dim):
    return lax.broadcasted_iota(jnp.int32, shape, dim)


def _head_of_lane(width=QK_W, lanes_per_head=HEAD_LANES):
    return jnp.right_shift(_iota((1, width), 1), int(np.log2(lanes_per_head)))


def _ones_where(cond):
    return jnp.where(cond, 1.0, 0.0).astype(BF16)


def _const_spec(shape):
    nd = len(shape)
    return pl.BlockSpec(shape, lambda *_: (0,) * nd, pipeline_mode=pl.Buffered(1))


def _params(*sem):
    return pltpu.CompilerParams(dimension_semantics=sem, vmem_limit_bytes=VMEM_LIMIT)


def _modulated_norm(x, gain, scale, shift):
    ms = jnp.mean(x * x, axis=-1, keepdims=True)
    y = x * lax.rsqrt(ms + NORM_EPS) * gain
    return y * (1.0 + scale) + shift


def _mod_kernel(c_ref, w_ref, b_ref, o_ref):
    c_hi, c_lo = _split_bf16(c_ref[...])
    w_hi, w_lo = _split_bf16(w_ref[...])
    o_ref[...] = _dot(c_hi, w_hi) + _dot(c_hi, w_lo) + _dot(c_lo, w_hi) + b_ref[...]


def _modulation(c, w_ada, b_ada):
    depth, d, six_d = w_ada.shape
    bsz = c.shape[0]
    out = pl.pallas_call(
        _mod_kernel,
        grid=(depth, six_d // d),
        in_specs=[
            pl.BlockSpec((bsz, d), lambda l, j: (0, 0)),
            pl.BlockSpec((None, d, d), lambda l, j: (l, 0, j)),
            pl.BlockSpec((None, 1, d), lambda l, j: (l, 0, j)),
        ],
        out_specs=pl.BlockSpec((None, bsz, d), lambda l, j: (l, 0, j)),
        out_shape=jax.ShapeDtypeStruct((depth, bsz, six_d), F32),
        compiler_params=_params("parallel", "parallel"),
        name="adaln_modulation",
    )(c, w_ada, b_ada.reshape(depth, 1, six_d))
    return out.reshape(depth, bsz, six_d // d, d)


def _inproj_kernel(x_ref, mod_ref, g_ref, wa_ref, wb_ref, wc_ref, wd_ref, qg_ref, kg_ref, bd_ref,
                   a_ref, y_ref, c_ref, d_ref):
    h = _modulated_norm(x_ref[...], g_ref[...], mod_ref[1:2, :], mod_ref[0:1, :]).astype(BF16)
    bd = bd_ref[...]
    a_all = _dot(h, wa_ref[...])
    for j in range(A_W // QK_W):
        cols = slice(j * QK_W, (j + 1) * QK_W)
        acc = a_all[:, cols]
        if j < 6:
            ms = _dot((acc * acc).astype(BF16), bd)
            gain = qg_ref[...] if j < 3 else kg_ref[...]
            acc = acc * lax.rsqrt(ms + NORM_EPS) * gain
        a_ref[:, cols] = acc.astype(BF16)
    u = _dot(h, wb_ref[...])
    y_ref[...] = u[:, :CONV_CH] * _sigmoid(u[:, CONV_CH:])
    c_ref[...] = _dot(h, wc_ref[...])
    d_all = _dot(h, wd_ref[...])
    d_ref[:, :QK_W] = (d_all[:, :QK_W] * (HEAD_LANES ** -0.5)).astype(BF16)
    d_ref[:, QK_W:] = d_all[:, QK_W:].astype(BF16)


def _in_projection(x2d, mod, g_mix, wa, wb, wc, wd, qg, kg, bd, seq):
    t = x2d.shape[0]
    tm = TOKEN_TILE
    per_b = seq // tm
    row = lambda w: pl.BlockSpec((tm, w), lambda i: (i, 0))
    return pl.pallas_call(
        _inproj_kernel,
        grid=(t // tm,),
        in_specs=[
            row(D_MODEL),
            pl.BlockSpec((None, 6, D_MODEL), lambda i: (i // per_b, 0, 0)),
            _const_spec((1, D_MODEL)),
            _const_spec(wa.shape), _const_spec(wb.shape), _const_spec(wc.shape), _const_spec(wd.shape),
            _const_spec((1, QK_W)), _const_spec((1, QK_W)), _const_spec((QK_W, QK_W)),
        ],
        out_specs=[row(A_W), row(CONV_CH), row(C_W), row(D_W)],
        out_shape=[
            jax.ShapeDtypeStruct((t, A_W), BF16),
            jax.ShapeDtypeStruct((t, CONV_CH), F32),
            jax.ShapeDtypeStruct((t, C_W), F32),
            jax.ShapeDtypeStruct((t, D_W), BF16),
        ],
        compiler_params=_params("parallel"),
        name="in_projection",
    )(x2d, mod, g_mix, wa, wb, wc, wd, qg, kg, bd)


def _dswa_group(position, span_idx, q_ref, kp_ref, kc_ref, vp_ref, vc_ref, o_ref, q_st, k_st, v_st, m_st, den_st, acc_st,
                *, slopes, window, dilation):
    blk, span = DSWA_BLOCK, DSWA_SPAN
    halo = blk * dilation
    shift = int(np.log2(dilation))
    for s in range(QK_W // LANE_TILE):
        lanes = slice(s * LANE_TILE, (s + 1) * LANE_TILE)
        q_st[s, :, :] = q_ref[:, lanes].astype(F32)
        k_st[s, 0:halo, :] = kp_ref[span - halo:span, lanes].astype(F32)
        k_st[s, halo:halo + span, :] = kc_ref[:, lanes].astype(F32)
        v_st[s, 0:halo, :] = vp_ref[span - halo:span, lanes].astype(F32)
        v_st[s, halo:halo + span, :] = vc_ref[:, lanes].astype(F32)
    head = _head_of_lane()
    qi = _iota((blk, 2 * blk), 0)
    kj = _iota((blk, 2 * blk), 1)
    rel = qi - kj + blk
    in_window = (rel >= 0) & (rel <= window // dilation)
    dist = (dilation * rel).astype(F32)
    bias = jnp.concatenate([jnp.where(in_window, -slopes[h] * dist, MASK_VALUE) for h in range(N_HEADS)], axis=0)
    first_block_penalty = jnp.where(_iota((1, 2 * blk), 1) < blk, MASK_VALUE, 0.0)

    def rows_of(ref, start, n):
        return jnp.concatenate([ref[s, pl.ds(start, n, stride=dilation), :] for s in range(QK_W // LANE_TILE)], axis=1)

    def put_rows(ref, start, val):
        for s in range(QK_W // LANE_TILE):
            ref[s, pl.ds(start, blk, stride=dilation), :] = val[:, s * LANE_TILE:(s + 1) * LANE_TILE]

    def step(it, carry):
        n = jnp.right_shift(it, shift)
        start = n * halo + jnp.bitwise_and(it, dilation - 1)
        q = rows_of(q_st, start, blk).astype(BF16)
        k = rows_of(k_st, start, 2 * blk).astype(BF16)
        v = rows_of(v_st, start, 2 * blk).astype(BF16)
        no_prev = jnp.where((span_idx == 0) & (n == 0), first_block_penalty, 0.0)
        q_stack = jnp.concatenate([jnp.where(head == h, q, jnp.zeros_like(q)) for h in range(N_HEADS)], axis=0)
        s = _dot_nt(q_stack, k) + bias + no_prev
        mx = jnp.max(s, axis=1, keepdims=True)
        p = jnp.exp(s - mx)
        den = jnp.sum(p, axis=1, keepdims=True)
        o_all = _dot(p.astype(BF16), v) * (1.0 / den)
        lse_all = mx + jnp.log(den)
        o_new = jnp.zeros((blk, QK_W), F32)
        lse = jnp.zeros((blk, QK_W), F32)
        for h in range(N_HEADS):
            in_head = head == h
            o_new = jnp.where(in_head, o_all[h * blk:(h + 1) * blk], o_new)
            lse = jnp.where(in_head, lse_all[h * blk:(h + 1) * blk], lse)
        if position == 0:
            m_new, den_new, acc_new = lse, jnp.ones_like(lse), o_new
        else:
            m_old = rows_of(m_st, start, blk)
            m_new = jnp.maximum(m_old, lse)
            w_old, w_new = jnp.exp(m_old - m_new), jnp.exp(lse - m_new)
            den_new = rows_of(den_st, start, blk) * w_old + w_new
            acc_new = rows_of(acc_st, start, blk) * w_old + o_new * w_new
        put_rows(m_st, start, m_new)
        put_rows(den_st, start, den_new)
        put_rows(acc_st, start, acc_new)
        return carry

    lax.fori_loop(0, span // blk, step, 0, unroll=16)
    if position == len(DSWA_PATTERNS) - 1:
        for s in range(QK_W // LANE_TILE):
            o_ref[:, s * LANE_TILE:(s + 1) * LANE_TILE] = (acc_st[s, :, :] / den_st[s, :, :]).astype(BF16)


def _dswa_kernel(q_ref, kp_ref, kc_ref, vp_ref, vc_ref, o_ref, *scratch, slopes):
    span_idx, position = pl.program_id(1), pl.program_id(2)
    for pos in range(len(DSWA_PATTERNS)):
        g = _dswa_group_at(pos)
        window, dilation = DSWA_PATTERNS[g]
        pl.when(position == pos)(functools.partial(
            _dswa_group, pos, span_idx, q_ref, kp_ref, kc_ref, vp_ref, vc_ref, o_ref, *scratch,
            slopes=slopes[g * N_HEADS:(g + 1) * N_HEADS], window=window, dilation=dilation))


def _dswa_group_at(position):
    return len(DSWA_PATTERNS) - 1 - position


def _alibi_slopes(n):
    return [float(np.float32(2.0 ** (-8.0 * i / n))) for i in range(1, n + 1)]


def _dilated_attention(a, bsz, seq):
    span = DSWA_SPAN
    a3 = a.reshape(bsz, seq, A_W)
    n_groups = len(DSWA_PATTERNS)
    col = lambda part, pos: part * n_groups + _dswa_group_at(pos)
    cur = lambda part: pl.BlockSpec((None, span, QK_W), lambda b, i, p: (b, i, col(part, p)))
    prev = lambda part: pl.BlockSpec((None, span, QK_W), lambda b, i, p: (b, jnp.maximum(i - 1, 0), col(part, p)))
    slab = lambda rows: pltpu.VMEM((QK_W // LANE_TILE, rows, LANE_TILE), F32)
    out = pl.pallas_call(
        functools.partial(_dswa_kernel, slopes=_alibi_slopes(n_groups * N_HEADS)),
        grid=(bsz, seq // span, n_groups),
        in_specs=[cur(0), prev(1), cur(1), prev(2), cur(2)],
        out_specs=pl.BlockSpec((None, span, QK_W), lambda b, i, g: (b, i, 0)),
        out_shape=jax.ShapeDtypeStruct((bsz, seq, QK_W), BF16),
        scratch_shapes=[slab(span), slab(2 * span), slab(2 * span), slab(span), slab(span), slab(span)],
        compiler_params=_params("parallel", "arbitrary", "arbitrary"),
        name="dilated_attention",
    )(a3, a3, a3, a3, a3)
    return out.reshape(bsz * seq, QK_W)


def _conv_kernel(prev_ref, cur_ref, w_ref, b_ref, lg_ref, lb_ref, o_ref, buf_ref):
    ts = cur_ref.shape[0]
    first = pl.program_id(1) == 0
    buf_ref[0, 0:CONV_HALO, :] = jnp.where(first, 0.0, prev_ref[...])
    buf_ref[0, CONV_HALO:, :] = cur_ref[...]
    rows = ts + CONV_HALO - SUBLANES
    for r in range(1, SUBLANES):
        buf_ref[r, 0:rows, :] = buf_ref[0, r:r + rows, :]
    off = CONV_HALO - (CONV_WIDTH - 1)
    acc = jnp.zeros((ts, CONV_CH), F32) + b_ref[...]
    for w in range(CONV_WIDTH):
        r, base = (off + w) % SUBLANES, (off + w) // SUBLANES * SUBLANES
        acc = acc + buf_ref[r, base:base + ts, :] * w_ref[w:w + 1, :]
    mu = jnp.mean(acc, axis=-1, keepdims=True)
    ctr = acc - mu
    var = jnp.mean(ctr * ctr, axis=-1, keepdims=True)
    y = ctr * lax.rsqrt(var + NORM_EPS) * lg_ref[...] + lb_ref[...]
    o_ref[...] = (y * _sigmoid(y)).astype(BF16)


def _conv_mixer(y, conv_w, conv_b, ln_g, ln_b, bsz, seq):
    ts = TOKEN_TILE
    per_halo = ts // CONV_HALO
    y3 = y.reshape(bsz, seq, CONV_CH)
    w_pad = jnp.zeros((CONV_HALO, CONV_CH), F32).at[:CONV_WIDTH].set(conv_w)
    out = pl.pallas_call(
        _conv_kernel,
        grid=(bsz, seq // ts),
        in_specs=[
            pl.BlockSpec((None, CONV_HALO, CONV_CH), lambda b, i: (b, jnp.maximum(i * per_halo - 1, 0), 0)),
            pl.BlockSpec((None, ts, CONV_CH), lambda b, i: (b, i, 0)),
            _const_spec((CONV_HALO, CONV_CH)), _const_spec((1, CONV_CH)), _const_spec((1, CONV_CH)),
            _const_spec((1, CONV_CH)),
        ],
        out_specs=pl.BlockSpec((None, ts, CONV_CH), lambda b, i: (b, i, 0)),
        out_shape=jax.ShapeDtypeStruct((bsz, seq, CONV_CH), BF16),
        scratch_shapes=[pltpu.VMEM((SUBLANES, ts + CONV_HALO, CONV_CH), F32)],
        compiler_params=_params("parallel", "parallel"),
        name="conv_mixer",
    )(y3, y3, w_pad, conv_b.reshape(1, -1), ln_g.reshape(1, -1), ln_b.reshape(1, -1))
    return out.reshape(bsz * seq, CONV_CH)


def _gla_kernel(q_ref, k_ref, v_ref, r_ref, low_ref, wg_ref, bg_ref, og_ref, tri_ref, am_ref, sm_ref, o_ref, st_ref):
    ch, sub = GLA_CHUNK, GLA_SUB
    tc = q_ref.shape[0]
    n_sub = ch // sub

    @pl.when(pl.program_id(1) == 0)
    def _():
        st_ref[...] = jnp.zeros_like(st_ref)

    head = _head_of_lane()
    key_row = _iota((ch, QK_W), 0)
    att_mask = am_ref[...]
    st_mask = sm_ref[...]

    logit = _dot(low_ref[...].astype(BF16), wg_ref[...]) + bg_ref[...]
    log_a = _neg_softplus(-logit) * (1.0 / GLA_TAU)
    la_hi, la_lo = _split_bf16(log_a)
    cr = GLA_CUM_ROWS
    cum_all = jnp.concatenate(
        [_dot(tri_ref[...], jnp.concatenate([la_hi[g * cr:(g + 1) * cr], la_lo[g * cr:(g + 1) * cr]], axis=0))
         for g in range(tc // cr)], axis=0)

    state = st_ref[...]
    for c in range(tc // ch):
        rows = slice(c * ch, (c + 1) * ch)
        cum = cum_all[rows, :]
        total = cum[ch - 1:ch, :]
        q = q_ref[rows, :] * (HEAD_LANES ** -0.5)
        k = k_ref[rows, :]
        v = v_ref[rows, :].astype(BF16)

        bases = [jnp.zeros((1, QK_W), F32)] + [cum[i * sub - 1:i * sub, :] for i in range(1, n_sub)]
        base_rows = jnp.concatenate([jnp.broadcast_to(b, (sub, QK_W)) for b in bases], axis=0)
        q_rel = q * jnp.exp(cum - base_rows)
        q_all = jnp.concatenate([jnp.where(head == h, q_rel, 0.0) for h in range(N_HEADS)], axis=0).astype(BF16)
        k_all = jnp.concatenate([jnp.where(key_row < (i + 1) * sub, k * jnp.exp(bases[i] - cum), 0.0)
                                 for i in range(n_sub)], axis=0).astype(BF16)
        att = (_dot_nt(q_all, k_all) * att_mask).astype(BF16)
        intra = _dot(att, jnp.concatenate([v] * n_sub, axis=0))
        o = jnp.concatenate([intra[h * ch:(h + 1) * ch, h * GLA_DV:(h + 1) * GLA_DV] for h in range(N_HEADS)], axis=1)

        o = o + _dot_nt((q * jnp.exp(cum)).astype(BF16), state.astype(BF16))
        k_dec = (k * jnp.exp(total - cum)).astype(BF16)
        state = state * jnp.exp(total) + _dot_tn(v, k_dec) * st_mask

        r = r_ref[rows, :]
        outs = []
        for h in range(N_HEADS):
            o_h = o[:, h * GLA_DV:(h + 1) * GLA_DV]
            ms = jnp.mean(o_h * o_h, axis=-1, keepdims=True)
            outs.append(o_h * lax.rsqrt(ms + NORM_EPS) * og_ref[...])
        o_ref[rows, :] = (r * _sigmoid(r) * jnp.concatenate(outs, axis=1)).astype(BF16)
    st_ref[...] = state


def _gla_masks(tc):
    ch, sub = GLA_CHUNK, GLA_SUB
    t = np.arange(tc)
    tri = (t[:, None] >= t[None, :]) & (t[:, None] // ch == t[None, :] // ch)
    a_t = np.arange(N_HEADS * ch) % ch
    a_col = np.arange((ch // sub) * ch)
    att = (a_col[None, :] // ch == a_t[:, None] // sub) & (a_col[None, :] % ch <= a_t[:, None])
    state = np.arange(GLA_V_W)[:, None] // GLA_DV == np.arange(QK_W)[None, :] // HEAD_LANES
    return (jnp.asarray(np.concatenate([tri, tri], axis=1), BF16), jnp.asarray(att, F32), jnp.asarray(state, F32))


def _gla_mixer(cbuf, w_gate, b_gate, o_gain, bsz, seq):
    tc = GLA_CHUNK * GLA_CHUNKS_PER_STEP
    c3 = cbuf.reshape(bsz, seq, C_W)
    wg = jnp.zeros((GLA_LOW_PAD, QK_W), F32).at[:GLA_RANK].set(w_gate).astype(BF16)
    tri2, att_mask, st_mask = _gla_masks(GLA_CUM_ROWS)
    spec = lambda w, j: pl.BlockSpec((None, tc, w), lambda b, i: (b, i, j))
    out = pl.pallas_call(
        _gla_kernel,
        grid=(bsz, seq // tc),
        in_specs=[
            spec(QK_W, 0), spec(QK_W, 1), spec(GLA_V_W, 1), spec(GLA_V_W, 2),
            spec(GLA_LOW_PAD, (2 * QK_W + 2 * GLA_V_W) // GLA_LOW_PAD),
            _const_spec((GLA_LOW_PAD, QK_W)), _const_spec((1, QK_W)), _const_spec((1, GLA_DV)),
            _const_spec(tri2.shape), _const_spec(att_mask.shape), _const_spec(st_mask.shape),
        ],
        out_specs=pl.BlockSpec((None, tc, GLA_V_W), lambda b, i: (b, i, 0)),
        out_shape=jax.ShapeDtypeStruct((bsz, seq, GLA_V_W), BF16),
        scratch_shapes=[pltpu.VMEM((GLA_V_W, QK_W), F32)],
        compiler_params=_params("parallel", "arbitrary"),
        name="gla_mixer",
    )(c3, c3, c3, c3, c3, wg, b_gate.reshape(1, -1), o_gain.reshape(1, -1), tri2, att_mask, st_mask)
    return out.reshape(bsz * seq, GLA_V_W)


def _sb_kernel(q_ref, k_ref, v_ref, o_ref, kmax_ref):
    t = SB_T
    seq = k_ref.shape[0]
    i = pl.program_id(1)
    head = _head_of_lane()
    head_ones = _ones_where(jnp.right_shift(_iota((QK_W, QK_W), 0), int(np.log2(HEAD_LANES))) == head)

    @pl.when(i == 0)
    def _():
        best = jnp.zeros((SB_NORM_ROWS, QK_W), F32)
        for c in range(seq // SB_NORM_ROWS):
            kf = k_ref[c * SB_NORM_ROWS:(c + 1) * SB_NORM_ROWS, :].astype(F32)
            sq_hi, sq_lo = _split_bf16(kf * kf)
            best = jnp.maximum(best, _dot(sq_hi, head_ones) + _dot(sq_lo, head_ones))
        kmax_ref[...] = jnp.zeros_like(kmax_ref) + jnp.max(best)

    q = q_ref[...]
    q_stack = jnp.concatenate([jnp.where(head == h, q, jnp.zeros_like(q)) for h in range(N_HEADS)], axis=0)
    qf = q_stack.astype(F32)
    z_bound = jnp.sqrt(jnp.sum(qf * qf, axis=1, keepdims=True) * kmax_ref[0:1, 0:1]) * SB_BOUND_SLACK
    tri2 = _ones_where(jnp.bitwise_and(_iota((2 * t, t), 0), t - 1) >= _iota((2 * t, t), 1))
    keep = _iota((N_HEADS * t, t), 1) < jnp.bitwise_and(_iota((N_HEADS * t, t), 0), t - 1)

    def block(j, acc, run, masked):
        start = pl.multiple_of(j * t, t)
        k_blk = k_ref[pl.ds(start, t), :]
        v_blk = v_ref[pl.ds(start, t), :]
        z = _dot_nt(q_stack, k_blk)
        drop = jnp.maximum(z, 0.0) + jnp.log(1.0 + jnp.exp2(jnp.abs(z) * (-LOG2_E)))
        if masked:
            drop = jnp.where(keep, drop, 0.0)
        d_hi, d_lo = _split_bf16(drop)
        suffix = _dot(jnp.concatenate([d_hi, d_lo], axis=1), tri2)
        w = jnp.exp(z - suffix - run)
        if masked:
            w = jnp.where(keep, w, 0.0)
        w = w.astype(BF16)
        w_wide = jnp.concatenate([w[h * t:(h + 1) * t, :] for h in range(N_HEADS)], axis=1)
        v_stack = jnp.concatenate([jnp.where(head == h, v_blk, jnp.zeros_like(v_blk)) for h in range(N_HEADS)], axis=0)
        acc = acc + _dot(w_wide, v_stack)
        run = run + jnp.sum(drop, axis=1, keepdims=True)
        done = (jnp.max(z_bound - run) < SB_ZERO_EXPONENT).astype(jnp.int32)
        return acc, run, done

    def body(c):
        j, _, acc, run = c
        acc, run, done = block(j, acc, run, False)
        return j - 1, done, acc, run

    def walk(first_blocks):
        acc, run, done = block(i, jnp.zeros((t, QK_W), F32), jnp.zeros((N_HEADS * t, 1), F32), True)
        for n in range(1, first_blocks):
            acc, run, done = block(i - n, acc, run, False)
        carry = lax.while_loop(lambda c: jnp.logical_and(c[0] >= 0, c[1] == 0), body, (i - first_blocks, done, acc, run))
        o_ref[...] = carry[2].astype(BF16)

    pl.when(i >= SB_STRAIGHT_BLOCKS - 1)(functools.partial(walk, SB_STRAIGHT_BLOCKS))
    pl.when(i < SB_STRAIGHT_BLOCKS - 1)(functools.partial(walk, 1))


def _stick_breaking_mixer(dbuf, bsz, seq):
    d3 = dbuf.reshape(bsz, seq, D_W)
    full = lambda j: pl.BlockSpec((None, seq, QK_W), lambda b, i: (b, 0, j))
    out = pl.pallas_call(
        _sb_kernel,
        grid=(bsz, seq // SB_T),
        in_specs=[pl.BlockSpec((None, SB_T, QK_W), lambda b, i: (b, i, 0)), full(1), full(2)],
        out_specs=pl.BlockSpec((None, SB_T, QK_W), lambda b, i: (b, i, 0)),
        out_shape=jax.ShapeDtypeStruct((bsz, seq, QK_W), BF16),
        scratch_shapes=[pltpu.VMEM((8, 128), F32)],
        compiler_params=_params("parallel", "arbitrary"),
        name="stick_breaking_mixer",
    )(d3, d3, d3)
    return out.reshape(bsz * seq, QK_W)


def _merge_kernel(x_ref, mod_ref, g_ref, oa_ref, ob_ref, oc_ref, od_ref,
                  wg_ref, wa_ref, wb_ref, wc_ref, wd_ref, wo_ref, out_ref):
    x = x_ref[...]
    h = _modulated_norm(x, g_ref[...], mod_ref[1:2, :], mod_ref[0:1, :]).astype(BF16)
    branches = ((oa_ref[...], wa_ref), (ob_ref[...], wb_ref), (oc_ref[...], wc_ref), (od_ref[...], wd_ref))
    merged = jnp.zeros(x.shape, F32)
    for j, (o_j, w_ref) in enumerate(branches):
        gate = _sigmoid(_dot(h, wg_ref[:, j * D_MODEL:(j + 1) * D_MODEL]))
        merged = merged + gate * _dot(o_j, w_ref[...])
    out_ref[...] = x + mod_ref[2:3, :] * _dot(merged.astype(BF16), wo_ref[...])


def _merge(x2d, mod, g_mix, oa, ob, oc, od, wg, wa, wb, wc, wd, wo, seq):
    t = x2d.shape[0]
    tm = TOKEN_TILE
    per_b = seq // tm
    row = lambda w: pl.BlockSpec((tm, w), lambda i: (i, 0))
    return pl.pallas_call(
        _merge_kernel,
        grid=(t // tm,),
        in_specs=[
            row(D_MODEL),
            pl.BlockSpec((None, 6, D_MODEL), lambda i: (i // per_b, 0, 0)),
            _const_spec((1, D_MODEL)),
            row(QK_W), row(CONV_CH), row(GLA_V_W), row(QK_W),
            _const_spec(wg.shape), _const_spec(wa.shape), _const_spec(wb.shape), _const_spec(wc.shape),
            _const_spec(wd.shape), _const_spec(wo.shape),
        ],
        out_specs=row(D_MODEL),
        out_shape=jax.ShapeDtypeStruct((t, D_MODEL), F32),
        compiler_params=_params("parallel"),
        name="gated_merge",
    )(x2d, mod, g_mix, oa, ob, oc, od, wg, wa, wb, wc, wd, wo)


def _mlp_kernel(x_ref, mod_ref, g_ref, wu_ref, wd_ref, out_ref):
    x = x_ref[...]
    h = _modulated_norm(x, g_ref[...], mod_ref[4:5, :], mod_ref[3:4, :]).astype(BF16)
    acc = jnp.zeros(x.shape, F32)
    for j in range(D_FF // D_MODEL):
        cols = slice(j * D_MODEL, (j + 1) * D_MODEL)
        u = jnp.maximum(_dot(h, wu_ref[:, cols]), 0.0)
        acc = acc + _dot((u * u).astype(BF16), wd_ref[cols, :])
    out_ref[...] = x + mod_ref[5:6, :] * acc


def _mlp(x2d, mod, g_mlp, wu, wd, seq):
    t = x2d.shape[0]
    tm = TOKEN_TILE
    per_b = seq // tm
    row = pl.BlockSpec((tm, D_MODEL), lambda i: (i, 0))
    return pl.pallas_call(
        _mlp_kernel,
        grid=(t // tm,),
        in_specs=[
            row,
            pl.BlockSpec((None, 6, D_MODEL), lambda i: (i // per_b, 0, 0)),
            _const_spec((1, D_MODEL)), _const_spec(wu.shape), _const_spec(wd.shape),
        ],
        out_specs=row,
        out_shape=jax.ShapeDtypeStruct((t, D_MODEL), F32),
        compiler_params=_params("parallel"),
        name="relu2_mlp",
    )(x2d, mod, g_mlp, wu, wd)


def _head_block_diag():
    idx = np.arange(QK_W) // HEAD_LANES
    return jnp.asarray((idx[:, None] == idx[None, :]).astype(np.float32) / HEAD_LANES, dtype=BF16)


def kernel(x, c, w_ada, b_ada, g_mix, w_in, q_gain, k_gain, conv_w, conv_b, conv_ln_g, conv_ln_b,
           gla_w_gate, gla_b_gate, gla_o_gain, w_br_a, w_br_b, w_br_c, w_br_d, w_out, g_mlp, w_up, w_down):
    bsz, seq, d = x.shape
    depth = w_ada.shape[0]
    assert d == D_MODEL and seq % (DSWA_BLOCK * DSWA_PATTERNS[-1][1]) == 0 and seq % TOKEN_TILE == 0
    mods = _modulation(c, w_ada, b_ada)
    bd = _head_block_diag()
    x2d = x.reshape(bsz * seq, d)
    qk_scale = HEAD_LANES ** -0.5
    for l in range(depth):
        w = w_in[l]
        o_conv = A_W
        o_c = o_conv + 2 * CONV_CH
        o_low = o_c + 2 * QK_W + 2 * GLA_V_W
        o_d = o_low + GLA_RANK
        o_gate = o_d + D_W
        wa = w[:, :A_W].astype(BF16)
        wb = w[:, o_conv:o_c].astype(BF16)
        wc = jnp.concatenate([w[:, o_c:o_low + GLA_RANK], jnp.zeros((d, GLA_LOW_PAD - GLA_RANK), F32)], axis=1).astype(BF16)
        wd = w[:, o_d:o_gate].astype(BF16)
        wg = w[:, o_gate:].astype(BF16)
        qg = jnp.tile(q_gain[l], N_HEADS).reshape(1, QK_W) * qk_scale
        kg = jnp.tile(k_gain[l], N_HEADS).reshape(1, QK_W)
        mod = mods[l]
        gm = g_mix[l].reshape(1, d)

        a, y, cbuf, dbuf = _in_projection(x2d, mod, gm, wa, wb, wc, wd, qg, kg, bd, seq)
        oa = _dilated_attention(a, bsz, seq)
        ob = _conv_mixer(y, conv_w[l], conv_b[l], conv_ln_g[l], conv_ln_b[l], bsz, seq)
        oc = _gla_mixer(cbuf, gla_w_gate[l], gla_b_gate[l], gla_o_gain[l], bsz, seq)
        od = _stick_breaking_mixer(dbuf, bsz, seq)
        x2d = _merge(x2d, mod, gm, oa, ob, oc, od, wg, w_br_a[l].astype(BF16), w_br_b[l].astype(BF16),
                     w_br_c[l].astype(BF16), w_br_d[l].astype(BF16), w_out[l].astype(BF16), seq)
        x2d = _mlp(x2d, mod, g_mlp[l].reshape(1, d), w_up[l].astype(BF16), w_down[l].astype(BF16), seq)
    return x2d.reshape(bsz, seq, d)
```

```python
import functools

import numpy as np
import jax
import jax.numpy as jnp
from jax import lax
from jax.experimental import pallas as pl
from jax.experimental.pallas import tpu as pltpu

F32 = jnp.float32
BF16 = jnp.bfloat16

D_MODEL = 1024
D_FF = 4 * D_MODEL
NORM_EPS = 1e-6
LOG2_E = 1.4426950408889634
HEAD_LANES = 64
N_HEADS = 4
QK_W = N_HEADS * HEAD_LANES
DSWA_PATTERNS = ((128, 1), (512, 4), (2048, 16))
DSWA_BLOCK = 128
DSWA_SPAN = DSWA_BLOCK * DSWA_PATTERNS[-1][1]
LANE_TILE = 128
SUBLANES = 8
CONV_CH = 256
CONV_WIDTH = 31
CONV_HALO = 32
GLA_DV = 128
GLA_V_W = N_HEADS * GLA_DV
GLA_RANK = 16
GLA_LOW_PAD = 128
GLA_TAU = 16.0
GLA_CHUNK = 64
GLA_SUB = 16
GLA_CHUNKS_PER_STEP = 16
GLA_CUM_ROWS = 256
SB_T = 256
SB_STRAIGHT_BLOCKS = 4
SB_NORM_ROWS = 512
SB_BOUND_SLACK = 1.02
SB_ZERO_EXPONENT = -110.0
TOKEN_TILE = 512
MASK_VALUE = -1e30
VMEM_LIMIT = 56 * 1024 * 1024

A_W = 3 * 3 * QK_W
C_W = 2 * QK_W + 2 * GLA_V_W + GLA_LOW_PAD
D_W = 3 * QK_W


def _dot(a, b):
    return jnp.dot(a, b, preferred_element_type=F32)


def _dot_nt(a, b):
    return lax.dot_general(a, b, (((1,), (1,)), ((), ())), preferred_element_type=F32)


def _dot_tn(a, b):
    return lax.dot_general(a, b, (((0,), (0,)), ((), ())), preferred_element_type=F32)


def _split_bf16(x):
    hi = x.astype(BF16)
    lo = (x - hi.astype(F32)).astype(BF16)
    return hi, lo


def _sigmoid(x):
    return 1.0 / (1.0 + jnp.exp(-x))


def _neg_softplus(x):
    return -(jnp.maximum(x, 0.0) + jnp.log(1.0 + jnp.exp(-jnp.abs(x))))


def _iota(shape, dim):
    return lax.broadcasted_iota(jnp.int32, shape, dim)


def _head_of_lane(width=QK_W, lanes_per_head=HEAD_LANES):
    return jnp.right_shift(_iota((1, width), 1), int(np.log2(lanes_per_head)))


def _ones_where(cond):
    return jnp.where(cond, 1.0, 0.0).astype(BF16)


def _const_spec(shape):
    nd = len(shape)
    return pl.BlockSpec(shape, lambda *_: (0,) * nd, pipeline_mode=pl.Buffered(1))


def _params(*sem):
    return pltpu.CompilerParams(dimension_semantics=sem, vmem_limit_bytes=VMEM_LIMIT)


def _modulated_norm(x, gain, scale, shift):
    ms = jnp.mean(x * x, axis=-1, keepdims=True)
    y = x * lax.rsqrt(ms + NORM_EPS) * gain
    return y * (1.0 + scale) + shift


def _mod_kernel(c_ref, w_ref, b_ref, o_ref):
    c_hi, c_lo = _split_bf16(c_ref[...])
    w_hi, w_lo = _split_bf16(w_ref[...])
    o_ref[...] = _dot(c_hi, w_hi) + _dot(c_hi, w_lo) + _dot(c_lo, w_hi) + b_ref[...]


def _modulation(c, w_ada, b_ada):
    depth, d, six_d = w_ada.shape
    bsz = c.shape[0]
    out = pl.pallas_call(
        _mod_kernel,
        grid=(depth, six_d // d),
        in_specs=[
            pl.BlockSpec((bsz, d), lambda l, j: (0, 0)),
            pl.BlockSpec((None, d, d), lambda l, j: (l, 0, j)),
            pl.BlockSpec((None, 1, d), lambda l, j: (l, 0, j)),
        ],
        out_specs=pl.BlockSpec((None, bsz, d), lambda l, j: (l, 0, j)),
        out_shape=jax.ShapeDtypeStruct((depth, bsz, six_d), F32),
        compiler_params=_params("parallel", "parallel"),
        name="adaln_modulation",
    )(c, w_ada, b_ada.reshape(depth, 1, six_d))
    return out.reshape(depth, bsz, six_d // d, d)


def _inproj_kernel(x_ref, mod_ref, g_ref, wabc_ref, wd_ref, qg_ref, kg_ref, bd_ref,
                   a_ref, y_ref, c_ref, d_ref):
    h = _modulated_norm(x_ref[...], g_ref[...], mod_ref[1:2, :], mod_ref[0:1, :]).astype(BF16)
    bd = bd_ref[...]
    a_all = _dot(h, wabc_ref[:, :A_W])
    for j in range(A_W // QK_W):
        cols = slice(j * QK_W, (j + 1) * QK_W)
        acc = a_all[:, cols]
        if j < 6:
            ms = _dot((acc * acc).astype(BF16), bd)
            gain = qg_ref[...] if j < 3 else kg_ref[...]
            acc = acc * lax.rsqrt(ms + NORM_EPS) * gain
        a_ref[:, cols] = acc.astype(BF16)
    u = _dot(h, wabc_ref[:, A_W:A_W + 2 * CONV_CH])
    y_ref[...] = u[:, :CONV_CH] * _sigmoid(u[:, CONV_CH:])
    c_ref[...] = _dot(h, wabc_ref[:, A_W + 2 * CONV_CH:])
    d_all = _dot(h, wd_ref[...])
    d_ref[:, :QK_W] = (d_all[:, :QK_W] * (HEAD_LANES ** -0.5)).astype(BF16)
    d_ref[:, QK_W:] = d_all[:, QK_W:].astype(BF16)


def _in_projection(x2d, mod, g_mix, wabc, wdg, qg, kg, bd, seq):
    t = x2d.shape[0]
    tm = TOKEN_TILE
    per_b = seq // tm
    row = lambda w: pl.BlockSpec((tm, w), lambda i: (i, 0))
    return pl.pallas_call(
        _inproj_kernel,
        grid=(t // tm,),
        in_specs=[
            row(D_MODEL),
            pl.BlockSpec((None, 6, D_MODEL), lambda i: (i // per_b, 0, 0)),
            _const_spec((1, D_MODEL)),
            _const_spec(wabc.shape),
            pl.BlockSpec((wdg.shape[0], D_W), lambda i: (0, 0), pipeline_mode=pl.Buffered(1)),
            _const_spec((1, QK_W)), _const_spec((1, QK_W)), _const_spec((QK_W, QK_W)),
        ],
        out_specs=[row(A_W), row(CONV_CH), row(C_W), row(D_W)],
        out_shape=[
            jax.ShapeDtypeStruct((t, A_W), BF16),
            jax.ShapeDtypeStruct((t, CONV_CH), F32),
            jax.ShapeDtypeStruct((t, C_W), F32),
            jax.ShapeDtypeStruct((t, D_W), BF16),
        ],
        compiler_params=_params("parallel"),
        name="in_projection",
    )(x2d, mod, g_mix, wabc, wdg, qg, kg, bd)


def _dswa_group(position, span_idx, q_ref, kp_ref, kc_ref, vp_ref, vc_ref, o_ref, q_st, k_st, v_st, m_st, den_st, acc_st,
                *, slopes, window, dilation):
    blk, span = DSWA_BLOCK, DSWA_SPAN
    halo = blk * dilation
    shift = int(np.log2(dilation))
    for s in range(QK_W // LANE_TILE):
        lanes = slice(s * LANE_TILE, (s + 1) * LANE_TILE)
        q_st[s, :, :] = q_ref[:, lanes].astype(F32)
        k_st[s, 0:halo, :] = kp_ref[span - halo:span, lanes].astype(F32)
        k_st[s, halo:halo + span, :] = kc_ref[:, lanes].astype(F32)
        v_st[s, 0:halo, :] = vp_ref[span - halo:span, lanes].astype(F32)
        v_st[s, halo:halo + span, :] = vc_ref[:, lanes].astype(F32)
    head = _head_of_lane()
    qi = _iota((blk, 2 * blk), 0)
    kj = _iota((blk, 2 * blk), 1)
    rel = qi - kj + blk
    in_window = (rel >= 0) & (rel <= window // dilation)
    dist = (dilation * rel).astype(F32)
    bias = jnp.concatenate([jnp.where(in_window, -slopes[h] * dist, MASK_VALUE) for h in range(N_HEADS)], axis=0)
    first_block_penalty = jnp.where(_iota((1, 2 * blk), 1) < blk, MASK_VALUE, 0.0)

    def rows_of(ref, start, n):
        return jnp.concatenate([ref[s, pl.ds(start, n, stride=dilation), :] for s in range(QK_W // LANE_TILE)], axis=1)

    def put_rows(ref, start, val):
        for s in range(QK_W // LANE_TILE):
            ref[s, pl.ds(start, blk, stride=dilation), :] = val[:, s * LANE_TILE:(s + 1) * LANE_TILE]

    def step(it, carry):
        n = jnp.right_shift(it, shift)
        start = n * halo + jnp.bitwise_and(it, dilation - 1)
        q = rows_of(q_st, start, blk).astype(BF16)
        k = rows_of(k_st, start, 2 * blk).astype(BF16)
        v = rows_of(v_st, start, 2 * blk).astype(BF16)
        no_prev = jnp.where((span_idx == 0) & (n == 0), first_block_penalty, 0.0)
        q_stack = jnp.concatenate([jnp.where(head == h, q, jnp.zeros_like(q)) for h in range(N_HEADS)], axis=0)
        s = _dot_nt(q_stack, k) + bias + no_prev
        mx = jnp.max(s, axis=1, keepdims=True)
        p = jnp.exp(s - mx)
        den = jnp.sum(p, axis=1, keepdims=True)
        o_all = _dot(p.astype(BF16), v) * (1.0 / den)
        lse_all = mx + jnp.log(den)
        o_new = jnp.zeros((blk, QK_W), F32)
        lse = jnp.zeros((blk, QK_W), F32)
        for h in range(N_HEADS):
            in_head = head == h
            o_new = jnp.where(in_head, o_all[h * blk:(h + 1) * blk], o_new)
            lse = jnp.where(in_head, lse_all[h * blk:(h + 1) * blk], lse)
        if position == 0:
            m_new, den_new, acc_new = lse, jnp.ones_like(lse), o_new
        else:
            m_old = rows_of(m_st, start, blk)
            m_new = jnp.maximum(m_old, lse)
            w_old, w_new = jnp.exp(m_old - m_new), jnp.exp(lse - m_new)
            den_new = rows_of(den_st, start, blk) * w_old + w_new
            acc_new = rows_of(acc_st, start, blk) * w_old + o_new * w_new
        put_rows(m_st, start, m_new)
        put_rows(den_st, start, den_new)
        put_rows(acc_st, start, acc_new)
        return carry

    lax.fori_loop(0, span // blk, step, 0, unroll=16)
    if position == len(DSWA_PATTERNS) - 1:
        for s in range(QK_W // LANE_TILE):
            o_ref[:, s * LANE_TILE:(s + 1) * LANE_TILE] = (acc_st[s, :, :] / den_st[s, :, :]).astype(BF16)


def _dswa_kernel(q_ref, kp_ref, kc_ref, vp_ref, vc_ref, o_ref, *scratch, slopes):
    span_idx, position = pl.program_id(1), pl.program_id(2)
    for pos in range(len(DSWA_PATTERNS)):
        g = _dswa_group_at(pos)
        window, dilation = DSWA_PATTERNS[g]
        pl.when(position == pos)(functools.partial(
            _dswa_group, pos, span_idx, q_ref, kp_ref, kc_ref, vp_ref, vc_ref, o_ref, *scratch,
            slopes=slopes[g * N_HEADS:(g + 1) * N_HEADS], window=window, dilation=dilation))


def _dswa_group_at(position):
    return len(DSWA_PATTERNS) - 1 - position


def _alibi_slopes(n):
    return [float(np.float32(2.0 ** (-8.0 * i / n))) for i in range(1, n + 1)]


def _dilated_attention(a, bsz, seq):
    span = DSWA_SPAN
    a3 = a.reshape(bsz, seq, A_W)
    n_groups = len(DSWA_PATTERNS)
    col = lambda part, pos: part * n_groups + _dswa_group_at(pos)
    cur = lambda part: pl.BlockSpec((None, span, QK_W), lambda b, i, p: (b, i, col(part, p)))
    prev = lambda part: pl.BlockSpec((None, span, QK_W), lambda b, i, p: (b, jnp.maximum(i - 1, 0), col(part, p)))
    slab = lambda rows: pltpu.VMEM((QK_W // LANE_TILE, rows, LANE_TILE), F32)
    out = pl.pallas_call(
        functools.partial(_dswa_kernel, slopes=_alibi_slopes(n_groups * N_HEADS)),
        grid=(bsz, seq // span, n_groups),
        in_specs=[cur(0), prev(1), cur(1), prev(2), cur(2)],
        out_specs=pl.BlockSpec((None, span, QK_W), lambda b, i, g: (b, i, 0)),
        out_shape=jax.ShapeDtypeStruct((bsz, seq, QK_W), BF16),
        scratch_shapes=[slab(span), slab(2 * span), slab(2 * span), slab(span), slab(span), slab(span)],
        compiler_params=_params("parallel", "arbitrary", "arbitrary"),
        name="dilated_attention",
    )(a3, a3, a3, a3, a3)
    return out.reshape(bsz * seq, QK_W)


def _conv_kernel(prev_ref, cur_ref, w_ref, b_ref, lg_ref, lb_ref, o_ref, buf_ref):
    ts = cur_ref.shape[0]
    first = pl.program_id(1) == 0
    buf_ref[0, 0:CONV_HALO, :] = jnp.where(first, 0.0, prev_ref[...])
    buf_ref[0, CONV_HALO:, :] = cur_ref[...]
    rows = ts + CONV_HALO - SUBLANES
    for r in range(1, SUBLANES):
        buf_ref[r, 0:rows, :] = buf_ref[0, r:r + rows, :]
    off = CONV_HALO - (CONV_WIDTH - 1)
    acc = jnp.zeros((ts, CONV_CH), F32) + b_ref[...]
    for w in range(CONV_WIDTH):
        r, base = (off + w) % SUBLANES, (off + w) // SUBLANES * SUBLANES
        acc = acc + buf_ref[r, base:base + ts, :] * w_ref[w:w + 1, :]
    mu = jnp.mean(acc, axis=-1, keepdims=True)
    ctr = acc - mu
    var = jnp.mean(ctr * ctr, axis=-1, keepdims=True)
    y = ctr * lax.rsqrt(var + NORM_EPS) * lg_ref[...] + lb_ref[...]
    o_ref[...] = (y * _sigmoid(y)).astype(BF16)


def _conv_mixer(y, conv_w, conv_b, ln_g, ln_b, bsz, seq):
    ts = TOKEN_TILE
    per_halo = ts // CONV_HALO
    y3 = y.reshape(bsz, seq, CONV_CH)
    w_pad = jnp.zeros((CONV_HALO, CONV_CH), F32).at[:CONV_WIDTH].set(conv_w)
    out = pl.pallas_call(
        _conv_kernel,
        grid=(bsz, seq // ts),
        in_specs=[
            pl.BlockSpec((None, CONV_HALO, CONV_CH), lambda b, i: (b, jnp.maximum(i * per_halo - 1, 0), 0)),
            pl.BlockSpec((None, ts, CONV_CH), lambda b, i: (b, i, 0)),
            _const_spec((CONV_HALO, CONV_CH)), _const_spec((1, CONV_CH)), _const_spec((1, CONV_CH)),
            _const_spec((1, CONV_CH)),
        ],
        out_specs=pl.BlockSpec((None, ts, CONV_CH), lambda b, i: (b, i, 0)),
        out_shape=jax.ShapeDtypeStruct((bsz, seq, CONV_CH), BF16),
        scratch_shapes=[pltpu.VMEM((SUBLANES, ts + CONV_HALO, CONV_CH), F32)],
        compiler_params=_params("parallel", "parallel"),
        name="conv_mixer",
    )(y3, y3, w_pad, conv_b.reshape(1, -1), ln_g.reshape(1, -1), ln_b.reshape(1, -1))
    return out.reshape(bsz * seq, CONV_CH)


def _gla_kernel(q_ref, k_ref, v_ref, r_ref, low_ref, wg_ref, bg_ref, og_ref, tri_ref, am_ref, sm_ref, o_ref, st_ref):
    ch, sub = GLA_CHUNK, GLA_SUB
    tc = q_ref.shape[0]
    n_sub = ch // sub

    @pl.when(pl.program_id(1) == 0)
    def _():
        st_ref[...] = jnp.zeros_like(st_ref)

    head = _head_of_lane()
    key_row = _iota((ch, QK_W), 0)
    att_mask = am_ref[...]
    st_mask = sm_ref[...]

    logit = _dot(low_ref[...].astype(BF16), wg_ref[...]) + bg_ref[...]
    log_a = _neg_softplus(-logit) * (1.0 / GLA_TAU)
    la_hi, la_lo = _split_bf16(log_a)
    cr = GLA_CUM_ROWS
    cum_all = jnp.concatenate(
        [_dot(tri_ref[...], jnp.concatenate([la_hi[g * cr:(g + 1) * cr], la_lo[g * cr:(g + 1) * cr]], axis=0))
         for g in range(tc // cr)], axis=0)

    state = st_ref[...]
    for c in range(tc // ch):
        rows = slice(c * ch, (c + 1) * ch)
        cum = cum_all[rows, :]
        total = cum[ch - 1:ch, :]
        q = q_ref[rows, :] * (HEAD_LANES ** -0.5)
        k = k_ref[rows, :]
        v = v_ref[rows, :].astype(BF16)

        bases = [jnp.zeros((1, QK_W), F32)] + [cum[i * sub - 1:i * sub, :] for i in range(1, n_sub)]
        base_rows = jnp.concatenate([jnp.broadcast_to(b, (sub, QK_W)) for b in bases], axis=0)
        q_rel = q * jnp.exp(cum - base_rows)
        q_all = jnp.concatenate([jnp.where(head == h, q_rel, 0.0) for h in range(N_HEADS)], axis=0).astype(BF16)
        k_all = jnp.concatenate([jnp.where(key_row < (i + 1) * sub, k * jnp.exp(bases[i] - cum), 0.0)
                                 for i in range(n_sub)], axis=0).astype(BF16)
        att = (_dot_nt(q_all, k_all) * att_mask).astype(BF16)
        intra = _dot(att, jnp.concatenate([v] * n_sub, axis=0))
        o = jnp.concatenate([intra[h * ch:(h + 1) * ch, h * GLA_DV:(h + 1) * GLA_DV] for h in range(N_HEADS)], axis=1)

        o = o + _dot_nt((q * jnp.exp(cum)).astype(BF16), state.astype(BF16))
        k_dec = (k * jnp.exp(total - cum)).astype(BF16)
        state = state * jnp.exp(total) + _dot_tn(v, k_dec) * st_mask

        r = r_ref[rows, :]
        outs = []
        for h in range(N_HEADS):
            o_h = o[:, h * GLA_DV:(h + 1) * GLA_DV]
            ms = jnp.mean(o_h * o_h, axis=-1, keepdims=True)
            outs.append(o_h * lax.rsqrt(ms + NORM_EPS) * og_ref[...])
        o_ref[rows, :] = (r * _sigmoid(r) * jnp.concatenate(outs, axis=1)).astype(BF16)
    st_ref[...] = state


def _gla_masks(tc):
    ch, sub = GLA_CHUNK, GLA_SUB
    t = np.arange(tc)
    tri = (t[:, None] >= t[None, :]) & (t[:, None] // ch == t[None, :] // ch)
    a_t = np.arange(N_HEADS * ch) % ch
    a_col = np.arange((ch // sub) * ch)
    att = (a_col[None, :] // ch == a_t[:, None] // sub) & (a_col[None, :] % ch <= a_t[:, None])
    state = np.arange(GLA_V_W)[:, None] // GLA_DV == np.arange(QK_W)[None, :] // HEAD_LANES
    return (jnp.asarray(np.concatenate([tri, tri], axis=1), BF16), jnp.asarray(att, F32), jnp.asarray(state, F32))


def _gla_mixer(cbuf, w_gate, b_gate, o_gain, bsz, seq):
    tc = GLA_CHUNK * GLA_CHUNKS_PER_STEP
    c3 = cbuf.reshape(bsz, seq, C_W)
    wg = jnp.zeros((GLA_LOW_PAD, QK_W), F32).at[:GLA_RANK].set(w_gate).astype(BF16)
    tri2, att_mask, st_mask = _gla_masks(GLA_CUM_ROWS)
    spec = lambda w, j: pl.BlockSpec((None, tc, w), lambda b, i: (b, i, j))
    out = pl.pallas_call(
        _gla_kernel,
        grid=(bsz, seq // tc),
        in_specs=[
            spec(QK_W, 0), spec(QK_W, 1), spec(GLA_V_W, 1), spec(GLA_V_W, 2),
            spec(GLA_LOW_PAD, (2 * QK_W + 2 * GLA_V_W) // GLA_LOW_PAD),
            _const_spec((GLA_LOW_PAD, QK_W)), _const_spec((1, QK_W)), _const_spec((1, GLA_DV)),
            _const_spec(tri2.shape), _const_spec(att_mask.shape), _const_spec(st_mask.shape),
        ],
        out_specs=pl.BlockSpec((None, tc, GLA_V_W), lambda b, i: (b, i, 0)),
        out_shape=jax.ShapeDtypeStruct((bsz, seq, GLA_V_W), BF16),
        scratch_shapes=[pltpu.VMEM((GLA_V_W, QK_W), F32)],
        compiler_params=_params("parallel", "arbitrary"),
        name="gla_mixer",
    )(c3, c3, c3, c3, c3, wg, b_gate.reshape(1, -1), o_gain.reshape(1, -1), tri2, att_mask, st_mask)
    return out.reshape(bsz * seq, GLA_V_W)


def _sb_kernel(q_ref, k_ref, v_ref, o_ref, kmax_ref):
    t = SB_T
    seq = k_ref.shape[0]
    i = pl.program_id(1)
    head = _head_of_lane()
    head_ones = _ones_where(jnp.right_shift(_iota((QK_W, QK_W), 0), int(np.log2(HEAD_LANES))) == head)

    @pl.when(i == 0)
    def _():
        best = jnp.zeros((SB_NORM_ROWS, QK_W), F32)
        for c in range(seq // SB_NORM_ROWS):
            kf = k_ref[c * SB_NORM_ROWS:(c + 1) * SB_NORM_ROWS, :].astype(F32)
            sq_hi, sq_lo = _split_bf16(kf * kf)
            best = jnp.maximum(best, _dot(sq_hi, head_ones) + _dot(sq_lo, head_ones))
        kmax_ref[...] = jnp.zeros_like(kmax_ref) + jnp.max(best)

    q = q_ref[...]
    q_stack = jnp.concatenate([jnp.where(head == h, q, jnp.zeros_like(q)) for h in range(N_HEADS)], axis=0)
    qf = q_stack.astype(F32)
    z_bound = jnp.sqrt(jnp.sum(qf * qf, axis=1, keepdims=True) * kmax_ref[0:1, 0:1]) * SB_BOUND_SLACK
    tri2 = _ones_where(jnp.bitwise_and(_iota((2 * t, t), 0), t - 1) >= _iota((2 * t, t), 1))
    keep = _iota((N_HEADS * t, t), 1) < jnp.bitwise_and(_iota((N_HEADS * t, t), 0), t - 1)

    def block(j, acc, run, masked):
        start = pl.multiple_of(j * t, t)
        k_blk = k_ref[pl.ds(start, t), :]
        v_blk = v_ref[pl.ds(start, t), :]
        z = _dot_nt(q_stack, k_blk)
        drop = jnp.maximum(z, 0.0) + jnp.log(1.0 + jnp.exp2(jnp.abs(z) * (-LOG2_E)))
        if masked:
            drop = jnp.where(keep, drop, 0.0)
        d_hi, d_lo = _split_bf16(drop)
        suffix = _dot(jnp.concatenate([d_hi, d_lo], axis=1), tri2)
        w = jnp.exp(z - suffix - run)
        if masked:
            w = jnp.where(keep, w, 0.0)
        w = w.astype(BF16)
        w_wide = jnp.concatenate([w[h * t:(h + 1) * t, :] for h in range(N_HEADS)], axis=1)
        v_stack = jnp.concatenate([jnp.where(head == h, v_blk, jnp.zeros_like(v_blk)) for h in range(N_HEADS)], axis=0)
        acc = acc + _dot(w_wide, v_stack)
        run = run + jnp.sum(drop, axis=1, keepdims=True)
        done = (jnp.max(z_bound - run) < SB_ZERO_EXPONENT).astype(jnp.int32)
        return acc, run, done

    def body(c):
        j, _, acc, run = c
        acc, run, done = block(j, acc, run, False)
        return j - 1, done, acc, run

    def walk(first_blocks):
        acc, run, done = block(i, jnp.zeros((t, QK_W), F32), jnp.zeros((N_HEADS * t, 1), F32), True)
        for n in range(1, first_blocks):
            acc, run, done = block(i - n, acc, run, False)
        carry = lax.while_loop(lambda c: jnp.logical_and(c[0] >= 0, c[1] == 0), body, (i - first_blocks, done, acc, run))
        o_ref[...] = carry[2].astype(BF16)

    pl.when(i >= SB_STRAIGHT_BLOCKS - 1)(functools.partial(walk, SB_STRAIGHT_BLOCKS))
    pl.when(i < SB_STRAIGHT_BLOCKS - 1)(functools.partial(walk, 1))


def _stick_breaking_mixer(dbuf, bsz, seq):
    d3 = dbuf.reshape(bsz, seq, D_W)
    full = lambda j: pl.BlockSpec((None, seq, QK_W), lambda b, i: (b, 0, j))
    out = pl.pallas_call(
        _sb_kernel,
        grid=(bsz, seq // SB_T),
        in_specs=[pl.BlockSpec((None, SB_T, QK_W), lambda b, i: (b, i, 0)), full(1), full(2)],
        out_specs=pl.BlockSpec((None, SB_T, QK_W), lambda b, i: (b, i, 0)),
        out_shape=jax.ShapeDtypeStruct((bsz, seq, QK_W), BF16),
        scratch_shapes=[pltpu.VMEM((8, 128), F32)],
        compiler_params=_params("parallel", "arbitrary"),
        name="stick_breaking_mixer",
    )(d3, d3, d3)
    return out.reshape(bsz * seq, QK_W)


def _merge_kernel(x_ref, mod_ref, g_ref, oa_ref, ob_ref, oc_ref, od_ref,
                  wg_ref, wa_ref, wb_ref, wc_ref, wd_ref, wo_ref, out_ref):
    x = x_ref[...]
    h = _modulated_norm(x, g_ref[...], mod_ref[1:2, :], mod_ref[0:1, :]).astype(BF16)
    branches = ((oa_ref[...], wa_ref), (ob_ref[...], wb_ref), (oc_ref[...], wc_ref), (od_ref[...], wd_ref))
    merged = jnp.zeros(x.shape, F32)
    for j, (o_j, w_ref) in enumerate(branches):
        gate = _sigmoid(_dot(h, wg_ref[:, D_W + j * D_MODEL:D_W + (j + 1) * D_MODEL]))
        merged = merged + gate * _dot(o_j, w_ref[...])
    out_ref[...] = x + mod_ref[2:3, :] * _dot(merged.astype(BF16), wo_ref[...])


def _merge(x2d, mod, g_mix, oa, ob, oc, od, wg, wa, wb, wc, wd, wo, seq):
    t = x2d.shape[0]
    tm = TOKEN_TILE
    per_b = seq // tm
    row = lambda w: pl.BlockSpec((tm, w), lambda i: (i, 0))
    return pl.pallas_call(
        _merge_kernel,
        grid=(t // tm,),
        in_specs=[
            row(D_MODEL),
            pl.BlockSpec((None, 6, D_MODEL), lambda i: (i // per_b, 0, 0)),
            _const_spec((1, D_MODEL)),
            row(QK_W), row(CONV_CH), row(GLA_V_W), row(QK_W),
            _const_spec(wg.shape), _const_spec(wa.shape), _const_spec(wb.shape), _const_spec(wc.shape),
            _const_spec(wd.shape), _const_spec(wo.shape),
        ],
        out_specs=row(D_MODEL),
        out_shape=jax.ShapeDtypeStruct((t, D_MODEL), F32),
        compiler_params=_params("parallel"),
        name="gated_merge",
    )(x2d, mod, g_mix, oa, ob, oc, od, wg, wa, wb, wc, wd, wo)


def _mlp_kernel(x_ref, mod_ref, g_ref, wu_ref, wd_ref, out_ref):
    x = x_ref[...]
    h = _modulated_norm(x, g_ref[...], mod_ref[4:5, :], mod_ref[3:4, :]).astype(BF16)
    acc = jnp.zeros(x.shape, F32)
    for j in range(D_FF // D_MODEL):
        cols = slice(j * D_MODEL, (j + 1) * D_MODEL)
        u = jnp.maximum(_dot(h, wu_ref[:, cols]), 0.0)
        acc = acc + _dot((u * u).astype(BF16), wd_ref[cols, :])
    out_ref[...] = x + mod_ref[5:6, :] * acc


def _mlp(x2d, mod, g_mlp, wu, wd, seq):
    t = x2d.shape[0]
    tm = TOKEN_TILE
    per_b = seq // tm
    row = pl.BlockSpec((tm, D_MODEL), lambda i: (i, 0))
    return pl.pallas_call(
        _mlp_kernel,
        grid=(t // tm,),
        in_specs=[
            row,
            pl.BlockSpec((None, 6, D_MODEL), lambda i: (i // per_b, 0, 0)),
            _const_spec((1, D_MODEL)), _const_spec(wu.shape), _const_spec(wd.shape),
        ],
        out_specs=row,
        out_shape=jax.ShapeDtypeStruct((t, D_MODEL), F32),
        compiler_params=_params("parallel"),
        name="relu2_mlp",
    )(x2d, mod, g_mlp, wu, wd)


def _head_block_diag():
    idx = np.arange(QK_W) // HEAD_LANES
    return jnp.asarray((idx[:, None] == idx[None, :]).astype(np.float32) / HEAD_LANES, dtype=BF16)


def kernel(x, c, w_ada, b_ada, g_mix, w_in, q_gain, k_gain, conv_w, conv_b, conv_ln_g, conv_ln_b,
           gla_w_gate, gla_b_gate, gla_o_gain, w_br_a, w_br_b, w_br_c, w_br_d, w_out, g_mlp, w_up, w_down):
    bsz, seq, d = x.shape
    depth = w_ada.shape[0]
    assert d == D_MODEL and seq % (DSWA_BLOCK * DSWA_PATTERNS[-1][1]) == 0 and seq % TOKEN_TILE == 0
    mods = _modulation(c, w_ada, b_ada)
    bd = _head_block_diag()
    x2d = x.reshape(bsz * seq, d)
    qk_scale = HEAD_LANES ** -0.5
    for l in range(depth):
        w = w_in[l]
        o_d = A_W + 2 * CONV_CH + 2 * QK_W + 2 * GLA_V_W + GLA_RANK
        wabc = jnp.pad(w[:, :o_d], ((0, 0), (0, GLA_LOW_PAD - GLA_RANK))).astype(BF16)
        wdg = w[:, o_d:].astype(BF16)
        qg = jnp.tile(q_gain[l], N_HEADS).reshape(1, QK_W) * qk_scale
        kg = jnp.tile(k_gain[l], N_HEADS).reshape(1, QK_W)
        mod = mods[l]
        gm = g_mix[l].reshape(1, d)

        a, y, cbuf, dbuf = _in_projection(x2d, mod, gm, wabc, wdg, qg, kg, bd, seq)
        oa = _dilated_attention(a, bsz, seq)
        ob = _conv_mixer(y, conv_w[l], conv_b[l], conv_ln_g[l], conv_ln_b[l], bsz, seq)
        oc = _gla_mixer(cbuf, gla_w_gate[l], gla_b_gate[l], gla_o_gain[l], bsz, seq)
        od = _stick_breaking_mixer(dbuf, bsz, seq)
        x2d = _merge(x2d, mod, gm, oa, ob, oc, od, wdg, w_br_a[l].astype(BF16), w_br_b[l].astype(BF16),
                     w_br_c[l].astype(BF16), w_br_d[l].astype(BF16), w_out[l].astype(BF16), seq)
        x2d = _mlp(x2d, mod, g_mlp[l].reshape(1, d), w_up[l].astype(BF16), w_down[l].astype(BF16), seq)
    return x2d.reshape(bsz, seq, d)
```

```python
import functools

import numpy as np
import jax
import jax.numpy as jnp
from jax import lax
from jax.experimental import pallas as pl
from jax.experimental.pallas import tpu as pltpu

F32 = jnp.float32
BF16 = jnp.bfloat16

D_MODEL = 1024
D_FF = 4 * D_MODEL
NORM_EPS = 1e-6
LOG2_E = 1.4426950408889634
HEAD_LANES = 64
N_HEADS = 4
QK_W = N_HEADS * HEAD_LANES
DSWA_PATTERNS = ((128, 1), (512, 4), (2048, 16))
DSWA_BLOCK = 128
DSWA_SPAN = DSWA_BLOCK * DSWA_PATTERNS[-1][1]
LANE_TILE = 128
SUBLANES = 8
CONV_CH = 256
CONV_WIDTH = 31
CONV_HALO = 32
GLA_DV = 128
GLA_V_W = N_HEADS * GLA_DV
GLA_RANK = 16
GLA_LOW_PAD = 128
GLA_TAU = 16.0
GLA_CHUNK = 64
GLA_SUB = 16
GLA_CHUNKS_PER_STEP = 16
GLA_CUM_ROWS = 256
SB_T = 256
SB_QBLOCKS = 4
SB_STRAIGHT_BLOCKS = 4
SB_NORM_ROWS = 512
SB_BOUND_SLACK = 1.02
SB_ZERO_EXPONENT = -110.0
TOKEN_TILE = 512
MASK_VALUE = -1e30
VMEM_LIMIT = 56 * 1024 * 1024

A_W = 3 * 3 * QK_W
C_W = 2 * QK_W + 2 * GLA_V_W + GLA_LOW_PAD
D_W = 3 * QK_W


def _dot(a, b):
    return jnp.dot(a, b, preferred_element_type=F32)


def _dot_nt(a, b):
    return lax.dot_general(a, b, (((1,), (1,)), ((), ())), preferred_element_type=F32)


def _dot_tn(a, b):
    return lax.dot_general(a, b, (((0,), (0,)), ((), ())), preferred_element_type=F32)


def _split_bf16(x):
    hi = x.astype(BF16)
    lo = (x - hi.astype(F32)).astype(BF16)
    return hi, lo


def _sigmoid(x):
    return 1.0 / (1.0 + jnp.exp(-x))


def _neg_softplus(x):
    return -(jnp.maximum(x, 0.0) + jnp.log(1.0 + jnp.exp(-jnp.abs(x))))


def _iota(shape, dim):
    return lax.broadcasted_iota(jnp.int32, shape, dim)


def _head_of_lane(width=QK_W, lanes_per_head=HEAD_LANES):
    return jnp.right_shift(_iota((1, width), 1), int(np.log2(lanes_per_head)))


def _ones_where(cond):
    return jnp.where(cond, 1.0, 0.0).astype(BF16)


def _const_spec(shape):
    nd = len(shape)
    return pl.BlockSpec(shape, lambda *_: (0,) * nd, pipeline_mode=pl.Buffered(1))


def _params(*sem):
    return pltpu.CompilerParams(dimension_semantics=sem, vmem_limit_bytes=VMEM_LIMIT)


def _modulated_norm(x, gain, scale, shift):
    ms = jnp.mean(x * x, axis=-1, keepdims=True)
    y = x * lax.rsqrt(ms + NORM_EPS) * gain
    return y * (1.0 + scale) + shift


def _mod_kernel(c_ref, w_ref, b_ref, o_ref):
    c_hi, c_lo = _split_bf16(c_ref[...])
    w_hi, w_lo = _split_bf16(w_ref[...])
    o_ref[...] = _dot(c_hi, w_hi) + _dot(c_hi, w_lo) + _dot(c_lo, w_hi) + b_ref[...]


def _modulation(c, w_ada, b_ada):
    depth, d, six_d = w_ada.shape
    bsz = c.shape[0]
    out = pl.pallas_call(
        _mod_kernel,
        grid=(depth, six_d // d),
        in_specs=[
            pl.BlockSpec((bsz, d), lambda l, j: (0, 0)),
            pl.BlockSpec((None, d, d), lambda l, j: (l, 0, j)),
            pl.BlockSpec((None, 1, d), lambda l, j: (l, 0, j)),
        ],
        out_specs=pl.BlockSpec((None, bsz, d), lambda l, j: (l, 0, j)),
        out_shape=jax.ShapeDtypeStruct((depth, bsz, six_d), F32),
        compiler_params=_params("parallel", "parallel"),
        name="adaln_modulation",
    )(c, w_ada, b_ada.reshape(depth, 1, six_d))
    return out.reshape(depth, bsz, six_d // d, d)


def _inproj_kernel(x_ref, mod_ref, g_ref, wa_ref, wb_ref, wc_ref, wd_ref, qg_ref, kg_ref, bd_ref,
                   a_ref, y_ref, c_ref, d_ref):
    h = _modulated_norm(x_ref[...], g_ref[...], mod_ref[1:2, :], mod_ref[0:1, :]).astype(BF16)
    bd = bd_ref[...]
    a_all = _dot(h, wa_ref[...])
    for j in range(A_W // QK_W):
        cols = slice(j * QK_W, (j + 1) * QK_W)
        acc = a_all[:, cols]
        if j < 6:
            ms = _dot((acc * acc).astype(BF16), bd)
            gain = qg_ref[...] if j < 3 else kg_ref[...]
            acc = acc * lax.rsqrt(ms + NORM_EPS) * gain
        a_ref[:, cols] = acc.astype(BF16)
    u = _dot(h, wb_ref[...])
    y_ref[...] = u[:, :CONV_CH] * _sigmoid(u[:, CONV_CH:])
    c_ref[...] = _dot(h, wc_ref[...])
    d_all = _dot(h, wd_ref[...])
    d_ref[:, :QK_W] = (d_all[:, :QK_W] * (HEAD_LANES ** -0.5)).astype(BF16)
    d_ref[:, QK_W:] = d_all[:, QK_W:].astype(BF16)


def _in_projection(x2d, mod, g_mix, wa, wb, wc, wd, qg, kg, bd, seq):
    t = x2d.shape[0]
    tm = TOKEN_TILE
    per_b = seq // tm
    row = lambda w: pl.BlockSpec((tm, w), lambda i: (i, 0))
    return pl.pallas_call(
        _inproj_kernel,
        grid=(t // tm,),
        in_specs=[
            row(D_MODEL),
            pl.BlockSpec((None, 6, D_MODEL), lambda i: (i // per_b, 0, 0)),
            _const_spec((1, D_MODEL)),
            _const_spec(wa.shape), _const_spec(wb.shape), _const_spec(wc.shape), _const_spec(wd.shape),
            _const_spec((1, QK_W)), _const_spec((1, QK_W)), _const_spec((QK_W, QK_W)),
        ],
        out_specs=[row(A_W), row(CONV_CH), row(C_W), row(D_W)],
        out_shape=[
            jax.ShapeDtypeStruct((t, A_W), BF16),
            jax.ShapeDtypeStruct((t, CONV_CH), F32),
            jax.ShapeDtypeStruct((t, C_W), F32),
            jax.ShapeDtypeStruct((t, D_W), BF16),
        ],
        compiler_params=_params("parallel"),
        name="in_projection",
    )(x2d, mod, g_mix, wa, wb, wc, wd, qg, kg, bd)


def _dswa_group(position, span_idx, q_ref, kp_ref, kc_ref, vp_ref, vc_ref, o_ref, q_st, k_st, v_st, m_st, den_st, acc_st,
                *, slopes, window, dilation):
    blk, span = DSWA_BLOCK, DSWA_SPAN
    halo = blk * dilation
    shift = int(np.log2(dilation))
    for s in range(QK_W // LANE_TILE):
        lanes = slice(s * LANE_TILE, (s + 1) * LANE_TILE)
        q_st[s, :, :] = q_ref[:, lanes].astype(F32)
        k_st[s, 0:halo, :] = kp_ref[span - halo:span, lanes].astype(F32)
        k_st[s, halo:halo + span, :] = kc_ref[:, lanes].astype(F32)
        v_st[s, 0:halo, :] = vp_ref[span - halo:span, lanes].astype(F32)
        v_st[s, halo:halo + span, :] = vc_ref[:, lanes].astype(F32)
    head = _head_of_lane()
    qi = _iota((blk, 2 * blk), 0)
    kj = _iota((blk, 2 * blk), 1)
    rel = qi - kj + blk
    in_window = (rel >= 0) & (rel <= window // dilation)
    dist = (dilation * rel).astype(F32)
    bias = jnp.concatenate([jnp.where(in_window, -slopes[h] * dist, MASK_VALUE) for h in range(N_HEADS)], axis=0)
    first_block_penalty = jnp.where(_iota((1, 2 * blk), 1) < blk, MASK_VALUE, 0.0)

    def rows_of(ref, start, n):
        return jnp.concatenate([ref[s, pl.ds(start, n, stride=dilation), :] for s in range(QK_W // LANE_TILE)], axis=1)

    def put_rows(ref, start, val):
        for s in range(QK_W // LANE_TILE):
            ref[s, pl.ds(start, blk, stride=dilation), :] = val[:, s * LANE_TILE:(s + 1) * LANE_TILE]

    def step(it, carry):
        n = jnp.right_shift(it, shift)
        start = n * halo + jnp.bitwise_and(it, dilation - 1)
        q = rows_of(q_st, start, blk).astype(BF16)
        k = rows_of(k_st, start, 2 * blk).astype(BF16)
        v = rows_of(v_st, start, 2 * blk).astype(BF16)
        no_prev = jnp.where((span_idx == 0) & (n == 0), first_block_penalty, 0.0)
        q_stack = jnp.concatenate([jnp.where(head == h, q, jnp.zeros_like(q)) for h in range(N_HEADS)], axis=0)
        s = _dot_nt(q_stack, k) + bias + no_prev
        mx = jnp.max(s, axis=1, keepdims=True)
        p = jnp.exp(s - mx)
        den = jnp.sum(p, axis=1, keepdims=True)
        o_all = _dot(p.astype(BF16), v) * (1.0 / den)
        lse_all = mx + jnp.log(den)
        o_new = jnp.zeros((blk, QK_W), F32)
        lse = jnp.zeros((blk, QK_W), F32)
        for h in range(N_HEADS):
            in_head = head == h
            o_new = jnp.where(in_head, o_all[h * blk:(h + 1) * blk], o_new)
            lse = jnp.where(in_head, lse_all[h * blk:(h + 1) * blk], lse)
        if position == 0:
            m_new, den_new, acc_new = lse, jnp.ones_like(lse), o_new
        else:
            m_old = rows_of(m_st, start, blk)
            m_new = jnp.maximum(m_old, lse)
            w_old, w_new = jnp.exp(m_old - m_new), jnp.exp(lse - m_new)
            den_new = rows_of(den_st, start, blk) * w_old + w_new
            acc_new = rows_of(acc_st, start, blk) * w_old + o_new * w_new
        put_rows(m_st, start, m_new)
        put_rows(den_st, start, den_new)
        put_rows(acc_st, start, acc_new)
        return carry

    lax.fori_loop(0, span // blk, step, 0, unroll=16)
    if position == len(DSWA_PATTERNS) - 1:
        for s in range(QK_W // LANE_TILE):
            o_ref[:, s * LANE_TILE:(s + 1) * LANE_TILE] = (acc_st[s, :, :] / den_st[s, :, :]).astype(BF16)


def _dswa_kernel(q_ref, kp_ref, kc_ref, vp_ref, vc_ref, o_ref, *scratch, slopes):
    span_idx, position = pl.program_id(1), pl.program_id(2)
    for pos in range(len(DSWA_PATTERNS)):
        g = _dswa_group_at(pos)
        window, dilation = DSWA_PATTERNS[g]
        pl.when(position == pos)(functools.partial(
            _dswa_group, pos, span_idx, q_ref, kp_ref, kc_ref, vp_ref, vc_ref, o_ref, *scratch,
            slopes=slopes[g * N_HEADS:(g + 1) * N_HEADS], window=window, dilation=dilation))


def _dswa_group_at(position):
    return len(DSWA_PATTERNS) - 1 - position


def _alibi_slopes(n):
    return [float(np.float32(2.0 ** (-8.0 * i / n))) for i in range(1, n + 1)]


def _dilated_attention(a, bsz, seq):
    span = DSWA_SPAN
    a3 = a.reshape(bsz, seq, A_W)
    n_groups = len(DSWA_PATTERNS)
    col = lambda part, pos: part * n_groups + _dswa_group_at(pos)
    cur = lambda part: pl.BlockSpec((None, span, QK_W), lambda b, i, p: (b, i, col(part, p)))
    prev = lambda part: pl.BlockSpec((None, span, QK_W), lambda b, i, p: (b, jnp.maximum(i - 1, 0), col(part, p)))
    slab = lambda rows: pltpu.VMEM((QK_W // LANE_TILE, rows, LANE_TILE), F32)
    out = pl.pallas_call(
        functools.partial(_dswa_kernel, slopes=_alibi_slopes(n_groups * N_HEADS)),
        grid=(bsz, seq // span, n_groups),
        in_specs=[cur(0), prev(1), cur(1), prev(2), cur(2)],
        out_specs=pl.BlockSpec((None, span, QK_W), lambda b, i, g: (b, i, 0)),
        out_shape=jax.ShapeDtypeStruct((bsz, seq, QK_W), BF16),
        scratch_shapes=[slab(span), slab(2 * span), slab(2 * span), slab(span), slab(span), slab(span)],
        compiler_params=_params("parallel", "arbitrary", "arbitrary"),
        name="dilated_attention",
    )(a3, a3, a3, a3, a3)
    return out.reshape(bsz * seq, QK_W)


def _conv_kernel(prev_ref, cur_ref, w_ref, b_ref, lg_ref, lb_ref, o_ref, buf_ref):
    ts = cur_ref.shape[0]
    first = pl.program_id(1) == 0
    buf_ref[0, 0:CONV_HALO, :] = jnp.where(first, 0.0, prev_ref[...])
    buf_ref[0, CONV_HALO:, :] = cur_ref[...]
    rows = ts + CONV_HALO - SUBLANES
    for r in range(1, SUBLANES):
        buf_ref[r, 0:rows, :] = buf_ref[0, r:r + rows, :]
    off = CONV_HALO - (CONV_WIDTH - 1)
    acc = jnp.zeros((ts, CONV_CH), F32) + b_ref[...]
    for w in range(CONV_WIDTH):
        r, base = (off + w) % SUBLANES, (off + w) // SUBLANES * SUBLANES
        acc = acc + buf_ref[r, base:base + ts, :] * w_ref[w:w + 1, :]
    mu = jnp.mean(acc, axis=-1, keepdims=True)
    ctr = acc - mu
    var = jnp.mean(ctr * ctr, axis=-1, keepdims=True)
    y = ctr * lax.rsqrt(var + NORM_EPS) * lg_ref[...] + lb_ref[...]
    o_ref[...] = (y * _sigmoid(y)).astype(BF16)


def _conv_mixer(y, conv_w, conv_b, ln_g, ln_b, bsz, seq):
    ts = TOKEN_TILE
    per_halo = ts // CONV_HALO
    y3 = y.reshape(bsz, seq, CONV_CH)
    w_pad = jnp.zeros((CONV_HALO, CONV_CH), F32).at[:CONV_WIDTH].set(conv_w)
    out = pl.pallas_call(
        _conv_kernel,
        grid=(bsz, seq // ts),
        in_specs=[
            pl.BlockSpec((None, CONV_HALO, CONV_CH), lambda b, i: (b, jnp.maximum(i * per_halo - 1, 0), 0)),
            pl.BlockSpec((None, ts, CONV_CH), lambda b, i: (b, i, 0)),
            _const_spec((CONV_HALO, CONV_CH)), _const_spec((1, CONV_CH)), _const_spec((1, CONV_CH)),
            _const_spec((1, CONV_CH)),
        ],
        out_specs=pl.BlockSpec((None, ts, CONV_CH), lambda b, i: (b, i, 0)),
        out_shape=jax.ShapeDtypeStruct((bsz, seq, CONV_CH), BF16),
        scratch_shapes=[pltpu.VMEM((SUBLANES, ts + CONV_HALO, CONV_CH), F32)],
        compiler_params=_params("parallel", "parallel"),
        name="conv_mixer",
    )(y3, y3, w_pad, conv_b.reshape(1, -1), ln_g.reshape(1, -1), ln_b.reshape(1, -1))
    return out.reshape(bsz * seq, CONV_CH)


def _gla_kernel(q_ref, k_ref, v_ref, r_ref, low_ref, wg_ref, bg_ref, og_ref, tri_ref, am_ref, sm_ref, o_ref, st_ref):
    ch, sub = GLA_CHUNK, GLA_SUB
    tc = q_ref.shape[0]
    n_sub = ch // sub

    @pl.when(pl.program_id(1) == 0)
    def _():
        st_ref[...] = jnp.zeros_like(st_ref)

    head = _head_of_lane()
    key_row = _iota((ch, QK_W), 0)
    att_mask = am_ref[...]
    st_mask = sm_ref[...]

    logit = _dot(low_ref[...].astype(BF16), wg_ref[...]) + bg_ref[...]
    log_a = _neg_softplus(-logit) * (1.0 / GLA_TAU)
    la_hi, la_lo = _split_bf16(log_a)
    cr = GLA_CUM_ROWS
    cum_all = jnp.concatenate(
        [_dot(tri_ref[...], jnp.concatenate([la_hi[g * cr:(g + 1) * cr], la_lo[g * cr:(g + 1) * cr]], axis=0))
         for g in range(tc // cr)], axis=0)

    state = st_ref[...]
    for c in range(tc // ch):
        rows = slice(c * ch, (c + 1) * ch)
        cum = cum_all[rows, :]
        total = cum[ch - 1:ch, :]
        q = q_ref[rows, :] * (HEAD_LANES ** -0.5)
        k = k_ref[rows, :]
        v = v_ref[rows, :].astype(BF16)

        bases = [jnp.zeros((1, QK_W), F32)] + [cum[i * sub - 1:i * sub, :] for i in range(1, n_sub)]
        base_rows = jnp.concatenate([jnp.broadcast_to(b, (sub, QK_W)) for b in bases], axis=0)
        q_rel = q * jnp.exp(cum - base_rows)
        q_all = jnp.concatenate([jnp.where(head == h, q_rel, 0.0) for h in range(N_HEADS)], axis=0).astype(BF16)
        k_all = jnp.concatenate([jnp.where(key_row < (i + 1) * sub, k * jnp.exp(bases[i] - cum), 0.0)
                                 for i in range(n_sub)], axis=0).astype(BF16)
        att = (_dot_nt(q_all, k_all) * att_mask).astype(BF16)
        intra = _dot(att, jnp.concatenate([v] * n_sub, axis=0))
        o = jnp.concatenate([intra[h * ch:(h + 1) * ch, h * GLA_DV:(h + 1) * GLA_DV] for h in range(N_HEADS)], axis=1)

        o = o + _dot_nt((q * jnp.exp(cum)).astype(BF16), state.astype(BF16))
        k_dec = (k * jnp.exp(total - cum)).astype(BF16)
        state = state * jnp.exp(total) + _dot_tn(v, k_dec) * st_mask

        r = r_ref[rows, :]
        outs = []
        for h in range(N_HEADS):
            o_h = o[:, h * GLA_DV:(h + 1) * GLA_DV]
            ms = jnp.mean(o_h * o_h, axis=-1, keepdims=True)
            outs.append(o_h * lax.rsqrt(ms + NORM_EPS) * og_ref[...])
        o_ref[rows, :] = (r * _sigmoid(r) * jnp.concatenate(outs, axis=1)).astype(BF16)
    st_ref[...] = state


def _gla_masks(tc):
    ch, sub = GLA_CHUNK, GLA_SUB
    t = np.arange(tc)
    tri = (t[:, None] >= t[None, :]) & (t[:, None] // ch == t[None, :] // ch)
    a_t = np.arange(N_HEADS * ch) % ch
    a_col = np.arange((ch // sub) * ch)
    att = (a_col[None, :] // ch == a_t[:, None] // sub) & (a_col[None, :] % ch <= a_t[:, None])
    state = np.arange(GLA_V_W)[:, None] // GLA_DV == np.arange(QK_W)[None, :] // HEAD_LANES
    return (jnp.asarray(np.concatenate([tri, tri], axis=1), BF16), jnp.asarray(att, F32), jnp.asarray(state, F32))


def _gla_mixer(cbuf, w_gate, b_gate, o_gain, bsz, seq):
    tc = GLA_CHUNK * GLA_CHUNKS_PER_STEP
    c3 = cbuf.reshape(bsz, seq, C_W)
    wg = jnp.zeros((GLA_LOW_PAD, QK_W), F32).at[:GLA_RANK].set(w_gate).astype(BF16)
    tri2, att_mask, st_mask = _gla_masks(GLA_CUM_ROWS)
    spec = lambda w, j: pl.BlockSpec((None, tc, w), lambda b, i: (b, i, j))
    out = pl.pallas_call(
        _gla_kernel,
        grid=(bsz, seq // tc),
        in_specs=[
            spec(QK_W, 0), spec(QK_W, 1), spec(GLA_V_W, 1), spec(GLA_V_W, 2),
            spec(GLA_LOW_PAD, (2 * QK_W + 2 * GLA_V_W) // GLA_LOW_PAD),
            _const_spec((GLA_LOW_PAD, QK_W)), _const_spec((1, QK_W)), _const_spec((1, GLA_DV)),
            _const_spec(tri2.shape), _const_spec(att_mask.shape), _const_spec(st_mask.shape),
        ],
        out_specs=pl.BlockSpec((None, tc, GLA_V_W), lambda b, i: (b, i, 0)),
        out_shape=jax.ShapeDtypeStruct((bsz, seq, GLA_V_W), BF16),
        scratch_shapes=[pltpu.VMEM((GLA_V_W, QK_W), F32)],
        compiler_params=_params("parallel", "arbitrary"),
        name="gla_mixer",
    )(c3, c3, c3, c3, c3, wg, b_gate.reshape(1, -1), o_gain.reshape(1, -1), tri2, att_mask, st_mask)
    return out.reshape(bsz * seq, GLA_V_W)


def _sb_kernel(q_ref, k_ref, v_ref, o_ref, kmax_ref):
    t = SB_T
    seq = k_ref.shape[0]
    i = pl.program_id(1)
    head = _head_of_lane()
    head_ones = _ones_where(jnp.right_shift(_iota((QK_W, QK_W), 0), int(np.log2(HEAD_LANES))) == head)

    @pl.when(i == 0)
    def _():
        best = jnp.zeros((SB_NORM_ROWS, QK_W), F32)
        for c in range(seq // SB_NORM_ROWS):
            kf = k_ref[c * SB_NORM_ROWS:(c + 1) * SB_NORM_ROWS, :].astype(F32)
            sq_hi, sq_lo = _split_bf16(kf * kf)
            best = jnp.maximum(best, _dot(sq_hi, head_ones) + _dot(sq_lo, head_ones))
        kmax_ref[...] = jnp.zeros_like(kmax_ref) + jnp.max(best)

    tri2 = _ones_where(jnp.bitwise_and(_iota((2 * t, t), 0), t - 1) >= _iota((2 * t, t), 1))
    keep = _iota((N_HEADS * t, t), 1) < jnp.bitwise_and(_iota((N_HEADS * t, t), 0), t - 1)

    def query_block(u):
        q = q_ref[u * t:(u + 1) * t, :]
        q_stack = jnp.concatenate([jnp.where(head == h, q, jnp.zeros_like(q)) for h in range(N_HEADS)], axis=0)
        qf = q_stack.astype(F32)
        z_bound = jnp.sqrt(jnp.sum(qf * qf, axis=1, keepdims=True) * kmax_ref[0:1, 0:1]) * SB_BOUND_SLACK
        return q_stack, z_bound

    def block(j, acc, run, masked, q_stack, z_bound):
        start = pl.multiple_of(j * t, t)
        k_blk = k_ref[pl.ds(start, t), :]
        v_blk = v_ref[pl.ds(start, t), :]
        z = _dot_nt(q_stack, k_blk)
        drop = jnp.maximum(z, 0.0) + jnp.log(1.0 + jnp.exp2(jnp.abs(z) * (-LOG2_E)))
        if masked:
            drop = jnp.where(keep, drop, 0.0)
        d_hi, d_lo = _split_bf16(drop)
        suffix = _dot(jnp.concatenate([d_hi, d_lo], axis=1), tri2)
        w = jnp.exp(z - suffix - run)
        if masked:
            w = jnp.where(keep, w, 0.0)
        w = w.astype(BF16)
        w_wide = jnp.concatenate([w[h * t:(h + 1) * t, :] for h in range(N_HEADS)], axis=1)
        v_stack = jnp.concatenate([jnp.where(head == h, v_blk, jnp.zeros_like(v_blk)) for h in range(N_HEADS)], axis=0)
        acc = acc + _dot(w_wide, v_stack)
        run = run + jnp.sum(drop, axis=1, keepdims=True)
        done = (jnp.max(z_bound - run) < SB_ZERO_EXPONENT).astype(jnp.int32)
        return acc, run, done

    def walk(first_blocks):
        states = []
        for u in range(SB_QBLOCKS):
            iu = i * SB_QBLOCKS + u
            qz = query_block(u)
            acc, run, done = block(iu, jnp.zeros((t, QK_W), F32), jnp.zeros((N_HEADS * t, 1), F32), True, *qz)
            for n in range(1, first_blocks):
                acc, run, done = block(iu - n, acc, run, False, *qz)
            states.append((iu - first_blocks, done, acc, run, qz))
        for u, (j, done, acc, run, qz) in enumerate(states):

            def body(c, qz=qz):
                j, _, acc, run = c
                acc, run, done = block(j, acc, run, False, *qz)
                return j - 1, done, acc, run

            carry = lax.while_loop(lambda c: jnp.logical_and(c[0] >= 0, c[1] == 0), body, (j, done, acc, run))
            o_ref[u * t:(u + 1) * t, :] = carry[2].astype(BF16)

    enough_history = i * SB_QBLOCKS >= SB_STRAIGHT_BLOCKS - 1
    pl.when(enough_history)(functools.partial(walk, SB_STRAIGHT_BLOCKS))
    pl.when(jnp.logical_not(enough_history))(functools.partial(walk, 1))


def _stick_breaking_mixer(dbuf, bsz, seq):
    d3 = dbuf.reshape(bsz, seq, D_W)
    rows = SB_T * SB_QBLOCKS
    full = lambda j: pl.BlockSpec((None, seq, QK_W), lambda b, i: (b, 0, j))
    out = pl.pallas_call(
        _sb_kernel,
        grid=(bsz, seq // rows),
        in_specs=[pl.BlockSpec((None, rows, QK_W), lambda b, i: (b, i, 0)), full(1), full(2)],
        out_specs=pl.BlockSpec((None, rows, QK_W), lambda b, i: (b, i, 0)),
        out_shape=jax.ShapeDtypeStruct((bsz, seq, QK_W), BF16),
        scratch_shapes=[pltpu.VMEM((8, 128), F32)],
        compiler_params=_params("parallel", "arbitrary"),
        name="stick_breaking_mixer",
    )(d3, d3, d3)
    return out.reshape(bsz * seq, QK_W)


def _merge_kernel(x_ref, mod_ref, g_ref, oa_ref, ob_ref, oc_ref, od_ref,
                  wg_ref, wa_ref, wb_ref, wc_ref, wd_ref, wo_ref, out_ref):
    x = x_ref[...]
    h = _modulated_norm(x, g_ref[...], mod_ref[1:2, :], mod_ref[0:1, :]).astype(BF16)
    branches = ((oa_ref[...], wa_ref), (ob_ref[...], wb_ref), (oc_ref[...], wc_ref), (od_ref[...], wd_ref))
    merged = jnp.zeros(x.shape, F32)
    for j, (o_j, w_ref) in enumerate(branches):
        gate = _sigmoid(_dot(h, wg_ref[:, j * D_MODEL:(j + 1) * D_MODEL]))
        merged = merged + gate * _dot(o_j, w_ref[...])
    out_ref[...] = x + mod_ref[2:3, :] * _dot(merged.astype(BF16), wo_ref[...])


def _merge(x2d, mod, g_mix, oa, ob, oc, od, wg, wa, wb, wc, wd, wo, seq):
    t = x2d.shape[0]
    tm = TOKEN_TILE
    per_b = seq // tm
    row = lambda w: pl.BlockSpec((tm, w), lambda i: (i, 0))
    return pl.pallas_call(
        _merge_kernel,
        grid=(t // tm,),
        in_specs=[
            row(D_MODEL),
            pl.BlockSpec((None, 6, D_MODEL), lambda i: (i // per_b, 0, 0)),
            _const_spec((1, D_MODEL)),
            row(QK_W), row(CONV_CH), row(GLA_V_W), row(QK_W),
            _const_spec(wg.shape), _const_spec(wa.shape), _const_spec(wb.shape), _const_spec(wc.shape),
            _const_spec(wd.shape), _const_spec(wo.shape),
        ],
        out_specs=row(D_MODEL),
        out_shape=jax.ShapeDtypeStruct((t, D_MODEL), F32),
        compiler_params=_params("parallel"),
        name="gated_merge",
    )(x2d, mod, g_mix, oa, ob, oc, od, wg, wa, wb, wc, wd, wo)


def _mlp_kernel(x_ref, mod_ref, g_ref, wu_ref, wd_ref, out_ref):
    x = x_ref[...]
    h = _modulated_norm(x, g_ref[...], mod_ref[4:5, :], mod_ref[3:4, :]).astype(BF16)
    acc = jnp.zeros(x.shape, F32)
    for j in range(D_FF // D_MODEL):
        cols = slice(j * D_MODEL, (j + 1) * D_MODEL)
        u = jnp.maximum(_dot(h, wu_ref[:, cols]), 0.0)
        acc = acc + _dot((u * u).astype(BF16), wd_ref[cols, :])
    out_ref[...] = x + mod_ref[5:6, :] * acc


def _mlp(x2d, mod, g_mlp, wu, wd, seq):
    t = x2d.shape[0]
    tm = TOKEN_TILE
    per_b = seq // tm
    row = pl.BlockSpec((tm, D_MODEL), lambda i: (i, 0))
    return pl.pallas_call(
        _mlp_kernel,
        grid=(t // tm,),
        in_specs=[
            row,
            pl.BlockSpec((None, 6, D_MODEL), lambda i: (i // per_b, 0, 0)),
            _const_spec((1, D_MODEL)), _const_spec(wu.shape), _const_spec(wd.shape),
        ],
        out_specs=row,
        out_shape=jax.ShapeDtypeStruct((t, D_MODEL), F32),
        compiler_params=_params("parallel"),
        name="relu2_mlp",
    )(x2d, mod, g_mlp, wu, wd)


def _head_block_diag():
    idx = np.arange(QK_W) // HEAD_LANES
    return jnp.asarray((idx[:, None] == idx[None, :]).astype(np.float32) / HEAD_LANES, dtype=BF16)


def kernel(x, c, w_ada, b_ada, g_mix, w_in, q_gain, k_gain, conv_w, conv_b, conv_ln_g, conv_ln_b,
           gla_w_gate, gla_b_gate, gla_o_gain, w_br_a, w_br_b, w_br_c, w_br_d, w_out, g_mlp, w_up, w_down):
    bsz, seq, d = x.shape
    depth = w_ada.shape[0]
    assert d == D_MODEL and seq % (DSWA_BLOCK * DSWA_PATTERNS[-1][1]) == 0 and seq % TOKEN_TILE == 0
    mods = _modulation(c, w_ada, b_ada)
    bd = _head_block_diag()
    x2d = x.reshape(bsz * seq, d)
    qk_scale = HEAD_LANES ** -0.5
    for l in range(depth):
        w = w_in[l]
        o_conv = A_W
        o_c = o_conv + 2 * CONV_CH
        o_low = o_c + 2 * QK_W + 2 * GLA_V_W
        o_d = o_low + GLA_RANK
        o_gate = o_d + D_W
        wa = w[:, :A_W].astype(BF16)
        wb = w[:, o_conv:o_c].astype(BF16)
        wc = jnp.concatenate([w[:, o_c:o_low + GLA_RANK], jnp.zeros((d, GLA_LOW_PAD - GLA_RANK), F32)], axis=1).astype(BF16)
        wd = w[:, o_d:o_gate].astype(BF16)
        wg = w[:, o_gate:].astype(BF16)
        qg = jnp.tile(q_gain[l], N_HEADS).reshape(1, QK_W) * qk_scale
        kg = jnp.tile(k_gain[l], N_HEADS).reshape(1, QK_W)
        mod = mods[l]
        gm = g_mix[l].reshape(1, d)

        a, y, cbuf, dbuf = _in_projection(x2d, mod, gm, wa, wb, wc, wd, qg, kg, bd, seq)
        oa = _dilated_attention(a, bsz, seq)
        ob = _conv_mixer(y, conv_w[l], conv_b[l], conv_ln_g[l], conv_ln_b[l], bsz, seq)
        oc = _gla_mixer(cbuf, gla_w_gate[l], gla_b_gate[l], gla_o_gain[l], bsz, seq)
        od = _stick_breaking_mixer(dbuf, bsz, seq)
        x2d = _merge(x2d, mod, gm, oa, ob, oc, od, wg, w_br_a[l].astype(BF16), w_br_b[l].astype(BF16),
                     w_br_c[l].astype(BF16), w_br_d[l].astype(BF16), w_out[l].astype(BF16), seq)
        x2d = _mlp(x2d, mod, g_mlp[l].reshape(1, d), w_up[l].astype(BF16), w_down[l].astype(BF16), seq)
    return x2d.reshape(bsz, seq, d)
```

```python
import functools

import numpy as np
import jax
import jax.numpy as jnp
from jax import lax
from jax.experimental import pallas as pl
from jax.experimental.pallas import tpu as pltpu

F32 = jnp.float32
BF16 = jnp.bfloat16

D_MODEL = 1024
D_FF = 4 * D_MODEL
NORM_EPS = 1e-6
LOG2_E = 1.4426950408889634
HEAD_LANES = 64
N_HEADS = 4
QK_W = N_HEADS * HEAD_LANES
DSWA_PATTERNS = ((128, 1), (512, 4), (2048, 16))
DSWA_BLOCK = 128
DSWA_SPAN = DSWA_BLOCK * DSWA_PATTERNS[-1][1]
LANE_TILE = 128
SUBLANES = 8
CONV_CH = 256
CONV_WIDTH = 31
CONV_HALO = 32
GLA_DV = 128
GLA_V_W = N_HEADS * GLA_DV
GLA_RANK = 16
GLA_LOW_PAD = 128
GLA_TAU = 16.0
GLA_CHUNK = 64
GLA_SUB = 16
GLA_CHUNKS_PER_STEP = 16
GLA_CUM_ROWS = 256
SB_T = 256
SB_QBLOCKS = 4
SB_STRAIGHT_BLOCKS = 3
SB_NORM_ROWS = 512
SB_BOUND_SLACK = 1.02
SB_ZERO_EXPONENT = -110.0
TOKEN_TILE = 512
MASK_VALUE = -1e30
VMEM_LIMIT = 56 * 1024 * 1024

A_W = 3 * 3 * QK_W
C_W = 2 * QK_W + 2 * GLA_V_W + GLA_LOW_PAD
D_W = 3 * QK_W


def _dot(a, b):
    return jnp.dot(a, b, preferred_element_type=F32)


def _dot_nt(a, b):
    return lax.dot_general(a, b, (((1,), (1,)), ((), ())), preferred_element_type=F32)


def _dot_tn(a, b):
    return lax.dot_general(a, b, (((0,), (0,)), ((), ())), preferred_element_type=F32)


def _split_bf16(x):
    hi = x.astype(BF16)
    lo = (x - hi.astype(F32)).astype(BF16)
    return hi, lo


def _sigmoid(x):
    return 1.0 / (1.0 + jnp.exp(-x))


def _neg_softplus(x):
    return -(jnp.maximum(x, 0.0) + jnp.log(1.0 + jnp.exp(-jnp.abs(x))))


def _iota(shape, dim):
    return lax.broadcasted_iota(jnp.int32, shape, dim)


def _head_of_lane(width=QK_W, lanes_per_head=HEAD_LANES):
    return jnp.right_shift(_iota((1, width), 1), int(np.log2(lanes_per_head)))


def _ones_where(cond):
    return jnp.where(cond, 1.0, 0.0).astype(BF16)


def _const_spec(shape):
    nd = len(shape)
    return pl.BlockSpec(shape, lambda *_: (0,) * nd, pipeline_mode=pl.Buffered(1))


def _params(*sem):
    return pltpu.CompilerParams(dimension_semantics=sem, vmem_limit_bytes=VMEM_LIMIT)


def _modulated_norm(x, gain, scale, shift):
    ms = jnp.mean(x * x, axis=-1, keepdims=True)
    y = x * lax.rsqrt(ms + NORM_EPS) * gain
    return y * (1.0 + scale) + shift


def _mod_kernel(c_ref, w_ref, b_ref, o_ref):
    c_hi, c_lo = _split_bf16(c_ref[...])
    w_hi, w_lo = _split_bf16(w_ref[...])
    o_ref[...] = _dot(c_hi, w_hi) + _dot(c_hi, w_lo) + _dot(c_lo, w_hi) + b_ref[...]


def _modulation(c, w_ada, b_ada):
    depth, d, six_d = w_ada.shape
    bsz = c.shape[0]
    out = pl.pallas_call(
        _mod_kernel,
        grid=(depth, six_d // d),
        in_specs=[
            pl.BlockSpec((bsz, d), lambda l, j: (0, 0)),
            pl.BlockSpec((None, d, d), lambda l, j: (l, 0, j)),
            pl.BlockSpec((None, 1, d), lambda l, j: (l, 0, j)),
        ],
        out_specs=pl.BlockSpec((None, bsz, d), lambda l, j: (l, 0, j)),
        out_shape=jax.ShapeDtypeStruct((depth, bsz, six_d), F32),
        compiler_params=_params("parallel", "parallel"),
        name="adaln_modulation",
    )(c, w_ada, b_ada.reshape(depth, 1, six_d))
    return out.reshape(depth, bsz, six_d // d, d)


def _inproj_kernel(x_ref, mod_ref, g_ref, wa_ref, wb_ref, wc_ref, wd_ref, qg_ref, kg_ref, bd_ref,
                   a_ref, y_ref, c_ref, d_ref):
    h = _modulated_norm(x_ref[...], g_ref[...], mod_ref[1:2, :], mod_ref[0:1, :]).astype(BF16)
    bd = bd_ref[...]
    a_all = _dot(h, wa_ref[...])
    for j in range(A_W // QK_W):
        cols = slice(j * QK_W, (j + 1) * QK_W)
        acc = a_all[:, cols]
        if j < 6:
            ms = _dot((acc * acc).astype(BF16), bd)
            gain = qg_ref[...] if j < 3 else kg_ref[...]
            acc = acc * lax.rsqrt(ms + NORM_EPS) * gain
        a_ref[:, cols] = acc.astype(BF16)
    u = _dot(h, wb_ref[...])
    y_ref[...] = u[:, :CONV_CH] * _sigmoid(u[:, CONV_CH:])
    c_ref[...] = _dot(h, wc_ref[...])
    d_all = _dot(h, wd_ref[...])
    d_ref[:, :QK_W] = (d_all[:, :QK_W] * (HEAD_LANES ** -0.5)).astype(BF16)
    d_ref[:, QK_W:] = d_all[:, QK_W:].astype(BF16)


def _in_projection(x2d, mod, g_mix, wa, wb, wc, wd, qg, kg, bd, seq):
    t = x2d.shape[0]
    tm = TOKEN_TILE
    per_b = seq // tm
    row = lambda w: pl.BlockSpec((tm, w), lambda i: (i, 0))
    return pl.pallas_call(
        _inproj_kernel,
        grid=(t // tm,),
        in_specs=[
            row(D_MODEL),
            pl.BlockSpec((None, 6, D_MODEL), lambda i: (i // per_b, 0, 0)),
            _const_spec((1, D_MODEL)),
            _const_spec(wa.shape), _const_spec(wb.shape), _const_spec(wc.shape), _const_spec(wd.shape),
            _const_spec((1, QK_W)), _const_spec((1, QK_W)), _const_spec((QK_W, QK_W)),
        ],
        out_specs=[row(A_W), row(CONV_CH), row(C_W), row(D_W)],
        out_shape=[
            jax.ShapeDtypeStruct((t, A_W), BF16),
            jax.ShapeDtypeStruct((t, CONV_CH), F32),
            jax.ShapeDtypeStruct((t, C_W), F32),
            jax.ShapeDtypeStruct((t, D_W), BF16),
        ],
        compiler_params=_params("parallel"),
        name="in_projection",
    )(x2d, mod, g_mix, wa, wb, wc, wd, qg, kg, bd)


def _dswa_group(position, span_idx, q_ref, kp_ref, kc_ref, vp_ref, vc_ref, o_ref, q_st, k_st, v_st, m_st, den_st, acc_st,
                *, slopes, window, dilation):
    blk, span = DSWA_BLOCK, DSWA_SPAN
    halo = blk * dilation
    for s in range(QK_W // LANE_TILE):
        lanes = slice(s * LANE_TILE, (s + 1) * LANE_TILE)
        q_st[s, :, :] = q_ref[:, lanes].astype(F32)
        k_st[s, 0:halo, :] = kp_ref[span - halo:span, lanes].astype(F32)
        k_st[s, halo:halo + span, :] = kc_ref[:, lanes].astype(F32)
        v_st[s, 0:halo, :] = vp_ref[span - halo:span, lanes].astype(F32)
        v_st[s, halo:halo + span, :] = vc_ref[:, lanes].astype(F32)
    head = _head_of_lane()
    qi = _iota((blk, 2 * blk), 0)
    kj = _iota((blk, 2 * blk), 1)
    rel = qi - kj + blk
    in_window = (rel >= 0) & (rel <= window // dilation)
    dist = (dilation * rel).astype(F32)
    bias = jnp.concatenate([jnp.where(in_window, -slopes[h] * dist, MASK_VALUE) for h in range(N_HEADS)], axis=0)
    first_block_penalty = jnp.where(_iota((1, 2 * blk), 1) < blk, MASK_VALUE, 0.0)

    def rows_of(ref, start, n):
        return jnp.concatenate([ref[s, pl.ds(start, n, stride=dilation), :] for s in range(QK_W // LANE_TILE)], axis=1)

    def put_rows(ref, start, val):
        for s in range(QK_W // LANE_TILE):
            ref[s, pl.ds(start, blk, stride=dilation), :] = val[:, s * LANE_TILE:(s + 1) * LANE_TILE]

    def step(it):
        n = it // dilation
        start = n * halo + it % dilation
        q = rows_of(q_st, start, blk).astype(BF16)
        k = rows_of(k_st, start, 2 * blk).astype(BF16)
        v = rows_of(v_st, start, 2 * blk).astype(BF16)
        q_stack = jnp.concatenate([jnp.where(head == h, q, jnp.zeros_like(q)) for h in range(N_HEADS)], axis=0)
        s = _dot_nt(q_stack, k) + bias
        if n == 0:
            s = s + jnp.where(span_idx == 0, first_block_penalty, 0.0)
        mx = jnp.max(s, axis=1, keepdims=True)
        p = jnp.exp(s - mx)
        den = jnp.sum(p, axis=1, keepdims=True)
        o_all = _dot(p.astype(BF16), v) * (1.0 / den)
        lse_all = mx + jnp.log(den)
        o_new = jnp.zeros((blk, QK_W), F32)
        lse = jnp.zeros((blk, QK_W), F32)
        for h in range(N_HEADS):
            in_head = head == h
            o_new = jnp.where(in_head, o_all[h * blk:(h + 1) * blk], o_new)
            lse = jnp.where(in_head, lse_all[h * blk:(h + 1) * blk], lse)
        if position == 0:
            m_new, acc_new = lse, o_new
        else:
            m_old = rows_of(m_st, start, blk)
            m_new = jnp.maximum(m_old, lse)
            w_old, w_new = jnp.exp(m_old - m_new), jnp.exp(lse - m_new)
            den_old = w_old if position == 1 else rows_of(den_st, start, blk) * w_old
            put_rows(den_st, start, den_old + w_new)
            acc_new = rows_of(acc_st, start, blk) * w_old + o_new * w_new
        if position < len(DSWA_PATTERNS) - 1:
            put_rows(m_st, start, m_new)
        put_rows(acc_st, start, acc_new)

    for it in range(span // blk):
        step(it)
    if position == len(DSWA_PATTERNS) - 1:
        for s in range(QK_W // LANE_TILE):
            o_ref[:, s * LANE_TILE:(s + 1) * LANE_TILE] = (acc_st[s, :, :] / den_st[s, :, :]).astype(BF16)


def _dswa_kernel(q_ref, kp_ref, kc_ref, vp_ref, vc_ref, o_ref, *scratch, slopes):
    span_idx, position = pl.program_id(1), pl.program_id(2)
    for pos in range(len(DSWA_PATTERNS)):
        g = _dswa_group_at(pos)
        window, dilation = DSWA_PATTERNS[g]
        pl.when(position == pos)(functools.partial(
            _dswa_group, pos, span_idx, q_ref, kp_ref, kc_ref, vp_ref, vc_ref, o_ref, *scratch,
            slopes=slopes[g * N_HEADS:(g + 1) * N_HEADS], window=window, dilation=dilation))


def _dswa_group_at(position):
    return len(DSWA_PATTERNS) - 1 - position


def _alibi_slopes(n):
    return [float(np.float32(2.0 ** (-8.0 * i / n))) for i in range(1, n + 1)]


def _dilated_attention(a, bsz, seq):
    span = DSWA_SPAN
    a3 = a.reshape(bsz, seq, A_W)
    n_groups = len(DSWA_PATTERNS)
    col = lambda part, pos: part * n_groups + _dswa_group_at(pos)
    cur = lambda part: pl.BlockSpec((None, span, QK_W), lambda b, i, p: (b, i, col(part, p)))
    prev = lambda part: pl.BlockSpec((None, span, QK_W), lambda b, i, p: (b, jnp.maximum(i - 1, 0), col(part, p)))
    slab = lambda rows: pltpu.VMEM((QK_W // LANE_TILE, rows, LANE_TILE), F32)
    out = pl.pallas_call(
        functools.partial(_dswa_kernel, slopes=_alibi_slopes(n_groups * N_HEADS)),
        grid=(bsz, seq // span, n_groups),
        in_specs=[cur(0), prev(1), cur(1), prev(2), cur(2)],
        out_specs=pl.BlockSpec((None, span, QK_W), lambda b, i, g: (b, i, 0)),
        out_shape=jax.ShapeDtypeStruct((bsz, seq, QK_W), BF16),
        scratch_shapes=[slab(span), slab(2 * span), slab(2 * span), slab(span), slab(span), slab(span)],
        compiler_params=_params("parallel", "arbitrary", "arbitrary"),
        name="dilated_attention",
    )(a3, a3, a3, a3, a3)
    return out.reshape(bsz * seq, QK_W)


def _conv_kernel(prev_ref, cur_ref, w_ref, b_ref, lg_ref, lb_ref, o_ref, buf_ref):
    ts = cur_ref.shape[0]
    first = pl.program_id(1) == 0
    buf_ref[0, 0:CONV_HALO, :] = jnp.where(first, 0.0, prev_ref[...])
    buf_ref[0, CONV_HALO:, :] = cur_ref[...]
    rows = ts + CONV_HALO - SUBLANES
    for r in range(1, SUBLANES):
        buf_ref[r, 0:rows, :] = buf_ref[0, r:r + rows, :]
    off = CONV_HALO - (CONV_WIDTH - 1)
    acc = jnp.zeros((ts, CONV_CH), F32) + b_ref[...]
    for w in range(CONV_WIDTH):
        r, base = (off + w) % SUBLANES, (off + w) // SUBLANES * SUBLANES
        acc = acc + buf_ref[r, base:base + ts, :] * w_ref[w:w + 1, :]
    mu = jnp.mean(acc, axis=-1, keepdims=True)
    ctr = acc - mu
    var = jnp.mean(ctr * ctr, axis=-1, keepdims=True)
    y = ctr * lax.rsqrt(var + NORM_EPS) * lg_ref[...] + lb_ref[...]
    o_ref[...] = (y * _sigmoid(y)).astype(BF16)


def _conv_mixer(y, conv_w, conv_b, ln_g, ln_b, bsz, seq):
    ts = TOKEN_TILE
    per_halo = ts // CONV_HALO
    y3 = y.reshape(bsz, seq, CONV_CH)
    w_pad = jnp.zeros((CONV_HALO, CONV_CH), F32).at[:CONV_WIDTH].set(conv_w)
    out = pl.pallas_call(
        _conv_kernel,
        grid=(bsz, seq // ts),
        in_specs=[
            pl.BlockSpec((None, CONV_HALO, CONV_CH), lambda b, i: (b, jnp.maximum(i * per_halo - 1, 0), 0)),
            pl.BlockSpec((None, ts, CONV_CH), lambda b, i: (b, i, 0)),
            _const_spec((CONV_HALO, CONV_CH)), _const_spec((1, CONV_CH)), _const_spec((1, CONV_CH)),
            _const_spec((1, CONV_CH)),
        ],
        out_specs=pl.BlockSpec((None, ts, CONV_CH), lambda b, i: (b, i, 0)),
        out_shape=jax.ShapeDtypeStruct((bsz, seq, CONV_CH), BF16),
        scratch_shapes=[pltpu.VMEM((SUBLANES, ts + CONV_HALO, CONV_CH), F32)],
        compiler_params=_params("parallel", "parallel"),
        name="conv_mixer",
    )(y3, y3, w_pad, conv_b.reshape(1, -1), ln_g.reshape(1, -1), ln_b.reshape(1, -1))
    return out.reshape(bsz * seq, CONV_CH)


def _gla_kernel(q_ref, k_ref, v_ref, r_ref, low_ref, wg_ref, bg_ref, og_ref, tri_ref, am_ref, sm_ref, o_ref, st_ref):
    ch, sub = GLA_CHUNK, GLA_SUB
    tc = q_ref.shape[0]
    n_sub = ch // sub

    @pl.when(pl.program_id(1) == 0)
    def _():
        st_ref[...] = jnp.zeros_like(st_ref)

    head = _head_of_lane()
    key_row = _iota((ch, QK_W), 0)
    att_mask = am_ref[...]
    st_mask = sm_ref[...]

    logit = _dot(low_ref[...].astype(BF16), wg_ref[...]) + bg_ref[...]
    log_a = _neg_softplus(-logit) * (1.0 / GLA_TAU)
    la_hi, la_lo = _split_bf16(log_a)
    cr = GLA_CUM_ROWS
    cum_all = jnp.concatenate(
        [_dot(tri_ref[...], jnp.concatenate([la_hi[g * cr:(g + 1) * cr], la_lo[g * cr:(g + 1) * cr]], axis=0))
         for g in range(tc // cr)], axis=0)

    state = st_ref[...]
    for c in range(tc // ch):
        rows = slice(c * ch, (c + 1) * ch)
        cum = cum_all[rows, :]
        total = cum[ch - 1:ch, :]
        q = q_ref[rows, :] * (HEAD_LANES ** -0.5)
        k = k_ref[rows, :]
        v = v_ref[rows, :].astype(BF16)

        bases = [jnp.zeros((1, QK_W), F32)] + [cum[i * sub - 1:i * sub, :] for i in range(1, n_sub)]
        base_rows = jnp.concatenate([jnp.broadcast_to(b, (sub, QK_W)) for b in bases], axis=0)
        q_rel = q * jnp.exp(cum - base_rows)
        q_all = jnp.concatenate([jnp.where(head == h, q_rel, 0.0) for h in range(N_HEADS)], axis=0).astype(BF16)
        k_all = jnp.concatenate([jnp.where(key_row < (i + 1) * sub, k * jnp.exp(bases[i] - cum), 0.0)
                                 for i in range(n_sub)], axis=0).astype(BF16)
        att = (_dot_nt(q_all, k_all) * att_mask).astype(BF16)
        intra = _dot(att, jnp.concatenate([v] * n_sub, axis=0))
        o = jnp.concatenate([intra[h * ch:(h + 1) * ch, h * GLA_DV:(h + 1) * GLA_DV] for h in range(N_HEADS)], axis=1)

        o = o + _dot_nt((q * jnp.exp(cum)).astype(BF16), state.astype(BF16))
        k_dec = (k * jnp.exp(total - cum)).astype(BF16)
        state = state * jnp.exp(total) + _dot_tn(v, k_dec) * st_mask

        r = r_ref[rows, :]
        outs = []
        for h in range(N_HEADS):
            o_h = o[:, h * GLA_DV:(h + 1) * GLA_DV]
            ms = jnp.mean(o_h * o_h, axis=-1, keepdims=True)
            outs.append(o_h * lax.rsqrt(ms + NORM_EPS) * og_ref[...])
        o_ref[rows, :] = (r * _sigmoid(r) * jnp.concatenate(outs, axis=1)).astype(BF16)
    st_ref[...] = state


def _gla_masks(tc):
    ch, sub = GLA_CHUNK, GLA_SUB
    t = np.arange(tc)
    tri = (t[:, None] >= t[None, :]) & (t[:, None] // ch == t[None, :] // ch)
    a_t = np.arange(N_HEADS * ch) % ch
    a_col = np.arange((ch // sub) * ch)
    att = (a_col[None, :] // ch == a_t[:, None] // sub) & (a_col[None, :] % ch <= a_t[:, None])
    state = np.arange(GLA_V_W)[:, None] // GLA_DV == np.arange(QK_W)[None, :] // HEAD_LANES
    return (jnp.asarray(np.concatenate([tri, tri], axis=1), BF16), jnp.asarray(att, F32), jnp.asarray(state, F32))


def _gla_mixer(cbuf, w_gate, b_gate, o_gain, bsz, seq):
    tc = GLA_CHUNK * GLA_CHUNKS_PER_STEP
    c3 = cbuf.reshape(bsz, seq, C_W)
    wg = jnp.zeros((GLA_LOW_PAD, QK_W), F32).at[:GLA_RANK].set(w_gate).astype(BF16)
    tri2, att_mask, st_mask = _gla_masks(GLA_CUM_ROWS)
    spec = lambda w, j: pl.BlockSpec((None, tc, w), lambda b, i: (b, i, j))
    out = pl.pallas_call(
        _gla_kernel,
        grid=(bsz, seq // tc),
        in_specs=[
            spec(QK_W, 0), spec(QK_W, 1), spec(GLA_V_W, 1), spec(GLA_V_W, 2),
            spec(GLA_LOW_PAD, (2 * QK_W + 2 * GLA_V_W) // GLA_LOW_PAD),
            _const_spec((GLA_LOW_PAD, QK_W)), _const_spec((1, QK_W)), _const_spec((1, GLA_DV)),
            _const_spec(tri2.shape), _const_spec(att_mask.shape), _const_spec(st_mask.shape),
        ],
        out_specs=pl.BlockSpec((None, tc, GLA_V_W), lambda b, i: (b, i, 0)),
        out_shape=jax.ShapeDtypeStruct((bsz, seq, GLA_V_W), BF16),
        scratch_shapes=[pltpu.VMEM((GLA_V_W, QK_W), F32)],
        compiler_params=_params("parallel", "arbitrary"),
        name="gla_mixer",
    )(c3, c3, c3, c3, c3, wg, b_gate.reshape(1, -1), o_gain.reshape(1, -1), tri2, att_mask, st_mask)
    return out.reshape(bsz * seq, GLA_V_W)


def _sb_kernel(q_ref, k_ref, v_ref, o_ref, kmax_ref):
    t = SB_T
    seq = k_ref.shape[0]
    i = pl.program_id(1)
    head = _head_of_lane()
    head_ones = _ones_where(jnp.right_shift(_iota((QK_W, QK_W), 0), int(np.log2(HEAD_LANES))) == head)

    @pl.when(i == 0)
    def _():
        best = jnp.zeros((SB_NORM_ROWS, QK_W), F32)
        for c in range(seq // SB_NORM_ROWS):
            kf = k_ref[c * SB_NORM_ROWS:(c + 1) * SB_NORM_ROWS, :].astype(F32)
            sq_hi, sq_lo = _split_bf16(kf * kf)
            best = jnp.maximum(best, _dot(sq_hi, head_ones) + _dot(sq_lo, head_ones))
        kmax_ref[...] = jnp.zeros_like(kmax_ref) + jnp.max(best)

    tri2 = _ones_where(jnp.bitwise_and(_iota((2 * t, t), 0), t - 1) >= _iota((2 * t, t), 1))
    keep = _iota((N_HEADS * t, t), 1) < jnp.bitwise_and(_iota((N_HEADS * t, t), 0), t - 1)

    def query_block(u):
        q = q_ref[u * t:(u + 1) * t, :]
        q_stack = jnp.concatenate([jnp.where(head == h, q, jnp.zeros_like(q)) for h in range(N_HEADS)], axis=0)
        qf = q_stack.astype(F32)
        z_bound = jnp.sqrt(jnp.sum(qf * qf, axis=1, keepdims=True) * kmax_ref[0:1, 0:1]) * SB_BOUND_SLACK
        return q_stack, z_bound

    def block(j, acc, run, masked, q_stack, z_bound):
        start = pl.multiple_of(j * t, t)
        k_blk = k_ref[pl.ds(start, t), :]
        v_blk = v_ref[pl.ds(start, t), :]
        z = _dot_nt(q_stack, k_blk)
        drop = jnp.maximum(z, 0.0) + jnp.log(1.0 + jnp.exp2(jnp.abs(z) * (-LOG2_E)))
        if masked:
            drop = jnp.where(keep, drop, 0.0)
        d_hi, d_lo = _split_bf16(drop)
        suffix = _dot(jnp.concatenate([d_hi, d_lo], axis=1), tri2)
        w = jnp.exp(z - suffix - run)
        if masked:
            w = jnp.where(keep, w, 0.0)
        w = w.astype(BF16)
        w_wide = jnp.concatenate([w[h * t:(h + 1) * t, :] for h in range(N_HEADS)], axis=1)
        v_stack = jnp.concatenate([jnp.where(head == h, v_blk, jnp.zeros_like(v_blk)) for h in range(N_HEADS)], axis=0)
        acc = acc + _dot(w_wide, v_stack)
        run = run + jnp.sum(drop, axis=1, keepdims=True)
        done = (jnp.max(z_bound - run) < SB_ZERO_EXPONENT).astype(jnp.int32)
        return acc, run, done

    def walk(first_blocks):
        states = []
        for u in range(SB_QBLOCKS):
            iu = i * SB_QBLOCKS + u
            qz = query_block(u)
            acc, run, done = block(iu, jnp.zeros((t, QK_W), F32), jnp.zeros((N_HEADS * t, 1), F32), True, *qz)
            for n in range(1, first_blocks):
                acc, run, done = block(iu - n, acc, run, False, *qz)
            states.append((iu - first_blocks, done, acc, run, qz))
        for u, (j, done, acc, run, qz) in enumerate(states):

            def body(c, qz=qz):
                j, _, acc, run = c
                acc, run, done = block(j, acc, run, False, *qz)
                return j - 1, done, acc, run

            carry = lax.while_loop(lambda c: jnp.logical_and(c[0] >= 0, c[1] == 0), body, (j, done, acc, run))
            o_ref[u * t:(u + 1) * t, :] = carry[2].astype(BF16)

    enough_history = i * SB_QBLOCKS >= SB_STRAIGHT_BLOCKS - 1
    pl.when(enough_history)(functools.partial(walk, SB_STRAIGHT_BLOCKS))
    pl.when(jnp.logical_not(enough_history))(functools.partial(walk, 1))


def _stick_breaking_mixer(dbuf, bsz, seq):
    d3 = dbuf.reshape(bsz, seq, D_W)
    rows = SB_T * SB_QBLOCKS
    full = lambda j: pl.BlockSpec((None, seq, QK_W), lambda b, i: (b, 0, j))
    out = pl.pallas_call(
        _sb_kernel,
        grid=(bsz, seq // rows),
        in_specs=[pl.BlockSpec((None, rows, QK_W), lambda b, i: (b, i, 0)), full(1), full(2)],
        out_specs=pl.BlockSpec((None, rows, QK_W), lambda b, i: (b, i, 0)),
        out_shape=jax.ShapeDtypeStruct((bsz, seq, QK_W), BF16),
        scratch_shapes=[pltpu.VMEM((8, 128), F32)],
        compiler_params=_params("parallel", "arbitrary"),
        name="stick_breaking_mixer",
    )(d3, d3, d3)
    return out.reshape(bsz * seq, QK_W)


def _merge_kernel(x_ref, mod_ref, g_ref, oa_ref, ob_ref, oc_ref, od_ref,
                  wg_ref, wa_ref, wb_ref, wc_ref, wd_ref, wo_ref, out_ref):
    x = x_ref[...]
    h = _modulated_norm(x, g_ref[...], mod_ref[1:2, :], mod_ref[0:1, :]).astype(BF16)
    branches = ((oa_ref[...], wa_ref), (ob_ref[...], wb_ref), (oc_ref[...], wc_ref), (od_ref[...], wd_ref))
    merged = jnp.zeros(x.shape, F32)
    for j, (o_j, w_ref) in enumerate(branches):
        gate = _sigmoid(_dot(h, wg_ref[:, j * D_MODEL:(j + 1) * D_MODEL]))
        merged = merged + gate * _dot(o_j, w_ref[...])
    out_ref[...] = x + mod_ref[2:3, :] * _dot(merged.astype(BF16), wo_ref[...])


def _merge(x2d, mod, g_mix, oa, ob, oc, od, wg, wa, wb, wc, wd, wo, seq):
    t = x2d.shape[0]
    tm = TOKEN_TILE
    per_b = seq // tm
    row = lambda w: pl.BlockSpec((tm, w), lambda i: (i, 0))
    return pl.pallas_call(
        _merge_kernel,
        grid=(t // tm,),
        in_specs=[
            row(D_MODEL),
            pl.BlockSpec((None, 6, D_MODEL), lambda i: (i // per_b, 0, 0)),
            _const_spec((1, D_MODEL)),
            row(QK_W), row(CONV_CH), row(GLA_V_W), row(QK_W),
            _const_spec(wg.shape), _const_spec(wa.shape), _const_spec(wb.shape), _const_spec(wc.shape),
            _const_spec(wd.shape), _const_spec(wo.shape),
        ],
        out_specs=row(D_MODEL),
        out_shape=jax.ShapeDtypeStruct((t, D_MODEL), F32),
        compiler_params=_params("parallel"),
        name="gated_merge",
    )(x2d, mod, g_mix, oa, ob, oc, od, wg, wa, wb, wc, wd, wo)


def _mlp_kernel(x_ref, mod_ref, g_ref, wu_ref, wd_ref, out_ref):
    x = x_ref[...]
    h = _modulated_norm(x, g_ref[...], mod_ref[4:5, :], mod_ref[3:4, :]).astype(BF16)
    acc = jnp.zeros(x.shape, F32)
    for j in range(D_FF // D_MODEL):
        cols = slice(j * D_MODEL, (j + 1) * D_MODEL)
        u = jnp.maximum(_dot(h, wu_ref[:, cols]), 0.0)
        acc = acc + _dot((u * u).astype(BF16), wd_ref[cols, :])
    out_ref[...] = x + mod_ref[5:6, :] * acc


def _mlp(x2d, mod, g_mlp, wu, wd, seq):
    t = x2d.shape[0]
    tm = TOKEN_TILE
    per_b = seq // tm
    row = pl.BlockSpec((tm, D_MODEL), lambda i: (i, 0))
    return pl.pallas_call(
        _mlp_kernel,
        grid=(t // tm,),
        in_specs=[
            row,
            pl.BlockSpec((None, 6, D_MODEL), lambda i: (i // per_b, 0, 0)),
            _const_spec((1, D_MODEL)), _const_spec(wu.shape), _const_spec(wd.shape),
        ],
        out_specs=row,
        out_shape=jax.ShapeDtypeStruct((t, D_MODEL), F32),
        compiler_params=_params("parallel"),
        name="relu2_mlp",
    )(x2d, mod, g_mlp, wu, wd)


def _head_block_diag():
    idx = np.arange(QK_W) // HEAD_LANES
    return jnp.asarray((idx[:, None] == idx[None, :]).astype(np.float32) / HEAD_LANES, dtype=BF16)


def kernel(x, c, w_ada, b_ada, g_mix, w_in, q_gain, k_gain, conv_w, conv_b, conv_ln_g, conv_ln_b,
           gla_w_gate, gla_b_gate, gla_o_gain, w_br_a, w_br_b, w_br_c, w_br_d, w_out, g_mlp, w_up, w_down):
    bsz, seq, d = x.shape
    depth = w_ada.shape[0]
    assert d == D_MODEL and seq % (DSWA_BLOCK * DSWA_PATTERNS[-1][1]) == 0 and seq % TOKEN_TILE == 0
    mods = _modulation(c, w_ada, b_ada)
    bd = _head_block_diag()
    x2d = x.reshape(bsz * seq, d)
    qk_scale = HEAD_LANES ** -0.5
    for l in range(depth):
        w = w_in[l]
        o_conv = A_W
        o_c = o_conv + 2 * CONV_CH
        o_low = o_c + 2 * QK_W + 2 * GLA_V_W
        o_d = o_low + GLA_RANK
        o_gate = o_d + D_W
        wa = w[:, :A_W].astype(BF16)
        wb = w[:, o_conv:o_c].astype(BF16)
        wc = jnp.concatenate([w[:, o_c:o_low + GLA_RANK], jnp.zeros((d, GLA_LOW_PAD - GLA_RANK), F32)], axis=1).astype(BF16)
        wd = w[:, o_d:o_gate].astype(BF16)
        wg = w[:, o_gate:].astype(BF16)
        qg = jnp.tile(q_gain[l], N_HEADS).reshape(1, QK_W) * qk_scale
        kg = jnp.tile(k_gain[l], N_HEADS).reshape(1, QK_W)
        mod = mods[l]
        gm = g_mix[l].reshape(1, d)

        a, y, cbuf, dbuf = _in_projection(x2d, mod, gm, wa, wb, wc, wd, qg, kg, bd, seq)
        oa = _dilated_attention(a, bsz, seq)
        ob = _conv_mixer(y, conv_w[l], conv_b[l], conv_ln_g[l], conv_ln_b[l], bsz, seq)
        oc = _gla_mixer(cbuf, gla_w_gate[l], gla_b_gate[l], gla_o_gain[l], bsz, seq)
        od = _stick_breaking_mixer(dbuf, bsz, seq)
        x2d = _merge(x2d, mod, gm, oa, ob, oc, od, wg, w_br_a[l].astype(BF16), w_br_b[l].astype(BF16),
                     w_br_c[l].astype(BF16), w_br_d[l].astype(BF16), w_out[l].astype(BF16), seq)
        x2d = _mlp(x2d, mod, g_mlp[l].reshape(1, d), w_up[l].astype(BF16), w_down[l].astype(BF16), seq)
    return x2d.reshape(bsz, seq, d)
```

```python
import functools

import numpy as np
import jax
import jax.numpy as jnp
from jax import lax
from jax.experimental import pallas as pl
from jax.experimental.pallas import tpu as pltpu

F32 = jnp.float32
BF16 = jnp.bfloat16

D_MODEL = 1024
D_FF = 4 * D_MODEL
NORM_EPS = 1e-6
LOG2_E = 1.4426950408889634
HEAD_LANES = 64
N_HEADS = 4
QK_W = N_HEADS * HEAD_LANES
DSWA_PATTERNS = ((128, 1), (512, 4), (2048, 16))
DSWA_BLOCK = 128
DSWA_SPAN = DSWA_BLOCK * DSWA_PATTERNS[-1][1]
LANE_TILE = 128
SUBLANES = 8
CONV_CH = 256
CONV_WIDTH = 31
CONV_HALO = 32
GLA_DV = 128
GLA_V_W = N_HEADS * GLA_DV
GLA_RANK = 16
GLA_LOW_PAD = 128
GLA_TAU = 16.0
GLA_CHUNK = 64
GLA_SUB = 16
GLA_CHUNKS_PER_STEP = 16
GLA_CUM_ROWS = 256
SB_T = 256
SB_QBLOCKS = 4
SB_STRAIGHT_BLOCKS = 5
SB_NORM_ROWS = 512
SB_BOUND_SLACK = 1.02
SB_ZERO_EXPONENT = -110.0
TOKEN_TILE = 512
MASK_VALUE = -1e30
VMEM_LIMIT = 56 * 1024 * 1024

A_W = 3 * 3 * QK_W
C_W = 2 * QK_W + 2 * GLA_V_W + GLA_LOW_PAD
D_W = 3 * QK_W


def _dot(a, b):
    return jnp.dot(a, b, preferred_element_type=F32)


def _dot_nt(a, b):
    return lax.dot_general(a, b, (((1,), (1,)), ((), ())), preferred_element_type=F32)


def _dot_tn(a, b):
    return lax.dot_general(a, b, (((0,), (0,)), ((), ())), preferred_element_type=F32)


def _split_bf16(x):
    hi = x.astype(BF16)
    lo = (x - hi.astype(F32)).astype(BF16)
    return hi, lo


def _sigmoid(x):
    return 1.0 / (1.0 + jnp.exp(-x))


def _neg_softplus(x):
    return -(jnp.maximum(x, 0.0) + jnp.log(1.0 + jnp.exp(-jnp.abs(x))))


def _iota(shape, dim):
    return lax.broadcasted_iota(jnp.int32, shape, dim)


def _head_of_lane(width=QK_W, lanes_per_head=HEAD_LANES):
    return jnp.right_shift(_iota((1, width), 1), int(np.log2(lanes_per_head)))


def _ones_where(cond):
    return jnp.where(cond, 1.0, 0.0).astype(BF16)


def _const_spec(shape):
    nd = len(shape)
    return pl.BlockSpec(shape, lambda *_: (0,) * nd, pipeline_mode=pl.Buffered(1))


def _params(*sem):
    return pltpu.CompilerParams(dimension_semantics=sem, vmem_limit_bytes=VMEM_LIMIT)


def _modulated_norm(x, gain, scale, shift):
    ms = jnp.mean(x * x, axis=-1, keepdims=True)
    y = x * lax.rsqrt(ms + NORM_EPS) * gain
    return y * (1.0 + scale) + shift


def _mod_kernel(c_ref, w_ref, b_ref, o_ref):
    c_hi, c_lo = _split_bf16(c_ref[...])
    w_hi, w_lo = _split_bf16(w_ref[...])
    o_ref[...] = _dot(c_hi, w_hi) + _dot(c_hi, w_lo) + _dot(c_lo, w_hi) + b_ref[...]


def _modulation(c, w_ada, b_ada):
    depth, d, six_d = w_ada.shape
    bsz = c.shape[0]
    out = pl.pallas_call(
        _mod_kernel,
        grid=(depth, six_d // d),
        in_specs=[
            pl.BlockSpec((bsz, d), lambda l, j: (0, 0)),
            pl.BlockSpec((None, d, d), lambda l, j: (l, 0, j)),
            pl.BlockSpec((None, 1, d), lambda l, j: (l, 0, j)),
        ],
        out_specs=pl.BlockSpec((None, bsz, d), lambda l, j: (l, 0, j)),
        out_shape=jax.ShapeDtypeStruct((depth, bsz, six_d), F32),
        compiler_params=_params("parallel", "parallel"),
        name="adaln_modulation",
    )(c, w_ada, b_ada.reshape(depth, 1, six_d))
    return out.reshape(depth, bsz, six_d // d, d)


def _inproj_kernel(x_ref, mod_ref, g_ref, wa_ref, wb_ref, wc_ref, wd_ref, qg_ref, kg_ref, bd_ref,
                   a_ref, y_ref, c_ref, d_ref):
    h = _modulated_norm(x_ref[...], g_ref[...], mod_ref[1:2, :], mod_ref[0:1, :]).astype(BF16)
    bd = bd_ref[...]
    a_all = _dot(h, wa_ref[...])
    for j in range(A_W // QK_W):
        cols = slice(j * QK_W, (j + 1) * QK_W)
        acc = a_all[:, cols]
        if j < 6:
            ms = _dot((acc * acc).astype(BF16), bd)
            gain = qg_ref[...] if j < 3 else kg_ref[...]
            acc = acc * lax.rsqrt(ms + NORM_EPS) * gain
        a_ref[:, cols] = acc.astype(BF16)
    u = _dot(h, wb_ref[...])
    y_ref[...] = u[:, :CONV_CH] * _sigmoid(u[:, CONV_CH:])
    c_ref[...] = _dot(h, wc_ref[...])
    d_all = _dot(h, wd_ref[...])
    d_ref[:, :QK_W] = (d_all[:, :QK_W] * (HEAD_LANES ** -0.5)).astype(BF16)
    d_ref[:, QK_W:] = d_all[:, QK_W:].astype(BF16)


def _in_projection(x2d, mod, g_mix, wa, wb, wc, wd, qg, kg, bd, seq):
    t = x2d.shape[0]
    tm = TOKEN_TILE
    per_b = seq // tm
    row = lambda w: pl.BlockSpec((tm, w), lambda i: (i, 0))
    return pl.pallas_call(
        _inproj_kernel,
        grid=(t // tm,),
        in_specs=[
            row(D_MODEL),
            pl.BlockSpec((None, 6, D_MODEL), lambda i: (i // per_b, 0, 0)),
            _const_spec((1, D_MODEL)),
            _const_spec(wa.shape), _const_spec(wb.shape), _const_spec(wc.shape), _const_spec(wd.shape),
            _const_spec((1, QK_W)), _const_spec((1, QK_W)), _const_spec((QK_W, QK_W)),
        ],
        out_specs=[row(A_W), row(CONV_CH), row(C_W), row(D_W)],
        out_shape=[
            jax.ShapeDtypeStruct((t, A_W), BF16),
            jax.ShapeDtypeStruct((t, CONV_CH), F32),
            jax.ShapeDtypeStruct((t, C_W), F32),
            jax.ShapeDtypeStruct((t, D_W), BF16),
        ],
        compiler_params=_params("parallel"),
        name="in_projection",
    )(x2d, mod, g_mix, wa, wb, wc, wd, qg, kg, bd)


def _dswa_group(position, span_idx, q_ref, kp_ref, kc_ref, vp_ref, vc_ref, o_ref, q_st, k_st, v_st, m_st, den_st, acc_st,
                *, slopes, window, dilation):
    blk, span = DSWA_BLOCK, DSWA_SPAN
    halo = blk * dilation
    for s in range(QK_W // LANE_TILE):
        lanes = slice(s * LANE_TILE, (s + 1) * LANE_TILE)
        q_st[s, :, :] = q_ref[:, lanes].astype(F32)
        k_st[s, 0:halo, :] = kp_ref[span - halo:span, lanes].astype(F32)
        k_st[s, halo:halo + span, :] = kc_ref[:, lanes].astype(F32)
        v_st[s, 0:halo, :] = vp_ref[span - halo:span, lanes].astype(F32)
        v_st[s, halo:halo + span, :] = vc_ref[:, lanes].astype(F32)
    head = _head_of_lane()
    qi = _iota((blk, 2 * blk), 0)
    kj = _iota((blk, 2 * blk), 1)
    rel = qi - kj + blk
    in_window = (rel >= 0) & (rel <= window // dilation)
    dist = (dilation * rel).astype(F32)
    bias = jnp.concatenate([jnp.where(in_window, -slopes[h] * dist, MASK_VALUE) for h in range(N_HEADS)], axis=0)
    first_block_penalty = jnp.where(_iota((1, 2 * blk), 1) < blk, MASK_VALUE, 0.0)

    def rows_of(ref, start, n):
        return jnp.concatenate([ref[s, pl.ds(start, n, stride=dilation), :] for s in range(QK_W // LANE_TILE)], axis=1)

    def put_rows(ref, start, val):
        for s in range(QK_W // LANE_TILE):
            ref[s, pl.ds(start, blk, stride=dilation), :] = val[:, s * LANE_TILE:(s + 1) * LANE_TILE]

    def step(it):
        n = it // dilation
        start = n * halo + it % dilation
        q = rows_of(q_st, start, blk).astype(BF16)
        k = rows_of(k_st, start, 2 * blk).astype(BF16)
        v = rows_of(v_st, start, 2 * blk).astype(BF16)
        q_stack = jnp.concatenate([jnp.where(head == h, q, jnp.zeros_like(q)) for h in range(N_HEADS)], axis=0)
        s = _dot_nt(q_stack, k) + bias
        if n == 0:
            s = s + jnp.where(span_idx == 0, first_block_penalty, 0.0)
        mx = jnp.max(s, axis=1, keepdims=True)
        p = jnp.exp(s - mx)
        den = jnp.sum(p, axis=1, keepdims=True)
        o_all = _dot(p.astype(BF16), v) * (1.0 / den)
        lse_all = mx + jnp.log(den)
        o_new = jnp.zeros((blk, QK_W), F32)
        lse = jnp.zeros((blk, QK_W), F32)
        for h in range(N_HEADS):
            in_head = head == h
            o_new = jnp.where(in_head, o_all[h * blk:(h + 1) * blk], o_new)
            lse = jnp.where(in_head, lse_all[h * blk:(h + 1) * blk], lse)
        if position == 0:
            m_new, acc_new = lse, o_new
        else:
            m_old = rows_of(m_st, start, blk)
            m_new = jnp.maximum(m_old, lse)
            w_old, w_new = jnp.exp(m_old - m_new), jnp.exp(lse - m_new)
            den_old = w_old if position == 1 else rows_of(den_st, start, blk) * w_old
            put_rows(den_st, start, den_old + w_new)
            acc_new = rows_of(acc_st, start, blk) * w_old + o_new * w_new
        if position < len(DSWA_PATTERNS) - 1:
            put_rows(m_st, start, m_new)
        put_rows(acc_st, start, acc_new)

    for it in range(span // blk):
        step(it)
    if position == len(DSWA_PATTERNS) - 1:
        for s in range(QK_W // LANE_TILE):
            o_ref[:, s * LANE_TILE:(s + 1) * LANE_TILE] = (acc_st[s, :, :] / den_st[s, :, :]).astype(BF16)


def _dswa_kernel(q_ref, kp_ref, kc_ref, vp_ref, vc_ref, o_ref, *scratch, slopes):
    span_idx, position = pl.program_id(1), pl.program_id(2)
    for pos in range(len(DSWA_PATTERNS)):
        g = _dswa_group_at(pos)
        window, dilation = DSWA_PATTERNS[g]
        pl.when(position == pos)(functools.partial(
            _dswa_group, pos, span_idx, q_ref, kp_ref, kc_ref, vp_ref, vc_ref, o_ref, *scratch,
            slopes=slopes[g * N_HEADS:(g + 1) * N_HEADS], window=window, dilation=dilation))


def _dswa_group_at(position):
    return len(DSWA_PATTERNS) - 1 - position


def _alibi_slopes(n):
    return [float(np.float32(2.0 ** (-8.0 * i / n))) for i in range(1, n + 1)]


def _dilated_attention(a, bsz, seq):
    span = DSWA_SPAN
    a3 = a.reshape(bsz, seq, A_W)
    n_groups = len(DSWA_PATTERNS)
    col = lambda part, pos: part * n_groups + _dswa_group_at(pos)
    cur = lambda part: pl.BlockSpec((None, span, QK_W), lambda b, i, p: (b, i, col(part, p)))
    prev = lambda part: pl.BlockSpec((None, span, QK_W), lambda b, i, p: (b, jnp.maximum(i - 1, 0), col(part, p)))
    slab = lambda rows: pltpu.VMEM((QK_W // LANE_TILE, rows, LANE_TILE), F32)
    out = pl.pallas_call(
        functools.partial(_dswa_kernel, slopes=_alibi_slopes(n_groups * N_HEADS)),
        grid=(bsz, seq // span, n_groups),
        in_specs=[cur(0), prev(1), cur(1), prev(2), cur(2)],
        out_specs=pl.BlockSpec((None, span, QK_W), lambda b, i, g: (b, i, 0)),
        out_shape=jax.ShapeDtypeStruct((bsz, seq, QK_W), BF16),
        scratch_shapes=[slab(span), slab(2 * span), slab(2 * span), slab(span), slab(span), slab(span)],
        compiler_params=_params("parallel", "arbitrary", "arbitrary"),
        name="dilated_attention",
    )(a3, a3, a3, a3, a3)
    return out.reshape(bsz * seq, QK_W)


def _conv_kernel(prev_ref, cur_ref, w_ref, b_ref, lg_ref, lb_ref, o_ref, buf_ref):
    ts = cur_ref.shape[0]
    first = pl.program_id(1) == 0
    buf_ref[0, 0:CONV_HALO, :] = jnp.where(first, 0.0, prev_ref[...])
    buf_ref[0, CONV_HALO:, :] = cur_ref[...]
    rows = ts + CONV_HALO - SUBLANES
    for r in range(1, SUBLANES):
        buf_ref[r, 0:rows, :] = buf_ref[0, r:r + rows, :]
    off = CONV_HALO - (CONV_WIDTH - 1)
    acc = jnp.zeros((ts, CONV_CH), F32) + b_ref[...]
    for w in range(CONV_WIDTH):
        r, base = (off + w) % SUBLANES, (off + w) // SUBLANES * SUBLANES
        acc = acc + buf_ref[r, base:base + ts, :] * w_ref[w:w + 1, :]
    mu = jnp.mean(acc, axis=-1, keepdims=True)
    ctr = acc - mu
    var = jnp.mean(ctr * ctr, axis=-1, keepdims=True)
    y = ctr * lax.rsqrt(var + NORM_EPS) * lg_ref[...] + lb_ref[...]
    o_ref[...] = (y * _sigmoid(y)).astype(BF16)


def _conv_mixer(y, conv_w, conv_b, ln_g, ln_b, bsz, seq):
    ts = TOKEN_TILE
    per_halo = ts // CONV_HALO
    y3 = y.reshape(bsz, seq, CONV_CH)
    w_pad = jnp.zeros((CONV_HALO, CONV_CH), F32).at[:CONV_WIDTH].set(conv_w)
    out = pl.pallas_call(
        _conv_kernel,
        grid=(bsz, seq // ts),
        in_specs=[
            pl.BlockSpec((None, CONV_HALO, CONV_CH), lambda b, i: (b, jnp.maximum(i * per_halo - 1, 0), 0)),
            pl.BlockSpec((None, ts, CONV_CH), lambda b, i: (b, i, 0)),
            _const_spec((CONV_HALO, CONV_CH)), _const_spec((1, CONV_CH)), _const_spec((1, CONV_CH)),
            _const_spec((1, CONV_CH)),
        ],
        out_specs=pl.BlockSpec((None, ts, CONV_CH), lambda b, i: (b, i, 0)),
        out_shape=jax.ShapeDtypeStruct((bsz, seq, CONV_CH), BF16),
        scratch_shapes=[pltpu.VMEM((SUBLANES, ts + CONV_HALO, CONV_CH), F32)],
        compiler_params=_params("parallel", "parallel"),
        name="conv_mixer",
    )(y3, y3, w_pad, conv_b.reshape(1, -1), ln_g.reshape(1, -1), ln_b.reshape(1, -1))
    return out.reshape(bsz * seq, CONV_CH)


def _gla_kernel(q_ref, k_ref, v_ref, r_ref, low_ref, wg_ref, bg_ref, og_ref, tri_ref, am_ref, sm_ref, o_ref, st_ref):
    ch, sub = GLA_CHUNK, GLA_SUB
    tc = q_ref.shape[0]
    n_sub = ch // sub

    @pl.when(pl.program_id(1) == 0)
    def _():
        st_ref[...] = jnp.zeros_like(st_ref)

    head = _head_of_lane()
    key_row = _iota((ch, QK_W), 0)
    att_mask = am_ref[...]
    st_mask = sm_ref[...]

    logit = _dot(low_ref[...].astype(BF16), wg_ref[...]) + bg_ref[...]
    log_a = _neg_softplus(-logit) * (1.0 / GLA_TAU)
    la_hi, la_lo = _split_bf16(log_a)
    cr = GLA_CUM_ROWS
    cum_all = jnp.concatenate(
        [_dot(tri_ref[...], jnp.concatenate([la_hi[g * cr:(g + 1) * cr], la_lo[g * cr:(g + 1) * cr]], axis=0))
         for g in range(tc // cr)], axis=0)

    state = st_ref[...]
    for c in range(tc // ch):
        rows = slice(c * ch, (c + 1) * ch)
        cum = cum_all[rows, :]
        total = cum[ch - 1:ch, :]
        q = q_ref[rows, :] * (HEAD_LANES ** -0.5)
        k = k_ref[rows, :]
        v = v_ref[rows, :].astype(BF16)

        bases = [jnp.zeros((1, QK_W), F32)] + [cum[i * sub - 1:i * sub, :] for i in range(1, n_sub)]
        base_rows = jnp.concatenate([jnp.broadcast_to(b, (sub, QK_W)) for b in bases], axis=0)
        q_rel = q * jnp.exp(cum - base_rows)
        q_all = jnp.concatenate([jnp.where(head == h, q_rel, 0.0) for h in range(N_HEADS)], axis=0).astype(BF16)
        k_all = jnp.concatenate([jnp.where(key_row < (i + 1) * sub, k * jnp.exp(bases[i] - cum), 0.0)
                                 for i in range(n_sub)], axis=0).astype(BF16)
        att = (_dot_nt(q_all, k_all) * att_mask).astype(BF16)
        intra = _dot(att, jnp.concatenate([v] * n_sub, axis=0))
        o = jnp.concatenate([intra[h * ch:(h + 1) * ch, h * GLA_DV:(h + 1) * GLA_DV] for h in range(N_HEADS)], axis=1)

        o = o + _dot_nt((q * jnp.exp(cum)).astype(BF16), state.astype(BF16))
        k_dec = (k * jnp.exp(total - cum)).astype(BF16)
        state = state * jnp.exp(total) + _dot_tn(v, k_dec) * st_mask

        r = r_ref[rows, :]
        outs = []
        for h in range(N_HEADS):
            o_h = o[:, h * GLA_DV:(h + 1) * GLA_DV]
            ms = jnp.mean(o_h * o_h, axis=-1, keepdims=True)
            outs.append(o_h * lax.rsqrt(ms + NORM_EPS) * og_ref[...])
        o_ref[rows, :] = (r * _sigmoid(r) * jnp.concatenate(outs, axis=1)).astype(BF16)
    st_ref[...] = state


def _gla_masks(tc):
    ch, sub = GLA_CHUNK, GLA_SUB
    t = np.arange(tc)
    tri = (t[:, None] >= t[None, :]) & (t[:, None] // ch == t[None, :] // ch)
    a_t = np.arange(N_HEADS * ch) % ch
    a_col = np.arange((ch // sub) * ch)
    att = (a_col[None, :] // ch == a_t[:, None] // sub) & (a_col[None, :] % ch <= a_t[:, None])
    state = np.arange(GLA_V_W)[:, None] // GLA_DV == np.arange(QK_W)[None, :] // HEAD_LANES
    return (jnp.asarray(np.concatenate([tri, tri], axis=1), BF16), jnp.asarray(att, F32), jnp.asarray(state, F32))


def _gla_mixer(cbuf, w_gate, b_gate, o_gain, bsz, seq):
    tc = GLA_CHUNK * GLA_CHUNKS_PER_STEP
    c3 = cbuf.reshape(bsz, seq, C_W)
    wg = jnp.zeros((GLA_LOW_PAD, QK_W), F32).at[:GLA_RANK].set(w_gate).astype(BF16)
    tri2, att_mask, st_mask = _gla_masks(GLA_CUM_ROWS)
    spec = lambda w, j: pl.BlockSpec((None, tc, w), lambda b, i: (b, i, j))
    out = pl.pallas_call(
        _gla_kernel,
        grid=(bsz, seq // tc),
        in_specs=[
            spec(QK_W, 0), spec(QK_W, 1), spec(GLA_V_W, 1), spec(GLA_V_W, 2),
            spec(GLA_LOW_PAD, (2 * QK_W + 2 * GLA_V_W) // GLA_LOW_PAD),
            _const_spec((GLA_LOW_PAD, QK_W)), _const_spec((1, QK_W)), _const_spec((1, GLA_DV)),
            _const_spec(tri2.shape), _const_spec(att_mask.shape), _const_spec(st_mask.shape),
        ],
        out_specs=pl.BlockSpec((None, tc, GLA_V_W), lambda b, i: (b, i, 0)),
        out_shape=jax.ShapeDtypeStruct((bsz, seq, GLA_V_W), BF16),
        scratch_shapes=[pltpu.VMEM((GLA_V_W, QK_W), F32)],
        compiler_params=_params("parallel", "arbitrary"),
        name="gla_mixer",
    )(c3, c3, c3, c3, c3, wg, b_gate.reshape(1, -1), o_gain.reshape(1, -1), tri2, att_mask, st_mask)
    return out.reshape(bsz * seq, GLA_V_W)


def _sb_kernel(q_ref, k_ref, v_ref, o_ref, kmax_ref):
    t = SB_T
    seq = k_ref.shape[0]
    i = pl.program_id(1)
    head = _head_of_lane()
    head_ones = _ones_where(jnp.right_shift(_iota((QK_W, QK_W), 0), int(np.log2(HEAD_LANES))) == head)

    @pl.when(i == 0)
    def _():
        best = jnp.zeros((SB_NORM_ROWS, QK_W), F32)
        for c in range(seq // SB_NORM_ROWS):
            kf = k_ref[c * SB_NORM_ROWS:(c + 1) * SB_NORM_ROWS, :].astype(F32)
            sq_hi, sq_lo = _split_bf16(kf * kf)
            best = jnp.maximum(best, _dot(sq_hi, head_ones) + _dot(sq_lo, head_ones))
        kmax_ref[...] = jnp.zeros_like(kmax_ref) + jnp.max(best)

    tri2 = _ones_where(jnp.bitwise_and(_iota((2 * t, t), 0), t - 1) >= _iota((2 * t, t), 1))
    keep = _iota((N_HEADS * t, t), 1) < jnp.bitwise_and(_iota((N_HEADS * t, t), 0), t - 1)

    def query_block(u):
        q = q_ref[u * t:(u + 1) * t, :]
        q_stack = jnp.concatenate([jnp.where(head == h, q, jnp.zeros_like(q)) for h in range(N_HEADS)], axis=0)
        qf = q_stack.astype(F32)
        z_bound = jnp.sqrt(jnp.sum(qf * qf, axis=1, keepdims=True) * kmax_ref[0:1, 0:1]) * SB_BOUND_SLACK
        return q_stack, z_bound

    def block(j, acc, run, masked, q_stack, z_bound):
        start = pl.multiple_of(j * t, t)
        k_blk = k_ref[pl.ds(start, t), :]
        v_blk = v_ref[pl.ds(start, t), :]
        z = _dot_nt(q_stack, k_blk)
        drop = jnp.maximum(z, 0.0) + jnp.log(1.0 + jnp.exp2(jnp.abs(z) * (-LOG2_E)))
        if masked:
            drop = jnp.where(keep, drop, 0.0)
        d_hi, d_lo = _split_bf16(drop)
        suffix = _dot(jnp.concatenate([d_hi, d_lo], axis=1), tri2)
        w = jnp.exp(z - suffix - run)
        if masked:
            w = jnp.where(keep, w, 0.0)
        w = w.astype(BF16)
        w_wide = jnp.concatenate([w[h * t:(h + 1) * t, :] for h in range(N_HEADS)], axis=1)
        v_stack = jnp.concatenate([jnp.where(head == h, v_blk, jnp.zeros_like(v_blk)) for h in range(N_HEADS)], axis=0)
        acc = acc + _dot(w_wide, v_stack)
        run = run + jnp.sum(drop, axis=1, keepdims=True)
        done = (jnp.max(z_bound - run) < SB_ZERO_EXPONENT).astype(jnp.int32)
        return acc, run, done

    def walk(first_blocks):
        states = []
        for u in range(SB_QBLOCKS):
            iu = i * SB_QBLOCKS + u
            qz = query_block(u)
            acc, run, done = block(iu, jnp.zeros((t, QK_W), F32), jnp.zeros((N_HEADS * t, 1), F32), True, *qz)
            for n in range(1, first_blocks):
                acc, run, done = block(iu - n, acc, run, False, *qz)
            states.append((iu - first_blocks, done, acc, run, qz))
        for u, (j, done, acc, run, qz) in enumerate(states):

            def body(c, qz=qz):
                j, _, acc, run = c
                acc, run, done = block(j, acc, run, False, *qz)
                return j - 1, done, acc, run

            carry = lax.while_loop(lambda c: jnp.logical_and(c[0] >= 0, c[1] == 0), body, (j, done, acc, run))
            o_ref[u * t:(u + 1) * t, :] = carry[2].astype(BF16)

    enough_history = i * SB_QBLOCKS >= SB_STRAIGHT_BLOCKS - 1
    pl.when(enough_history)(functools.partial(walk, SB_STRAIGHT_BLOCKS))
    pl.when(jnp.logical_not(enough_history))(functools.partial(walk, 1))


def _stick_breaking_mixer(dbuf, bsz, seq):
    d3 = dbuf.reshape(bsz, seq, D_W)
    rows = SB_T * SB_QBLOCKS
    full = lambda j: pl.BlockSpec((None, seq, QK_W), lambda b, i: (b, 0, j))
    out = pl.pallas_call(
        _sb_kernel,
        grid=(bsz, seq // rows),
        in_specs=[pl.BlockSpec((None, rows, QK_W), lambda b, i: (b, i, 0)), full(1), full(2)],
        out_specs=pl.BlockSpec((None, rows, QK_W), lambda b, i: (b, i, 0)),
        out_shape=jax.ShapeDtypeStruct((bsz, seq, QK_W), BF16),
        scratch_shapes=[pltpu.VMEM((8, 128), F32)],
        compiler_params=_params("parallel", "arbitrary"),
        name="stick_breaking_mixer",
    )(d3, d3, d3)
    return out.reshape(bsz * seq, QK_W)


def _merge_kernel(x_ref, mod_ref, g_ref, oa_ref, ob_ref, oc_ref, od_ref,
                  wg_ref, wa_ref, wb_ref, wc_ref, wd_ref, wo_ref, out_ref):
    x = x_ref[...]
    h = _modulated_norm(x, g_ref[...], mod_ref[1:2, :], mod_ref[0:1, :]).astype(BF16)
    branches = ((oa_ref[...], wa_ref), (ob_ref[...], wb_ref), (oc_ref[...], wc_ref), (od_ref[...], wd_ref))
    merged = jnp.zeros(x.shape, F32)
    for j, (o_j, w_ref) in enumerate(branches):
        gate = _sigmoid(_dot(h, wg_ref[:, j * D_MODEL:(j + 1) * D_MODEL]))
        merged = merged + gate * _dot(o_j, w_ref[...])
    out_ref[...] = x + mod_ref[2:3, :] * _dot(merged.astype(BF16), wo_ref[...])


def _merge(x2d, mod, g_mix, oa, ob, oc, od, wg, wa, wb, wc, wd, wo, seq):
    t = x2d.shape[0]
    tm = TOKEN_TILE
    per_b = seq // tm
    row = lambda w: pl.BlockSpec((tm, w), lambda i: (i, 0))
    return pl.pallas_call(
        _merge_kernel,
        grid=(t // tm,),
        in_specs=[
            row(D_MODEL),
            pl.BlockSpec((None, 6, D_MODEL), lambda i: (i // per_b, 0, 0)),
            _const_spec((1, D_MODEL)),
            row(QK_W), row(CONV_CH), row(GLA_V_W), row(QK_W),
            _const_spec(wg.shape), _const_spec(wa.shape), _const_spec(wb.shape), _const_spec(wc.shape),
            _const_spec(wd.shape), _const_spec(wo.shape),
        ],
        out_specs=row(D_MODEL),
        out_shape=jax.ShapeDtypeStruct((t, D_MODEL), F32),
        compiler_params=_params("parallel"),
        name="gated_merge",
    )(x2d, mod, g_mix, oa, ob, oc, od, wg, wa, wb, wc, wd, wo)


def _mlp_kernel(x_ref, mod_ref, g_ref, wu_ref, wd_ref, out_ref):
    x = x_ref[...]
    h = _modulated_norm(x, g_ref[...], mod_ref[4:5, :], mod_ref[3:4, :]).astype(BF16)
    acc = jnp.zeros(x.shape, F32)
    for j in range(D_FF // D_MODEL):
        cols = slice(j * D_MODEL, (j + 1) * D_MODEL)
        u = jnp.maximum(_dot(h, wu_ref[:, cols]), 0.0)
        acc = acc + _dot((u * u).astype(BF16), wd_ref[cols, :])
    out_ref[...] = x + mod_ref[5:6, :] * acc


def _mlp(x2d, mod, g_mlp, wu, wd, seq):
    t = x2d.shape[0]
    tm = TOKEN_TILE
    per_b = seq // tm
    row = pl.BlockSpec((tm, D_MODEL), lambda i: (i, 0))
    return pl.pallas_call(
        _mlp_kernel,
        grid=(t // tm,),
        in_specs=[
            row,
            pl.BlockSpec((None, 6, D_MODEL), lambda i: (i // per_b, 0, 0)),
            _const_spec((1, D_MODEL)), _const_spec(wu.shape), _const_spec(wd.shape),
        ],
        out_specs=row,
        out_shape=jax.ShapeDtypeStruct((t, D_MODEL), F32),
        compiler_params=_params("parallel"),
        name="relu2_mlp",
    )(x2d, mod, g_mlp, wu, wd)


def _head_block_diag():
    idx = np.arange(QK_W) // HEAD_LANES
    return jnp.asarray((idx[:, None] == idx[None, :]).astype(np.float32) / HEAD_LANES, dtype=BF16)


def kernel(x, c, w_ada, b_ada, g_mix, w_in, q_gain, k_gain, conv_w, conv_b, conv_ln_g, conv_ln_b,
           gla_w_gate, gla_b_gate, gla_o_gain, w_br_a, w_br_b, w_br_c, w_br_d, w_out, g_mlp, w_up, w_down):
    bsz, seq, d = x.shape
    depth = w_ada.shape[0]
    assert d == D_MODEL and seq % (DSWA_BLOCK * DSWA_PATTERNS[-1][1]) == 0 and seq % TOKEN_TILE == 0
    mods = _modulation(c, w_ada, b_ada)
    bd = _head_block_diag()
    x2d = x.reshape(bsz * seq, d)
    qk_scale = HEAD_LANES ** -0.5
    for l in range(depth):
        w = w_in[l]
        o_conv = A_W
        o_c = o_conv + 2 * CONV_CH
        o_low = o_c + 2 * QK_W + 2 * GLA_V_W
        o_d = o_low + GLA_RANK
        o_gate = o_d + D_W
        wa = w[:, :A_W].astype(BF16)
        wb = w[:, o_conv:o_c].astype(BF16)
        wc = jnp.concatenate([w[:, o_c:o_low + GLA_RANK], jnp.zeros((d, GLA_LOW_PAD - GLA_RANK), F32)], axis=1).astype(BF16)
        wd = w[:, o_d:o_gate].astype(BF16)
        wg = w[:, o_gate:].astype(BF16)
        qg = jnp.tile(q_gain[l], N_HEADS).reshape(1, QK_W) * qk_scale
        kg = jnp.tile(k_gain[l], N_HEADS).reshape(1, QK_W)
        mod = mods[l]
        gm = g_mix[l].reshape(1, d)

        a, y, cbuf, dbuf = _in_projection(x2d, mod, gm, wa, wb, wc, wd, qg, kg, bd, seq)
        oa = _dilated_attention(a, bsz, seq)
        ob = _conv_mixer(y, conv_w[l], conv_b[l], conv_ln_g[l], conv_ln_b[l], bsz, seq)
        oc = _gla_mixer(cbuf, gla_w_gate[l], gla_b_gate[l], gla_o_gain[l], bsz, seq)
        od = _stick_breaking_mixer(dbuf, bsz, seq)
        x2d = _merge(x2d, mod, gm, oa, ob, oc, od, wg, w_br_a[l].astype(BF16), w_br_b[l].astype(BF16),
                     w_br_c[l].astype(BF16), w_br_d[l].astype(BF16), w_out[l].astype(BF16), seq)
        x2d = _mlp(x2d, mod, g_mlp[l].reshape(1, d), w_up[l].astype(BF16), w_down[l].astype(BF16), seq)
    return x2d.reshape(bsz, seq, d)
```

```python
import functools

import numpy as np
import jax
import jax.numpy as jnp
from jax import lax
from jax.experimental import pallas as pl
from jax.experimental.pallas import tpu as pltpu

F32 = jnp.float32
BF16 = jnp.bfloat16

D_MODEL = 1024
D_FF = 4 * D_MODEL
NORM_EPS = 1e-6
LOG2_E = 1.4426950408889634
HEAD_LANES = 64
N_HEADS = 4
QK_W = N_HEADS * HEAD_LANES
DSWA_PATTERNS = ((128, 1), (512, 4), (2048, 16))
DSWA_BLOCK = 128
DSWA_SPAN = DSWA_BLOCK * DSWA_PATTERNS[-1][1]
LANE_TILE = 128
SUBLANES = 8
CONV_CH = 256
CONV_WIDTH = 31
CONV_HALO = 32
GLA_DV = 128
GLA_V_W = N_HEADS * GLA_DV
GLA_RANK = 16
GLA_LOW_PAD = 128
GLA_TAU = 16.0
GLA_CHUNK = 64
GLA_SUB = 16
GLA_CHUNKS_PER_STEP = 16
GLA_CUM_ROWS = 256
SB_T = 256
SB_QBLOCKS = 4
SB_STRAIGHT_BLOCKS = 4
SB_NORM_ROWS = 512
SB_BOUND_SLACK = 1.02
SB_ZERO_EXPONENT = -110.0
TOKEN_TILE = 512
MASK_VALUE = -1e30
VMEM_LIMIT = 56 * 1024 * 1024

A_W = 3 * 3 * QK_W
C_W = 2 * QK_W + 2 * GLA_V_W + GLA_LOW_PAD
D_W = 3 * QK_W


def _dot(a, b):
    return jnp.dot(a, b, preferred_element_type=F32)


def _dot_nt(a, b):
    return lax.dot_general(a, b, (((1,), (1,)), ((), ())), preferred_element_type=F32)


def _dot_tn(a, b):
    return lax.dot_general(a, b, (((0,), (0,)), ((), ())), preferred_element_type=F32)


def _split_bf16(x):
    hi = x.astype(BF16)
    lo = (x - hi.astype(F32)).astype(BF16)
    return hi, lo


def _sigmoid(x):
    return 1.0 / (1.0 + jnp.exp(-x))


def _neg_softplus(x):
    return -(jnp.maximum(x, 0.0) + jnp.log(1.0 + jnp.exp(-jnp.abs(x))))


def _iota(shape, dim):
    return lax.broadcasted_iota(jnp.int32, shape, dim)


def _head_of_lane(width=QK_W, lanes_per_head=HEAD_LANES):
    return jnp.right_shift(_iota((1, width), 1), int(np.log2(lanes_per_head)))


def _ones_where(cond):
    return jnp.where(cond, 1.0, 0.0).astype(BF16)


def _const_spec(shape):
    nd = len(shape)
    return pl.BlockSpec(shape, lambda *_: (0,) * nd, pipeline_mode=pl.Buffered(1))


def _params(*sem):
    return pltpu.CompilerParams(dimension_semantics=sem, vmem_limit_bytes=VMEM_LIMIT)


def _modulated_norm(x, gain, scale, shift):
    ms = jnp.mean(x * x, axis=-1, keepdims=True)
    y = x * lax.rsqrt(ms + NORM_EPS) * gain
    return y * (1.0 + scale) + shift


def _mod_kernel(c_ref, w_ref, b_ref, o_ref):
    c_hi, c_lo = _split_bf16(c_ref[...])
    w_hi, w_lo = _split_bf16(w_ref[...])
    o_ref[...] = _dot(c_hi, w_hi) + _dot(c_hi, w_lo) + _dot(c_lo, w_hi) + b_ref[...]


def _modulation(c, w_ada, b_ada):
    depth, d, six_d = w_ada.shape
    bsz = c.shape[0]
    out = pl.pallas_call(
        _mod_kernel,
        grid=(depth, six_d // d),
        in_specs=[
            pl.BlockSpec((bsz, d), lambda l, j: (0, 0)),
            pl.BlockSpec((None, d, d), lambda l, j: (l, 0, j)),
            pl.BlockSpec((None, 1, d), lambda l, j: (l, 0, j)),
        ],
        out_specs=pl.BlockSpec((None, bsz, d), lambda l, j: (l, 0, j)),
        out_shape=jax.ShapeDtypeStruct((depth, bsz, six_d), F32),
        compiler_params=_params("parallel", "parallel"),
        name="adaln_modulation",
    )(c, w_ada, b_ada.reshape(depth, 1, six_d))
    return out.reshape(depth, bsz, six_d // d, d)


def _inproj_kernel(x_ref, mod_ref, g_ref, wa_ref, wb_ref, wc_ref, wd_ref, qg_ref, kg_ref, bd_ref,
                   a_ref, y_ref, c_ref, d_ref):
    h = _modulated_norm(x_ref[...], g_ref[...], mod_ref[1:2, :], mod_ref[0:1, :]).astype(BF16)
    bd = bd_ref[...]
    a_all = _dot(h, wa_ref[...])
    for j in range(A_W // QK_W):
        cols = slice(j * QK_W, (j + 1) * QK_W)
        acc = a_all[:, cols]
        if j < 6:
            ms = _dot((acc * acc).astype(BF16), bd)
            gain = qg_ref[...] if j < 3 else kg_ref[...]
            acc = acc * lax.rsqrt(ms + NORM_EPS) * gain
        a_ref[:, cols] = acc.astype(BF16)
    u = _dot(h, wb_ref[...])
    y_ref[...] = u[:, :CONV_CH] * _sigmoid(u[:, CONV_CH:])
    c_ref[...] = _dot(h, wc_ref[...])
    d_all = _dot(h, wd_ref[...])
    d_ref[:, :QK_W] = (d_all[:, :QK_W] * (HEAD_LANES ** -0.5)).astype(BF16)
    d_ref[:, QK_W:] = d_all[:, QK_W:].astype(BF16)


def _in_projection(x2d, mod, g_mix, wa, wb, wc, wd, qg, kg, bd, seq):
    t = x2d.shape[0]
    tm = TOKEN_TILE
    per_b = seq // tm
    row = lambda w: pl.BlockSpec((tm, w), lambda i: (i, 0))
    return pl.pallas_call(
        _inproj_kernel,
        grid=(t // tm,),
        in_specs=[
            row(D_MODEL),
            pl.BlockSpec((None, 6, D_MODEL), lambda i: (i // per_b, 0, 0)),
            _const_spec((1, D_MODEL)),
            _const_spec(wa.shape), _const_spec(wb.shape), _const_spec(wc.shape), _const_spec(wd.shape),
            _const_spec((1, QK_W)), _const_spec((1, QK_W)), _const_spec((QK_W, QK_W)),
        ],
        out_specs=[row(A_W), row(CONV_CH), row(C_W), row(D_W)],
        out_shape=[
            jax.ShapeDtypeStruct((t, A_W), BF16),
            jax.ShapeDtypeStruct((t, CONV_CH), F32),
            jax.ShapeDtypeStruct((t, C_W), F32),
            jax.ShapeDtypeStruct((t, D_W), BF16),
        ],
        compiler_params=_params("parallel"),
        name="in_projection",
    )(x2d, mod, g_mix, wa, wb, wc, wd, qg, kg, bd)


def _dswa_group(position, span_idx, q_ref, kp_ref, kc_ref, vp_ref, vc_ref, o_ref, q_st, k_st, v_st, m_st, den_st, acc_st,
                *, slopes, window, dilation):
    blk, span = DSWA_BLOCK, DSWA_SPAN
    halo = blk * dilation
    for s in range(QK_W // LANE_TILE):
        lanes = slice(s * LANE_TILE, (s + 1) * LANE_TILE)
        q_st[s, :, :] = q_ref[:, lanes].astype(F32)
        k_st[s, 0:halo, :] = kp_ref[span - halo:span, lanes].astype(F32)
        k_st[s, halo:halo + span, :] = kc_ref[:, lanes].astype(F32)
        v_st[s, 0:halo, :] = vp_ref[span - halo:span, lanes].astype(F32)
        v_st[s, halo:halo + span, :] = vc_ref[:, lanes].astype(F32)
    head = _head_of_lane()
    qi = _iota((blk, 2 * blk), 0)
    kj = _iota((blk, 2 * blk), 1)
    rel = qi - kj + blk
    in_window = (rel >= 0) & (rel <= window // dilation)
    dist = (dilation * rel).astype(F32)
    bias = jnp.concatenate([jnp.where(in_window, -slopes[h] * dist, MASK_VALUE) for h in range(N_HEADS)], axis=0)
    first_block_penalty = jnp.where(_iota((1, 2 * blk), 1) < blk, MASK_VALUE, 0.0)

    def rows_of(ref, start, n):
        return jnp.concatenate([ref[s, pl.ds(start, n, stride=dilation), :] for s in range(QK_W // LANE_TILE)], axis=1)

    def put_rows(ref, start, val):
        for s in range(QK_W // LANE_TILE):
            ref[s, pl.ds(start, blk, stride=dilation), :] = val[:, s * LANE_TILE:(s + 1) * LANE_TILE]

    def step(it):
        n = it // dilation
        start = n * halo + it % dilation
        q = rows_of(q_st, start, blk).astype(BF16)
        k = rows_of(k_st, start, 2 * blk).astype(BF16)
        v = rows_of(v_st, start, 2 * blk).astype(BF16)
        q_stack = jnp.concatenate([jnp.where(head == h, q, jnp.zeros_like(q)) for h in range(N_HEADS)], axis=0)
        s = _dot_nt(q_stack, k) + bias
        if n == 0:
            s = s + jnp.where(span_idx == 0, first_block_penalty, 0.0)
        mx = jnp.max(s, axis=1, keepdims=True)
        p = jnp.exp(s - mx)
        den = jnp.sum(p, axis=1, keepdims=True)
        o_all = _dot(p.astype(BF16), v) * (1.0 / den)
        lse_all = mx + jnp.log(den)
        o_new = jnp.zeros((blk, QK_W), F32)
        lse = jnp.zeros((blk, QK_W), F32)
        for h in range(N_HEADS):
            in_head = head == h
            o_new = jnp.where(in_head, o_all[h * blk:(h + 1) * blk], o_new)
            lse = jnp.where(in_head, lse_all[h * blk:(h + 1) * blk], lse)
        if position == 0:
            m_new, acc_new = lse, o_new
        else:
            m_old = rows_of(m_st, start, blk)
            m_new = jnp.maximum(m_old, lse)
            w_old, w_new = jnp.exp(m_old - m_new), jnp.exp(lse - m_new)
            den_old = w_old if position == 1 else rows_of(den_st, start, blk) * w_old
            put_rows(den_st, start, den_old + w_new)
            acc_new = rows_of(acc_st, start, blk) * w_old + o_new * w_new
        if position < len(DSWA_PATTERNS) - 1:
            put_rows(m_st, start, m_new)
        put_rows(acc_st, start, acc_new)

    for it in range(span // blk):
        step(it)
    if position == len(DSWA_PATTERNS) - 1:
        for s in range(QK_W // LANE_TILE):
            o_ref[:, s * LANE_TILE:(s + 1) * LANE_TILE] = (acc_st[s, :, :] / den_st[s, :, :]).astype(BF16)


def _dswa_kernel(q_ref, kp_ref, kc_ref, vp_ref, vc_ref, o_ref, *scratch, slopes):
    span_idx, position = pl.program_id(1), pl.program_id(2)
    for pos in range(len(DSWA_PATTERNS)):
        g = _dswa_group_at(pos)
        window, dilation = DSWA_PATTERNS[g]
        pl.when(position == pos)(functools.partial(
            _dswa_group, pos, span_idx, q_ref, kp_ref, kc_ref, vp_ref, vc_ref, o_ref, *scratch,
            slopes=slopes[g * N_HEADS:(g + 1) * N_HEADS], window=window, dilation=dilation))


def _dswa_group_at(position):
    return len(DSWA_PATTERNS) - 1 - position


def _alibi_slopes(n):
    return [float(np.float32(2.0 ** (-8.0 * i / n))) for i in range(1, n + 1)]


def _dilated_attention(a, bsz, seq):
    span = DSWA_SPAN
    a3 = a.reshape(bsz, seq, A_W)
    n_groups = len(DSWA_PATTERNS)
    col = lambda part, pos: part * n_groups + _dswa_group_at(pos)
    cur = lambda part: pl.BlockSpec((None, span, QK_W), lambda b, i, p: (b, i, col(part, p)))
    prev = lambda part: pl.BlockSpec((None, span, QK_W), lambda b, i, p: (b, jnp.maximum(i - 1, 0), col(part, p)))
    slab = lambda rows: pltpu.VMEM((QK_W // LANE_TILE, rows, LANE_TILE), F32)
    out = pl.pallas_call(
        functools.partial(_dswa_kernel, slopes=_alibi_slopes(n_groups * N_HEADS)),
        grid=(bsz, seq // span, n_groups),
        in_specs=[cur(0), prev(1), cur(1), prev(2), cur(2)],
        out_specs=pl.BlockSpec((None, span, QK_W), lambda b, i, g: (b, i, 0)),
        out_shape=jax.ShapeDtypeStruct((bsz, seq, QK_W), BF16),
        scratch_shapes=[slab(span), slab(2 * span), slab(2 * span), slab(span), slab(span), slab(span)],
        compiler_params=_params("parallel", "arbitrary", "arbitrary"),
        name="dilated_attention",
    )(a3, a3, a3, a3, a3)
    return out.reshape(bsz * seq, QK_W)


def _conv_kernel(prev_ref, cur_ref, w_ref, b_ref, lg_ref, lb_ref, o_ref, buf_ref):
    ts = cur_ref.shape[0]
    first = pl.program_id(1) == 0
    buf_ref[0, 0:CONV_HALO, :] = jnp.where(first, 0.0, prev_ref[...])
    buf_ref[0, CONV_HALO:, :] = cur_ref[...]
    rows = ts + CONV_HALO - SUBLANES
    for r in range(1, SUBLANES):
        buf_ref[r, 0:rows, :] = buf_ref[0, r:r + rows, :]
    off = CONV_HALO - (CONV_WIDTH - 1)
    acc = jnp.zeros((ts, CONV_CH), F32) + b_ref[...]
    for w in range(CONV_WIDTH):
        r, base = (off + w) % SUBLANES, (off + w) // SUBLANES * SUBLANES
        acc = acc + buf_ref[r, base:base + ts, :] * w_ref[w:w + 1, :]
    mu = jnp.mean(acc, axis=-1, keepdims=True)
    ctr = acc - mu
    var = jnp.mean(ctr * ctr, axis=-1, keepdims=True)
    y = ctr * lax.rsqrt(var + NORM_EPS) * lg_ref[...] + lb_ref[...]
    o_ref[...] = (y * _sigmoid(y)).astype(BF16)


def _conv_mixer(y, conv_w, conv_b, ln_g, ln_b, bsz, seq):
    ts = TOKEN_TILE
    per_halo = ts // CONV_HALO
    y3 = y.reshape(bsz, seq, CONV_CH)
    w_pad = jnp.zeros((CONV_HALO, CONV_CH), F32).at[:CONV_WIDTH].set(conv_w)
    out = pl.pallas_call(
        _conv_kernel,
        grid=(bsz, seq // ts),
        in_specs=[
            pl.BlockSpec((None, CONV_HALO, CONV_CH), lambda b, i: (b, jnp.maximum(i * per_halo - 1, 0), 0)),
            pl.BlockSpec((None, ts, CONV_CH), lambda b, i: (b, i, 0)),
            _const_spec((CONV_HALO, CONV_CH)), _const_spec((1, CONV_CH)), _const_spec((1, CONV_CH)),
            _const_spec((1, CONV_CH)),
        ],
        out_specs=pl.BlockSpec((None, ts, CONV_CH), lambda b, i: (b, i, 0)),
        out_shape=jax.ShapeDtypeStruct((bsz, seq, CONV_CH), BF16),
        scratch_shapes=[pltpu.VMEM((SUBLANES, ts + CONV_HALO, CONV_CH), F32)],
        compiler_params=_params("parallel", "parallel"),
        name="conv_mixer",
    )(y3, y3, w_pad, conv_b.reshape(1, -1), ln_g.reshape(1, -1), ln_b.reshape(1, -1))
    return out.reshape(bsz * seq, CONV_CH)


def _gla_kernel(q_ref, k_ref, v_ref, r_ref, low_ref, wg_ref, bg_ref, og_ref, tri_ref, am_ref, sm_ref, o_ref, st_ref):
    ch, sub = GLA_CHUNK, GLA_SUB
    tc = q_ref.shape[0]
    n_sub = ch // sub

    @pl.when(pl.program_id(1) == 0)
    def _():
        st_ref[...] = jnp.zeros_like(st_ref)

    head = _head_of_lane()
    key_row = _iota((ch, QK_W), 0)
    att_mask = am_ref[...]
    st_mask = sm_ref[...]

    logit = _dot(low_ref[...].astype(BF16), wg_ref[...]) + bg_ref[...]
    log_a = _neg_softplus(-logit) * (1.0 / GLA_TAU)
    la_hi, la_lo = _split_bf16(log_a)
    cr = GLA_CUM_ROWS
    cum_all = jnp.concatenate(
        [_dot(tri_ref[...], jnp.concatenate([la_hi[g * cr:(g + 1) * cr], la_lo[g * cr:(g + 1) * cr]], axis=0))
         for g in range(tc // cr)], axis=0)

    state = st_ref[...]
    for c in range(tc // ch):
        rows = slice(c * ch, (c + 1) * ch)
        cum = cum_all[rows, :]
        total = cum[ch - 1:ch, :]
        q = q_ref[rows, :] * (HEAD_LANES ** -0.5)
        k = k_ref[rows, :]
        v = v_ref[rows, :].astype(BF16)

        bases = [jnp.zeros((1, QK_W), F32)] + [cum[i * sub - 1:i * sub, :] for i in range(1, n_sub)]
        base_rows = jnp.concatenate([jnp.broadcast_to(b, (sub, QK_W)) for b in bases], axis=0)
        q_rel = q * jnp.exp(cum - base_rows)
        q_all = jnp.concatenate([jnp.where(head == h, q_rel, 0.0) for h in range(N_HEADS)], axis=0).astype(BF16)
        k_all = jnp.concatenate([jnp.where(key_row < (i + 1) * sub, k * jnp.exp(bases[i] - cum), 0.0)
                                 for i in range(n_sub)], axis=0).astype(BF16)
        att = (_dot_nt(q_all, k_all) * att_mask).astype(BF16)
        intra = _dot(att, jnp.concatenate([v] * n_sub, axis=0))
        o = jnp.concatenate([intra[h * ch:(h + 1) * ch, h * GLA_DV:(h + 1) * GLA_DV] for h in range(N_HEADS)], axis=1)

        o = o + _dot_nt((q * jnp.exp(cum)).astype(BF16), state.astype(BF16))
        k_dec = (k * jnp.exp(total - cum)).astype(BF16)
        state = state * jnp.exp(total) + _dot_tn(v, k_dec) * st_mask

        r = r_ref[rows, :]
        outs = []
        for h in range(N_HEADS):
            o_h = o[:, h * GLA_DV:(h + 1) * GLA_DV]
            ms = jnp.mean(o_h * o_h, axis=-1, keepdims=True)
            outs.append(o_h * lax.rsqrt(ms + NORM_EPS) * og_ref[...])
        o_ref[rows, :] = (r * _sigmoid(r) * jnp.concatenate(outs, axis=1)).astype(BF16)
    st_ref[...] = state


def _gla_masks(tc):
    ch, sub = GLA_CHUNK, GLA_SUB
    t = np.arange(tc)
    tri = (t[:, None] >= t[None, :]) & (t[:, None] // ch == t[None, :] // ch)
    a_t = np.arange(N_HEADS * ch) % ch
    a_col = np.arange((ch // sub) * ch)
    att = (a_col[None, :] // ch == a_t[:, None] // sub) & (a_col[None, :] % ch <= a_t[:, None])
    state = np.arange(GLA_V_W)[:, None] // GLA_DV == np.arange(QK_W)[None, :] // HEAD_LANES
    return (jnp.asarray(np.concatenate([tri, tri], axis=1), BF16), jnp.asarray(att, F32), jnp.asarray(state, F32))


def _gla_mixer(cbuf, w_gate, b_gate, o_gain, bsz, seq):
    tc = GLA_CHUNK * GLA_CHUNKS_PER_STEP
    c3 = cbuf.reshape(bsz, seq, C_W)
    wg = jnp.zeros((GLA_LOW_PAD, QK_W), F32).at[:GLA_RANK].set(w_gate).astype(BF16)
    tri2, att_mask, st_mask = _gla_masks(GLA_CUM_ROWS)
    spec = lambda w, j: pl.BlockSpec((None, tc, w), lambda b, i: (b, i, j))
    out = pl.pallas_call(
        _gla_kernel,
        grid=(bsz, seq // tc),
        in_specs=[
            spec(QK_W, 0), spec(QK_W, 1), spec(GLA_V_W, 1), spec(GLA_V_W, 2),
            spec(GLA_LOW_PAD, (2 * QK_W + 2 * GLA_V_W) // GLA_LOW_PAD),
            _const_spec((GLA_LOW_PAD, QK_W)), _const_spec((1, QK_W)), _const_spec((1, GLA_DV)),
            _const_spec(tri2.shape), _const_spec(att_mask.shape), _const_spec(st_mask.shape),
        ],
        out_specs=pl.BlockSpec((None, tc, GLA_V_W), lambda b, i: (b, i, 0)),
        out_shape=jax.ShapeDtypeStruct((bsz, seq, GLA_V_W), BF16),
        scratch_shapes=[pltpu.VMEM((GLA_V_W, QK_W), F32)],
        compiler_params=_params("parallel", "arbitrary"),
        name="gla_mixer",
    )(c3, c3, c3, c3, c3, wg, b_gate.reshape(1, -1), o_gain.reshape(1, -1), tri2, att_mask, st_mask)
    return out.reshape(bsz * seq, GLA_V_W)


def _sb_kernel(q_ref, k_ref, v_ref, o_ref, kmax_ref):
    t = SB_T
    seq = k_ref.shape[0]
    i = pl.program_id(1)
    head = _head_of_lane()
    head_ones = _ones_where(jnp.right_shift(_iota((QK_W, QK_W), 0), int(np.log2(HEAD_LANES))) == head)

    @pl.when(i == 0)
    def _():
        best = jnp.zeros((SB_NORM_ROWS, QK_W), F32)
        for c in range(seq // SB_NORM_ROWS):
            kf = k_ref[c * SB_NORM_ROWS:(c + 1) * SB_NORM_ROWS, :].astype(F32)
            sq_hi, sq_lo = _split_bf16(kf * kf)
            best = jnp.maximum(best, _dot(sq_hi, head_ones) + _dot(sq_lo, head_ones))
        kmax_ref[...] = jnp.zeros_like(kmax_ref) + jnp.max(best)

    tri2 = _ones_where(jnp.bitwise_and(_iota((2 * t, t), 0), t - 1) >= _iota((2 * t, t), 1))
    keep = _iota((N_HEADS * t, t), 1) < jnp.bitwise_and(_iota((N_HEADS * t, t), 0), t - 1)

    def query_block(u):
        q = q_ref[u * t:(u + 1) * t, :]
        q_stack = jnp.concatenate([jnp.where(head == h, q, jnp.zeros_like(q)) for h in range(N_HEADS)], axis=0)
        qf = q_stack.astype(F32)
        z_bound = jnp.sqrt(jnp.sum(qf * qf, axis=1, keepdims=True) * kmax_ref[0:1, 0:1]) * SB_BOUND_SLACK
        return q_stack, z_bound

    def block(j, acc, run, masked, q_stack, z_bound):
        start = pl.multiple_of(j * t, t)
        k_blk = k_ref[pl.ds(start, t), :]
        v_blk = v_ref[pl.ds(start, t), :]
        z = _dot_nt(q_stack, k_blk)
        drop = jnp.maximum(z, 0.0) + jnp.log(1.0 + jnp.exp2(jnp.abs(z) * (-LOG2_E)))
        if masked:
            drop = jnp.where(keep, drop, 0.0)
        d_hi, d_lo = _split_bf16(drop)
        suffix = _dot(jnp.concatenate([d_hi, d_lo], axis=1), tri2)
        w = jnp.exp(z - suffix - run)
        if masked:
            w = jnp.where(keep, w, 0.0)
        w = w.astype(BF16)
        w_wide = jnp.concatenate([w[h * t:(h + 1) * t, :] for h in range(N_HEADS)], axis=1)
        v_stack = jnp.concatenate([jnp.where(head == h, v_blk, jnp.zeros_like(v_blk)) for h in range(N_HEADS)], axis=0)
        acc = acc + _dot(w_wide, v_stack)
        run = run + jnp.sum(drop, axis=1, keepdims=True)
        done = (jnp.max(z_bound - run) < SB_ZERO_EXPONENT).astype(jnp.int32)
        return acc, run, done

    def walk(first_blocks):
        states = []
        for u in range(SB_QBLOCKS):
            iu = i * SB_QBLOCKS + u
            qz = query_block(u)
            acc, run, done = block(iu, jnp.zeros((t, QK_W), F32), jnp.zeros((N_HEADS * t, 1), F32), True, *qz)
            for n in range(1, first_blocks):
                acc, run, done = block(iu - n, acc, run, False, *qz)
            states.append((iu - first_blocks, done, acc, run, qz))
        for u, (j, done, acc, run, qz) in enumerate(states):

            def body(c, qz=qz):
                j, _, acc, run = c
                acc, run, done = block(j, acc, run, False, *qz)
                return j - 1, done, acc, run

            carry = lax.while_loop(lambda c: jnp.logical_and(c[0] >= 0, c[1] == 0), body, (j, done, acc, run))
            o_ref[u * t:(u + 1) * t, :] = carry[2].astype(BF16)

    enough_history = i * SB_QBLOCKS >= SB_STRAIGHT_BLOCKS - 1
    pl.when(enough_history)(functools.partial(walk, SB_STRAIGHT_BLOCKS))
    pl.when(jnp.logical_not(enough_history))(functools.partial(walk, 1))


def _stick_breaking_mixer(dbuf, bsz, seq):
    d3 = dbuf.reshape(bsz, seq, D_W)
    rows = SB_T * SB_QBLOCKS
    full = lambda j: pl.BlockSpec((None, seq, QK_W), lambda b, i: (b, 0, j))
    out = pl.pallas_call(
        _sb_kernel,
        grid=(bsz, seq // rows),
        in_specs=[pl.BlockSpec((None, rows, QK_W), lambda b, i: (b, i, 0)), full(1), full(2)],
        out_specs=pl.BlockSpec((None, rows, QK_W), lambda b, i: (b, i, 0)),
        out_shape=jax.ShapeDtypeStruct((bsz, seq, QK_W), BF16),
        scratch_shapes=[pltpu.VMEM((8, 128), F32)],
        compiler_params=_params("parallel", "arbitrary"),
        name="stick_breaking_mixer",
    )(d3, d3, d3)
    return out.reshape(bsz * seq, QK_W)


def _merge_kernel(x_ref, mod_ref, g_ref, oa_ref, ob_ref, oc_ref, od_ref,
                  wg_ref, wa_ref, wb_ref, wc_ref, wd_ref, wo_ref, out_ref):
    x = x_ref[...]
    h = _modulated_norm(x, g_ref[...], mod_ref[1:2, :], mod_ref[0:1, :]).astype(BF16)
    branches = ((oa_ref[...], wa_ref), (ob_ref[...], wb_ref), (oc_ref[...], wc_ref), (od_ref[...], wd_ref))
    merged = jnp.zeros(x.shape, F32)
    for j, (o_j, w_ref) in enumerate(branches):
        gate = _sigmoid(_dot(h, wg_ref[:, j * D_MODEL:(j + 1) * D_MODEL]))
        merged = merged + gate * _dot(o_j, w_ref[...])
    out_ref[...] = x + mod_ref[2:3, :] * _dot(merged.astype(BF16), wo_ref[...])


def _merge(x2d, mod, g_mix, oa, ob, oc, od, wg, wa, wb, wc, wd, wo, seq):
    t = x2d.shape[0]
    tm = TOKEN_TILE
    per_b = seq // tm
    row = lambda w: pl.BlockSpec((tm, w), lambda i: (i, 0))
    return pl.pallas_call(
        _merge_kernel,
        grid=(t // tm,),
        in_specs=[
            row(D_MODEL),
            pl.BlockSpec((None, 6, D_MODEL), lambda i: (i // per_b, 0, 0)),
            _const_spec((1, D_MODEL)),
            row(QK_W), row(CONV_CH), row(GLA_V_W), row(QK_W),
            _const_spec(wg.shape), _const_spec(wa.shape), _const_spec(wb.shape), _const_spec(wc.shape),
            _const_spec(wd.shape), _const_spec(wo.shape),
        ],
        out_specs=row(D_MODEL),
        out_shape=jax.ShapeDtypeStruct((t, D_MODEL), F32),
        compiler_params=_params("parallel"),
        name="gated_merge",
    )(x2d, mod, g_mix, oa, ob, oc, od, wg, wa, wb, wc, wd, wo)


def _mlp_kernel(x_ref, mod_ref, g_ref, wu_ref, wd_ref, out_ref):
    x = x_ref[...]
    h = _modulated_norm(x, g_ref[...], mod_ref[4:5, :], mod_ref[3:4, :]).astype(BF16)
    acc = jnp.zeros(x.shape, F32)
    for j in range(D_FF // D_MODEL):
        cols = slice(j * D_MODEL, (j + 1) * D_MODEL)
        u = jnp.maximum(_dot(h, wu_ref[:, cols]), 0.0)
        acc = acc + _dot((u * u).astype(BF16), wd_ref[cols, :])
    out_ref[...] = x + mod_ref[5:6, :] * acc


def _mlp(x2d, mod, g_mlp, wu, wd, seq):
    t = x2d.shape[0]
    tm = TOKEN_TILE
    per_b = seq // tm
    row = pl.BlockSpec((tm, D_MODEL), lambda i: (i, 0))
    return pl.pallas_call(
        _mlp_kernel,
        grid=(t // tm,),
        in_specs=[
            row,
            pl.BlockSpec((None, 6, D_MODEL), lambda i: (i // per_b, 0, 0)),
            _const_spec((1, D_MODEL)), _const_spec(wu.shape), _const_spec(wd.shape),
        ],
        out_specs=row,
        out_shape=jax.ShapeDtypeStruct((t, D_MODEL), F32),
        compiler_params=_params("parallel"),
        name="relu2_mlp",
    )(x2d, mod, g_mlp, wu, wd)


def _head_block_diag():
    idx = np.arange(QK_W) // HEAD_LANES
    return jnp.asarray((idx[:, None] == idx[None, :]).astype(np.float32) / HEAD_LANES, dtype=BF16)


def kernel(x, c, w_ada, b_ada, g_mix, w_in, q_gain, k_gain, conv_w, conv_b, conv_ln_g, conv_ln_b,
           gla_w_gate, gla_b_gate, gla_o_gain, w_br_a, w_br_b, w_br_c, w_br_d, w_out, g_mlp, w_up, w_down):
    bsz, seq, d = x.shape
    depth = w_ada.shape[0]
    assert d == D_MODEL and seq % (DSWA_BLOCK * DSWA_PATTERNS[-1][1]) == 0 and seq % TOKEN_TILE == 0
    mods = _modulation(c, w_ada, b_ada)
    bd = _head_block_diag()
    x2d = x.reshape(bsz * seq, d)
    qk_scale = HEAD_LANES ** -0.5
    for l in range(depth):
        w = w_in[l]
        o_conv = A_W
        o_c = o_conv + 2 * CONV_CH
        o_low = o_c + 2 * QK_W + 2 * GLA_V_W
        o_d = o_low + GLA_RANK
        o_gate = o_d + D_W
        wa = w[:, :A_W].astype(BF16)
        wb = w[:, o_conv:o_c].astype(BF16)
        wc = jnp.concatenate([w[:, o_c:o_low + GLA_RANK], jnp.zeros((d, GLA_LOW_PAD - GLA_RANK), F32)], axis=1).astype(BF16)
        wd = w[:, o_d:o_gate].astype(BF16)
        wg = w[:, o_gate:].astype(BF16)
        qg = jnp.tile(q_gain[l], N_HEADS).reshape(1, QK_W) * qk_scale
        kg = jnp.tile(k_gain[l], N_HEADS).reshape(1, QK_W)
        mod = mods[l]
        gm = g_mix[l].reshape(1, d)

        a, y, cbuf, dbuf = _in_projection(x2d, mod, gm, wa, wb, wc, wd, qg, kg, bd, seq)
        oa = _dilated_attention(a, bsz, seq)
        ob = _conv_mixer(y, conv_w[l], conv_b[l], conv_ln_g[l], conv_ln_b[l], bsz, seq)
        oc = _gla_mixer(cbuf, gla_w_gate[l], gla_b_gate[l], gla_o_gain[l], bsz, seq)
        od = _stick_breaking_mixer(dbuf, bsz, seq)
        x2d = _merge(x2d, mod, gm, oa, ob, oc, od, wg, w_br_a[l].astype(BF16), w_br_b[l].astype(BF16),
                     w_br_c[l].astype(BF16), w_br_d[l].astype(BF16), w_out[l].astype(BF16), seq)
        x2d = _mlp(x2d, mod, g_mlp[l].reshape(1, d), w_up[l].astype(BF16), w_down[l].astype(BF16), seq)
    return x2d.reshape(bsz, seq, d)
```

```python
import functools

import numpy as np
import jax
import jax.numpy as jnp
from jax import lax
from jax.experimental import pallas as pl
from jax.experimental.pallas import tpu as pltpu

F32 = jnp.float32
BF16 = jnp.bfloat16

D_MODEL = 1024
D_FF = 4 * D_MODEL
NORM_EPS = 1e-6
LOG2_E = 1.4426950408889634
HEAD_LANES = 64
N_HEADS = 4
QK_W = N_HEADS * HEAD_LANES
DSWA_PATTERNS = ((128, 1), (512, 4), (2048, 16))
DSWA_BLOCK = 128
DSWA_SPAN = DSWA_BLOCK * DSWA_PATTERNS[-1][1]
LANE_TILE = 128
SUBLANES = 8
CONV_CH = 256
CONV_WIDTH = 31
CONV_HALO = 32
GLA_DV = 128
GLA_V_W = N_HEADS * GLA_DV
GLA_RANK = 16
GLA_LOW_PAD = 128
GLA_TAU = 16.0
GLA_CHUNK = 64
GLA_SUB = 16
GLA_CHUNKS_PER_STEP = 16
GLA_CUM_ROWS = 256
SB_T = 256
SB_QBLOCKS = 2
SB_STRAIGHT_BLOCKS = 4
SB_NORM_ROWS = 512
SB_BOUND_SLACK = 1.02
SB_ZERO_EXPONENT = -110.0
TOKEN_TILE = 512
MASK_VALUE = -1e30
VMEM_LIMIT = 56 * 1024 * 1024

A_W = 3 * 3 * QK_W
C_W = 2 * QK_W + 2 * GLA_V_W + GLA_LOW_PAD
D_W = 3 * QK_W


def _dot(a, b):
    return jnp.dot(a, b, preferred_element_type=F32)


def _dot_nt(a, b):
    return lax.dot_general(a, b, (((1,), (1,)), ((), ())), preferred_element_type=F32)


def _dot_tn(a, b):
    return lax.dot_general(a, b, (((0,), (0,)), ((), ())), preferred_element_type=F32)


def _split_bf16(x):
    hi = x.astype(BF16)
    lo = (x - hi.astype(F32)).astype(BF16)
    return hi, lo


def _sigmoid(x):
    return 1.0 / (1.0 + jnp.exp(-x))


def _neg_softplus(x):
    return -(jnp.maximum(x, 0.0) + jnp.log(1.0 + jnp.exp(-jnp.abs(x))))


def _iota(shape, dim):
    return lax.broadcasted_iota(jnp.int32, shape, dim)


def _head_of_lane(width=QK_W, lanes_per_head=HEAD_LANES):
    return jnp.right_shift(_iota((1, width), 1), int(np.log2(lanes_per_head)))


def _ones_where(cond):
    return jnp.where(cond, 1.0, 0.0).astype(BF16)


def _const_spec(shape):
    nd = len(shape)
    return pl.BlockSpec(shape, lambda *_: (0,) * nd, pipeline_mode=pl.Buffered(1))


def _params(*sem):
    return pltpu.CompilerParams(dimension_semantics=sem, vmem_limit_bytes=VMEM_LIMIT)


def _modulated_norm(x, gain, scale, shift):
    ms = jnp.mean(x * x, axis=-1, keepdims=True)
    y = x * lax.rsqrt(ms + NORM_EPS) * gain
    return y * (1.0 + scale) + shift


def _mod_kernel(c_ref, w_ref, b_ref, o_ref):
    c_hi, c_lo = _split_bf16(c_ref[...])
    w_hi, w_lo = _split_bf16(w_ref[...])
    o_ref[...] = _dot(c_hi, w_hi) + _dot(c_hi, w_lo) + _dot(c_lo, w_hi) + b_ref[...]


def _modulation(c, w_ada, b_ada):
    depth, d, six_d = w_ada.shape
    bsz = c.shape[0]
    out = pl.pallas_call(
        _mod_kernel,
        grid=(depth, six_d // d),
        in_specs=[
            pl.BlockSpec((bsz, d), lambda l, j: (0, 0)),
            pl.BlockSpec((None, d, d), lambda l, j: (l, 0, j)),
            pl.BlockSpec((None, 1, d), lambda l, j: (l, 0, j)),
        ],
        out_specs=pl.BlockSpec((None, bsz, d), lambda l, j: (l, 0, j)),
        out_shape=jax.ShapeDtypeStruct((depth, bsz, six_d), F32),
        compiler_params=_params("parallel", "parallel"),
        name="adaln_modulation",
    )(c, w_ada, b_ada.reshape(depth, 1, six_d))
    return out.reshape(depth, bsz, six_d // d, d)


def _inproj_kernel(x_ref, mod_ref, g_ref, wa_ref, wb_ref, wc_ref, wd_ref, qg_ref, kg_ref, bd_ref,
                   a_ref, y_ref, c_ref, d_ref):
    h = _modulated_norm(x_ref[...], g_ref[...], mod_ref[1:2, :], mod_ref[0:1, :]).astype(BF16)
    bd = bd_ref[...]
    a_all = _dot(h, wa_ref[...])
    for j in range(A_W // QK_W):
        cols = slice(j * QK_W, (j + 1) * QK_W)
        acc = a_all[:, cols]
        if j < 6:
            ms = _dot((acc * acc).astype(BF16), bd)
            gain = qg_ref[...] if j < 3 else kg_ref[...]
            acc = acc * lax.rsqrt(ms + NORM_EPS) * gain
        a_ref[:, cols] = acc.astype(BF16)
    u = _dot(h, wb_ref[...])
    y_ref[...] = u[:, :CONV_CH] * _sigmoid(u[:, CONV_CH:])
    c_ref[...] = _dot(h, wc_ref[...])
    d_all = _dot(h, wd_ref[...])
    d_ref[:, :QK_W] = (d_all[:, :QK_W] * (HEAD_LANES ** -0.5)).astype(BF16)
    d_ref[:, QK_W:] = d_all[:, QK_W:].astype(BF16)


def _in_projection(x2d, mod, g_mix, wa, wb, wc, wd, qg, kg, bd, seq):
    t = x2d.shape[0]
    tm = TOKEN_TILE
    per_b = seq // tm
    row = lambda w: pl.BlockSpec((tm, w), lambda i: (i, 0))
    return pl.pallas_call(
        _inproj_kernel,
        grid=(t // tm,),
        in_specs=[
            row(D_MODEL),
            pl.BlockSpec((None, 6, D_MODEL), lambda i: (i // per_b, 0, 0)),
            _const_spec((1, D_MODEL)),
            _const_spec(wa.shape), _const_spec(wb.shape), _const_spec(wc.shape), _const_spec(wd.shape),
            _const_spec((1, QK_W)), _const_spec((1, QK_W)), _const_spec((QK_W, QK_W)),
        ],
        out_specs=[row(A_W), row(CONV_CH), row(C_W), row(D_W)],
        out_shape=[
            jax.ShapeDtypeStruct((t, A_W), BF16),
            jax.ShapeDtypeStruct((t, CONV_CH), F32),
            jax.ShapeDtypeStruct((t, C_W), F32),
            jax.ShapeDtypeStruct((t, D_W), BF16),
        ],
        compiler_params=_params("parallel"),
        name="in_projection",
    )(x2d, mod, g_mix, wa, wb, wc, wd, qg, kg, bd)


def _dswa_group(position, span_idx, q_ref, kp_ref, kc_ref, vp_ref, vc_ref, o_ref, q_st, k_st, v_st, m_st, den_st, acc_st,
                *, slopes, window, dilation):
    blk, span = DSWA_BLOCK, DSWA_SPAN
    halo = blk * dilation
    for s in range(QK_W // LANE_TILE):
        lanes = slice(s * LANE_TILE, (s + 1) * LANE_TILE)
        q_st[s, :, :] = q_ref[:, lanes].astype(F32)
        k_st[s, 0:halo, :] = kp_ref[span - halo:span, lanes].astype(F32)
        k_st[s, halo:halo + span, :] = kc_ref[:, lanes].astype(F32)
        v_st[s, 0:halo, :] = vp_ref[span - halo:span, lanes].astype(F32)
        v_st[s, halo:halo + span, :] = vc_ref[:, lanes].astype(F32)
    head = _head_of_lane()
    qi = _iota((blk, 2 * blk), 0)
    kj = _iota((blk, 2 * blk), 1)
    rel = qi - kj + blk
    in_window = (rel >= 0) & (rel <= window // dilation)
    dist = (dilation * rel).astype(F32)
    bias = jnp.concatenate([jnp.where(in_window, -slopes[h] * dist, MASK_VALUE) for h in range(N_HEADS)], axis=0)
    first_block_penalty = jnp.where(_iota((1, 2 * blk), 1) < blk, MASK_VALUE, 0.0)

    def rows_of(ref, start, n):
        return jnp.concatenate([ref[s, pl.ds(start, n, stride=dilation), :] for s in range(QK_W // LANE_TILE)], axis=1)

    def put_rows(ref, start, val):
        for s in range(QK_W // LANE_TILE):
            ref[s, pl.ds(start, blk, stride=dilation), :] = val[:, s * LANE_TILE:(s + 1) * LANE_TILE]

    def step(it):
        n = it // dilation
        start = n * halo + it % dilation
        q = rows_of(q_st, start, blk).astype(BF16)
        k = rows_of(k_st, start, 2 * blk).astype(BF16)
        v = rows_of(v_st, start, 2 * blk).astype(BF16)
        q_stack = jnp.concatenate([jnp.where(head == h, q, jnp.zeros_like(q)) for h in range(N_HEADS)], axis=0)
        s = _dot_nt(q_stack, k) + bias
        if n == 0:
            s = s + jnp.where(span_idx == 0, first_block_penalty, 0.0)
        mx = jnp.max(s, axis=1, keepdims=True)
        p = jnp.exp(s - mx)
        den = jnp.sum(p, axis=1, keepdims=True)
        o_all = _dot(p.astype(BF16), v) * (1.0 / den)
        lse_all = mx + jnp.log(den)
        o_new = jnp.zeros((blk, QK_W), F32)
        lse = jnp.zeros((blk, QK_W), F32)
        for h in range(N_HEADS):
            in_head = head == h
            o_new = jnp.where(in_head, o_all[h * blk:(h + 1) * blk], o_new)
            lse = jnp.where(in_head, lse_all[h * blk:(h + 1) * blk], lse)
        if position == 0:
            m_new, acc_new = lse, o_new
        else:
            m_old = rows_of(m_st, start, blk)
            m_new = jnp.maximum(m_old, lse)
            w_old, w_new = jnp.exp(m_old - m_new), jnp.exp(lse - m_new)
            den_old = w_old if position == 1 else rows_of(den_st, start, blk) * w_old
            put_rows(den_st, start, den_old + w_new)
            acc_new = rows_of(acc_st, start, blk) * w_old + o_new * w_new
        if position < len(DSWA_PATTERNS) - 1:
            put_rows(m_st, start, m_new)
        put_rows(acc_st, start, acc_new)

    for it in range(span // blk):
        step(it)
    if position == len(DSWA_PATTERNS) - 1:
        for s in range(QK_W // LANE_TILE):
            o_ref[:, s * LANE_TILE:(s + 1) * LANE_TILE] = (acc_st[s, :, :] / den_st[s, :, :]).astype(BF16)


def _dswa_kernel(q_ref, kp_ref, kc_ref, vp_ref, vc_ref, o_ref, *scratch, slopes):
    span_idx, position = pl.program_id(1), pl.program_id(2)
    for pos in range(len(DSWA_PATTERNS)):
        g = _dswa_group_at(pos)
        window, dilation = DSWA_PATTERNS[g]
        pl.when(position == pos)(functools.partial(
            _dswa_group, pos, span_idx, q_ref, kp_ref, kc_ref, vp_ref, vc_ref, o_ref, *scratch,
            slopes=slopes[g * N_HEADS:(g + 1) * N_HEADS], window=window, dilation=dilation))


def _dswa_group_at(position):
    return len(DSWA_PATTERNS) - 1 - position


def _alibi_slopes(n):
    return [float(np.float32(2.0 ** (-8.0 * i / n))) for i in range(1, n + 1)]


def _dilated_attention(a, bsz, seq):
    span = DSWA_SPAN
    a3 = a.reshape(bsz, seq, A_W)
    n_groups = len(DSWA_PATTERNS)
    col = lambda part, pos: part * n_groups + _dswa_group_at(pos)
    cur = lambda part: pl.BlockSpec((None, span, QK_W), lambda b, i, p: (b, i, col(part, p)))
    prev = lambda part: pl.BlockSpec((None, span, QK_W), lambda b, i, p: (b, jnp.maximum(i - 1, 0), col(part, p)))
    slab = lambda rows: pltpu.VMEM((QK_W // LANE_TILE, rows, LANE_TILE), F32)
    out = pl.pallas_call(
        functools.partial(_dswa_kernel, slopes=_alibi_slopes(n_groups * N_HEADS)),
        grid=(bsz, seq // span, n_groups),
        in_specs=[cur(0), prev(1), cur(1), prev(2), cur(2)],
        out_specs=pl.BlockSpec((None, span, QK_W), lambda b, i, g: (b, i, 0)),
        out_shape=jax.ShapeDtypeStruct((bsz, seq, QK_W), BF16),
        scratch_shapes=[slab(span), slab(2 * span), slab(2 * span), slab(span), slab(span), slab(span)],
        compiler_params=_params("parallel", "arbitrary", "arbitrary"),
        name="dilated_attention",
    )(a3, a3, a3, a3, a3)
    return out.reshape(bsz * seq, QK_W)


def _conv_kernel(prev_ref, cur_ref, w_ref, b_ref, lg_ref, lb_ref, o_ref, buf_ref):
    ts = cur_ref.shape[0]
    first = pl.program_id(1) == 0
    buf_ref[0, 0:CONV_HALO, :] = jnp.where(first, 0.0, prev_ref[...])
    buf_ref[0, CONV_HALO:, :] = cur_ref[...]
    rows = ts + CONV_HALO - SUBLANES
    for r in range(1, SUBLANES):
        buf_ref[r, 0:rows, :] = buf_ref[0, r:r + rows, :]
    off = CONV_HALO - (CONV_WIDTH - 1)
    acc = jnp.zeros((ts, CONV_CH), F32) + b_ref[...]
    for w in range(CONV_WIDTH):
        r, base = (off + w) % SUBLANES, (off + w) // SUBLANES * SUBLANES
        acc = acc + buf_ref[r, base:base + ts, :] * w_ref[w:w + 1, :]
    mu = jnp.mean(acc, axis=-1, keepdims=True)
    ctr = acc - mu
    var = jnp.mean(ctr * ctr, axis=-1, keepdims=True)
    y = ctr * lax.rsqrt(var + NORM_EPS) * lg_ref[...] + lb_ref[...]
    o_ref[...] = (y * _sigmoid(y)).astype(BF16)


def _conv_mixer(y, conv_w, conv_b, ln_g, ln_b, bsz, seq):
    ts = TOKEN_TILE
    per_halo = ts // CONV_HALO
    y3 = y.reshape(bsz, seq, CONV_CH)
    w_pad = jnp.zeros((CONV_HALO, CONV_CH), F32).at[:CONV_WIDTH].set(conv_w)
    out = pl.pallas_call(
        _conv_kernel,
        grid=(bsz, seq // ts),
        in_specs=[
            pl.BlockSpec((None, CONV_HALO, CONV_CH), lambda b, i: (b, jnp.maximum(i * per_halo - 1, 0), 0)),
            pl.BlockSpec((None, ts, CONV_CH), lambda b, i: (b, i, 0)),
            _const_spec((CONV_HALO, CONV_CH)), _const_spec((1, CONV_CH)), _const_spec((1, CONV_CH)),
            _const_spec((1, CONV_CH)),
        ],
        out_specs=pl.BlockSpec((None, ts, CONV_CH), lambda b, i: (b, i, 0)),
        out_shape=jax.ShapeDtypeStruct((bsz, seq, CONV_CH), BF16),
        scratch_shapes=[pltpu.VMEM((SUBLANES, ts + CONV_HALO, CONV_CH), F32)],
        compiler_params=_params("parallel", "parallel"),
        name="conv_mixer",
    )(y3, y3, w_pad, conv_b.reshape(1, -1), ln_g.reshape(1, -1), ln_b.reshape(1, -1))
    return out.reshape(bsz * seq, CONV_CH)


def _gla_kernel(q_ref, k_ref, v_ref, r_ref, low_ref, wg_ref, bg_ref, og_ref, tri_ref, am_ref, sm_ref, o_ref, st_ref):
    ch, sub = GLA_CHUNK, GLA_SUB
    tc = q_ref.shape[0]
    n_sub = ch // sub

    @pl.when(pl.program_id(1) == 0)
    def _():
        st_ref[...] = jnp.zeros_like(st_ref)

    head = _head_of_lane()
    key_row = _iota((ch, QK_W), 0)
    att_mask = am_ref[...]
    st_mask = sm_ref[...]

    logit = _dot(low_ref[...].astype(BF16), wg_ref[...]) + bg_ref[...]
    log_a = _neg_softplus(-logit) * (1.0 / GLA_TAU)
    la_hi, la_lo = _split_bf16(log_a)
    cr = GLA_CUM_ROWS
    cum_all = jnp.concatenate(
        [_dot(tri_ref[...], jnp.concatenate([la_hi[g * cr:(g + 1) * cr], la_lo[g * cr:(g + 1) * cr]], axis=0))
         for g in range(tc // cr)], axis=0)

    state = st_ref[...]
    for c in range(tc // ch):
        rows = slice(c * ch, (c + 1) * ch)
        cum = cum_all[rows, :]
        total = cum[ch - 1:ch, :]
        q = q_ref[rows, :] * (HEAD_LANES ** -0.5)
        k = k_ref[rows, :]
        v = v_ref[rows, :].astype(BF16)

        bases = [jnp.zeros((1, QK_W), F32)] + [cum[i * sub - 1:i * sub, :] for i in range(1, n_sub)]
        base_rows = jnp.concatenate([jnp.broadcast_to(b, (sub, QK_W)) for b in bases], axis=0)
        q_rel = q * jnp.exp(cum - base_rows)
        q_all = jnp.concatenate([jnp.where(head == h, q_rel, 0.0) for h in range(N_HEADS)], axis=0).astype(BF16)
        k_all = jnp.concatenate([jnp.where(key_row < (i + 1) * sub, k * jnp.exp(bases[i] - cum), 0.0)
                                 for i in range(n_sub)], axis=0).astype(BF16)
        att = (_dot_nt(q_all, k_all) * att_mask).astype(BF16)
        intra = _dot(att, jnp.concatenate([v] * n_sub, axis=0))
        o = jnp.concatenate([intra[h * ch:(h + 1) * ch, h * GLA_DV:(h + 1) * GLA_DV] for h in range(N_HEADS)], axis=1)

        o = o + _dot_nt((q * jnp.exp(cum)).astype(BF16), state.astype(BF16))
        k_dec = (k * jnp.exp(total - cum)).astype(BF16)
        state = state * jnp.exp(total) + _dot_tn(v, k_dec) * st_mask

        r = r_ref[rows, :]
        outs = []
        for h in range(N_HEADS):
            o_h = o[:, h * GLA_DV:(h + 1) * GLA_DV]
            ms = jnp.mean(o_h * o_h, axis=-1, keepdims=True)
            outs.append(o_h * lax.rsqrt(ms + NORM_EPS) * og_ref[...])
        o_ref[rows, :] = (r * _sigmoid(r) * jnp.concatenate(outs, axis=1)).astype(BF16)
    st_ref[...] = state


def _gla_masks(tc):
    ch, sub = GLA_CHUNK, GLA_SUB
    t = np.arange(tc)
    tri = (t[:, None] >= t[None, :]) & (t[:, None] // ch == t[None, :] // ch)
    a_t = np.arange(N_HEADS * ch) % ch
    a_col = np.arange((ch // sub) * ch)
    att = (a_col[None, :] // ch == a_t[:, None] // sub) & (a_col[None, :] % ch <= a_t[:, None])
    state = np.arange(GLA_V_W)[:, None] // GLA_DV == np.arange(QK_W)[None, :] // HEAD_LANES
    return (jnp.asarray(np.concatenate([tri, tri], axis=1), BF16), jnp.asarray(att, F32), jnp.asarray(state, F32))


def _gla_mixer(cbuf, w_gate, b_gate, o_gain, bsz, seq):
    tc = GLA_CHUNK * GLA_CHUNKS_PER_STEP
    c3 = cbuf.reshape(bsz, seq, C_W)
    wg = jnp.zeros((GLA_LOW_PAD, QK_W), F32).at[:GLA_RANK].set(w_gate).astype(BF16)
    tri2, att_mask, st_mask = _gla_masks(GLA_CUM_ROWS)
    spec = lambda w, j: pl.BlockSpec((None, tc, w), lambda b, i: (b, i, j))
    out = pl.pallas_call(
        _gla_kernel,
        grid=(bsz, seq // tc),
        in_specs=[
            spec(QK_W, 0), spec(QK_W, 1), spec(GLA_V_W, 1), spec(GLA_V_W, 2),
            spec(GLA_LOW_PAD, (2 * QK_W + 2 * GLA_V_W) // GLA_LOW_PAD),
            _const_spec((GLA_LOW_PAD, QK_W)), _const_spec((1, QK_W)), _const_spec((1, GLA_DV)),
            _const_spec(tri2.shape), _const_spec(att_mask.shape), _const_spec(st_mask.shape),
        ],
        out_specs=pl.BlockSpec((None, tc, GLA_V_W), lambda b, i: (b, i, 0)),
        out_shape=jax.ShapeDtypeStruct((bsz, seq, GLA_V_W), BF16),
        scratch_shapes=[pltpu.VMEM((GLA_V_W, QK_W), F32)],
        compiler_params=_params("parallel", "arbitrary"),
        name="gla_mixer",
    )(c3, c3, c3, c3, c3, wg, b_gate.reshape(1, -1), o_gain.reshape(1, -1), tri2, att_mask, st_mask)
    return out.reshape(bsz * seq, GLA_V_W)


def _sb_kernel(q_ref, k_ref, v_ref, o_ref, kmax_ref):
    t = SB_T
    seq = k_ref.shape[0]
    i = pl.program_id(1)
    head = _head_of_lane()
    head_ones = _ones_where(jnp.right_shift(_iota((QK_W, QK_W), 0), int(np.log2(HEAD_LANES))) == head)

    @pl.when(i == 0)
    def _():
        best = jnp.zeros((SB_NORM_ROWS, QK_W), F32)
        for c in range(seq // SB_NORM_ROWS):
            kf = k_ref[c * SB_NORM_ROWS:(c + 1) * SB_NORM_ROWS, :].astype(F32)
            sq_hi, sq_lo = _split_bf16(kf * kf)
            best = jnp.maximum(best, _dot(sq_hi, head_ones) + _dot(sq_lo, head_ones))
        kmax_ref[...] = jnp.zeros_like(kmax_ref) + jnp.max(best)

    tri2 = _ones_where(jnp.bitwise_and(_iota((2 * t, t), 0), t - 1) >= _iota((2 * t, t), 1))
    keep = _iota((N_HEADS * t, t), 1) < jnp.bitwise_and(_iota((N_HEADS * t, t), 0), t - 1)

    def query_block(u):
        q = q_ref[u * t:(u + 1) * t, :]
        q_stack = jnp.concatenate([jnp.where(head == h, q, jnp.zeros_like(q)) for h in range(N_HEADS)], axis=0)
        qf = q_stack.astype(F32)
        z_bound = jnp.sqrt(jnp.sum(qf * qf, axis=1, keepdims=True) * kmax_ref[0:1, 0:1]) * SB_BOUND_SLACK
        return q_stack, z_bound

    def block(j, acc, run, masked, q_stack, z_bound):
        start = pl.multiple_of(j * t, t)
        k_blk = k_ref[pl.ds(start, t), :]
        v_blk = v_ref[pl.ds(start, t), :]
        z = _dot_nt(q_stack, k_blk)
        drop = jnp.maximum(z, 0.0) + jnp.log(1.0 + jnp.exp2(jnp.abs(z) * (-LOG2_E)))
        if masked:
            drop = jnp.where(keep, drop, 0.0)
        d_hi, d_lo = _split_bf16(drop)
        suffix = _dot(jnp.concatenate([d_hi, d_lo], axis=1), tri2)
        w = jnp.exp(z - suffix - run)
        if masked:
            w = jnp.where(keep, w, 0.0)
        w = w.astype(BF16)
        w_wide = jnp.concatenate([w[h * t:(h + 1) * t, :] for h in range(N_HEADS)], axis=1)
        v_stack = jnp.concatenate([jnp.where(head == h, v_blk, jnp.zeros_like(v_blk)) for h in range(N_HEADS)], axis=0)
        acc = acc + _dot(w_wide, v_stack)
        run = run + jnp.sum(drop, axis=1, keepdims=True)
        done = (jnp.max(z_bound - run) < SB_ZERO_EXPONENT).astype(jnp.int32)
        return acc, run, done

    def walk(first_blocks):
        states = []
        for u in range(SB_QBLOCKS):
            iu = i * SB_QBLOCKS + u
            qz = query_block(u)
            acc, run, done = block(iu, jnp.zeros((t, QK_W), F32), jnp.zeros((N_HEADS * t, 1), F32), True, *qz)
            for n in range(1, first_blocks):
                acc, run, done = block(iu - n, acc, run, False, *qz)
            states.append((iu - first_blocks, done, acc, run, qz))
        for u, (j, done, acc, run, qz) in enumerate(states):

            def body(c, qz=qz):
                j, _, acc, run = c
                acc, run, done = block(j, acc, run, False, *qz)
                return j - 1, done, acc, run

            carry = lax.while_loop(lambda c: jnp.logical_and(c[0] >= 0, c[1] == 0), body, (j, done, acc, run))
            o_ref[u * t:(u + 1) * t, :] = carry[2].astype(BF16)

    enough_history = i * SB_QBLOCKS >= SB_STRAIGHT_BLOCKS - 1
    pl.when(enough_history)(functools.partial(walk, SB_STRAIGHT_BLOCKS))
    pl.when(jnp.logical_not(enough_history))(functools.partial(walk, 1))


def _stick_breaking_mixer(dbuf, bsz, seq):
    d3 = dbuf.reshape(bsz, seq, D_W)
    rows = SB_T * SB_QBLOCKS
    full = lambda j: pl.BlockSpec((None, seq, QK_W), lambda b, i: (b, 0, j))
    out = pl.pallas_call(
        _sb_kernel,
        grid=(bsz, seq // rows),
        in_specs=[pl.BlockSpec((None, rows, QK_W), lambda b, i: (b, i, 0)), full(1), full(2)],
        out_specs=pl.BlockSpec((None, rows, QK_W), lambda b, i: (b, i, 0)),
        out_shape=jax.ShapeDtypeStruct((bsz, seq, QK_W), BF16),
        scratch_shapes=[pltpu.VMEM((8, 128), F32)],
        compiler_params=_params("parallel", "arbitrary"),
        name="stick_breaking_mixer",
    )(d3, d3, d3)
    return out.reshape(bsz * seq, QK_W)


def _merge_kernel(x_ref, mod_ref, g_ref, oa_ref, ob_ref, oc_ref, od_ref,
                  wg_ref, wa_ref, wb_ref, wc_ref, wd_ref, wo_ref, out_ref):
    x = x_ref[...]
    h = _modulated_norm(x, g_ref[...], mod_ref[1:2, :], mod_ref[0:1, :]).astype(BF16)
    branches = ((oa_ref[...], wa_ref), (ob_ref[...], wb_ref), (oc_ref[...], wc_ref), (od_ref[...], wd_ref))
    merged = jnp.zeros(x.shape, F32)
    for j, (o_j, w_ref) in enumerate(branches):
        gate = _sigmoid(_dot(h, wg_ref[:, j * D_MODEL:(j + 1) * D_MODEL]))
        merged = merged + gate * _dot(o_j, w_ref[...])
    out_ref[...] = x + mod_ref[2:3, :] * _dot(merged.astype(BF16), wo_ref[...])


def _merge(x2d, mod, g_mix, oa, ob, oc, od, wg, wa, wb, wc, wd, wo, seq):
    t = x2d.shape[0]
    tm = TOKEN_TILE
    per_b = seq // tm
    row = lambda w: pl.BlockSpec((tm, w), lambda i: (i, 0))
    return pl.pallas_call(
        _merge_kernel,
        grid=(t // tm,),
        in_specs=[
            row(D_MODEL),
            pl.BlockSpec((None, 6, D_MODEL), lambda i: (i // per_b, 0, 0)),
            _const_spec((1, D_MODEL)),
            row(QK_W), row(CONV_CH), row(GLA_V_W), row(QK_W),
            _const_spec(wg.shape), _const_spec(wa.shape), _const_spec(wb.shape), _const_spec(wc.shape),
            _const_spec(wd.shape), _const_spec(wo.shape),
        ],
        out_specs=row(D_MODEL),
        out_shape=jax.ShapeDtypeStruct((t, D_MODEL), F32),
        compiler_params=_params("parallel"),
        name="gated_merge",
    )(x2d, mod, g_mix, oa, ob, oc, od, wg, wa, wb, wc, wd, wo)


def _mlp_kernel(x_ref, mod_ref, g_ref, wu_ref, wd_ref, out_ref):
    x = x_ref[...]
    h = _modulated_norm(x, g_ref[...], mod_ref[4:5, :], mod_ref[3:4, :]).astype(BF16)
    acc = jnp.zeros(x.shape, F32)
    for j in range(D_FF // D_MODEL):
        cols = slice(j * D_MODEL, (j + 1) * D_MODEL)
        u = jnp.maximum(_dot(h, wu_ref[:, cols]), 0.0)
        acc = acc + _dot((u * u).astype(BF16), wd_ref[cols, :])
    out_ref[...] = x + mod_ref[5:6, :] * acc


def _mlp(x2d, mod, g_mlp, wu, wd, seq):
    t = x2d.shape[0]
    tm = TOKEN_TILE
    per_b = seq // tm
    row = pl.BlockSpec((tm, D_MODEL), lambda i: (i, 0))
    return pl.pallas_call(
        _mlp_kernel,
        grid=(t // tm,),
        in_specs=[
            row,
            pl.BlockSpec((None, 6, D_MODEL), lambda i: (i // per_b, 0, 0)),
            _const_spec((1, D_MODEL)), _const_spec(wu.shape), _const_spec(wd.shape),
        ],
        out_specs=row,
        out_shape=jax.ShapeDtypeStruct((t, D_MODEL), F32),
        compiler_params=_params("parallel"),
        name="relu2_mlp",
    )(x2d, mod, g_mlp, wu, wd)


def _head_block_diag():
    idx = np.arange(QK_W) // HEAD_LANES
    return jnp.asarray((idx[:, None] == idx[None, :]).astype(np.float32) / HEAD_LANES, dtype=BF16)


def kernel(x, c, w_ada, b_ada, g_mix, w_in, q_gain, k_gain, conv_w, conv_b, conv_ln_g, conv_ln_b,
           gla_w_gate, gla_b_gate, gla_o_gain, w_br_a, w_br_b, w_br_c, w_br_d, w_out, g_mlp, w_up, w_down):
    bsz, seq, d = x.shape
    depth = w_ada.shape[0]
    assert d == D_MODEL and seq % (DSWA_BLOCK * DSWA_PATTERNS[-1][1]) == 0 and seq % TOKEN_TILE == 0
    mods = _modulation(c, w_ada, b_ada)
    bd = _head_block_diag()
    x2d = x.reshape(bsz * seq, d)
    qk_scale = HEAD_LANES ** -0.5
    for l in range(depth):
        w = w_in[l]
        o_conv = A_W
        o_c = o_conv + 2 * CONV_CH
        o_low = o_c + 2 * QK_W + 2 * GLA_V_W
        o_d = o_low + GLA_RANK
        o_gate = o_d + D_W
        wa = w[:, :A_W].astype(BF16)
        wb = w[:, o_conv:o_c].astype(BF16)
        wc = jnp.concatenate([w[:, o_c:o_low + GLA_RANK], jnp.zeros((d, GLA_LOW_PAD - GLA_RANK), F32)], axis=1).astype(BF16)
        wd = w[:, o_d:o_gate].astype(BF16)
        wg = w[:, o_gate:].astype(BF16)
        qg = jnp.tile(q_gain[l], N_HEADS).reshape(1, QK_W) * qk_scale
        kg = jnp.tile(k_gain[l], N_HEADS).reshape(1, QK_W)
        mod = mods[l]
        gm = g_mix[l].reshape(1, d)

        a, y, cbuf, dbuf = _in_projection(x2d, mod, gm, wa, wb, wc, wd, qg, kg, bd, seq)
        oa = _dilated_attention(a, bsz, seq)
        ob = _conv_mixer(y, conv_w[l], conv_b[l], conv_ln_g[l], conv_ln_b[l], bsz, seq)
        oc = _gla_mixer(cbuf, gla_w_gate[l], gla_b_gate[l], gla_o_gain[l], bsz, seq)
        od = _stick_breaking_mixer(dbuf, bsz, seq)
        x2d = _merge(x2d, mod, gm, oa, ob, oc, od, wg, w_br_a[l].astype(BF16), w_br_b[l].astype(BF16),
                     w_br_c[l].astype(BF16), w_br_d[l].astype(BF16), w_out[l].astype(BF16), seq)
        x2d = _mlp(x2d, mod, g_mlp[l].reshape(1, d), w_up[l].astype(BF16), w_down[l].astype(BF16), seq)
    return x2d.reshape(bsz, seq, d)
```

```python
import functools

import numpy as np
import jax
import jax.numpy as jnp
from jax import lax
from jax.experimental import pallas as pl
from jax.experimental.pallas import tpu as pltpu

F32 = jnp.float32
BF16 = jnp.bfloat16

D_MODEL = 1024
D_FF = 4 * D_MODEL
NORM_EPS = 1e-6
LOG2_E = 1.4426950408889634
HEAD_LANES = 64
N_HEADS = 4
QK_W = N_HEADS * HEAD_LANES
DSWA_PATTERNS = ((128, 1), (512, 4), (2048, 16))
DSWA_BLOCK = 128
DSWA_SPAN = DSWA_BLOCK * DSWA_PATTERNS[-1][1]
LANE_TILE = 128
SUBLANES = 8
CONV_CH = 256
CONV_WIDTH = 31
CONV_HALO = 32
GLA_DV = 128
GLA_V_W = N_HEADS * GLA_DV
GLA_RANK = 16
GLA_LOW_PAD = 128
GLA_TAU = 16.0
GLA_CHUNK = 64
GLA_SUB = 16
GLA_CHUNKS_PER_STEP = 16
GLA_CUM_ROWS = 256
SB_T = 256
SB_QBLOCKS = 4
SB_STRAIGHT_BLOCKS = 4
SB_NORM_ROWS = 512
SB_BOUND_SLACK = 1.02
SB_ZERO_EXPONENT = -110.0
TOKEN_TILE = 512
MASK_VALUE = -1e30
VMEM_LIMIT = 56 * 1024 * 1024

A_W = 3 * 3 * QK_W
C_W = 2 * QK_W + 2 * GLA_V_W + GLA_LOW_PAD
D_W = 3 * QK_W


def _dot(a, b):
    return jnp.dot(a, b, preferred_element_type=F32)


def _dot_nt(a, b):
    return lax.dot_general(a, b, (((1,), (1,)), ((), ())), preferred_element_type=F32)


def _dot_tn(a, b):
    return lax.dot_general(a, b, (((0,), (0,)), ((), ())), preferred_element_type=F32)


def _split_bf16(x):
    hi = x.astype(BF16)
    lo = (x - hi.astype(F32)).astype(BF16)
    return hi, lo


def _sigmoid(x):
    return 1.0 / (1.0 + jnp.exp(-x))


def _neg_softplus(x):
    return -(jnp.maximum(x, 0.0) + jnp.log(1.0 + jnp.exp(-jnp.abs(x))))


def _iota(shape, dim):
    return lax.broadcasted_iota(jnp.int32, shape, dim)


def _head_of_lane(width=QK_W, lanes_per_head=HEAD_LANES):
    return jnp.right_shift(_iota((1, width), 1), int(np.log2(lanes_per_head)))


def _ones_where(cond):
    return jnp.where(cond, 1.0, 0.0).astype(BF16)


def _const_spec(shape):
    nd = len(shape)
    return pl.BlockSpec(shape, lambda *_: (0,) * nd, pipeline_mode=pl.Buffered(1))


def _params(*sem):
    return pltpu.CompilerParams(dimension_semantics=sem, vmem_limit_bytes=VMEM_LIMIT)


def _modulated_norm(x, gain, scale, shift):
    ms = jnp.mean(x * x, axis=-1, keepdims=True)
    y = x * lax.rsqrt(ms + NORM_EPS) * gain
    return y * (1.0 + scale) + shift


def _mod_kernel(c_ref, w_ref, b_ref, o_ref):
    c_hi, c_lo = _split_bf16(c_ref[...])
    w_hi, w_lo = _split_bf16(w_ref[...])
    o_ref[...] = _dot(c_hi, w_hi) + _dot(c_hi, w_lo) + _dot(c_lo, w_hi) + b_ref[...]


def _modulation(c, w_ada, b_ada):
    depth, d, six_d = w_ada.shape
    bsz = c.shape[0]
    out = pl.pallas_call(
        _mod_kernel,
        grid=(depth, six_d // d),
        in_specs=[
            pl.BlockSpec((bsz, d), lambda l, j: (0, 0)),
            pl.BlockSpec((None, d, d), lambda l, j: (l, 0, j)),
            pl.BlockSpec((None, 1, d), lambda l, j: (l, 0, j)),
        ],
        out_specs=pl.BlockSpec((None, bsz, d), lambda l, j: (l, 0, j)),
        out_shape=jax.ShapeDtypeStruct((depth, bsz, six_d), F32),
        compiler_params=_params("parallel", "parallel"),
        name="adaln_modulation",
    )(c, w_ada, b_ada.reshape(depth, 1, six_d))
    return out.reshape(depth, bsz, six_d // d, d)


def _inproj_kernel(x_ref, mod_ref, g_ref, wa_ref, wb_ref, wc_ref, wd_ref, qg_ref, kg_ref, bd_ref,
                   a_ref, y_ref, c_ref, d_ref):
    h = _modulated_norm(x_ref[...], g_ref[...], mod_ref[1:2, :], mod_ref[0:1, :]).astype(BF16)
    bd = bd_ref[...]
    a_all = _dot(h, wa_ref[...])
    for j in range(A_W // QK_W):
        cols = slice(j * QK_W, (j + 1) * QK_W)
        acc = a_all[:, cols]
        if j < 6:
            ms = _dot((acc * acc).astype(BF16), bd)
            gain = qg_ref[...] if j < 3 else kg_ref[...]
            acc = acc * lax.rsqrt(ms + NORM_EPS) * gain
        a_ref[:, cols] = acc.astype(BF16)
    u = _dot(h, wb_ref[...])
    y_ref[...] = u[:, :CONV_CH] * _sigmoid(u[:, CONV_CH:])
    c_ref[...] = _dot(h, wc_ref[...])
    d_all = _dot(h, wd_ref[...])
    d_ref[:, :QK_W] = (d_all[:, :QK_W] * (HEAD_LANES ** -0.5)).astype(BF16)
    d_ref[:, QK_W:] = d_all[:, QK_W:].astype(BF16)


def _in_projection(x2d, mod, g_mix, wa, wb, wc, wd, qg, kg, bd, seq):
    t = x2d.shape[0]
    tm = TOKEN_TILE
    per_b = seq // tm
    row = lambda w: pl.BlockSpec((tm, w), lambda i: (i, 0))
    return pl.pallas_call(
        _inproj_kernel,
        grid=(t // tm,),
        in_specs=[
            row(D_MODEL),
            pl.BlockSpec((None, 6, D_MODEL), lambda i: (i // per_b, 0, 0)),
            _const_spec((1, D_MODEL)),
            _const_spec(wa.shape), _const_spec(wb.shape), _const_spec(wc.shape), _const_spec(wd.shape),
            _const_spec((1, QK_W)), _const_spec((1, QK_W)), _const_spec((QK_W, QK_W)),
        ],
        out_specs=[row(A_W), row(CONV_CH), row(C_W), row(D_W)],
        out_shape=[
            jax.ShapeDtypeStruct((t, A_W), BF16),
            jax.ShapeDtypeStruct((t, CONV_CH), F32),
            jax.ShapeDtypeStruct((t, C_W), F32),
            jax.ShapeDtypeStruct((t, D_W), BF16),
        ],
        compiler_params=_params("parallel"),
        name="in_projection",
    )(x2d, mod, g_mix, wa, wb, wc, wd, qg, kg, bd)


def _dswa_group(position, span_idx, q_ref, kp_ref, kc_ref, vp_ref, vc_ref, o_ref, q_st, k_st, v_st, m_st, den_st, acc_st,
                *, slopes, window, dilation):
    blk, span = DSWA_BLOCK, DSWA_SPAN
    halo = blk * dilation
    for s in range(QK_W // LANE_TILE):
        lanes = slice(s * LANE_TILE, (s + 1) * LANE_TILE)
        q_st[s, :, :] = q_ref[:, lanes].astype(F32)
        k_st[s, 0:halo, :] = kp_ref[span - halo:span, lanes].astype(F32)
        k_st[s, halo:halo + span, :] = kc_ref[:, lanes].astype(F32)
        v_st[s, 0:halo, :] = vp_ref[span - halo:span, lanes].astype(F32)
        v_st[s, halo:halo + span, :] = vc_ref[:, lanes].astype(F32)
    head = _head_of_lane()
    qi = _iota((blk, 2 * blk), 0)
    kj = _iota((blk, 2 * blk), 1)
    rel = qi - kj + blk
    in_window = (rel >= 0) & (rel <= window // dilation)
    dist = (dilation * rel).astype(F32)
    bias = jnp.concatenate([jnp.where(in_window, -slopes[h] * dist, MASK_VALUE) for h in range(N_HEADS)], axis=0)
    first_block_penalty = jnp.where(_iota((1, 2 * blk), 1) < blk, MASK_VALUE, 0.0)

    def rows_of(ref, start, n):
        return jnp.concatenate([ref[s, pl.ds(start, n, stride=dilation), :] for s in range(QK_W // LANE_TILE)], axis=1)

    def put_rows(ref, start, val):
        for s in range(QK_W // LANE_TILE):
            ref[s, pl.ds(start, blk, stride=dilation), :] = val[:, s * LANE_TILE:(s + 1) * LANE_TILE]

    def step(it):
        n = it // dilation
        start = n * halo + it % dilation
        q = rows_of(q_st, start, blk).astype(BF16)
        k = rows_of(k_st, start, 2 * blk).astype(BF16)
        v = rows_of(v_st, start, 2 * blk).astype(BF16)
        q_stack = jnp.concatenate([jnp.where(head == h, q, jnp.zeros_like(q)) for h in range(N_HEADS)], axis=0)
        s = _dot_nt(q_stack, k) + bias
        if n == 0:
            s = s + jnp.where(span_idx == 0, first_block_penalty, 0.0)
        mx = jnp.max(s, axis=1, keepdims=True)
        p = jnp.exp(s - mx)
        den = jnp.sum(p, axis=1, keepdims=True)
        o_all = _dot(p.astype(BF16), v) * (1.0 / den)
        lse_all = mx + jnp.log(den)
        o_new = jnp.zeros((blk, QK_W), F32)
        lse = jnp.zeros((blk, QK_W), F32)
        for h in range(N_HEADS):
            in_head = head == h
            o_new = jnp.where(in_head, o_all[h * blk:(h + 1) * blk], o_new)
            lse = jnp.where(in_head, lse_all[h * blk:(h + 1) * blk], lse)
        if position == 0:
            m_new, acc_new = lse, o_new
        else:
            m_old = rows_of(m_st, start, blk)
            m_new = jnp.maximum(m_old, lse)
            w_old, w_new = jnp.exp(m_old - m_new), jnp.exp(lse - m_new)
            den_old = w_old if position == 1 else rows_of(den_st, start, blk) * w_old
            put_rows(den_st, start, den_old + w_new)
            acc_new = rows_of(acc_st, start, blk) * w_old + o_new * w_new
        if position < len(DSWA_PATTERNS) - 1:
            put_rows(m_st, start, m_new)
        put_rows(acc_st, start, acc_new)

    for it in range(span // blk):
        step(it)
    if position == len(DSWA_PATTERNS) - 1:
        for s in range(QK_W // LANE_TILE):
            o_ref[:, s * LANE_TILE:(s + 1) * LANE_TILE] = (acc_st[s, :, :] / den_st[s, :, :]).astype(BF16)


def _dswa_kernel(q_ref, kp_ref, kc_ref, vp_ref, vc_ref, o_ref, *scratch, slopes):
    span_idx, position = pl.program_id(1), pl.program_id(2)
    for pos in range(len(DSWA_PATTERNS)):
        g = _dswa_group_at(pos)
        window, dilation = DSWA_PATTERNS[g]
        pl.when(position == pos)(functools.partial(
            _dswa_group, pos, span_idx, q_ref, kp_ref, kc_ref, vp_ref, vc_ref, o_ref, *scratch,
            slopes=slopes[g * N_HEADS:(g + 1) * N_HEADS], window=window, dilation=dilation))


def _dswa_group_at(position):
    return len(DSWA_PATTERNS) - 1 - position


def _alibi_slopes(n):
    return [float(np.float32(2.0 ** (-8.0 * i / n))) for i in range(1, n + 1)]


def _dilated_attention(a, bsz, seq):
    span = DSWA_SPAN
    a3 = a.reshape(bsz, seq, A_W)
    n_groups = len(DSWA_PATTERNS)
    col = lambda part, pos: part * n_groups + _dswa_group_at(pos)
    cur = lambda part: pl.BlockSpec((None, span, QK_W), lambda b, i, p: (b, i, col(part, p)))
    prev = lambda part: pl.BlockSpec((None, span, QK_W), lambda b, i, p: (b, jnp.maximum(i - 1, 0), col(part, p)))
    slab = lambda rows: pltpu.VMEM((QK_W // LANE_TILE, rows, LANE_TILE), F32)
    out = pl.pallas_call(
        functools.partial(_dswa_kernel, slopes=_alibi_slopes(n_groups * N_HEADS)),
        grid=(bsz, seq // span, n_groups),
        in_specs=[cur(0), prev(1), cur(1), prev(2), cur(2)],
        out_specs=pl.BlockSpec((None, span, QK_W), lambda b, i, g: (b, i, 0)),
        out_shape=jax.ShapeDtypeStruct((bsz, seq, QK_W), BF16),
        scratch_shapes=[slab(span), slab(2 * span), slab(2 * span), slab(span), slab(span), slab(span)],
        compiler_params=_params("parallel", "arbitrary", "arbitrary"),
        name="dilated_attention",
    )(a3, a3, a3, a3, a3)
    return out.reshape(bsz * seq, QK_W)


def _conv_kernel(prev_ref, cur_ref, w_ref, b_ref, lg_ref, lb_ref, o_ref, buf_ref):
    ts = cur_ref.shape[0]
    first = pl.program_id(1) == 0
    buf_ref[0, 0:CONV_HALO, :] = jnp.where(first, 0.0, prev_ref[...])
    buf_ref[0, CONV_HALO:, :] = cur_ref[...]
    rows = ts + CONV_HALO - SUBLANES
    for r in range(1, SUBLANES):
        buf_ref[r, 0:rows, :] = buf_ref[0, r:r + rows, :]
    off = CONV_HALO - (CONV_WIDTH - 1)
    acc = jnp.zeros((ts, CONV_CH), F32) + b_ref[...]
    for w in range(CONV_WIDTH):
        r, base = (off + w) % SUBLANES, (off + w) // SUBLANES * SUBLANES
        acc = acc + buf_ref[r, base:base + ts, :] * w_ref[w:w + 1, :]
    mu = jnp.mean(acc, axis=-1, keepdims=True)
    ctr = acc - mu
    var = jnp.mean(ctr * ctr, axis=-1, keepdims=True)
    y = ctr * lax.rsqrt(var + NORM_EPS) * lg_ref[...] + lb_ref[...]
    o_ref[...] = (y * _sigmoid(y)).astype(BF16)


def _conv_mixer(y, conv_w, conv_b, ln_g, ln_b, bsz, seq):
    ts = TOKEN_TILE
    per_halo = ts // CONV_HALO
    y3 = y.reshape(bsz, seq, CONV_CH)
    w_pad = jnp.zeros((CONV_HALO, CONV_CH), F32).at[:CONV_WIDTH].set(conv_w)
    out = pl.pallas_call(
        _conv_kernel,
        grid=(bsz, seq // ts),
        in_specs=[
            pl.BlockSpec((None, CONV_HALO, CONV_CH), lambda b, i: (b, jnp.maximum(i * per_halo - 1, 0), 0)),
            pl.BlockSpec((None, ts, CONV_CH), lambda b, i: (b, i, 0)),
            _const_spec((CONV_HALO, CONV_CH)), _const_spec((1, CONV_CH)), _const_spec((1, CONV_CH)),
            _const_spec((1, CONV_CH)),
        ],
        out_specs=pl.BlockSpec((None, ts, CONV_CH), lambda b, i: (b, i, 0)),
        out_shape=jax.ShapeDtypeStruct((bsz, seq, CONV_CH), BF16),
        scratch_shapes=[pltpu.VMEM((SUBLANES, ts + CONV_HALO, CONV_CH), F32)],
        compiler_params=_params("parallel", "parallel"),
        name="conv_mixer",
    )(y3, y3, w_pad, conv_b.reshape(1, -1), ln_g.reshape(1, -1), ln_b.reshape(1, -1))
    return out.reshape(bsz * seq, CONV_CH)


def _gla_kernel(q_ref, k_ref, v_ref, r_ref, low_ref, wg_ref, bg_ref, og_ref, tri_ref, am_ref, sm_ref, o_ref, st_ref):
    ch, sub = GLA_CHUNK, GLA_SUB
    tc = q_ref.shape[0]
    n_sub = ch // sub

    @pl.when(pl.program_id(1) == 0)
    def _():
        st_ref[...] = jnp.zeros_like(st_ref)

    head = _head_of_lane()
    key_row = _iota((ch, QK_W), 0)
    att_mask = am_ref[...]
    st_mask = sm_ref[...]

    logit = _dot(low_ref[...].astype(BF16), wg_ref[...]) + bg_ref[...]
    log_a = _neg_softplus(-logit) * (1.0 / GLA_TAU)
    la_hi, la_lo = _split_bf16(log_a)
    cr = GLA_CUM_ROWS
    cum_all = jnp.concatenate(
        [_dot(tri_ref[...], jnp.concatenate([la_hi[g * cr:(g + 1) * cr], la_lo[g * cr:(g + 1) * cr]], axis=0))
         for g in range(tc // cr)], axis=0)

    state = st_ref[...]
    for c in range(tc // ch):
        rows = slice(c * ch, (c + 1) * ch)
        cum = cum_all[rows, :]
        total = cum[ch - 1:ch, :]
        q = q_ref[rows, :] * (HEAD_LANES ** -0.5)
        k = k_ref[rows, :]
        v = v_ref[rows, :].astype(BF16)

        bases = [jnp.zeros((1, QK_W), F32)] + [cum[i * sub - 1:i * sub, :] for i in range(1, n_sub)]
        base_rows = jnp.concatenate([jnp.broadcast_to(b, (sub, QK_W)) for b in bases], axis=0)
        q_rel = q * jnp.exp(cum - base_rows)
        q_all = jnp.concatenate([jnp.where(head == h, q_rel, 0.0) for h in range(N_HEADS)], axis=0).astype(BF16)
        k_all = jnp.concatenate([jnp.where(key_row < (i + 1) * sub, k * jnp.exp(bases[i] - cum), 0.0)
                                 for i in range(n_sub)], axis=0).astype(BF16)
        att = (_dot_nt(q_all, k_all) * att_mask).astype(BF16)
        intra = _dot(att, jnp.concatenate([v] * n_sub, axis=0))
        o = jnp.concatenate([intra[h * ch:(h + 1) * ch, h * GLA_DV:(h + 1) * GLA_DV] for h in range(N_HEADS)], axis=1)

        o = o + _dot_nt((q * jnp.exp(cum)).astype(BF16), state.astype(BF16))
        k_dec = (k * jnp.exp(total - cum)).astype(BF16)
        state = state * jnp.exp(total) + _dot_tn(v, k_dec) * st_mask

        r = r_ref[rows, :]
        outs = []
        for h in range(N_HEADS):
            o_h = o[:, h * GLA_DV:(h + 1) * GLA_DV]
            ms = jnp.mean(o_h * o_h, axis=-1, keepdims=True)
            outs.append(o_h * lax.rsqrt(ms + NORM_EPS) * og_ref[...])
        o_ref[rows, :] = (r * _sigmoid(r) * jnp.concatenate(outs, axis=1)).astype(BF16)
    st_ref[...] = state


def _gla_masks(tc):
    ch, sub = GLA_CHUNK, GLA_SUB
    t = np.arange(tc)
    tri = (t[:, None] >= t[None, :]) & (t[:, None] // ch == t[None, :] // ch)
    a_t = np.arange(N_HEADS * ch) % ch
    a_col = np.arange((ch // sub) * ch)
    att = (a_col[None, :] // ch == a_t[:, None] // sub) & (a_col[None, :] % ch <= a_t[:, None])
    state = np.arange(GLA_V_W)[:, None] // GLA_DV == np.arange(QK_W)[None, :] // HEAD_LANES
    return (jnp.asarray(np.concatenate([tri, tri], axis=1), BF16), jnp.asarray(att, F32), jnp.asarray(state, F32))


def _gla_mixer(cbuf, w_gate, b_gate, o_gain, bsz, seq):
    tc = GLA_CHUNK * GLA_CHUNKS_PER_STEP
    c3 = cbuf.reshape(bsz, seq, C_W)
    wg = jnp.zeros((GLA_LOW_PAD, QK_W), F32).at[:GLA_RANK].set(w_gate).astype(BF16)
    tri2, att_mask, st_mask = _gla_masks(GLA_CUM_ROWS)
    spec = lambda w, j: pl.BlockSpec((None, tc, w), lambda b, i: (b, i, j))
    out = pl.pallas_call(
        _gla_kernel,
        grid=(bsz, seq // tc),
        in_specs=[
            spec(QK_W, 0), spec(QK_W, 1), spec(GLA_V_W, 1), spec(GLA_V_W, 2),
            spec(GLA_LOW_PAD, (2 * QK_W + 2 * GLA_V_W) // GLA_LOW_PAD),
            _const_spec((GLA_LOW_PAD, QK_W)), _const_spec((1, QK_W)), _const_spec((1, GLA_DV)),
            _const_spec(tri2.shape), _const_spec(att_mask.shape), _const_spec(st_mask.shape),
        ],
        out_specs=pl.BlockSpec((None, tc, GLA_V_W), lambda b, i: (b, i, 0)),
        out_shape=jax.ShapeDtypeStruct((bsz, seq, GLA_V_W), BF16),
        scratch_shapes=[pltpu.VMEM((GLA_V_W, QK_W), F32)],
        compiler_params=_params("parallel", "arbitrary"),
        name="gla_mixer",
    )(c3, c3, c3, c3, c3, wg, b_gate.reshape(1, -1), o_gain.reshape(1, -1), tri2, att_mask, st_mask)
    return out.reshape(bsz * seq, GLA_V_W)


def _sb_kernel(q_ref, k_ref, v_ref, o_ref, kmax_ref):
    t = SB_T
    seq = k_ref.shape[0]
    i = pl.program_id(1)
    head = _head_of_lane()
    head_ones = _ones_where(jnp.right_shift(_iota((QK_W, QK_W), 0), int(np.log2(HEAD_LANES))) == head)

    @pl.when(i == 0)
    def _():
        best = jnp.zeros((SB_NORM_ROWS, QK_W), F32)
        for c in range(seq // SB_NORM_ROWS):
            kf = k_ref[c * SB_NORM_ROWS:(c + 1) * SB_NORM_ROWS, :].astype(F32)
            sq_hi, sq_lo = _split_bf16(kf * kf)
            best = jnp.maximum(best, _dot(sq_hi, head_ones) + _dot(sq_lo, head_ones))
        kmax_ref[...] = jnp.zeros_like(kmax_ref) + jnp.max(best)

    tri2 = _ones_where(jnp.bitwise_and(_iota((2 * t, t), 0), t - 1) >= _iota((2 * t, t), 1))
    keep = _iota((N_HEADS * t, t), 1) < jnp.bitwise_and(_iota((N_HEADS * t, t), 0), t - 1)

    def query_block(u):
        q = q_ref[u * t:(u + 1) * t, :]
        q_stack = jnp.concatenate([jnp.where(head == h, q, jnp.zeros_like(q)) for h in range(N_HEADS)], axis=0)
        qf = q_stack.astype(F32)
        z_bound = jnp.sqrt(jnp.sum(qf * qf, axis=1, keepdims=True) * kmax_ref[0:1, 0:1]) * SB_BOUND_SLACK
        return q_stack, z_bound

    def block(j, acc, run, masked, q_stack, z_bound):
        start = j * t if isinstance(j, int) else pl.multiple_of(j * t, t)
        k_blk = k_ref[pl.ds(start, t), :]
        v_blk = v_ref[pl.ds(start, t), :]
        z = _dot_nt(q_stack, k_blk)
        drop = jnp.maximum(z, 0.0) + jnp.log(1.0 + jnp.exp2(jnp.abs(z) * (-LOG2_E)))
        if masked:
            drop = jnp.where(keep, drop, 0.0)
        d_hi, d_lo = _split_bf16(drop)
        suffix = _dot(jnp.concatenate([d_hi, d_lo], axis=1), tri2)
        w = jnp.exp(z - suffix - run)
        if masked:
            w = jnp.where(keep, w, 0.0)
        w = w.astype(BF16)
        w_wide = jnp.concatenate([w[h * t:(h + 1) * t, :] for h in range(N_HEADS)], axis=1)
        v_stack = jnp.concatenate([jnp.where(head == h, v_blk, jnp.zeros_like(v_blk)) for h in range(N_HEADS)], axis=0)
        acc = acc + _dot(w_wide, v_stack)
        run = run + jnp.sum(drop, axis=1, keepdims=True)
        done = (jnp.max(z_bound - run) < SB_ZERO_EXPONENT).astype(jnp.int32)
        return acc, run, done

    def walk(first_blocks):
        states = []
        for u in range(SB_QBLOCKS):
            iu = i * SB_QBLOCKS + u
            qz = query_block(u)
            acc, run, done = block(iu, jnp.zeros((t, QK_W), F32), jnp.zeros((N_HEADS * t, 1), F32), True, *qz)
            for n in range(1, first_blocks):
                acc, run, done = block(iu - n, acc, run, False, *qz)
            states.append((iu - first_blocks, done, acc, run, qz))
        for u, (j, done, acc, run, qz) in enumerate(states):

            def body(c, qz=qz):
                j, _, acc, run = c
                acc, run, done = block(j, acc, run, False, *qz)
                return j - 1, done, acc, run

            carry = lax.while_loop(lambda c: jnp.logical_and(c[0] >= 0, c[1] == 0), body, (j, done, acc, run))
            o_ref[u * t:(u + 1) * t, :] = carry[2].astype(BF16)

    def first_step():
        for u in range(SB_QBLOCKS):
            qz = query_block(u)
            acc, run, _ = block(u, jnp.zeros((t, QK_W), F32), jnp.zeros((N_HEADS * t, 1), F32), True, *qz)
            for j in range(u - 1, -1, -1):
                acc, run, _ = block(j, acc, run, False, *qz)
            o_ref[u * t:(u + 1) * t, :] = acc.astype(BF16)

    assert SB_QBLOCKS >= SB_STRAIGHT_BLOCKS - 1
    pl.when(i > 0)(functools.partial(walk, SB_STRAIGHT_BLOCKS))
    pl.when(i == 0)(first_step)


def _stick_breaking_mixer(dbuf, bsz, seq):
    d3 = dbuf.reshape(bsz, seq, D_W)
    rows = SB_T * SB_QBLOCKS
    full = lambda j: pl.BlockSpec((None, seq, QK_W), lambda b, i: (b, 0, j))
    out = pl.pallas_call(
        _sb_kernel,
        grid=(bsz, seq // rows),
        in_specs=[pl.BlockSpec((None, rows, QK_W), lambda b, i: (b, i, 0)), full(1), full(2)],
        out_specs=pl.BlockSpec((None, rows, QK_W), lambda b, i: (b, i, 0)),
        out_shape=jax.ShapeDtypeStruct((bsz, seq, QK_W), BF16),
        scratch_shapes=[pltpu.VMEM((8, 128), F32)],
        compiler_params=_params("parallel", "arbitrary"),
        name="stick_breaking_mixer",
    )(d3, d3, d3)
    return out.reshape(bsz * seq, QK_W)


def _merge_kernel(x_ref, mod_ref, g_ref, oa_ref, ob_ref, oc_ref, od_ref,
                  wg_ref, wa_ref, wb_ref, wc_ref, wd_ref, wo_ref, out_ref):
    x = x_ref[...]
    h = _modulated_norm(x, g_ref[...], mod_ref[1:2, :], mod_ref[0:1, :]).astype(BF16)
    branches = ((oa_ref[...], wa_ref), (ob_ref[...], wb_ref), (oc_ref[...], wc_ref), (od_ref[...], wd_ref))
    merged = jnp.zeros(x.shape, F32)
    for j, (o_j, w_ref) in enumerate(branches):
        gate = _sigmoid(_dot(h, wg_ref[:, j * D_MODEL:(j + 1) * D_MODEL]))
        merged = merged + gate * _dot(o_j, w_ref[...])
    out_ref[...] = x + mod_ref[2:3, :] * _dot(merged.astype(BF16), wo_ref[...])


def _merge(x2d, mod, g_mix, oa, ob, oc, od, wg, wa, wb, wc, wd, wo, seq):
    t = x2d.shape[0]
    tm = TOKEN_TILE
    per_b = seq // tm
    row = lambda w: pl.BlockSpec((tm, w), lambda i: (i, 0))
    return pl.pallas_call(
        _merge_kernel,
        grid=(t // tm,),
        in_specs=[
            row(D_MODEL),
            pl.BlockSpec((None, 6, D_MODEL), lambda i: (i // per_b, 0, 0)),
            _const_spec((1, D_MODEL)),
            row(QK_W), row(CONV_CH), row(GLA_V_W), row(QK_W),
            _const_spec(wg.shape), _const_spec(wa.shape), _const_spec(wb.shape), _const_spec(wc.shape),
            _const_spec(wd.shape), _const_spec(wo.shape),
        ],
        out_specs=row(D_MODEL),
        out_shape=jax.ShapeDtypeStruct((t, D_MODEL), F32),
        compiler_params=_params("parallel"),
        name="gated_merge",
    )(x2d, mod, g_mix, oa, ob, oc, od, wg, wa, wb, wc, wd, wo)


def _mlp_kernel(x_ref, mod_ref, g_ref, wu_ref, wd_ref, out_ref):
    x = x_ref[...]
    h = _modulated_norm(x, g_ref[...], mod_ref[4:5, :], mod_ref[3:4, :]).astype(BF16)
    acc = jnp.zeros(x.shape, F32)
    for j in range(D_FF // D_MODEL):
        cols = slice(j * D_MODEL, (j + 1) * D_MODEL)
        u = jnp.maximum(_dot(h, wu_ref[:, cols]), 0.0)
        acc = acc + _dot((u * u).astype(BF16), wd_ref[cols, :])
    out_ref[...] = x + mod_ref[5:6, :] * acc


def _mlp(x2d, mod, g_mlp, wu, wd, seq):
    t = x2d.shape[0]
    tm = TOKEN_TILE
    per_b = seq // tm
    row = pl.BlockSpec((tm, D_MODEL), lambda i: (i, 0))
    return pl.pallas_call(
        _mlp_kernel,
        grid=(t // tm,),
        in_specs=[
            row,
            pl.BlockSpec((None, 6, D_MODEL), lambda i: (i // per_b, 0, 0)),
            _const_spec((1, D_MODEL)), _const_spec(wu.shape), _const_spec(wd.shape),
        ],
        out_specs=row,
        out_shape=jax.ShapeDtypeStruct((t, D_MODEL), F32),
        compiler_params=_params("parallel"),
        name="relu2_mlp",
    )(x2d, mod, g_mlp, wu, wd)


def _head_block_diag():
    idx = np.arange(QK_W) // HEAD_LANES
    return jnp.asarray((idx[:, None] == idx[None, :]).astype(np.float32) / HEAD_LANES, dtype=BF16)


def kernel(x, c, w_ada, b_ada, g_mix, w_in, q_gain, k_gain, conv_w, conv_b, conv_ln_g, conv_ln_b,
           gla_w_gate, gla_b_gate, gla_o_gain, w_br_a, w_br_b, w_br_c, w_br_d, w_out, g_mlp, w_up, w_down):
    bsz, seq, d = x.shape
    depth = w_ada.shape[0]
    assert d == D_MODEL and seq % (DSWA_BLOCK * DSWA_PATTERNS[-1][1]) == 0 and seq % TOKEN_TILE == 0
    mods = _modulation(c, w_ada, b_ada)
    bd = _head_block_diag()
    x2d = x.reshape(bsz * seq, d)
    qk_scale = HEAD_LANES ** -0.5
    for l in range(depth):
        w = w_in[l]
        o_conv = A_W
        o_c = o_conv + 2 * CONV_CH
        o_low = o_c + 2 * QK_W + 2 * GLA_V_W
        o_d = o_low + GLA_RANK
        o_gate = o_d + D_W
        wa = w[:, :A_W].astype(BF16)
        wb = w[:, o_conv:o_c].astype(BF16)
        wc = jnp.concatenate([w[:, o_c:o_low + GLA_RANK], jnp.zeros((d, GLA_LOW_PAD - GLA_RANK), F32)], axis=1).astype(BF16)
        wd = w[:, o_d:o_gate].astype(BF16)
        wg = w[:, o_gate:].astype(BF16)
        qg = jnp.tile(q_gain[l], N_HEADS).reshape(1, QK_W) * qk_scale
        kg = jnp.tile(k_gain[l], N_HEADS).reshape(1, QK_W)
        mod = mods[l]
        gm = g_mix[l].reshape(1, d)

        a, y, cbuf, dbuf = _in_projection(x2d, mod, gm, wa, wb, wc, wd, qg, kg, bd, seq)
        oa = _dilated_attention(a, bsz, seq)
        ob = _conv_mixer(y, conv_w[l], conv_b[l], conv_ln_g[l], conv_ln_b[l], bsz, seq)
        oc = _gla_mixer(cbuf, gla_w_gate[l], gla_b_gate[l], gla_o_gain[l], bsz, seq)
        od = _stick_breaking_mixer(dbuf, bsz, seq)
        x2d = _merge(x2d, mod, gm, oa, ob, oc, od, wg, w_br_a[l].astype(BF16), w_br_b[l].astype(BF16),
                     w_br_c[l].astype(BF16), w_br_d[l].astype(BF16), w_out[l].astype(BF16), seq)
        x2d = _mlp(x2d, mod, g_mlp[l].reshape(1, d), w_up[l].astype(BF16), w_down[l].astype(BF16), seq)
    return x2d.reshape(bsz, seq, d)
```

```python
import functools

import numpy as np
import jax
import jax.numpy as jnp
from jax import lax
from jax.experimental import pallas as pl
from jax.experimental.pallas import tpu as pltpu

F32 = jnp.float32
BF16 = jnp.bfloat16

D_MODEL = 1024
D_FF = 4 * D_MODEL
NORM_EPS = 1e-6
LOG2_E = 1.4426950408889634
HEAD_LANES = 64
N_HEADS = 4
QK_W = N_HEADS * HEAD_LANES
DSWA_PATTERNS = ((128, 1), (512, 4), (2048, 16))
DSWA_BLOCK = 128
DSWA_SPAN = DSWA_BLOCK * DSWA_PATTERNS[-1][1]
LANE_TILE = 128
SUBLANES = 8
CONV_CH = 256
CONV_WIDTH = 31
CONV_HALO = 32
GLA_DV = 128
GLA_V_W = N_HEADS * GLA_DV
GLA_RANK = 16
GLA_LOW_PAD = 128
GLA_TAU = 16.0
GLA_CHUNK = 64
GLA_SUB = 16
GLA_CHUNKS_PER_STEP = 16
GLA_CUM_ROWS = 256
SB_T = 256
SB_QBLOCKS = 4
SB_STRAIGHT_BLOCKS = 4
SB_NORM_ROWS = 512
SB_BOUND_SLACK = 1.02
SB_ZERO_EXPONENT = -110.0
TOKEN_TILE = 512
MASK_VALUE = -1e30
VMEM_LIMIT = 56 * 1024 * 1024

A_W = 3 * 3 * QK_W
C_W = 2 * QK_W + 2 * GLA_V_W + GLA_LOW_PAD
D_W = 3 * QK_W


def _dot(a, b):
    return jnp.dot(a, b, preferred_element_type=F32)


def _dot_nt(a, b):
    return lax.dot_general(a, b, (((1,), (1,)), ((), ())), preferred_element_type=F32)


def _dot_tn(a, b):
    return lax.dot_general(a, b, (((0,), (0,)), ((), ())), preferred_element_type=F32)


def _split_bf16(x):
    hi = x.astype(BF16)
    lo = (x - hi.astype(F32)).astype(BF16)
    return hi, lo


def _sigmoid(x):
    return 1.0 / (1.0 + jnp.exp(-x))


def _neg_softplus(x):
    return -(jnp.maximum(x, 0.0) + jnp.log(1.0 + jnp.exp(-jnp.abs(x))))


def _iota(shape, dim):
    return lax.broadcasted_iota(jnp.int32, shape, dim)


def _head_of_lane(width=QK_W, lanes_per_head=HEAD_LANES):
    return jnp.right_shift(_iota((1, width), 1), int(np.log2(lanes_per_head)))


def _ones_where(cond):
    return jnp.where(cond, 1.0, 0.0).astype(BF16)


def _const_spec(shape):
    nd = len(shape)
    return pl.BlockSpec(shape, lambda *_: (0,) * nd, pipeline_mode=pl.Buffered(1))


def _params(*sem):
    return pltpu.CompilerParams(dimension_semantics=sem, vmem_limit_bytes=VMEM_LIMIT)


def _modulated_norm(x, gain, scale, shift):
    ms = jnp.mean(x * x, axis=-1, keepdims=True)
    y = x * lax.rsqrt(ms + NORM_EPS) * gain
    return y * (1.0 + scale) + shift


def _mod_kernel(c_ref, w_ref, b_ref, o_ref):
    o_ref[...] = _dot(c_ref[...].astype(BF16), w_ref[...].astype(BF16)) + b_ref[...]


def _modulation(c, w_ada, b_ada):
    depth, d, six_d = w_ada.shape
    bsz = c.shape[0]
    out = pl.pallas_call(
        _mod_kernel,
        grid=(depth, six_d // d),
        in_specs=[
            pl.BlockSpec((bsz, d), lambda l, j: (0, 0)),
            pl.BlockSpec((None, d, d), lambda l, j: (l, 0, j)),
            pl.BlockSpec((None, 1, d), lambda l, j: (l, 0, j)),
        ],
        out_specs=pl.BlockSpec((None, bsz, d), lambda l, j: (l, 0, j)),
        out_shape=jax.ShapeDtypeStruct((depth, bsz, six_d), F32),
        compiler_params=_params("parallel", "parallel"),
        name="adaln_modulation",
    )(c, w_ada, b_ada.reshape(depth, 1, six_d))
    return out.reshape(depth, bsz, six_d // d, d)


def _inproj_kernel(x_ref, mod_ref, g_ref, wa_ref, wb_ref, wc_ref, wd_ref, qg_ref, kg_ref, bd_ref,
                   a_ref, y_ref, c_ref, d_ref):
    h = _modulated_norm(x_ref[...], g_ref[...], mod_ref[1:2, :], mod_ref[0:1, :]).astype(BF16)
    bd = bd_ref[...]
    a_all = _dot(h, wa_ref[...])
    for j in range(A_W // QK_W):
        cols = slice(j * QK_W, (j + 1) * QK_W)
        acc = a_all[:, cols]
        if j < 6:
            ms = _dot((acc * acc).astype(BF16), bd)
            gain = qg_ref[...] if j < 3 else kg_ref[...]
            acc = acc * lax.rsqrt(ms + NORM_EPS) * gain
        a_ref[:, cols] = acc.astype(BF16)
    u = _dot(h, wb_ref[...])
    y_ref[...] = u[:, :CONV_CH] * _sigmoid(u[:, CONV_CH:])
    c_ref[...] = _dot(h, wc_ref[...])
    d_all = _dot(h, wd_ref[...])
    d_ref[:, :QK_W] = (d_all[:, :QK_W] * (HEAD_LANES ** -0.5)).astype(BF16)
    d_ref[:, QK_W:] = d_all[:, QK_W:].astype(BF16)


def _in_projection(x2d, mod, g_mix, wa, wb, wc, wd, qg, kg, bd, seq):
    t = x2d.shape[0]
    tm = TOKEN_TILE
    per_b = seq // tm
    row = lambda w: pl.BlockSpec((tm, w), lambda i: (i, 0))
    return pl.pallas_call(
        _inproj_kernel,
        grid=(t // tm,),
        in_specs=[
            row(D_MODEL),
            pl.BlockSpec((None, 6, D_MODEL), lambda i: (i // per_b, 0, 0)),
            _const_spec((1, D_MODEL)),
            _const_spec(wa.shape), _const_spec(wb.shape), _const_spec(wc.shape), _const_spec(wd.shape),
            _const_spec((1, QK_W)), _const_spec((1, QK_W)), _const_spec((QK_W, QK_W)),
        ],
        out_specs=[row(A_W), row(CONV_CH), row(C_W), row(D_W)],
        out_shape=[
            jax.ShapeDtypeStruct((t, A_W), BF16),
            jax.ShapeDtypeStruct((t, CONV_CH), F32),
            jax.ShapeDtypeStruct((t, C_W), F32),
            jax.ShapeDtypeStruct((t, D_W), BF16),
        ],
        compiler_params=_params("parallel"),
        name="in_projection",
    )(x2d, mod, g_mix, wa, wb, wc, wd, qg, kg, bd)


def _dswa_group(position, span_idx, q_ref, kp_ref, kc_ref, vp_ref, vc_ref, o_ref, q_st, k_st, v_st, m_st, den_st, acc_st,
                *, slopes, window, dilation):
    blk, span = DSWA_BLOCK, DSWA_SPAN
    halo = blk * dilation
    for s in range(QK_W // LANE_TILE):
        lanes = slice(s * LANE_TILE, (s + 1) * LANE_TILE)
        q_st[s, :, :] = q_ref[:, lanes].astype(F32)
        k_st[s, 0:halo, :] = kp_ref[span - halo:span, lanes].astype(F32)
        k_st[s, halo:halo + span, :] = kc_ref[:, lanes].astype(F32)
        v_st[s, 0:halo, :] = vp_ref[span - halo:span, lanes].astype(F32)
        v_st[s, halo:halo + span, :] = vc_ref[:, lanes].astype(F32)
    head = _head_of_lane()
    qi = _iota((blk, 2 * blk), 0)
    kj = _iota((blk, 2 * blk), 1)
    rel = qi - kj + blk
    in_window = (rel >= 0) & (rel <= window // dilation)
    dist = (dilation * rel).astype(F32)
    bias = jnp.concatenate([jnp.where(in_window, -slopes[h] * dist, MASK_VALUE) for h in range(N_HEADS)], axis=0)
    first_block_penalty = jnp.where(_iota((1, 2 * blk), 1) < blk, MASK_VALUE, 0.0)

    def rows_of(ref, start, n):
        return jnp.concatenate([ref[s, pl.ds(start, n, stride=dilation), :] for s in range(QK_W // LANE_TILE)], axis=1)

    def put_rows(ref, start, val):
        for s in range(QK_W // LANE_TILE):
            ref[s, pl.ds(start, blk, stride=dilation), :] = val[:, s * LANE_TILE:(s + 1) * LANE_TILE]

    def step(it):
        n = it // dilation
        start = n * halo + it % dilation
        q = rows_of(q_st, start, blk).astype(BF16)
        k = rows_of(k_st, start, 2 * blk).astype(BF16)
        v = rows_of(v_st, start, 2 * blk).astype(BF16)
        q_stack = jnp.concatenate([jnp.where(head == h, q, jnp.zeros_like(q)) for h in range(N_HEADS)], axis=0)
        s = _dot_nt(q_stack, k) + bias
        if n == 0:
            s = s + jnp.where(span_idx == 0, first_block_penalty, 0.0)
        mx = jnp.max(s, axis=1, keepdims=True)
        p = jnp.exp(s - mx)
        den = jnp.sum(p, axis=1, keepdims=True)
        o_all = _dot(p.astype(BF16), v) * (1.0 / den)
        lse_all = mx + jnp.log(den)
        o_new = jnp.zeros((blk, QK_W), F32)
        lse = jnp.zeros((blk, QK_W), F32)
        for h in range(N_HEADS):
            in_head = head == h
            o_new = jnp.where(in_head, o_all[h * blk:(h + 1) * blk], o_new)
            lse = jnp.where(in_head, lse_all[h * blk:(h + 1) * blk], lse)
        if position == 0:
            m_new, acc_new = lse, o_new
        else:
            m_old = rows_of(m_st, start, blk)
            m_new = jnp.maximum(m_old, lse)
            w_old, w_new = jnp.exp(m_old - m_new), jnp.exp(lse - m_new)
            den_old = w_old if position == 1 else rows_of(den_st, start, blk) * w_old
            put_rows(den_st, start, den_old + w_new)
            acc_new = rows_of(acc_st, start, blk) * w_old + o_new * w_new
        if position < len(DSWA_PATTERNS) - 1:
            put_rows(m_st, start, m_new)
        put_rows(acc_st, start, acc_new)

    for it in range(span // blk):
        step(it)
    if position == len(DSWA_PATTERNS) - 1:
        for s in range(QK_W // LANE_TILE):
            o_ref[:, s * LANE_TILE:(s + 1) * LANE_TILE] = (acc_st[s, :, :] / den_st[s, :, :]).astype(BF16)


def _dswa_kernel(q_ref, kp_ref, kc_ref, vp_ref, vc_ref, o_ref, *scratch, slopes):
    span_idx, position = pl.program_id(1), pl.program_id(2)
    for pos in range(len(DSWA_PATTERNS)):
        g = _dswa_group_at(pos)
        window, dilation = DSWA_PATTERNS[g]
        pl.when(position == pos)(functools.partial(
            _dswa_group, pos, span_idx, q_ref, kp_ref, kc_ref, vp_ref, vc_ref, o_ref, *scratch,
            slopes=slopes[g * N_HEADS:(g + 1) * N_HEADS], window=window, dilation=dilation))


def _dswa_group_at(position):
    return len(DSWA_PATTERNS) - 1 - position


def _alibi_slopes(n):
    return [float(np.float32(2.0 ** (-8.0 * i / n))) for i in range(1, n + 1)]


def _dilated_attention(a, bsz, seq):
    span = DSWA_SPAN
    a3 = a.reshape(bsz, seq, A_W)
    n_groups = len(DSWA_PATTERNS)
    col = lambda part, pos: part * n_groups + _dswa_group_at(pos)
    cur = lambda part: pl.BlockSpec((None, span, QK_W), lambda b, i, p: (b, i, col(part, p)))
    prev = lambda part: pl.BlockSpec((None, span, QK_W), lambda b, i, p: (b, jnp.maximum(i - 1, 0), col(part, p)))
    slab = lambda rows: pltpu.VMEM((QK_W // LANE_TILE, rows, LANE_TILE), F32)
    out = pl.pallas_call(
        functools.partial(_dswa_kernel, slopes=_alibi_slopes(n_groups * N_HEADS)),
        grid=(bsz, seq // span, n_groups),
        in_specs=[cur(0), prev(1), cur(1), prev(2), cur(2)],
        out_specs=pl.BlockSpec((None, span, QK_W), lambda b, i, g: (b, i, 0)),
        out_shape=jax.ShapeDtypeStruct((bsz, seq, QK_W), BF16),
        scratch_shapes=[slab(span), slab(2 * span), slab(2 * span), slab(span), slab(span), slab(span)],
        compiler_params=_params("parallel", "arbitrary", "arbitrary"),
        name="dilated_attention",
    )(a3, a3, a3, a3, a3)
    return out.reshape(bsz * seq, QK_W)


def _conv_kernel(prev_ref, cur_ref, w_ref, b_ref, lg_ref, lb_ref, o_ref, buf_ref):
    ts = cur_ref.shape[0]
    first = pl.program_id(1) == 0
    buf_ref[0, 0:CONV_HALO, :] = jnp.where(first, 0.0, prev_ref[...])
    buf_ref[0, CONV_HALO:, :] = cur_ref[...]
    rows = ts + CONV_HALO - SUBLANES
    for r in range(1, SUBLANES):
        buf_ref[r, 0:rows, :] = buf_ref[0, r:r + rows, :]
    off = CONV_HALO - (CONV_WIDTH - 1)
    acc = jnp.zeros((ts, CONV_CH), F32) + b_ref[...]
    for w in range(CONV_WIDTH):
        r, base = (off + w) % SUBLANES, (off + w) // SUBLANES * SUBLANES
        acc = acc + buf_ref[r, base:base + ts, :] * w_ref[w:w + 1, :]
    mu = jnp.mean(acc, axis=-1, keepdims=True)
    ctr = acc - mu
    var = jnp.mean(ctr * ctr, axis=-1, keepdims=True)
    y = ctr * lax.rsqrt(var + NORM_EPS) * lg_ref[...] + lb_ref[...]
    o_ref[...] = (y * _sigmoid(y)).astype(BF16)


def _conv_mixer(y, conv_w, conv_b, ln_g, ln_b, bsz, seq):
    ts = TOKEN_TILE
    per_halo = ts // CONV_HALO
    y3 = y.reshape(bsz, seq, CONV_CH)
    w_pad = jnp.zeros((CONV_HALO, CONV_CH), F32).at[:CONV_WIDTH].set(conv_w)
    out = pl.pallas_call(
        _conv_kernel,
        grid=(bsz, seq // ts),
        in_specs=[
            pl.BlockSpec((None, CONV_HALO, CONV_CH), lambda b, i: (b, jnp.maximum(i * per_halo - 1, 0), 0)),
            pl.BlockSpec((None, ts, CONV_CH), lambda b, i: (b, i, 0)),
            _const_spec((CONV_HALO, CONV_CH)), _const_spec((1, CONV_CH)), _const_spec((1, CONV_CH)),
            _const_spec((1, CONV_CH)),
        ],
        out_specs=pl.BlockSpec((None, ts, CONV_CH), lambda b, i: (b, i, 0)),
        out_shape=jax.ShapeDtypeStruct((bsz, seq, CONV_CH), BF16),
        scratch_shapes=[pltpu.VMEM((SUBLANES, ts + CONV_HALO, CONV_CH), F32)],
        compiler_params=_params("parallel", "parallel"),
        name="conv_mixer",
    )(y3, y3, w_pad, conv_b.reshape(1, -1), ln_g.reshape(1, -1), ln_b.reshape(1, -1))
    return out.reshape(bsz * seq, CONV_CH)


def _gla_kernel(q_ref, k_ref, v_ref, r_ref, low_ref, wg_ref, bg_ref, og_ref, tri_ref, am_ref, sm_ref, o_ref, st_ref):
    ch, sub = GLA_CHUNK, GLA_SUB
    tc = q_ref.shape[0]
    n_sub = ch // sub

    @pl.when(pl.program_id(1) == 0)
    def _():
        st_ref[...] = jnp.zeros_like(st_ref)

    head = _head_of_lane()
    key_row = _iota((ch, QK_W), 0)
    att_mask = am_ref[...]
    st_mask = sm_ref[...]

    logit = _dot(low_ref[...].astype(BF16), wg_ref[...]) + bg_ref[...]
    log_a = _neg_softplus(-logit) * (1.0 / GLA_TAU)
    la_hi, la_lo = _split_bf16(log_a)
    cr = GLA_CUM_ROWS
    cum_all = jnp.concatenate(
        [_dot(tri_ref[...], jnp.concatenate([la_hi[g * cr:(g + 1) * cr], la_lo[g * cr:(g + 1) * cr]], axis=0))
         for g in range(tc // cr)], axis=0)

    state = st_ref[...]
    for c in range(tc // ch):
        rows = slice(c * ch, (c + 1) * ch)
        cum = cum_all[rows, :]
        total = cum[ch - 1:ch, :]
        q = q_ref[rows, :] * (HEAD_LANES ** -0.5)
        k = k_ref[rows, :]
        v = v_ref[rows, :].astype(BF16)

        bases = [jnp.zeros((1, QK_W), F32)] + [cum[i * sub - 1:i * sub, :] for i in range(1, n_sub)]
        base_rows = jnp.concatenate([jnp.broadcast_to(b, (sub, QK_W)) for b in bases], axis=0)
        q_rel = q * jnp.exp(cum - base_rows)
        q_all = jnp.concatenate([jnp.where(head == h, q_rel, 0.0) for h in range(N_HEADS)], axis=0).astype(BF16)
        k_all = jnp.concatenate([jnp.where(key_row < (i + 1) * sub, k * jnp.exp(bases[i] - cum), 0.0)
                                 for i in range(n_sub)], axis=0).astype(BF16)
        att = (_dot_nt(q_all, k_all) * att_mask).astype(BF16)
        intra = _dot(att, jnp.concatenate([v] * n_sub, axis=0))
        o = jnp.concatenate([intra[h * ch:(h + 1) * ch, h * GLA_DV:(h + 1) * GLA_DV] for h in range(N_HEADS)], axis=1)

        o = o + _dot_nt((q * jnp.exp(cum)).astype(BF16), state.astype(BF16))
        k_dec = (k * jnp.exp(total - cum)).astype(BF16)
        state = state * jnp.exp(total) + _dot_tn(v, k_dec) * st_mask

        r = r_ref[rows, :]
        outs = []
        for h in range(N_HEADS):
            o_h = o[:, h * GLA_DV:(h + 1) * GLA_DV]
            ms = jnp.mean(o_h * o_h, axis=-1, keepdims=True)
            outs.append(o_h * lax.rsqrt(ms + NORM_EPS) * og_ref[...])
        o_ref[rows, :] = (r * _sigmoid(r) * jnp.concatenate(outs, axis=1)).astype(BF16)
    st_ref[...] = state


def _gla_masks(tc):
    ch, sub = GLA_CHUNK, GLA_SUB
    t = np.arange(tc)
    tri = (t[:, None] >= t[None, :]) & (t[:, None] // ch == t[None, :] // ch)
    a_t = np.arange(N_HEADS * ch) % ch
    a_col = np.arange((ch // sub) * ch)
    att = (a_col[None, :] // ch == a_t[:, None] // sub) & (a_col[None, :] % ch <= a_t[:, None])
    state = np.arange(GLA_V_W)[:, None] // GLA_DV == np.arange(QK_W)[None, :] // HEAD_LANES
    return (jnp.asarray(np.concatenate([tri, tri], axis=1), BF16), jnp.asarray(att, F32), jnp.asarray(state, F32))


def _gla_mixer(cbuf, w_gate, b_gate, o_gain, bsz, seq):
    tc = GLA_CHUNK * GLA_CHUNKS_PER_STEP
    c3 = cbuf.reshape(bsz, seq, C_W)
    wg = jnp.zeros((GLA_LOW_PAD, QK_W), F32).at[:GLA_RANK].set(w_gate).astype(BF16)
    tri2, att_mask, st_mask = _gla_masks(GLA_CUM_ROWS)
    spec = lambda w, j: pl.BlockSpec((None, tc, w), lambda b, i: (b, i, j))
    out = pl.pallas_call(
        _gla_kernel,
        grid=(bsz, seq // tc),
        in_specs=[
            spec(QK_W, 0), spec(QK_W, 1), spec(GLA_V_W, 1), spec(GLA_V_W, 2),
            spec(GLA_LOW_PAD, (2 * QK_W + 2 * GLA_V_W) // GLA_LOW_PAD),
            _const_spec((GLA_LOW_PAD, QK_W)), _const_spec((1, QK_W)), _const_spec((1, GLA_DV)),
            _const_spec(tri2.shape), _const_spec(att_mask.shape), _const_spec(st_mask.shape),
        ],
        out_specs=pl.BlockSpec((None, tc, GLA_V_W), lambda b, i: (b, i, 0)),
        out_shape=jax.ShapeDtypeStruct((bsz, seq, GLA_V_W), BF16),
        scratch_shapes=[pltpu.VMEM((GLA_V_W, QK_W), F32)],
        compiler_params=_params("parallel", "arbitrary"),
        name="gla_mixer",
    )(c3, c3, c3, c3, c3, wg, b_gate.reshape(1, -1), o_gain.reshape(1, -1), tri2, att_mask, st_mask)
    return out.reshape(bsz * seq, GLA_V_W)


def _sb_kernel(q_ref, k_ref, v_ref, o_ref, kmax_ref):
    t = SB_T
    seq = k_ref.shape[0]
    i = pl.program_id(1)
    head = _head_of_lane()
    head_ones = _ones_where(jnp.right_shift(_iota((QK_W, QK_W), 0), int(np.log2(HEAD_LANES))) == head)

    @pl.when(i == 0)
    def _():
        best = jnp.zeros((SB_NORM_ROWS, QK_W), F32)
        for c in range(seq // SB_NORM_ROWS):
            kf = k_ref[c * SB_NORM_ROWS:(c + 1) * SB_NORM_ROWS, :].astype(F32)
            sq_hi, sq_lo = _split_bf16(kf * kf)
            best = jnp.maximum(best, _dot(sq_hi, head_ones) + _dot(sq_lo, head_ones))
        kmax_ref[...] = jnp.zeros_like(kmax_ref) + jnp.max(best)

    tri2 = _ones_where(jnp.bitwise_and(_iota((2 * t, t), 0), t - 1) >= _iota((2 * t, t), 1))
    keep = _iota((N_HEADS * t, t), 1) < jnp.bitwise_and(_iota((N_HEADS * t, t), 0), t - 1)

    def query_block(u):
        q = q_ref[u * t:(u + 1) * t, :]
        q_stack = jnp.concatenate([jnp.where(head == h, q, jnp.zeros_like(q)) for h in range(N_HEADS)], axis=0)
        qf = q_stack.astype(F32)
        z_bound = jnp.sqrt(jnp.sum(qf * qf, axis=1, keepdims=True) * kmax_ref[0:1, 0:1]) * SB_BOUND_SLACK
        return q_stack, z_bound

    def block(j, acc, run, masked, q_stack, z_bound):
        start = j * t if isinstance(j, int) else pl.multiple_of(j * t, t)
        k_blk = k_ref[pl.ds(start, t), :]
        v_blk = v_ref[pl.ds(start, t), :]
        z = _dot_nt(q_stack, k_blk)
        drop = jnp.maximum(z, 0.0) + jnp.log(1.0 + jnp.exp2(jnp.abs(z) * (-LOG2_E)))
        if masked:
            drop = jnp.where(keep, drop, 0.0)
        d_hi, d_lo = _split_bf16(drop)
        suffix = _dot(jnp.concatenate([d_hi, d_lo], axis=1), tri2)
        w = jnp.exp(z - suffix - run)
        if masked:
            w = jnp.where(keep, w, 0.0)
        w = w.astype(BF16)
        w_wide = jnp.concatenate([w[h * t:(h + 1) * t, :] for h in range(N_HEADS)], axis=1)
        v_stack = jnp.concatenate([jnp.where(head == h, v_blk, jnp.zeros_like(v_blk)) for h in range(N_HEADS)], axis=0)
        acc = acc + _dot(w_wide, v_stack)
        run = run + jnp.sum(drop, axis=1, keepdims=True)
        done = (jnp.max(z_bound - run) < SB_ZERO_EXPONENT).astype(jnp.int32)
        return acc, run, done

    def walk(first_blocks):
        states = []
        for u in range(SB_QBLOCKS):
            iu = i * SB_QBLOCKS + u
            qz = query_block(u)
            acc, run, done = block(iu, jnp.zeros((t, QK_W), F32), jnp.zeros((N_HEADS * t, 1), F32), True, *qz)
            for n in range(1, first_blocks):
                acc, run, done = block(iu - n, acc, run, False, *qz)
            states.append((iu - first_blocks, done, acc, run, qz))
        for u, (j, done, acc, run, qz) in enumerate(states):

            def body(c, qz=qz):
                j, _, acc, run = c
                acc, run, done = block(j, acc, run, False, *qz)
                return j - 1, done, acc, run

            carry = lax.while_loop(lambda c: jnp.logical_and(c[0] >= 0, c[1] == 0), body, (j, done, acc, run))
            o_ref[u * t:(u + 1) * t, :] = carry[2].astype(BF16)

    def first_step():
        for u in range(SB_QBLOCKS):
            qz = query_block(u)
            acc, run, _ = block(u, jnp.zeros((t, QK_W), F32), jnp.zeros((N_HEADS * t, 1), F32), True, *qz)
            for j in range(u - 1, -1, -1):
                acc, run, _ = block(j, acc, run, False, *qz)
            o_ref[u * t:(u + 1) * t, :] = acc.astype(BF16)

    assert SB_QBLOCKS >= SB_STRAIGHT_BLOCKS - 1
    pl.when(i > 0)(functools.partial(walk, SB_STRAIGHT_BLOCKS))
    pl.when(i == 0)(first_step)


def _stick_breaking_mixer(dbuf, bsz, seq):
    d3 = dbuf.reshape(bsz, seq, D_W)
    rows = SB_T * SB_QBLOCKS
    full = lambda j: pl.BlockSpec((None, seq, QK_W), lambda b, i: (b, 0, j))
    out = pl.pallas_call(
        _sb_kernel,
        grid=(bsz, seq // rows),
        in_specs=[pl.BlockSpec((None, rows, QK_W), lambda b, i: (b, i, 0)), full(1), full(2)],
        out_specs=pl.BlockSpec((None, rows, QK_W), lambda b, i: (b, i, 0)),
        out_shape=jax.ShapeDtypeStruct((bsz, seq, QK_W), BF16),
        scratch_shapes=[pltpu.VMEM((8, 128), F32)],
        compiler_params=_params("parallel", "arbitrary"),
        name="stick_breaking_mixer",
    )(d3, d3, d3)
    return out.reshape(bsz * seq, QK_W)


def _merge_kernel(x_ref, mod_ref, g_ref, oa_ref, ob_ref, oc_ref, od_ref,
                  wg_ref, wa_ref, wb_ref, wc_ref, wd_ref, wo_ref, out_ref):
    x = x_ref[...]
    h = _modulated_norm(x, g_ref[...], mod_ref[1:2, :], mod_ref[0:1, :]).astype(BF16)
    branches = ((oa_ref[...], wa_ref), (ob_ref[...], wb_ref), (oc_ref[...], wc_ref), (od_ref[...], wd_ref))
    merged = jnp.zeros(x.shape, F32)
    for j, (o_j, w_ref) in enumerate(branches):
        gate = _sigmoid(_dot(h, wg_ref[:, j * D_MODEL:(j + 1) * D_MODEL]))
        merged = merged + gate * _dot(o_j, w_ref[...])
    out_ref[...] = x + mod_ref[2:3, :] * _dot(merged.astype(BF16), wo_ref[...])


def _merge(x2d, mod, g_mix, oa, ob, oc, od, wg, wa, wb, wc, wd, wo, seq):
    t = x2d.shape[0]
    tm = TOKEN_TILE
    per_b = seq // tm
    row = lambda w: pl.BlockSpec((tm, w), lambda i: (i, 0))
    return pl.pallas_call(
        _merge_kernel,
        grid=(t // tm,),
        in_specs=[
            row(D_MODEL),
            pl.BlockSpec((None, 6, D_MODEL), lambda i: (i // per_b, 0, 0)),
            _const_spec((1, D_MODEL)),
            row(QK_W), row(CONV_CH), row(GLA_V_W), row(QK_W),
            _const_spec(wg.shape), _const_spec(wa.shape), _const_spec(wb.shape), _const_spec(wc.shape),
            _const_spec(wd.shape), _const_spec(wo.shape),
        ],
        out_specs=row(D_MODEL),
        out_shape=jax.ShapeDtypeStruct((t, D_MODEL), F32),
        compiler_params=_params("parallel"),
        name="gated_merge",
    )(x2d, mod, g_mix, oa, ob, oc, od, wg, wa, wb, wc, wd, wo)


def _mlp_kernel(x_ref, mod_ref, g_ref, wu_ref, wd_ref, out_ref):
    x = x_ref[...]
    h = _modulated_norm(x, g_ref[...], mod_ref[4:5, :], mod_ref[3:4, :]).astype(BF16)
    acc = jnp.zeros(x.shape, F32)
    for j in range(D_FF // D_MODEL):
        cols = slice(j * D_MODEL, (j + 1) * D_MODEL)
        u = jnp.maximum(_dot(h, wu_ref[:, cols]), 0.0)
        acc = acc + _dot((u * u).astype(BF16), wd_ref[cols, :])
    out_ref[...] = x + mod_ref[5:6, :] * acc


def _mlp(x2d, mod, g_mlp, wu, wd, seq):
    t = x2d.shape[0]
    tm = TOKEN_TILE
    per_b = seq // tm
    row = pl.BlockSpec((tm, D_MODEL), lambda i: (i, 0))
    return pl.pallas_call(
        _mlp_kernel,
        grid=(t // tm,),
        in_specs=[
            row,
            pl.BlockSpec((None, 6, D_MODEL), lambda i: (i // per_b, 0, 0)),
            _const_spec((1, D_MODEL)), _const_spec(wu.shape), _const_spec(wd.shape),
        ],
        out_specs=row,
        out_shape=jax.ShapeDtypeStruct((t, D_MODEL), F32),
        compiler_params=_params("parallel"),
        name="relu2_mlp",
    )(x2d, mod, g_mlp, wu, wd)


def _head_block_diag():
    idx = np.arange(QK_W) // HEAD_LANES
    return jnp.asarray((idx[:, None] == idx[None, :]).astype(np.float32) / HEAD_LANES, dtype=BF16)


def kernel(x, c, w_ada, b_ada, g_mix, w_in, q_gain, k_gain, conv_w, conv_b, conv_ln_g, conv_ln_b,
           gla_w_gate, gla_b_gate, gla_o_gain, w_br_a, w_br_b, w_br_c, w_br_d, w_out, g_mlp, w_up, w_down):
    bsz, seq, d = x.shape
    depth = w_ada.shape[0]
    assert d == D_MODEL and seq % (DSWA_BLOCK * DSWA_PATTERNS[-1][1]) == 0 and seq % TOKEN_TILE == 0
    mods = _modulation(c, w_ada, b_ada)
    bd = _head_block_diag()
    x2d = x.reshape(bsz * seq, d)
    qk_scale = HEAD_LANES ** -0.5
    for l in range(depth):
        w = w_in[l]
        o_conv = A_W
        o_c = o_conv + 2 * CONV_CH
        o_low = o_c + 2 * QK_W + 2 * GLA_V_W
        o_d = o_low + GLA_RANK
        o_gate = o_d + D_W
        wa = w[:, :A_W].astype(BF16)
        wb = w[:, o_conv:o_c].astype(BF16)
        wc = jnp.concatenate([w[:, o_c:o_low + GLA_RANK], jnp.zeros((d, GLA_LOW_PAD - GLA_RANK), F32)], axis=1).astype(BF16)
        wd = w[:, o_d:o_gate].astype(BF16)
        wg = w[:, o_gate:].astype(BF16)
        qg = jnp.tile(q_gain[l], N_HEADS).reshape(1, QK_W) * qk_scale
        kg = jnp.tile(k_gain[l], N_HEADS).reshape(1, QK_W)
        mod = mods[l]
        gm = g_mix[l].reshape(1, d)

        a, y, cbuf, dbuf = _in_projection(x2d, mod, gm, wa, wb, wc, wd, qg, kg, bd, seq)
        oa = _dilated_attention(a, bsz, seq)
        ob = _conv_mixer(y, conv_w[l], conv_b[l], conv_ln_g[l], conv_ln_b[l], bsz, seq)
        oc = _gla_mixer(cbuf, gla_w_gate[l], gla_b_gate[l], gla_o_gain[l], bsz, seq)
        od = _stick_breaking_mixer(dbuf, bsz, seq)
        x2d = _merge(x2d, mod, gm, oa, ob, oc, od, wg, w_br_a[l].astype(BF16), w_br_b[l].astype(BF16),
                     w_br_c[l].astype(BF16), w_br_d[l].astype(BF16), w_out[l].astype(BF16), seq)
        x2d = _mlp(x2d, mod, g_mlp[l].reshape(1, d), w_up[l].astype(BF16), w_down[l].astype(BF16), seq)
    return x2d.reshape(bsz, seq, d)
```

```python
import functools

import numpy as np
import jax
import jax.numpy as jnp
from jax import lax
from jax.experimental import pallas as pl
from jax.experimental.pallas import tpu as pltpu

F32 = jnp.float32
BF16 = jnp.bfloat16

D_MODEL = 1024
D_FF = 4 * D_MODEL
NORM_EPS = 1e-6
LOG2_E = 1.4426950408889634
HEAD_LANES = 64
N_HEADS = 4
QK_W = N_HEADS * HEAD_LANES
DSWA_PATTERNS = ((128, 1), (512, 4), (2048, 16))
DSWA_BLOCK = 128
DSWA_SPAN = DSWA_BLOCK * DSWA_PATTERNS[-1][1]
LANE_TILE = 128
SUBLANES = 8
CONV_CH = 256
CONV_WIDTH = 31
CONV_HALO = 32
GLA_DV = 128
GLA_V_W = N_HEADS * GLA_DV
GLA_RANK = 16
GLA_LOW_PAD = 128
GLA_TAU = 16.0
GLA_CHUNK = 64
GLA_SUB = 16
GLA_CHUNKS_PER_STEP = 16
GLA_CUM_ROWS = 256
SB_T = 256
SB_QBLOCKS = 4
SB_STRAIGHT_BLOCKS = 4
SB_NORM_ROWS = 512
SB_BOUND_SLACK = 1.02
SB_ZERO_EXPONENT = -110.0
TOKEN_TILE = 512
MASK_VALUE = -1e30
VMEM_LIMIT = 56 * 1024 * 1024

A_W = 3 * 3 * QK_W
C_W = 2 * QK_W + 2 * GLA_V_W + GLA_LOW_PAD
D_W = 3 * QK_W


def _dot(a, b):
    return jnp.dot(a, b, preferred_element_type=F32)


def _dot_nt(a, b):
    return lax.dot_general(a, b, (((1,), (1,)), ((), ())), preferred_element_type=F32)


def _dot_tn(a, b):
    return lax.dot_general(a, b, (((0,), (0,)), ((), ())), preferred_element_type=F32)


def _split_bf16(x):
    hi = x.astype(BF16)
    lo = (x - hi.astype(F32)).astype(BF16)
    return hi, lo


def _sigmoid(x):
    return 1.0 / (1.0 + jnp.exp(-x))


def _neg_softplus(x):
    return -(jnp.maximum(x, 0.0) + jnp.log(1.0 + jnp.exp(-jnp.abs(x))))


def _iota(shape, dim):
    return lax.broadcasted_iota(jnp.int32, shape, dim)


def _head_of_lane(width=QK_W, lanes_per_head=HEAD_LANES):
    return jnp.right_shift(_iota((1, width), 1), int(np.log2(lanes_per_head)))


def _ones_where(cond):
    return jnp.where(cond, 1.0, 0.0).astype(BF16)


def _const_spec(shape):
    nd = len(shape)
    return pl.BlockSpec(shape, lambda *_: (0,) * nd, pipeline_mode=pl.Buffered(1))


def _params(*sem):
    return pltpu.CompilerParams(dimension_semantics=sem, vmem_limit_bytes=VMEM_LIMIT)


def _modulated_norm(x, gain, scale, shift):
    ms = jnp.mean(x * x, axis=-1, keepdims=True)
    y = x * lax.rsqrt(ms + NORM_EPS) * gain
    return y * (1.0 + scale) + shift


def _mod_kernel(c_ref, w_ref, b_ref, o_ref):
    c_hi, c_lo = _split_bf16(c_ref[...])
    w_hi, w_lo = _split_bf16(w_ref[...])
    o_ref[...] = _dot(c_hi, w_hi) + _dot(c_hi, w_lo) + _dot(c_lo, w_hi) + b_ref[...]


def _modulation(c, w_ada, b_ada):
    depth, d, six_d = w_ada.shape
    bsz = c.shape[0]
    out = pl.pallas_call(
        _mod_kernel,
        grid=(depth, six_d // d),
        in_specs=[
            pl.BlockSpec((bsz, d), lambda l, j: (0, 0)),
            pl.BlockSpec((None, d, d), lambda l, j: (l, 0, j)),
            pl.BlockSpec((None, 1, d), lambda l, j: (l, 0, j)),
        ],
        out_specs=pl.BlockSpec((None, bsz, d), lambda l, j: (l, 0, j)),
        out_shape=jax.ShapeDtypeStruct((depth, bsz, six_d), F32),
        compiler_params=_params("parallel", "parallel"),
        name="adaln_modulation",
    )(c, w_ada, b_ada.reshape(depth, 1, six_d))
    return out.reshape(depth, bsz, six_d // d, d)


def _inproj_kernel(x_ref, mod_ref, g_ref, wa_ref, wb_ref, wc_ref, wd_ref, qg_ref, kg_ref, bd_ref,
                   a_ref, y_ref, c_ref, d_ref):
    h = _modulated_norm(x_ref[...], g_ref[...], mod_ref[1:2, :], mod_ref[0:1, :]).astype(BF16)
    bd = bd_ref[...]
    a_all = _dot(h, wa_ref[...])
    for j in range(A_W // QK_W):
        cols = slice(j * QK_W, (j + 1) * QK_W)
        acc = a_all[:, cols]
        if j < 6:
            ms = _dot((acc * acc).astype(BF16), bd)
            gain = qg_ref[...] if j < 3 else kg_ref[...]
            acc = acc * lax.rsqrt(ms + NORM_EPS) * gain
        a_ref[:, cols] = acc.astype(BF16)
    u = _dot(h, wb_ref[...])
    y_ref[...] = u[:, :CONV_CH] * _sigmoid(u[:, CONV_CH:])
    c_ref[...] = _dot(h, wc_ref[...])
    d_all = _dot(h, wd_ref[...])
    d_ref[:, :QK_W] = (d_all[:, :QK_W] * (HEAD_LANES ** -0.5)).astype(BF16)
    d_ref[:, QK_W:] = d_all[:, QK_W:].astype(BF16)


def _in_projection(x2d, mod, g_mix, wa, wb, wc, wd, qg, kg, bd, seq):
    t = x2d.shape[0]
    tm = TOKEN_TILE
    per_b = seq // tm
    row = lambda w: pl.BlockSpec((tm, w), lambda i: (i, 0))
    return pl.pallas_call(
        _inproj_kernel,
        grid=(t // tm,),
        in_specs=[
            row(D_MODEL),
            pl.BlockSpec((None, 6, D_MODEL), lambda i: (i // per_b, 0, 0)),
            _const_spec((1, D_MODEL)),
            _const_spec(wa.shape), _const_spec(wb.shape), _const_spec(wc.shape), _const_spec(wd.shape),
            _const_spec((1, QK_W)), _const_spec((1, QK_W)), _const_spec((QK_W, QK_W)),
        ],
        out_specs=[row(A_W), row(CONV_CH), row(C_W), row(D_W)],
        out_shape=[
            jax.ShapeDtypeStruct((t, A_W), BF16),
            jax.ShapeDtypeStruct((t, CONV_CH), F32),
            jax.ShapeDtypeStruct((t, C_W), F32),
            jax.ShapeDtypeStruct((t, D_W), BF16),
        ],
        compiler_params=_params("parallel"),
        name="in_projection",
    )(x2d, mod, g_mix, wa, wb, wc, wd, qg, kg, bd)


def _dswa_group(position, span_idx, q_ref, kp_ref, kc_ref, vp_ref, vc_ref, o_ref, q_st, k_st, v_st, m_st, den_st, acc_st,
                *, slopes, window, dilation):
    blk, span = DSWA_BLOCK, DSWA_SPAN
    halo = blk * dilation
    for s in range(QK_W // LANE_TILE):
        lanes = slice(s * LANE_TILE, (s + 1) * LANE_TILE)
        q_st[s, :, :] = q_ref[:, lanes].astype(F32)
        k_st[s, 0:halo, :] = kp_ref[span - halo:span, lanes].astype(F32)
        k_st[s, halo:halo + span, :] = kc_ref[:, lanes].astype(F32)
        v_st[s, 0:halo, :] = vp_ref[span - halo:span, lanes].astype(F32)
        v_st[s, halo:halo + span, :] = vc_ref[:, lanes].astype(F32)
    head = _head_of_lane()
    qi = _iota((blk, 2 * blk), 0)
    kj = _iota((blk, 2 * blk), 1)
    rel = qi - kj + blk
    in_window = (rel >= 0) & (rel <= window // dilation)
    dist = (dilation * rel).astype(F32)
    bias = jnp.concatenate([jnp.where(in_window, -slopes[h] * dist, MASK_VALUE) for h in range(N_HEADS)], axis=0)
    first_block_penalty = jnp.where(_iota((1, 2 * blk), 1) < blk, MASK_VALUE, 0.0)

    def rows_of(ref, start, n):
        return jnp.concatenate([ref[s, pl.ds(start, n, stride=dilation), :] for s in range(QK_W // LANE_TILE)], axis=1)

    def put_rows(ref, start, val):
        for s in range(QK_W // LANE_TILE):
            ref[s, pl.ds(start, blk, stride=dilation), :] = val[:, s * LANE_TILE:(s + 1) * LANE_TILE]

    def step(it):
        n = it // dilation
        start = n * halo + it % dilation
        q = rows_of(q_st, start, blk).astype(BF16)
        k = rows_of(k_st, start, 2 * blk).astype(BF16)
        v = rows_of(v_st, start, 2 * blk).astype(BF16)
        q_stack = jnp.concatenate([jnp.where(head == h, q, jnp.zeros_like(q)) for h in range(N_HEADS)], axis=0)
        s = _dot_nt(q_stack, k) + bias
        if n == 0:
            s = s + jnp.where(span_idx == 0, first_block_penalty, 0.0)
        mx = jnp.max(s, axis=1, keepdims=True)
        p = jnp.exp(s - mx)
        den = jnp.sum(p, axis=1, keepdims=True)
        o_all = _dot(p.astype(BF16), v) * (1.0 / den)
        lse_all = mx + jnp.log(den)
        o_new = jnp.zeros((blk, QK_W), F32)
        lse = jnp.zeros((blk, QK_W), F32)
        for h in range(N_HEADS):
            in_head = head == h
            o_new = jnp.where(in_head, o_all[h * blk:(h + 1) * blk], o_new)
            lse = jnp.where(in_head, lse_all[h * blk:(h + 1) * blk], lse)
        if position == 0:
            m_new, acc_new = lse, o_new
        else:
            m_old = rows_of(m_st, start, blk)
            m_new = jnp.maximum(m_old, lse)
            w_old, w_new = jnp.exp(m_old - m_new), jnp.exp(lse - m_new)
            den_old = w_old if position == 1 else rows_of(den_st, start, blk) * w_old
            put_rows(den_st, start, den_old + w_new)
            acc_new = rows_of(acc_st, start, blk) * w_old + o_new * w_new
        if position < len(DSWA_PATTERNS) - 1:
            put_rows(m_st, start, m_new)
        put_rows(acc_st, start, acc_new)

    for it in range(span // blk):
        step(it)
    if position == len(DSWA_PATTERNS) - 1:
        for s in range(QK_W // LANE_TILE):
            o_ref[:, s * LANE_TILE:(s + 1) * LANE_TILE] = (acc_st[s, :, :] / den_st[s, :, :]).astype(BF16)


def _dswa_kernel(q_ref, kp_ref, kc_ref, vp_ref, vc_ref, o_ref, *scratch, slopes):
    span_idx, position = pl.program_id(1), pl.program_id(2)
    for pos in range(len(DSWA_PATTERNS)):
        g = _dswa_group_at(pos)
        window, dilation = DSWA_PATTERNS[g]
        pl.when(position == pos)(functools.partial(
            _dswa_group, pos, span_idx, q_ref, kp_ref, kc_ref, vp_ref, vc_ref, o_ref, *scratch,
            slopes=slopes[g * N_HEADS:(g + 1) * N_HEADS], window=window, dilation=dilation))


def _dswa_group_at(position):
    return len(DSWA_PATTERNS) - 1 - position


def _alibi_slopes(n):
    return [float(np.float32(2.0 ** (-8.0 * i / n))) for i in range(1, n + 1)]


def _dilated_attention(a, bsz, seq):
    span = DSWA_SPAN
    a3 = a.reshape(bsz, seq, A_W)
    n_groups = len(DSWA_PATTERNS)
    col = lambda part, pos: part * n_groups + _dswa_group_at(pos)
    cur = lambda part: pl.BlockSpec((None, span, QK_W), lambda b, i, p: (b, i, col(part, p)))
    prev = lambda part: pl.BlockSpec((None, span, QK_W), lambda b, i, p: (b, jnp.maximum(i - 1, 0), col(part, p)))
    slab = lambda rows: pltpu.VMEM((QK_W // LANE_TILE, rows, LANE_TILE), F32)
    out = pl.pallas_call(
        functools.partial(_dswa_kernel, slopes=_alibi_slopes(n_groups * N_HEADS)),
        grid=(bsz, seq // span, n_groups),
        in_specs=[cur(0), prev(1), cur(1), prev(2), cur(2)],
        out_specs=pl.BlockSpec((None, span, QK_W), lambda b, i, g: (b, i, 0)),
        out_shape=jax.ShapeDtypeStruct((bsz, seq, QK_W), BF16),
        scratch_shapes=[slab(span), slab(2 * span), slab(2 * span), slab(span), slab(span), slab(span)],
        compiler_params=_params("parallel", "arbitrary", "arbitrary"),
        name="dilated_attention",
    )(a3, a3, a3, a3, a3)
    return out.reshape(bsz * seq, QK_W)


def _conv_kernel(prev_ref, cur_ref, w_ref, b_ref, lg_ref, lb_ref, o_ref, buf_ref):
    ts = cur_ref.shape[0]
    first = pl.program_id(1) == 0
    buf_ref[0, 0:CONV_HALO, :] = jnp.where(first, 0.0, prev_ref[...])
    buf_ref[0, CONV_HALO:, :] = cur_ref[...]
    rows = ts + CONV_HALO - SUBLANES
    for r in range(1, SUBLANES):
        buf_ref[r, 0:rows, :] = buf_ref[0, r:r + rows, :]
    off = CONV_HALO - (CONV_WIDTH - 1)
    acc = jnp.zeros((ts, CONV_CH), F32) + b_ref[...]
    for w in range(CONV_WIDTH):
        r, base = (off + w) % SUBLANES, (off + w) // SUBLANES * SUBLANES
        acc = acc + buf_ref[r, base:base + ts, :] * w_ref[w:w + 1, :]
    mu = jnp.mean(acc, axis=-1, keepdims=True)
    ctr = acc - mu
    var = jnp.mean(ctr * ctr, axis=-1, keepdims=True)
    y = ctr * lax.rsqrt(var + NORM_EPS) * lg_ref[...] + lb_ref[...]
    o_ref[...] = (y * _sigmoid(y)).astype(BF16)


def _conv_mixer(y, conv_w, conv_b, ln_g, ln_b, bsz, seq):
    ts = TOKEN_TILE
    per_halo = ts // CONV_HALO
    y3 = y.reshape(bsz, seq, CONV_CH)
    w_pad = jnp.zeros((CONV_HALO, CONV_CH), F32).at[:CONV_WIDTH].set(conv_w)
    out = pl.pallas_call(
        _conv_kernel,
        grid=(bsz, seq // ts),
        in_specs=[
            pl.BlockSpec((None, CONV_HALO, CONV_CH), lambda b, i: (b, jnp.maximum(i * per_halo - 1, 0), 0)),
            pl.BlockSpec((None, ts, CONV_CH), lambda b, i: (b, i, 0)),
            _const_spec((CONV_HALO, CONV_CH)), _const_spec((1, CONV_CH)), _const_spec((1, CONV_CH)),
            _const_spec((1, CONV_CH)),
        ],
        out_specs=pl.BlockSpec((None, ts, CONV_CH), lambda b, i: (b, i, 0)),
        out_shape=jax.ShapeDtypeStruct((bsz, seq, CONV_CH), BF16),
        scratch_shapes=[pltpu.VMEM((SUBLANES, ts + CONV_HALO, CONV_CH), F32)],
        compiler_params=_params("parallel", "parallel"),
        name="conv_mixer",
    )(y3, y3, w_pad, conv_b.reshape(1, -1), ln_g.reshape(1, -1), ln_b.reshape(1, -1))
    return out.reshape(bsz * seq, CONV_CH)


def _gla_kernel(q_ref, k_ref, v_ref, r_ref, low_ref, wg_ref, bg_ref, og_ref, tri_ref, am_ref, sm_ref, o_ref, st_ref):
    ch, sub = GLA_CHUNK, GLA_SUB
    tc = q_ref.shape[0]
    n_sub = ch // sub

    @pl.when(pl.program_id(1) == 0)
    def _():
        st_ref[...] = jnp.zeros_like(st_ref)

    head = _head_of_lane()
    key_row = _iota((ch, QK_W), 0)
    att_mask = am_ref[...]
    st_mask = sm_ref[...]

    logit = _dot(low_ref[...].astype(BF16), wg_ref[...]) + bg_ref[...]
    log_a = _neg_softplus(-logit) * (1.0 / GLA_TAU)
    la_hi, la_lo = _split_bf16(log_a)
    cr = GLA_CUM_ROWS
    cum_all = jnp.concatenate(
        [_dot(tri_ref[...], jnp.concatenate([la_hi[g * cr:(g + 1) * cr], la_lo[g * cr:(g + 1) * cr]], axis=0))
         for g in range(tc // cr)], axis=0)

    state = st_ref[...]
    for c in range(tc // ch):
        rows = slice(c * ch, (c + 1) * ch)
        cum = cum_all[rows, :]
        total = cum[ch - 1:ch, :]
        q = q_ref[rows, :] * (HEAD_LANES ** -0.5)
        k = k_ref[rows, :]
        v = v_ref[rows, :].astype(BF16)

        bases = [jnp.zeros((1, QK_W), F32)] + [cum[i * sub - 1:i * sub, :] for i in range(1, n_sub)]
        base_rows = jnp.concatenate([jnp.broadcast_to(b, (sub, QK_W)) for b in bases], axis=0)
        q_rel = q * jnp.exp(cum - base_rows)
        q_all = jnp.concatenate([jnp.where(head == h, q_rel, 0.0) for h in range(N_HEADS)], axis=0).astype(BF16)
        k_all = jnp.concatenate([jnp.where(key_row < (i + 1) * sub, k * jnp.exp(bases[i] - cum), 0.0)
                                 for i in range(n_sub)], axis=0).astype(BF16)
        att = (_dot_nt(q_all, k_all) * att_mask).astype(BF16)
        intra = _dot(att, jnp.concatenate([v] * n_sub, axis=0))
        o = jnp.concatenate([intra[h * ch:(h + 1) * ch, h * GLA_DV:(h + 1) * GLA_DV] for h in range(N_HEADS)], axis=1)

        o = o + _dot_nt((q * jnp.exp(cum)).astype(BF16), state.astype(BF16))
        k_dec = (k * jnp.exp(total - cum)).astype(BF16)
        state = state * jnp.exp(total) + _dot_tn(v, k_dec) * st_mask

        r = r_ref[rows, :]
        outs = []
        for h in range(N_HEADS):
            o_h = o[:, h * GLA_DV:(h + 1) * GLA_DV]
            ms = jnp.mean(o_h * o_h, axis=-1, keepdims=True)
            outs.append(o_h * lax.rsqrt(ms + NORM_EPS) * og_ref[...])
        o_ref[rows, :] = (r * _sigmoid(r) * jnp.concatenate(outs, axis=1)).astype(BF16)
    st_ref[...] = state


def _gla_masks(tc):
    ch, sub = GLA_CHUNK, GLA_SUB
    t = np.arange(tc)
    tri = (t[:, None] >= t[None, :]) & (t[:, None] // ch == t[None, :] // ch)
    a_t = np.arange(N_HEADS * ch) % ch
    a_col = np.arange((ch // sub) * ch)
    att = (a_col[None, :] // ch == a_t[:, None] // sub) & (a_col[None, :] % ch <= a_t[:, None])
    state = np.arange(GLA_V_W)[:, None] // GLA_DV == np.arange(QK_W)[None, :] // HEAD_LANES
    return (jnp.asarray(np.concatenate([tri, tri], axis=1), BF16), jnp.asarray(att, F32), jnp.asarray(state, F32))


def _gla_mixer(cbuf, w_gate, b_gate, o_gain, bsz, seq):
    tc = GLA_CHUNK * GLA_CHUNKS_PER_STEP
    assert seq % tc == 0 and tc % GLA_CUM_ROWS == 0 and GLA_CUM_ROWS % GLA_CHUNK == 0
    c3 = cbuf.reshape(bsz, seq, C_W)
    wg = jnp.zeros((GLA_LOW_PAD, QK_W), F32).at[:GLA_RANK].set(w_gate).astype(BF16)
    tri2, att_mask, st_mask = _gla_masks(GLA_CUM_ROWS)
    spec = lambda w, j: pl.BlockSpec((None, tc, w), lambda b, i: (b, i, j))
    out = pl.pallas_call(
        _gla_kernel,
        grid=(bsz, seq // tc),
        in_specs=[
            spec(QK_W, 0), spec(QK_W, 1), spec(GLA_V_W, 1), spec(GLA_V_W, 2),
            spec(GLA_LOW_PAD, (2 * QK_W + 2 * GLA_V_W) // GLA_LOW_PAD),
            _const_spec((GLA_LOW_PAD, QK_W)), _const_spec((1, QK_W)), _const_spec((1, GLA_DV)),
            _const_spec(tri2.shape), _const_spec(att_mask.shape), _const_spec(st_mask.shape),
        ],
        out_specs=pl.BlockSpec((None, tc, GLA_V_W), lambda b, i: (b, i, 0)),
        out_shape=jax.ShapeDtypeStruct((bsz, seq, GLA_V_W), BF16),
        scratch_shapes=[pltpu.VMEM((GLA_V_W, QK_W), F32)],
        compiler_params=_params("parallel", "arbitrary"),
        name="gla_mixer",
    )(c3, c3, c3, c3, c3, wg, b_gate.reshape(1, -1), o_gain.reshape(1, -1), tri2, att_mask, st_mask)
    return out.reshape(bsz * seq, GLA_V_W)


def _sb_kernel(q_ref, k_ref, v_ref, o_ref, kmax_ref):
    t = SB_T
    seq = k_ref.shape[0]
    i = pl.program_id(1)
    head = _head_of_lane()
    head_ones = _ones_where(jnp.right_shift(_iota((QK_W, QK_W), 0), int(np.log2(HEAD_LANES))) == head)

    @pl.when(i == 0)
    def _():
        best = jnp.zeros((SB_NORM_ROWS, QK_W), F32)
        for c in range(seq // SB_NORM_ROWS):
            kf = k_ref[c * SB_NORM_ROWS:(c + 1) * SB_NORM_ROWS, :].astype(F32)
            sq_hi, sq_lo = _split_bf16(kf * kf)
            best = jnp.maximum(best, _dot(sq_hi, head_ones) + _dot(sq_lo, head_ones))
        kmax_ref[...] = jnp.zeros_like(kmax_ref) + jnp.max(best)

    tri2 = _ones_where(jnp.bitwise_and(_iota((2 * t, t), 0), t - 1) >= _iota((2 * t, t), 1))
    keep = _iota((N_HEADS * t, t), 1) < jnp.bitwise_and(_iota((N_HEADS * t, t), 0), t - 1)

    def query_block(u):
        q = q_ref[u * t:(u + 1) * t, :]
        q_stack = jnp.concatenate([jnp.where(head == h, q, jnp.zeros_like(q)) for h in range(N_HEADS)], axis=0)
        qf = q_stack.astype(F32)
        z_bound = jnp.sqrt(jnp.sum(qf * qf, axis=1, keepdims=True) * kmax_ref[0:1, 0:1]) * SB_BOUND_SLACK
        return q_stack, z_bound

    def block(j, acc, run, masked, q_stack, z_bound):
        start = j * t if isinstance(j, int) else pl.multiple_of(j * t, t)
        k_blk = k_ref[pl.ds(start, t), :]
        v_blk = v_ref[pl.ds(start, t), :]
        z = _dot_nt(q_stack, k_blk)
        drop = jnp.maximum(z, 0.0) + jnp.log(1.0 + jnp.exp2(jnp.abs(z) * (-LOG2_E)))
        if masked:
            drop = jnp.where(keep, drop, 0.0)
        d_hi, d_lo = _split_bf16(drop)
        suffix = _dot(jnp.concatenate([d_hi, d_lo], axis=1), tri2)
        w = jnp.exp(z - suffix - run)
        if masked:
            w = jnp.where(keep, w, 0.0)
        w = w.astype(BF16)
        w_wide = jnp.concatenate([w[h * t:(h + 1) * t, :] for h in range(N_HEADS)], axis=1)
        v_stack = jnp.concatenate([jnp.where(head == h, v_blk, jnp.zeros_like(v_blk)) for h in range(N_HEADS)], axis=0)
        acc = acc + _dot(w_wide, v_stack)
        run = run + jnp.sum(drop, axis=1, keepdims=True)
        done = (jnp.max(z_bound - run) < SB_ZERO_EXPONENT).astype(jnp.int32)
        return acc, run, done

    def walk(first_blocks):
        states = []
        for u in range(SB_QBLOCKS):
            iu = i * SB_QBLOCKS + u
            qz = query_block(u)
            acc, run, done = block(iu, jnp.zeros((t, QK_W), F32), jnp.zeros((N_HEADS * t, 1), F32), True, *qz)
            for n in range(1, first_blocks):
                acc, run, done = block(iu - n, acc, run, False, *qz)
            states.append((iu - first_blocks, done, acc, run, qz))
        for u, (j, done, acc, run, qz) in enumerate(states):

            def body(c, qz=qz):
                j, _, acc, run = c
                acc, run, done = block(j, acc, run, False, *qz)
                return j - 1, done, acc, run

            carry = lax.while_loop(lambda c: jnp.logical_and(c[0] >= 0, c[1] == 0), body, (j, done, acc, run))
            o_ref[u * t:(u + 1) * t, :] = carry[2].astype(BF16)

    def first_step():
        for u in range(SB_QBLOCKS):
            qz = query_block(u)
            acc, run, _ = block(u, jnp.zeros((t, QK_W), F32), jnp.zeros((N_HEADS * t, 1), F32), True, *qz)
            for j in range(u - 1, -1, -1):
                acc, run, _ = block(j, acc, run, False, *qz)
            o_ref[u * t:(u + 1) * t, :] = acc.astype(BF16)

    assert SB_QBLOCKS >= SB_STRAIGHT_BLOCKS - 1
    pl.when(i > 0)(functools.partial(walk, SB_STRAIGHT_BLOCKS))
    pl.when(i == 0)(first_step)


def _stick_breaking_mixer(dbuf, bsz, seq):
    d3 = dbuf.reshape(bsz, seq, D_W)
    rows = SB_T * SB_QBLOCKS
    assert seq % rows == 0 and seq % SB_NORM_ROWS == 0
    full = lambda j: pl.BlockSpec((None, seq, QK_W), lambda b, i: (b, 0, j))
    out = pl.pallas_call(
        _sb_kernel,
        grid=(bsz, seq // rows),
        in_specs=[pl.BlockSpec((None, rows, QK_W), lambda b, i: (b, i, 0)), full(1), full(2)],
        out_specs=pl.BlockSpec((None, rows, QK_W), lambda b, i: (b, i, 0)),
        out_shape=jax.ShapeDtypeStruct((bsz, seq, QK_W), BF16),
        scratch_shapes=[pltpu.VMEM((8, 128), F32)],
        compiler_params=_params("parallel", "arbitrary"),
        name="stick_breaking_mixer",
    )(d3, d3, d3)
    return out.reshape(bsz * seq, QK_W)


def _merge_kernel(x_ref, mod_ref, g_ref, oa_ref, ob_ref, oc_ref, od_ref,
                  wg_ref, wa_ref, wb_ref, wc_ref, wd_ref, wo_ref, out_ref):
    x = x_ref[...]
    h = _modulated_norm(x, g_ref[...], mod_ref[1:2, :], mod_ref[0:1, :]).astype(BF16)
    branches = ((oa_ref[...], wa_ref), (ob_ref[...], wb_ref), (oc_ref[...], wc_ref), (od_ref[...], wd_ref))
    merged = jnp.zeros(x.shape, F32)
    for j, (o_j, w_ref) in enumerate(branches):
        gate = _sigmoid(_dot(h, wg_ref[:, j * D_MODEL:(j + 1) * D_MODEL]))
        merged = merged + gate * _dot(o_j, w_ref[...])
    out_ref[...] = x + mod_ref[2:3, :] * _dot(merged.astype(BF16), wo_ref[...])


def _merge(x2d, mod, g_mix, oa, ob, oc, od, wg, wa, wb, wc, wd, wo, seq):
    t = x2d.shape[0]
    tm = TOKEN_TILE
    per_b = seq // tm
    row = lambda w: pl.BlockSpec((tm, w), lambda i: (i, 0))
    return pl.pallas_call(
        _merge_kernel,
        grid=(t // tm,),
        in_specs=[
            row(D_MODEL),
            pl.BlockSpec((None, 6, D_MODEL), lambda i: (i // per_b, 0, 0)),
            _const_spec((1, D_MODEL)),
            row(QK_W), row(CONV_CH), row(GLA_V_W), row(QK_W),
            _const_spec(wg.shape), _const_spec(wa.shape), _const_spec(wb.shape), _const_spec(wc.shape),
            _const_spec(wd.shape), _const_spec(wo.shape),
        ],
        out_specs=row(D_MODEL),
        out_shape=jax.ShapeDtypeStruct((t, D_MODEL), F32),
        compiler_params=_params("parallel"),
        name="gated_merge",
    )(x2d, mod, g_mix, oa, ob, oc, od, wg, wa, wb, wc, wd, wo)


def _mlp_kernel(x_ref, mod_ref, g_ref, wu_ref, wd_ref, out_ref):
    x = x_ref[...]
    h = _modulated_norm(x, g_ref[...], mod_ref[4:5, :], mod_ref[3:4, :]).astype(BF16)
    acc = jnp.zeros(x.shape, F32)
    for j in range(D_FF // D_MODEL):
        cols = slice(j * D_MODEL, (j + 1) * D_MODEL)
        u = jnp.maximum(_dot(h, wu_ref[:, cols]), 0.0)
        acc = acc + _dot((u * u).astype(BF16), wd_ref[cols, :])
    out_ref[...] = x + mod_ref[5:6, :] * acc


def _mlp(x2d, mod, g_mlp, wu, wd, seq):
    t = x2d.shape[0]
    tm = TOKEN_TILE
    per_b = seq // tm
    row = pl.BlockSpec((tm, D_MODEL), lambda i: (i, 0))
    return pl.pallas_call(
        _mlp_kernel,
        grid=(t // tm,),
        in_specs=[
            row,
            pl.BlockSpec((None, 6, D_MODEL), lambda i: (i // per_b, 0, 0)),
            _const_spec((1, D_MODEL)), _const_spec(wu.shape), _const_spec(wd.shape),
        ],
        out_specs=row,
        out_shape=jax.ShapeDtypeStruct((t, D_MODEL), F32),
        compiler_params=_params("parallel"),
        name="relu2_mlp",
    )(x2d, mod, g_mlp, wu, wd)


def _head_block_diag():
    idx = np.arange(QK_W) // HEAD_LANES
    return jnp.asarray((idx[:, None] == idx[None, :]).astype(np.float32) / HEAD_LANES, dtype=BF16)


def kernel(x, c, w_ada, b_ada, g_mix, w_in, q_gain, k_gain, conv_w, conv_b, conv_ln_g, conv_ln_b,
           gla_w_gate, gla_b_gate, gla_o_gain, w_br_a, w_br_b, w_br_c, w_br_d, w_out, g_mlp, w_up, w_down):
    bsz, seq, d = x.shape
    depth = w_ada.shape[0]
    assert d == D_MODEL and seq % (DSWA_BLOCK * DSWA_PATTERNS[-1][1]) == 0 and seq % TOKEN_TILE == 0
    mods = _modulation(c, w_ada, b_ada)
    bd = _head_block_diag()
    x2d = x.reshape(bsz * seq, d)
    qk_scale = HEAD_LANES ** -0.5
    for l in range(depth):
        w = w_in[l]
        o_conv = A_W
        o_c = o_conv + 2 * CONV_CH
        o_low = o_c + 2 * QK_W + 2 * GLA_V_W
        o_d = o_low + GLA_RANK
        o_gate = o_d + D_W
        assert w.shape == (d, o_gate + 4 * d) and w_up.shape[-1] == D_FF
        wa = w[:, :A_W].astype(BF16)
        wb = w[:, o_conv:o_c].astype(BF16)
        wc = jnp.concatenate([w[:, o_c:o_low + GLA_RANK], jnp.zeros((d, GLA_LOW_PAD - GLA_RANK), F32)], axis=1).astype(BF16)
        wd = w[:, o_d:o_gate].astype(BF16)
        wg = w[:, o_gate:].astype(BF16)
        qg = jnp.tile(q_gain[l], N_HEADS).reshape(1, QK_W) * qk_scale
        kg = jnp.tile(k_gain[l], N_HEADS).reshape(1, QK_W)
        mod = mods[l]
        gm = g_mix[l].reshape(1, d)

        a, y, cbuf, dbuf = _in_projection(x2d, mod, gm, wa, wb, wc, wd, qg, kg, bd, seq)
        oa = _dilated_attention(a, bsz, seq)
        ob = _conv_mixer(y, conv_w[l], conv_b[l], conv_ln_g[l], conv_ln_b[l], bsz, seq)
        oc = _gla_mixer(cbuf, gla_w_gate[l], gla_b_gate[l], gla_o_gain[l], bsz, seq)
        od = _stick_breaking_mixer(dbuf, bsz, seq)
        x2d = _merge(x2d, mod, gm, oa, ob, oc, od, wg, w_br_a[l].astype(BF16), w_br_b[l].astype(BF16),
                     w_br_c[l].astype(BF16), w_br_d[l].astype(BF16), w_out[l].astype(BF16), seq)
        x2d = _mlp(x2d, mod, g_mlp[l].reshape(1, d), w_up[l].astype(BF16), w_down[l].astype(BF16), seq)
    return x2d.reshape(bsz, seq, d)
```

```python
import functools

import numpy as np
import jax
import jax.numpy as jnp
from jax import lax
from jax.experimental import pallas as pl
from jax.experimental.pallas import tpu as pltpu

F32 = jnp.float32
BF16 = jnp.bfloat16

D_MODEL = 1024
D_FF = 4 * D_MODEL
NORM_EPS = 1e-6
LOG2_E = 1.4426950408889634
HEAD_LANES = 64
N_HEADS = 4
QK_W = N_HEADS * HEAD_LANES
DSWA_PATTERNS = ((128, 1), (512, 4), (2048, 16))
DSWA_BLOCK = 128
DSWA_SPAN = DSWA_BLOCK * DSWA_PATTERNS[-1][1]
LANE_TILE = 128
SUBLANES = 8
CONV_CH = 256
CONV_WIDTH = 31
CONV_HALO = 32
GLA_DV = 128
GLA_V_W = N_HEADS * GLA_DV
GLA_RANK = 16
GLA_LOW_PAD = 128
GLA_TAU = 16.0
GLA_CHUNK = 64
GLA_SUB = 16
GLA_CHUNKS_PER_STEP = 32
GLA_CUM_ROWS = 256
SB_T = 256
SB_QBLOCKS = 4
SB_STRAIGHT_BLOCKS = 4
SB_NORM_ROWS = 512
SB_BOUND_SLACK = 1.02
SB_ZERO_EXPONENT = -110.0
TOKEN_TILE = 512
MASK_VALUE = -1e30
VMEM_LIMIT = 56 * 1024 * 1024

A_W = 3 * 3 * QK_W
C_W = 2 * QK_W + 2 * GLA_V_W + GLA_LOW_PAD
D_W = 3 * QK_W


def _dot(a, b):
    return jnp.dot(a, b, preferred_element_type=F32)


def _dot_nt(a, b):
    return lax.dot_general(a, b, (((1,), (1,)), ((), ())), preferred_element_type=F32)


def _dot_tn(a, b):
    return lax.dot_general(a, b, (((0,), (0,)), ((), ())), preferred_element_type=F32)


def _split_bf16(x):
    hi = x.astype(BF16)
    lo = (x - hi.astype(F32)).astype(BF16)
    return hi, lo


def _sigmoid(x):
    return 1.0 / (1.0 + jnp.exp(-x))


def _neg_softplus(x):
    return -(jnp.maximum(x, 0.0) + jnp.log(1.0 + jnp.exp(-jnp.abs(x))))


def _iota(shape, dim):
    return lax.broadcasted_iota(jnp.int32, shape, dim)


def _head_of_lane(width=QK_W, lanes_per_head=HEAD_LANES):
    return jnp.right_shift(_iota((1, width), 1), int(np.log2(lanes_per_head)))


def _ones_where(cond):
    return jnp.where(cond, 1.0, 0.0).astype(BF16)


def _const_spec(shape):
    nd = len(shape)
    return pl.BlockSpec(shape, lambda *_: (0,) * nd, pipeline_mode=pl.Buffered(1))


def _params(*sem):
    return pltpu.CompilerParams(dimension_semantics=sem, vmem_limit_bytes=VMEM_LIMIT)


def _modulated_norm(x, gain, scale, shift):
    ms = jnp.mean(x * x, axis=-1, keepdims=True)
    y = x * lax.rsqrt(ms + NORM_EPS) * gain
    return y * (1.0 + scale) + shift


def _mod_kernel(c_ref, w_ref, b_ref, o_ref):
    c_hi, c_lo = _split_bf16(c_ref[...])
    w_hi, w_lo = _split_bf16(w_ref[...])
    o_ref[...] = _dot(c_hi, w_hi) + _dot(c_hi, w_lo) + _dot(c_lo, w_hi) + b_ref[...]


def _modulation(c, w_ada, b_ada):
    depth, d, six_d = w_ada.shape
    bsz = c.shape[0]
    out = pl.pallas_call(
        _mod_kernel,
        grid=(depth, six_d // d),
        in_specs=[
            pl.BlockSpec((bsz, d), lambda l, j: (0, 0)),
            pl.BlockSpec((None, d, d), lambda l, j: (l, 0, j)),
            pl.BlockSpec((None, 1, d), lambda l, j: (l, 0, j)),
        ],
        out_specs=pl.BlockSpec((None, bsz, d), lambda l, j: (l, 0, j)),
        out_shape=jax.ShapeDtypeStruct((depth, bsz, six_d), F32),
        compiler_params=_params("parallel", "parallel"),
        name="adaln_modulation",
    )(c, w_ada, b_ada.reshape(depth, 1, six_d))
    return out.reshape(depth, bsz, six_d // d, d)


def _inproj_kernel(x_ref, mod_ref, g_ref, wa_ref, wb_ref, wc_ref, wd_ref, qg_ref, kg_ref, bd_ref,
                   a_ref, y_ref, c_ref, d_ref):
    h = _modulated_norm(x_ref[...], g_ref[...], mod_ref[1:2, :], mod_ref[0:1, :]).astype(BF16)
    bd = bd_ref[...]
    a_all = _dot(h, wa_ref[...])
    for j in range(A_W // QK_W):
        cols = slice(j * QK_W, (j + 1) * QK_W)
        acc = a_all[:, cols]
        if j < 6:
            ms = _dot((acc * acc).astype(BF16), bd)
            gain = qg_ref[...] if j < 3 else kg_ref[...]
            acc = acc * lax.rsqrt(ms + NORM_EPS) * gain
        a_ref[:, cols] = acc.astype(BF16)
    u = _dot(h, wb_ref[...])
    y_ref[...] = u[:, :CONV_CH] * _sigmoid(u[:, CONV_CH:])
    c_ref[...] = _dot(h, wc_ref[...])
    d_all = _dot(h, wd_ref[...])
    d_ref[:, :QK_W] = (d_all[:, :QK_W] * (HEAD_LANES ** -0.5)).astype(BF16)
    d_ref[:, QK_W:] = d_all[:, QK_W:].astype(BF16)


def _in_projection(x2d, mod, g_mix, wa, wb, wc, wd, qg, kg, bd, seq):
    t = x2d.shape[0]
    tm = TOKEN_TILE
    per_b = seq // tm
    row = lambda w: pl.BlockSpec((tm, w), lambda i: (i, 0))
    return pl.pallas_call(
        _inproj_kernel,
        grid=(t // tm,),
        in_specs=[
            row(D_MODEL),
            pl.BlockSpec((None, 6, D_MODEL), lambda i: (i // per_b, 0, 0)),
            _const_spec((1, D_MODEL)),
            _const_spec(wa.shape), _const_spec(wb.shape), _const_spec(wc.shape), _const_spec(wd.shape),
            _const_spec((1, QK_W)), _const_spec((1, QK_W)), _const_spec((QK_W, QK_W)),
        ],
        out_specs=[row(A_W), row(CONV_CH), row(C_W), row(D_W)],
        out_shape=[
            jax.ShapeDtypeStruct((t, A_W), BF16),
            jax.ShapeDtypeStruct((t, CONV_CH), F32),
            jax.ShapeDtypeStruct((t, C_W), F32),
            jax.ShapeDtypeStruct((t, D_W), BF16),
        ],
        compiler_params=_params("parallel"),
        name="in_projection",
    )(x2d, mod, g_mix, wa, wb, wc, wd, qg, kg, bd)


def _dswa_group(position, span_idx, q_ref, kp_ref, kc_ref, vp_ref, vc_ref, o_ref, q_st, k_st, v_st, m_st, den_st, acc_st,
                *, slopes, window, dilation):
    blk, span = DSWA_BLOCK, DSWA_SPAN
    halo = blk * dilation
    for s in range(QK_W // LANE_TILE):
        lanes = slice(s * LANE_TILE, (s + 1) * LANE_TILE)
        q_st[s, :, :] = q_ref[:, lanes].astype(F32)
        k_st[s, 0:halo, :] = kp_ref[span - halo:span, lanes].astype(F32)
        k_st[s, halo:halo + span, :] = kc_ref[:, lanes].astype(F32)
        v_st[s, 0:halo, :] = vp_ref[span - halo:span, lanes].astype(F32)
        v_st[s, halo:halo + span, :] = vc_ref[:, lanes].astype(F32)
    head = _head_of_lane()
    qi = _iota((blk, 2 * blk), 0)
    kj = _iota((blk, 2 * blk), 1)
    rel = qi - kj + blk
    in_window = (rel >= 0) & (rel <= window // dilation)
    dist = (dilation * rel).astype(F32)
    bias = jnp.concatenate([jnp.where(in_window, -slopes[h] * dist, MASK_VALUE) for h in range(N_HEADS)], axis=0)
    first_block_penalty = jnp.where(_iota((1, 2 * blk), 1) < blk, MASK_VALUE, 0.0)

    def rows_of(ref, start, n):
        return jnp.concatenate([ref[s, pl.ds(start, n, stride=dilation), :] for s in range(QK_W // LANE_TILE)], axis=1)

    def put_rows(ref, start, val):
        for s in range(QK_W // LANE_TILE):
            ref[s, pl.ds(start, blk, stride=dilation), :] = val[:, s * LANE_TILE:(s + 1) * LANE_TILE]

    def step(it):
        n = it // dilation
        start = n * halo + it % dilation
        q = rows_of(q_st, start, blk).astype(BF16)
        k = rows_of(k_st, start, 2 * blk).astype(BF16)
        v = rows_of(v_st, start, 2 * blk).astype(BF16)
        q_stack = jnp.concatenate([jnp.where(head == h, q, jnp.zeros_like(q)) for h in range(N_HEADS)], axis=0)
        s = _dot_nt(q_stack, k) + bias
        if n == 0:
            s = s + jnp.where(span_idx == 0, first_block_penalty, 0.0)
        mx = jnp.max(s, axis=1, keepdims=True)
        p = jnp.exp(s - mx)
        den = jnp.sum(p, axis=1, keepdims=True)
        o_all = _dot(p.astype(BF16), v) * (1.0 / den)
        lse_all = mx + jnp.log(den)
        o_new = jnp.zeros((blk, QK_W), F32)
        lse = jnp.zeros((blk, QK_W), F32)
        for h in range(N_HEADS):
            in_head = head == h
            o_new = jnp.where(in_head, o_all[h * blk:(h + 1) * blk], o_new)
            lse = jnp.where(in_head, lse_all[h * blk:(h + 1) * blk], lse)
        if position == 0:
            m_new, acc_new = lse, o_new
        else:
            m_old = rows_of(m_st, start, blk)
            m_new = jnp.maximum(m_old, lse)
            w_old, w_new = jnp.exp(m_old - m_new), jnp.exp(lse - m_new)
            den_old = w_old if position == 1 else rows_of(den_st, start, blk) * w_old
            put_rows(den_st, start, den_old + w_new)
            acc_new = rows_of(acc_st, start, blk) * w_old + o_new * w_new
        if position < len(DSWA_PATTERNS) - 1:
            put_rows(m_st, start, m_new)
        put_rows(acc_st, start, acc_new)

    for it in range(span // blk):
        step(it)
    if position == len(DSWA_PATTERNS) - 1:
        for s in range(QK_W // LANE_TILE):
            o_ref[:, s * LANE_TILE:(s + 1) * LANE_TILE] = (acc_st[s, :, :] / den_st[s, :, :]).astype(BF16)


def _dswa_kernel(q_ref, kp_ref, kc_ref, vp_ref, vc_ref, o_ref, *scratch, slopes):
    span_idx, position = pl.program_id(1), pl.program_id(2)
    for pos in range(len(DSWA_PATTERNS)):
        g = _dswa_group_at(pos)
        window, dilation = DSWA_PATTERNS[g]
        pl.when(position == pos)(functools.partial(
            _dswa_group, pos, span_idx, q_ref, kp_ref, kc_ref, vp_ref, vc_ref, o_ref, *scratch,
            slopes=slopes[g * N_HEADS:(g + 1) * N_HEADS], window=window, dilation=dilation))


def _dswa_group_at(position):
    return len(DSWA_PATTERNS) - 1 - position


def _alibi_slopes(n):
    return [float(np.float32(2.0 ** (-8.0 * i / n))) for i in range(1, n + 1)]


def _dilated_attention(a, bsz, seq):
    span = DSWA_SPAN
    a3 = a.reshape(bsz, seq, A_W)
    n_groups = len(DSWA_PATTERNS)
    col = lambda part, pos: part * n_groups + _dswa_group_at(pos)
    cur = lambda part: pl.BlockSpec((None, span, QK_W), lambda b, i, p: (b, i, col(part, p)))
    prev = lambda part: pl.BlockSpec((None, span, QK_W), lambda b, i, p: (b, jnp.maximum(i - 1, 0), col(part, p)))
    slab = lambda rows: pltpu.VMEM((QK_W // LANE_TILE, rows, LANE_TILE), F32)
    out = pl.pallas_call(
        functools.partial(_dswa_kernel, slopes=_alibi_slopes(n_groups * N_HEADS)),
        grid=(bsz, seq // span, n_groups),
        in_specs=[cur(0), prev(1), cur(1), prev(2), cur(2)],
        out_specs=pl.BlockSpec((None, span, QK_W), lambda b, i, g: (b, i, 0)),
        out_shape=jax.ShapeDtypeStruct((bsz, seq, QK_W), BF16),
        scratch_shapes=[slab(span), slab(2 * span), slab(2 * span), slab(span), slab(span), slab(span)],
        compiler_params=_params("parallel", "arbitrary", "arbitrary"),
        name="dilated_attention",
    )(a3, a3, a3, a3, a3)
    return out.reshape(bsz * seq, QK_W)


def _conv_kernel(prev_ref, cur_ref, w_ref, b_ref, lg_ref, lb_ref, o_ref, buf_ref):
    ts = cur_ref.shape[0]
    first = pl.program_id(1) == 0
    buf_ref[0, 0:CONV_HALO, :] = jnp.where(first, 0.0, prev_ref[...])
    buf_ref[0, CONV_HALO:, :] = cur_ref[...]
    rows = ts + CONV_HALO - SUBLANES
    for r in range(1, SUBLANES):
        buf_ref[r, 0:rows, :] = buf_ref[0, r:r + rows, :]
    off = CONV_HALO - (CONV_WIDTH - 1)
    acc = jnp.zeros((ts, CONV_CH), F32) + b_ref[...]
    for w in range(CONV_WIDTH):
        r, base = (off + w) % SUBLANES, (off + w) // SUBLANES * SUBLANES
        acc = acc + buf_ref[r, base:base + ts, :] * w_ref[w:w + 1, :]
    mu = jnp.mean(acc, axis=-1, keepdims=True)
    ctr = acc - mu
    var = jnp.mean(ctr * ctr, axis=-1, keepdims=True)
    y = ctr * lax.rsqrt(var + NORM_EPS) * lg_ref[...] + lb_ref[...]
    o_ref[...] = (y * _sigmoid(y)).astype(BF16)


def _conv_mixer(y, conv_w, conv_b, ln_g, ln_b, bsz, seq):
    ts = TOKEN_TILE
    per_halo = ts // CONV_HALO
    y3 = y.reshape(bsz, seq, CONV_CH)
    w_pad = jnp.zeros((CONV_HALO, CONV_CH), F32).at[:CONV_WIDTH].set(conv_w)
    out = pl.pallas_call(
        _conv_kernel,
        grid=(bsz, seq // ts),
        in_specs=[
            pl.BlockSpec((None, CONV_HALO, CONV_CH), lambda b, i: (b, jnp.maximum(i * per_halo - 1, 0), 0)),
            pl.BlockSpec((None, ts, CONV_CH), lambda b, i: (b, i, 0)),
            _const_spec((CONV_HALO, CONV_CH)), _const_spec((1, CONV_CH)), _const_spec((1, CONV_CH)),
            _const_spec((1, CONV_CH)),
        ],
        out_specs=pl.BlockSpec((None, ts, CONV_CH), lambda b, i: (b, i, 0)),
        out_shape=jax.ShapeDtypeStruct((bsz, seq, CONV_CH), BF16),
        scratch_shapes=[pltpu.VMEM((SUBLANES, ts + CONV_HALO, CONV_CH), F32)],
        compiler_params=_params("parallel", "parallel"),
        name="conv_mixer",
    )(y3, y3, w_pad, conv_b.reshape(1, -1), ln_g.reshape(1, -1), ln_b.reshape(1, -1))
    return out.reshape(bsz * seq, CONV_CH)


def _gla_kernel(q_ref, k_ref, v_ref, r_ref, low_ref, wg_ref, bg_ref, og_ref, tri_ref, am_ref, sm_ref, o_ref, st_ref):
    ch, sub = GLA_CHUNK, GLA_SUB
    tc = q_ref.shape[0]
    n_sub = ch // sub

    @pl.when(pl.program_id(1) == 0)
    def _():
        st_ref[...] = jnp.zeros_like(st_ref)

    head = _head_of_lane()
    key_row = _iota((ch, QK_W), 0)
    att_mask = am_ref[...]
    st_mask = sm_ref[...]

    logit = _dot(low_ref[...].astype(BF16), wg_ref[...]) + bg_ref[...]
    log_a = _neg_softplus(-logit) * (1.0 / GLA_TAU)
    la_hi, la_lo = _split_bf16(log_a)
    cr = GLA_CUM_ROWS
    cum_all = jnp.concatenate(
        [_dot(tri_ref[...], jnp.concatenate([la_hi[g * cr:(g + 1) * cr], la_lo[g * cr:(g + 1) * cr]], axis=0))
         for g in range(tc // cr)], axis=0)

    state = st_ref[...]
    for c in range(tc // ch):
        rows = slice(c * ch, (c + 1) * ch)
        cum = cum_all[rows, :]
        total = cum[ch - 1:ch, :]
        q = q_ref[rows, :] * (HEAD_LANES ** -0.5)
        k = k_ref[rows, :]
        v = v_ref[rows, :].astype(BF16)

        bases = [jnp.zeros((1, QK_W), F32)] + [cum[i * sub - 1:i * sub, :] for i in range(1, n_sub)]
        base_rows = jnp.concatenate([jnp.broadcast_to(b, (sub, QK_W)) for b in bases], axis=0)
        q_rel = q * jnp.exp(cum - base_rows)
        q_all = jnp.concatenate([jnp.where(head == h, q_rel, 0.0) for h in range(N_HEADS)], axis=0).astype(BF16)
        k_all = jnp.concatenate([jnp.where(key_row < (i + 1) * sub, k * jnp.exp(bases[i] - cum), 0.0)
                                 for i in range(n_sub)], axis=0).astype(BF16)
        att = (_dot_nt(q_all, k_all) * att_mask).astype(BF16)
        intra = _dot(att, jnp.concatenate([v] * n_sub, axis=0))
        o = jnp.concatenate([intra[h * ch:(h + 1) * ch, h * GLA_DV:(h + 1) * GLA_DV] for h in range(N_HEADS)], axis=1)

        o = o + _dot_nt((q * jnp.exp(cum)).astype(BF16), state.astype(BF16))
        k_dec = (k * jnp.exp(total - cum)).astype(BF16)
        state = state * jnp.exp(total) + _dot_tn(v, k_dec) * st_mask

        r = r_ref[rows, :]
        outs = []
        for h in range(N_HEADS):
            o_h = o[:, h * GLA_DV:(h + 1) * GLA_DV]
            ms = jnp.mean(o_h * o_h, axis=-1, keepdims=True)
            outs.append(o_h * lax.rsqrt(ms + NORM_EPS) * og_ref[...])
        o_ref[rows, :] = (r * _sigmoid(r) * jnp.concatenate(outs, axis=1)).astype(BF16)
    st_ref[...] = state


def _gla_masks(tc):
    ch, sub = GLA_CHUNK, GLA_SUB
    t = np.arange(tc)
    tri = (t[:, None] >= t[None, :]) & (t[:, None] // ch == t[None, :] // ch)
    a_t = np.arange(N_HEADS * ch) % ch
    a_col = np.arange((ch // sub) * ch)
    att = (a_col[None, :] // ch == a_t[:, None] // sub) & (a_col[None, :] % ch <= a_t[:, None])
    state = np.arange(GLA_V_W)[:, None] // GLA_DV == np.arange(QK_W)[None, :] // HEAD_LANES
    return (jnp.asarray(np.concatenate([tri, tri], axis=1), BF16), jnp.asarray(att, F32), jnp.asarray(state, F32))


def _gla_mixer(cbuf, w_gate, b_gate, o_gain, bsz, seq):
    tc = GLA_CHUNK * GLA_CHUNKS_PER_STEP
    assert seq % tc == 0 and tc % GLA_CUM_ROWS == 0 and GLA_CUM_ROWS % GLA_CHUNK == 0
    c3 = cbuf.reshape(bsz, seq, C_W)
    wg = jnp.zeros((GLA_LOW_PAD, QK_W), F32).at[:GLA_RANK].set(w_gate).astype(BF16)
    tri2, att_mask, st_mask = _gla_masks(GLA_CUM_ROWS)
    spec = lambda w, j: pl.BlockSpec((None, tc, w), lambda b, i: (b, i, j))
    out = pl.pallas_call(
        _gla_kernel,
        grid=(bsz, seq // tc),
        in_specs=[
            spec(QK_W, 0), spec(QK_W, 1), spec(GLA_V_W, 1), spec(GLA_V_W, 2),
            spec(GLA_LOW_PAD, (2 * QK_W + 2 * GLA_V_W) // GLA_LOW_PAD),
            _const_spec((GLA_LOW_PAD, QK_W)), _const_spec((1, QK_W)), _const_spec((1, GLA_DV)),
            _const_spec(tri2.shape), _const_spec(att_mask.shape), _const_spec(st_mask.shape),
        ],
        out_specs=pl.BlockSpec((None, tc, GLA_V_W), lambda b, i: (b, i, 0)),
        out_shape=jax.ShapeDtypeStruct((bsz, seq, GLA_V_W), BF16),
        scratch_shapes=[pltpu.VMEM((GLA_V_W, QK_W), F32)],
        compiler_params=_params("parallel", "arbitrary"),
        name="gla_mixer",
    )(c3, c3, c3, c3, c3, wg, b_gate.reshape(1, -1), o_gain.reshape(1, -1), tri2, att_mask, st_mask)
    return out.reshape(bsz * seq, GLA_V_W)


def _sb_kernel(q_ref, k_ref, v_ref, o_ref, kmax_ref):
    t = SB_T
    seq = k_ref.shape[0]
    i = pl.program_id(1)
    head = _head_of_lane()
    head_ones = _ones_where(jnp.right_shift(_iota((QK_W, QK_W), 0), int(np.log2(HEAD_LANES))) == head)

    @pl.when(i == 0)
    def _():
        best = jnp.zeros((SB_NORM_ROWS, QK_W), F32)
        for c in range(seq // SB_NORM_ROWS):
            kf = k_ref[c * SB_NORM_ROWS:(c + 1) * SB_NORM_ROWS, :].astype(F32)
            sq_hi, sq_lo = _split_bf16(kf * kf)
            best = jnp.maximum(best, _dot(sq_hi, head_ones) + _dot(sq_lo, head_ones))
        kmax_ref[...] = jnp.zeros_like(kmax_ref) + jnp.max(best)

    tri2 = _ones_where(jnp.bitwise_and(_iota((2 * t, t), 0), t - 1) >= _iota((2 * t, t), 1))
    keep = _iota((N_HEADS * t, t), 1) < jnp.bitwise_and(_iota((N_HEADS * t, t), 0), t - 1)

    def query_block(u):
        q = q_ref[u * t:(u + 1) * t, :]
        q_stack = jnp.concatenate([jnp.where(head == h, q, jnp.zeros_like(q)) for h in range(N_HEADS)], axis=0)
        qf = q_stack.astype(F32)
        z_bound = jnp.sqrt(jnp.sum(qf * qf, axis=1, keepdims=True) * kmax_ref[0:1, 0:1]) * SB_BOUND_SLACK
        return q_stack, z_bound

    def block(j, acc, run, masked, q_stack, z_bound):
        start = j * t if isinstance(j, int) else pl.multiple_of(j * t, t)
        k_blk = k_ref[pl.ds(start, t), :]
        v_blk = v_ref[pl.ds(start, t), :]
        z = _dot_nt(q_stack, k_blk)
        drop = jnp.maximum(z, 0.0) + jnp.log(1.0 + jnp.exp2(jnp.abs(z) * (-LOG2_E)))
        if masked:
            drop = jnp.where(keep, drop, 0.0)
        d_hi, d_lo = _split_bf16(drop)
        suffix = _dot(jnp.concatenate([d_hi, d_lo], axis=1), tri2)
        w = jnp.exp(z - suffix - run)
        if masked:
            w = jnp.where(keep, w, 0.0)
        w = w.astype(BF16)
        w_wide = jnp.concatenate([w[h * t:(h + 1) * t, :] for h in range(N_HEADS)], axis=1)
        v_stack = jnp.concatenate([jnp.where(head == h, v_blk, jnp.zeros_like(v_blk)) for h in range(N_HEADS)], axis=0)
        acc = acc + _dot(w_wide, v_stack)
        run = run + jnp.sum(drop, axis=1, keepdims=True)
        done = (jnp.max(z_bound - run) < SB_ZERO_EXPONENT).astype(jnp.int32)
        return acc, run, done

    def walk(first_blocks):
        states = []
        for u in range(SB_QBLOCKS):
            iu = i * SB_QBLOCKS + u
            qz = query_block(u)
            acc, run, done = block(iu, jnp.zeros((t, QK_W), F32), jnp.zeros((N_HEADS * t, 1), F32), True, *qz)
            for n in range(1, first_blocks):
                acc, run, done = block(iu - n, acc, run, False, *qz)
            states.append((iu - first_blocks, done, acc, run, qz))
        for u, (j, done, acc, run, qz) in enumerate(states):

            def body(c, qz=qz):
                j, _, acc, run = c
                acc, run, done = block(j, acc, run, False, *qz)
                return j - 1, done, acc, run

            carry = lax.while_loop(lambda c: jnp.logical_and(c[0] >= 0, c[1] == 0), body, (j, done, acc, run))
            o_ref[u * t:(u + 1) * t, :] = carry[2].astype(BF16)

    def first_step():
        for u in range(SB_QBLOCKS):
            qz = query_block(u)
            acc, run, _ = block(u, jnp.zeros((t, QK_W), F32), jnp.zeros((N_HEADS * t, 1), F32), True, *qz)
            for j in range(u - 1, -1, -1):
                acc, run, _ = block(j, acc, run, False, *qz)
            o_ref[u * t:(u + 1) * t, :] = acc.astype(BF16)

    assert SB_QBLOCKS >= SB_STRAIGHT_BLOCKS - 1
    pl.when(i > 0)(functools.partial(walk, SB_STRAIGHT_BLOCKS))
    pl.when(i == 0)(first_step)


def _stick_breaking_mixer(dbuf, bsz, seq):
    d3 = dbuf.reshape(bsz, seq, D_W)
    rows = SB_T * SB_QBLOCKS
    assert seq % rows == 0 and seq % SB_NORM_ROWS == 0
    full = lambda j: pl.BlockSpec((None, seq, QK_W), lambda b, i: (b, 0, j))
    out = pl.pallas_call(
        _sb_kernel,
        grid=(bsz, seq // rows),
        in_specs=[pl.BlockSpec((None, rows, QK_W), lambda b, i: (b, i, 0)), full(1), full(2)],
        out_specs=pl.BlockSpec((None, rows, QK_W), lambda b, i: (b, i, 0)),
        out_shape=jax.ShapeDtypeStruct((bsz, seq, QK_W), BF16),
        scratch_shapes=[pltpu.VMEM((8, 128), F32)],
        compiler_params=_params("parallel", "arbitrary"),
        name="stick_breaking_mixer",
    )(d3, d3, d3)
    return out.reshape(bsz * seq, QK_W)


def _merge_kernel(x_ref, mod_ref, g_ref, oa_ref, ob_ref, oc_ref, od_ref,
                  wg_ref, wa_ref, wb_ref, wc_ref, wd_ref, wo_ref, out_ref):
    x = x_ref[...]
    h = _modulated_norm(x, g_ref[...], mod_ref[1:2, :], mod_ref[0:1, :]).astype(BF16)
    branches = ((oa_ref[...], wa_ref), (ob_ref[...], wb_ref), (oc_ref[...], wc_ref), (od_ref[...], wd_ref))
    merged = jnp.zeros(x.shape, F32)
    for j, (o_j, w_ref) in enumerate(branches):
        gate = _sigmoid(_dot(h, wg_ref[:, j * D_MODEL:(j + 1) * D_MODEL]))
        merged = merged + gate * _dot(o_j, w_ref[...])
    out_ref[...] = x + mod_ref[2:3, :] * _dot(merged.astype(BF16), wo_ref[...])


def _merge(x2d, mod, g_mix, oa, ob, oc, od, wg, wa, wb, wc, wd, wo, seq):
    t = x2d.shape[0]
    tm = TOKEN_TILE
    per_b = seq // tm
    row = lambda w: pl.BlockSpec((tm, w), lambda i: (i, 0))
    return pl.pallas_call(
        _merge_kernel,
        grid=(t // tm,),
        in_specs=[
            row(D_MODEL),
            pl.BlockSpec((None, 6, D_MODEL), lambda i: (i // per_b, 0, 0)),
            _const_spec((1, D_MODEL)),
            row(QK_W), row(CONV_CH), row(GLA_V_W), row(QK_W),
            _const_spec(wg.shape), _const_spec(wa.shape), _const_spec(wb.shape), _const_spec(wc.shape),
            _const_spec(wd.shape), _const_spec(wo.shape),
        ],
        out_specs=row(D_MODEL),
        out_shape=jax.ShapeDtypeStruct((t, D_MODEL), F32),
        compiler_params=_params("parallel"),
        name="gated_merge",
    )(x2d, mod, g_mix, oa, ob, oc, od, wg, wa, wb, wc, wd, wo)


def _mlp_kernel(x_ref, mod_ref, g_ref, wu_ref, wd_ref, out_ref):
    x = x_ref[...]
    h = _modulated_norm(x, g_ref[...], mod_ref[4:5, :], mod_ref[3:4, :]).astype(BF16)
    acc = jnp.zeros(x.shape, F32)
    for j in range(D_FF // D_MODEL):
        cols = slice(j * D_MODEL, (j + 1) * D_MODEL)
        u = jnp.maximum(_dot(h, wu_ref[:, cols]), 0.0)
        acc = acc + _dot((u * u).astype(BF16), wd_ref[cols, :])
    out_ref[...] = x + mod_ref[5:6, :] * acc


def _mlp(x2d, mod, g_mlp, wu, wd, seq):
    t = x2d.shape[0]
    tm = TOKEN_TILE
    per_b = seq // tm
    row = pl.BlockSpec((tm, D_MODEL), lambda i: (i, 0))
    return pl.pallas_call(
        _mlp_kernel,
        grid=(t // tm,),
        in_specs=[
            row,
            pl.BlockSpec((None, 6, D_MODEL), lambda i: (i // per_b, 0, 0)),
            _const_spec((1, D_MODEL)), _const_spec(wu.shape), _const_spec(wd.shape),
        ],
        out_specs=row,
        out_shape=jax.ShapeDtypeStruct((t, D_MODEL), F32),
        compiler_params=_params("parallel"),
        name="relu2_mlp",
    )(x2d, mod, g_mlp, wu, wd)


def _head_block_diag():
    idx = np.arange(QK_W) // HEAD_LANES
    return jnp.asarray((idx[:, None] == idx[None, :]).astype(np.float32) / HEAD_LANES, dtype=BF16)


def kernel(x, c, w_ada, b_ada, g_mix, w_in, q_gain, k_gain, conv_w, conv_b, conv_ln_g, conv_ln_b,
           gla_w_gate, gla_b_gate, gla_o_gain, w_br_a, w_br_b, w_br_c, w_br_d, w_out, g_mlp, w_up, w_down):
    bsz, seq, d = x.shape
    depth = w_ada.shape[0]
    assert d == D_MODEL and seq % (DSWA_BLOCK * DSWA_PATTERNS[-1][1]) == 0 and seq % TOKEN_TILE == 0
    mods = _modulation(c, w_ada, b_ada)
    bd = _head_block_diag()
    x2d = x.reshape(bsz * seq, d)
    qk_scale = HEAD_LANES ** -0.5
    for l in range(depth):
        w = w_in[l]
        o_conv = A_W
        o_c = o_conv + 2 * CONV_CH
        o_low = o_c + 2 * QK_W + 2 * GLA_V_W
        o_d = o_low + GLA_RANK
        o_gate = o_d + D_W
        assert w.shape == (d, o_gate + 4 * d) and w_up.shape[-1] == D_FF
        wa = w[:, :A_W].astype(BF16)
        wb = w[:, o_conv:o_c].astype(BF16)
        wc = jnp.concatenate([w[:, o_c:o_low + GLA_RANK], jnp.zeros((d, GLA_LOW_PAD - GLA_RANK), F32)], axis=1).astype(BF16)
        wd = w[:, o_d:o_gate].astype(BF16)
        wg = w[:, o_gate:].astype(BF16)
        qg = jnp.tile(q_gain[l], N_HEADS).reshape(1, QK_W) * qk_scale
        kg = jnp.tile(k_gain[l], N_HEADS).reshape(1, QK_W)
        mod = mods[l]
        gm = g_mix[l].reshape(1, d)

        a, y, cbuf, dbuf = _in_projection(x2d, mod, gm, wa, wb, wc, wd, qg, kg, bd, seq)
        oa = _dilated_attention(a, bsz, seq)
        ob = _conv_mixer(y, conv_w[l], conv_b[l], conv_ln_g[l], conv_ln_b[l], bsz, seq)
        oc = _gla_mixer(cbuf, gla_w_gate[l], gla_b_gate[l], gla_o_gain[l], bsz, seq)
        od = _stick_breaking_mixer(dbuf, bsz, seq)
        x2d = _merge(x2d, mod, gm, oa, ob, oc, od, wg, w_br_a[l].astype(BF16), w_br_b[l].astype(BF16),
                     w_br_c[l].astype(BF16), w_br_d[l].astype(BF16), w_out[l].astype(BF16), seq)
        x2d = _mlp(x2d, mod, g_mlp[l].reshape(1, d), w_up[l].astype(BF16), w_down[l].astype(BF16), seq)
    return x2d.reshape(bsz, seq, d)
```

```python
import functools

import numpy as np
import jax
import jax.numpy as jnp
from jax import lax
from jax.experimental import pallas as pl
from jax.experimental.pallas import tpu as pltpu

F32 = jnp.float32
BF16 = jnp.bfloat16

D_MODEL = 1024
D_FF = 4 * D_MODEL
NORM_EPS = 1e-6
LOG2_E = 1.4426950408889634
HEAD_LANES = 64
N_HEADS = 4
QK_W = N_HEADS * HEAD_LANES
DSWA_PATTERNS = ((128, 1), (512, 4), (2048, 16))
DSWA_BLOCK = 128
DSWA_SPAN = DSWA_BLOCK * DSWA_PATTERNS[-1][1]
DSWA_GROUP = 16
DSWA_PADDED_GROUP = 24
LANE_TILE = 128
SUBLANES = 8
CONV_CH = 256
CONV_WIDTH = 31
CONV_HALO = 32
GLA_DV = 128
GLA_V_W = N_HEADS * GLA_DV
GLA_RANK = 16
GLA_LOW_PAD = 128
GLA_TAU = 16.0
GLA_CHUNK = 64
GLA_SUB = 16
GLA_CHUNKS_PER_STEP = 16
GLA_CUM_ROWS = 256
SB_T = 256
SB_QBLOCKS = 4
SB_STRAIGHT_BLOCKS = 4
SB_NORM_ROWS = 512
SB_BOUND_SLACK = 1.02
SB_ZERO_EXPONENT = -110.0
TOKEN_TILE = 512
MASK_VALUE = -1e30
VMEM_LIMIT = 56 * 1024 * 1024

A_W = 3 * 3 * QK_W
C_W = 2 * QK_W + 2 * GLA_V_W + GLA_LOW_PAD
D_W = 3 * QK_W


def _dot(a, b):
    return jnp.dot(a, b, preferred_element_type=F32)


def _dot_nt(a, b):
    return lax.dot_general(a, b, (((1,), (1,)), ((), ())), preferred_element_type=F32)


def _dot_tn(a, b):
    return lax.dot_general(a, b, (((0,), (0,)), ((), ())), preferred_element_type=F32)


def _split_bf16(x):
    hi = x.astype(BF16)
    lo = (x - hi.astype(F32)).astype(BF16)
    return hi, lo


def _sigmoid(x):
    return 1.0 / (1.0 + jnp.exp(-x))


def _neg_softplus(x):
    return -(jnp.maximum(x, 0.0) + jnp.log(1.0 + jnp.exp(-jnp.abs(x))))


def _iota(shape, dim):
    return lax.broadcasted_iota(jnp.int32, shape, dim)


def _head_of_lane(width=QK_W, lanes_per_head=HEAD_LANES):
    return jnp.right_shift(_iota((1, width), 1), int(np.log2(lanes_per_head)))


def _ones_where(cond):
    return jnp.where(cond, 1.0, 0.0).astype(BF16)


def _const_spec(shape):
    nd = len(shape)
    return pl.BlockSpec(shape, lambda *_: (0,) * nd, pipeline_mode=pl.Buffered(1))


def _params(*sem):
    return pltpu.CompilerParams(dimension_semantics=sem, vmem_limit_bytes=VMEM_LIMIT)


def _modulated_norm(x, gain, scale, shift):
    ms = jnp.mean(x * x, axis=-1, keepdims=True)
    y = x * lax.rsqrt(ms + NORM_EPS) * gain
    return y * (1.0 + scale) + shift


def _mod_kernel(c_ref, w_ref, b_ref, o_ref):
    c_hi, c_lo = _split_bf16(c_ref[...])
    w_hi, w_lo = _split_bf16(w_ref[...])
    o_ref[...] = _dot(c_hi, w_hi) + _dot(c_hi, w_lo) + _dot(c_lo, w_hi) + b_ref[...]


def _modulation(c, w_ada, b_ada):
    depth, d, six_d = w_ada.shape
    bsz = c.shape[0]
    out = pl.pallas_call(
        _mod_kernel,
        grid=(depth, six_d // d),
        in_specs=[
            pl.BlockSpec((bsz, d), lambda l, j: (0, 0)),
            pl.BlockSpec((None, d, d), lambda l, j: (l, 0, j)),
            pl.BlockSpec((None, 1, d), lambda l, j: (l, 0, j)),
        ],
        out_specs=pl.BlockSpec((None, bsz, d), lambda l, j: (l, 0, j)),
        out_shape=jax.ShapeDtypeStruct((depth, bsz, six_d), F32),
        compiler_params=_params("parallel", "parallel"),
        name="adaln_modulation",
    )(c, w_ada, b_ada.reshape(depth, 1, six_d))
    return out.reshape(depth, bsz, six_d // d, d)


def _inproj_kernel(x_ref, mod_ref, g_ref, wa_ref, wb_ref, wc_ref, wd_ref, qg_ref, kg_ref, bd_ref,
                   a_ref, y_ref, c_ref, d_ref):
    h = _modulated_norm(x_ref[...], g_ref[...], mod_ref[1:2, :], mod_ref[0:1, :]).astype(BF16)
    bd = bd_ref[...]
    a_all = _dot(h, wa_ref[...])
    for j in range(A_W // QK_W):
        cols = slice(j * QK_W, (j + 1) * QK_W)
        acc = a_all[:, cols]
        if j < 6:
            ms = _dot((acc * acc).astype(BF16), bd)
            gain = qg_ref[...] if j < 3 else kg_ref[...]
            acc = acc * lax.rsqrt(ms + NORM_EPS) * gain
        a_ref[:, cols] = acc.astype(BF16)
    u = _dot(h, wb_ref[...])
    y_ref[...] = u[:, :CONV_CH] * _sigmoid(u[:, CONV_CH:])
    c_ref[...] = _dot(h, wc_ref[...])
    d_all = _dot(h, wd_ref[...])
    d_ref[:, :QK_W] = (d_all[:, :QK_W] * (HEAD_LANES ** -0.5)).astype(BF16)
    d_ref[:, QK_W:] = d_all[:, QK_W:].astype(BF16)


def _in_projection(x2d, mod, g_mix, wa, wb, wc, wd, qg, kg, bd, seq):
    t = x2d.shape[0]
    tm = TOKEN_TILE
    per_b = seq // tm
    row = lambda w: pl.BlockSpec((tm, w), lambda i: (i, 0))
    return pl.pallas_call(
        _inproj_kernel,
        grid=(t // tm,),
        in_specs=[
            row(D_MODEL),
            pl.BlockSpec((None, 6, D_MODEL), lambda i: (i // per_b, 0, 0)),
            _const_spec((1, D_MODEL)),
            _const_spec(wa.shape), _const_spec(wb.shape), _const_spec(wc.shape), _const_spec(wd.shape),
            _const_spec((1, QK_W)), _const_spec((1, QK_W)), _const_spec((QK_W, QK_W)),
        ],
        out_specs=[row(A_W), row(CONV_CH), row(C_W), row(D_W)],
        out_shape=[
            jax.ShapeDtypeStruct((t, A_W), BF16),
            jax.ShapeDtypeStruct((t, CONV_CH), F32),
            jax.ShapeDtypeStruct((t, C_W), F32),
            jax.ShapeDtypeStruct((t, D_W), BF16),
        ],
        compiler_params=_params("parallel"),
        name="in_projection",
    )(x2d, mod, g_mix, wa, wb, wc, wd, qg, kg, bd)


def _dswa_group(position, span_idx, q_ref, kp_ref, kc_ref, vp_ref, vc_ref, o_ref, q_st, k_st, v_st, m_st, den_st, acc_st,
                *, slopes, window, dilation):
    blk, span = DSWA_BLOCK, DSWA_SPAN
    halo = blk * dilation
    padded = dilation % DSWA_GROUP == 0
    pitch = DSWA_PADDED_GROUP if padded else DSWA_GROUP
    stride = dilation // DSWA_GROUP * pitch if padded else dilation

    def stage(dst, s, first_group, src, lo, hi, lanes):
        if not padded:
            dst[s, first_group * pitch:first_group * pitch + hi - lo, :] = src[lo:hi, lanes].astype(F32)
            return
        for g in range((hi - lo) // DSWA_GROUP):
            row = (first_group + g) * pitch
            dst[s, row:row + DSWA_GROUP, :] = src[lo + g * DSWA_GROUP:lo + (g + 1) * DSWA_GROUP, lanes].astype(F32)

    for s in range(QK_W // LANE_TILE):
        lanes = slice(s * LANE_TILE, (s + 1) * LANE_TILE)
        stage(q_st, s, 0, q_ref, 0, span, lanes)
        stage(k_st, s, 0, kp_ref, span - halo, span, lanes)
        stage(k_st, s, halo // DSWA_GROUP, kc_ref, 0, span, lanes)
        stage(v_st, s, 0, vp_ref, span - halo, span, lanes)
        stage(v_st, s, halo // DSWA_GROUP, vc_ref, 0, span, lanes)
    head = _head_of_lane()
    qi = _iota((blk, 2 * blk), 0)
    kj = _iota((blk, 2 * blk), 1)
    rel = qi - kj + blk
    in_window = (rel >= 0) & (rel <= window // dilation)
    dist = (dilation * rel).astype(F32)
    bias = jnp.concatenate([jnp.where(in_window, -slopes[h] * dist, MASK_VALUE) for h in range(N_HEADS)], axis=0)
    first_block_penalty = jnp.where(_iota((1, 2 * blk), 1) < blk, MASK_VALUE, 0.0)

    def rows_of(ref, start, n):
        return jnp.concatenate([ref[s, pl.ds(start, n, stride=dilation), :] for s in range(QK_W // LANE_TILE)], axis=1)

    def staged_rows(ref, start, n):
        row = start // DSWA_GROUP * pitch + start % DSWA_GROUP
        return jnp.concatenate([ref[s, pl.ds(row, n, stride=stride), :] for s in range(QK_W // LANE_TILE)], axis=1)

    def put_rows(ref, start, val):
        for s in range(QK_W // LANE_TILE):
            ref[s, pl.ds(start, blk, stride=dilation), :] = val[:, s * LANE_TILE:(s + 1) * LANE_TILE]

    def step(it):
        n = it // dilation
        start = n * halo + it % dilation
        q = staged_rows(q_st, start, blk).astype(BF16)
        k = staged_rows(k_st, start, 2 * blk).astype(BF16)
        v = staged_rows(v_st, start, 2 * blk).astype(BF16)
        q_stack = jnp.concatenate([jnp.where(head == h, q, jnp.zeros_like(q)) for h in range(N_HEADS)], axis=0)
        s = _dot_nt(q_stack, k) + bias
        if n == 0:
            s = s + jnp.where(span_idx == 0, first_block_penalty, 0.0)
        mx = jnp.max(s, axis=1, keepdims=True)
        p = jnp.exp(s - mx)
        den = jnp.sum(p, axis=1, keepdims=True)
        o_all = _dot(p.astype(BF16), v) * (1.0 / den)
        lse_all = mx + jnp.log(den)
        o_new = jnp.zeros((blk, QK_W), F32)
        lse = jnp.zeros((blk, QK_W), F32)
        for h in range(N_HEADS):
            in_head = head == h
            o_new = jnp.where(in_head, o_all[h * blk:(h + 1) * blk], o_new)
            lse = jnp.where(in_head, lse_all[h * blk:(h + 1) * blk], lse)
        if position == 0:
            m_new, acc_new = lse, o_new
        else:
            m_old = rows_of(m_st, start, blk)
            m_new = jnp.maximum(m_old, lse)
            w_old, w_new = jnp.exp(m_old - m_new), jnp.exp(lse - m_new)
            den_old = w_old if position == 1 else rows_of(den_st, start, blk) * w_old
            put_rows(den_st, start, den_old + w_new)
            acc_new = rows_of(acc_st, start, blk) * w_old + o_new * w_new
        if position < len(DSWA_PATTERNS) - 1:
            put_rows(m_st, start, m_new)
        put_rows(acc_st, start, acc_new)

    for it in range(span // blk):
        step(it)
    if position == len(DSWA_PATTERNS) - 1:
        for s in range(QK_W // LANE_TILE):
            o_ref[:, s * LANE_TILE:(s + 1) * LANE_TILE] = (acc_st[s, :, :] / den_st[s, :, :]).astype(BF16)


def _dswa_kernel(q_ref, kp_ref, kc_ref, vp_ref, vc_ref, o_ref, *scratch, slopes):
    span_idx, position = pl.program_id(1), pl.program_id(2)
    for pos in range(len(DSWA_PATTERNS)):
        g = _dswa_group_at(pos)
        window, dilation = DSWA_PATTERNS[g]
        pl.when(position == pos)(functools.partial(
            _dswa_group, pos, span_idx, q_ref, kp_ref, kc_ref, vp_ref, vc_ref, o_ref, *scratch,
            slopes=slopes[g * N_HEADS:(g + 1) * N_HEADS], window=window, dilation=dilation))


def _dswa_group_at(position):
    return len(DSWA_PATTERNS) - 1 - position


def _alibi_slopes(n):
    return [float(np.float32(2.0 ** (-8.0 * i / n))) for i in range(1, n + 1)]


def _dilated_attention(a, bsz, seq):
    span = DSWA_SPAN
    a3 = a.reshape(bsz, seq, A_W)
    n_groups = len(DSWA_PATTERNS)
    col = lambda part, pos: part * n_groups + _dswa_group_at(pos)
    cur = lambda part: pl.BlockSpec((None, span, QK_W), lambda b, i, p: (b, i, col(part, p)))
    prev = lambda part: pl.BlockSpec((None, span, QK_W), lambda b, i, p: (b, jnp.maximum(i - 1, 0), col(part, p)))
    slab = lambda rows: pltpu.VMEM((QK_W // LANE_TILE, rows, LANE_TILE), F32)
    padded = lambda rows: rows // DSWA_GROUP * DSWA_PADDED_GROUP
    out = pl.pallas_call(
        functools.partial(_dswa_kernel, slopes=_alibi_slopes(n_groups * N_HEADS)),
        grid=(bsz, seq // span, n_groups),
        in_specs=[cur(0), prev(1), cur(1), prev(2), cur(2)],
        out_specs=pl.BlockSpec((None, span, QK_W), lambda b, i, g: (b, i, 0)),
        out_shape=jax.ShapeDtypeStruct((bsz, seq, QK_W), BF16),
        scratch_shapes=[slab(padded(span)), slab(padded(2 * span)), slab(padded(2 * span)),
                        slab(span), slab(span), slab(span)],
        compiler_params=_params("parallel", "arbitrary", "arbitrary"),
        name="dilated_attention",
    )(a3, a3, a3, a3, a3)
    return out.reshape(bsz * seq, QK_W)


def _conv_kernel(prev_ref, cur_ref, w_ref, b_ref, lg_ref, lb_ref, o_ref, buf_ref):
    ts = cur_ref.shape[0]
    first = pl.program_id(1) == 0
    buf_ref[0, 0:CONV_HALO, :] = jnp.where(first, 0.0, prev_ref[...])
    buf_ref[0, CONV_HALO:, :] = cur_ref[...]
    rows = ts + CONV_HALO - SUBLANES
    for r in range(1, SUBLANES):
        buf_ref[r, 0:rows, :] = buf_ref[0, r:r + rows, :]
    off = CONV_HALO - (CONV_WIDTH - 1)
    acc = jnp.zeros((ts, CONV_CH), F32) + b_ref[...]
    for w in range(CONV_WIDTH):
        r, base = (off + w) % SUBLANES, (off + w) // SUBLANES * SUBLANES
        acc = acc + buf_ref[r, base:base + ts, :] * w_ref[w:w + 1, :]
    mu = jnp.mean(acc, axis=-1, keepdims=True)
    ctr = acc - mu
    var = jnp.mean(ctr * ctr, axis=-1, keepdims=True)
    y = ctr * lax.rsqrt(var + NORM_EPS) * lg_ref[...] + lb_ref[...]
    o_ref[...] = (y * _sigmoid(y)).astype(BF16)


def _conv_mixer(y, conv_w, conv_b, ln_g, ln_b, bsz, seq):
    ts = TOKEN_TILE
    per_halo = ts // CONV_HALO
    y3 = y.reshape(bsz, seq, CONV_CH)
    w_pad = jnp.zeros((CONV_HALO, CONV_CH), F32).at[:CONV_WIDTH].set(conv_w)
    out = pl.pallas_call(
        _conv_kernel,
        grid=(bsz, seq // ts),
        in_specs=[
            pl.BlockSpec((None, CONV_HALO, CONV_CH), lambda b, i: (b, jnp.maximum(i * per_halo - 1, 0), 0)),
            pl.BlockSpec((None, ts, CONV_CH), lambda b, i: (b, i, 0)),
            _const_spec((CONV_HALO, CONV_CH)), _const_spec((1, CONV_CH)), _const_spec((1, CONV_CH)),
            _const_spec((1, CONV_CH)),
        ],
        out_specs=pl.BlockSpec((None, ts, CONV_CH), lambda b, i: (b, i, 0)),
        out_shape=jax.ShapeDtypeStruct((bsz, seq, CONV_CH), BF16),
        scratch_shapes=[pltpu.VMEM((SUBLANES, ts + CONV_HALO, CONV_CH), F32)],
        compiler_params=_params("parallel", "parallel"),
        name="conv_mixer",
    )(y3, y3, w_pad, conv_b.reshape(1, -1), ln_g.reshape(1, -1), ln_b.reshape(1, -1))
    return out.reshape(bsz * seq, CONV_CH)


def _gla_kernel(q_ref, k_ref, v_ref, r_ref, low_ref, wg_ref, bg_ref, og_ref, tri_ref, am_ref, sm_ref, o_ref, st_ref):
    ch, sub = GLA_CHUNK, GLA_SUB
    tc = q_ref.shape[0]
    n_sub = ch // sub

    @pl.when(pl.program_id(1) == 0)
    def _():
        st_ref[...] = jnp.zeros_like(st_ref)

    head = _head_of_lane()
    key_row = _iota((ch, QK_W), 0)
    att_mask = am_ref[...]
    st_mask = sm_ref[...]

    logit = _dot(low_ref[...].astype(BF16), wg_ref[...]) + bg_ref[...]
    log_a = _neg_softplus(-logit) * (1.0 / GLA_TAU)
    la_hi, la_lo = _split_bf16(log_a)
    cr = GLA_CUM_ROWS
    cum_all = jnp.concatenate(
        [_dot(tri_ref[...], jnp.concatenate([la_hi[g * cr:(g + 1) * cr], la_lo[g * cr:(g + 1) * cr]], axis=0))
         for g in range(tc // cr)], axis=0)

    state = st_ref[...]
    for c in range(tc // ch):
        rows = slice(c * ch, (c + 1) * ch)
        cum = cum_all[rows, :]
        total = cum[ch - 1:ch, :]
        q = q_ref[rows, :] * (HEAD_LANES ** -0.5)
        k = k_ref[rows, :]
        v = v_ref[rows, :].astype(BF16)

        bases = [jnp.zeros((1, QK_W), F32)] + [cum[i * sub - 1:i * sub, :] for i in range(1, n_sub)]
        base_rows = jnp.concatenate([jnp.broadcast_to(b, (sub, QK_W)) for b in bases], axis=0)
        q_rel = q * jnp.exp(cum - base_rows)
        q_all = jnp.concatenate([jnp.where(head == h, q_rel, 0.0) for h in range(N_HEADS)], axis=0).astype(BF16)
        k_all = jnp.concatenate([jnp.where(key_row < (i + 1) * sub, k * jnp.exp(bases[i] - cum), 0.0)
                                 for i in range(n_sub)], axis=0).astype(BF16)
        att = (_dot_nt(q_all, k_all) * att_mask).astype(BF16)
        intra = _dot(att, jnp.concatenate([v] * n_sub, axis=0))
        o = jnp.concatenate([intra[h * ch:(h + 1) * ch, h * GLA_DV:(h + 1) * GLA_DV] for h in range(N_HEADS)], axis=1)

        o = o + _dot_nt((q * jnp.exp(cum)).astype(BF16), state.astype(BF16))
        k_dec = (k * jnp.exp(total - cum)).astype(BF16)
        state = state * jnp.exp(total) + _dot_tn(v, k_dec) * st_mask

        r = r_ref[rows, :]
        outs = []
        for h in range(N_HEADS):
            o_h = o[:, h * GLA_DV:(h + 1) * GLA_DV]
            ms = jnp.mean(o_h * o_h, axis=-1, keepdims=True)
            outs.append(o_h * lax.rsqrt(ms + NORM_EPS) * og_ref[...])
        o_ref[rows, :] = (r * _sigmoid(r) * jnp.concatenate(outs, axis=1)).astype(BF16)
    st_ref[...] = state


def _gla_masks(tc):
    ch, sub = GLA_CHUNK, GLA_SUB
    t = np.arange(tc)
    tri = (t[:, None] >= t[None, :]) & (t[:, None] // ch == t[None, :] // ch)
    a_t = np.arange(N_HEADS * ch) % ch
    a_col = np.arange((ch // sub) * ch)
    att = (a_col[None, :] // ch == a_t[:, None] // sub) & (a_col[None, :] % ch <= a_t[:, None])
    state = np.arange(GLA_V_W)[:, None] // GLA_DV == np.arange(QK_W)[None, :] // HEAD_LANES
    return (jnp.asarray(np.concatenate([tri, tri], axis=1), BF16), jnp.asarray(att, F32), jnp.asarray(state, F32))


def _gla_mixer(cbuf, w_gate, b_gate, o_gain, bsz, seq):
    tc = GLA_CHUNK * GLA_CHUNKS_PER_STEP
    assert seq % tc == 0 and tc % GLA_CUM_ROWS == 0 and GLA_CUM_ROWS % GLA_CHUNK == 0
    c3 = cbuf.reshape(bsz, seq, C_W)
    wg = jnp.zeros((GLA_LOW_PAD, QK_W), F32).at[:GLA_RANK].set(w_gate).astype(BF16)
    tri2, att_mask, st_mask = _gla_masks(GLA_CUM_ROWS)
    spec = lambda w, j: pl.BlockSpec((None, tc, w), lambda b, i: (b, i, j))
    out = pl.pallas_call(
        _gla_kernel,
        grid=(bsz, seq // tc),
        in_specs=[
            spec(QK_W, 0), spec(QK_W, 1), spec(GLA_V_W, 1), spec(GLA_V_W, 2),
            spec(GLA_LOW_PAD, (2 * QK_W + 2 * GLA_V_W) // GLA_LOW_PAD),
            _const_spec((GLA_LOW_PAD, QK_W)), _const_spec((1, QK_W)), _const_spec((1, GLA_DV)),
            _const_spec(tri2.shape), _const_spec(att_mask.shape), _const_spec(st_mask.shape),
        ],
        out_specs=pl.BlockSpec((None, tc, GLA_V_W), lambda b, i: (b, i, 0)),
        out_shape=jax.ShapeDtypeStruct((bsz, seq, GLA_V_W), BF16),
        scratch_shapes=[pltpu.VMEM((GLA_V_W, QK_W), F32)],
        compiler_params=_params("parallel", "arbitrary"),
        name="gla_mixer",
    )(c3, c3, c3, c3, c3, wg, b_gate.reshape(1, -1), o_gain.reshape(1, -1), tri2, att_mask, st_mask)
    return out.reshape(bsz * seq, GLA_V_W)


def _sb_kernel(q_ref, k_ref, v_ref, o_ref, kmax_ref):
    t = SB_T
    seq = k_ref.shape[0]
    i = pl.program_id(1)
    head = _head_of_lane()
    head_ones = _ones_where(jnp.right_shift(_iota((QK_W, QK_W), 0), int(np.log2(HEAD_LANES))) == head)

    @pl.when(i == 0)
    def _():
        best = jnp.zeros((SB_NORM_ROWS, QK_W), F32)
        for c in range(seq // SB_NORM_ROWS):
            kf = k_ref[c * SB_NORM_ROWS:(c + 1) * SB_NORM_ROWS, :].astype(F32)
            sq_hi, sq_lo = _split_bf16(kf * kf)
            best = jnp.maximum(best, _dot(sq_hi, head_ones) + _dot(sq_lo, head_ones))
        kmax_ref[...] = jnp.zeros_like(kmax_ref) + jnp.max(best)

    tri2 = _ones_where(jnp.bitwise_and(_iota((2 * t, t), 0), t - 1) >= _iota((2 * t, t), 1))
    keep = _iota((N_HEADS * t, t), 1) < jnp.bitwise_and(_iota((N_HEADS * t, t), 0), t - 1)

    def query_block(u):
        q = q_ref[u * t:(u + 1) * t, :]
        q_stack = jnp.concatenate([jnp.where(head == h, q, jnp.zeros_like(q)) for h in range(N_HEADS)], axis=0)
        qf = q_stack.astype(F32)
        z_bound = jnp.sqrt(jnp.sum(qf * qf, axis=1, keepdims=True) * kmax_ref[0:1, 0:1]) * SB_BOUND_SLACK
        return q_stack, z_bound

    def block(j, acc, run, masked, q_stack, z_bound):
        start = j * t if isinstance(j, int) else pl.multiple_of(j * t, t)
        k_blk = k_ref[pl.ds(start, t), :]
        v_blk = v_ref[pl.ds(start, t), :]
        z = _dot_nt(q_stack, k_blk)
        drop = jnp.maximum(z, 0.0) + jnp.log(1.0 + jnp.exp2(jnp.abs(z) * (-LOG2_E)))
        if masked:
            drop = jnp.where(keep, drop, 0.0)
        d_hi, d_lo = _split_bf16(drop)
        suffix = _dot(jnp.concatenate([d_hi, d_lo], axis=1), tri2)
        w = jnp.exp(z - suffix - run)
        if masked:
            w = jnp.where(keep, w, 0.0)
        w = w.astype(BF16)
        w_wide = jnp.concatenate([w[h * t:(h + 1) * t, :] for h in range(N_HEADS)], axis=1)
        v_stack = jnp.concatenate([jnp.where(head == h, v_blk, jnp.zeros_like(v_blk)) for h in range(N_HEADS)], axis=0)
        acc = acc + _dot(w_wide, v_stack)
        run = run + jnp.sum(drop, axis=1, keepdims=True)
        done = (jnp.max(z_bound - run) < SB_ZERO_EXPONENT).astype(jnp.int32)
        return acc, run, done

    def walk(first_blocks):
        states = []
        for u in range(SB_QBLOCKS):
            iu = i * SB_QBLOCKS + u
            qz = query_block(u)
            acc, run, done = block(iu, jnp.zeros((t, QK_W), F32), jnp.zeros((N_HEADS * t, 1), F32), True, *qz)
            for n in range(1, first_blocks):
                acc, run, done = block(iu - n, acc, run, False, *qz)
            states.append((iu - first_blocks, done, acc, run, qz))
        for u, (j, done, acc, run, qz) in enumerate(states):

            def body(c, qz=qz):
                j, _, acc, run = c
                acc, run, done = block(j, acc, run, False, *qz)
                return j - 1, done, acc, run

            carry = lax.while_loop(lambda c: jnp.logical_and(c[0] >= 0, c[1] == 0), body, (j, done, acc, run))
            o_ref[u * t:(u + 1) * t, :] = carry[2].astype(BF16)

    def first_step():
        for u in range(SB_QBLOCKS):
            qz = query_block(u)
            acc, run, _ = block(u, jnp.zeros((t, QK_W), F32), jnp.zeros((N_HEADS * t, 1), F32), True, *qz)
            for j in range(u - 1, -1, -1):
                acc, run, _ = block(j, acc, run, False, *qz)
            o_ref[u * t:(u + 1) * t, :] = acc.astype(BF16)

    assert SB_QBLOCKS >= SB_STRAIGHT_BLOCKS - 1
    pl.when(i > 0)(functools.partial(walk, SB_STRAIGHT_BLOCKS))
    pl.when(i == 0)(first_step)


def _stick_breaking_mixer(dbuf, bsz, seq):
    d3 = dbuf.reshape(bsz, seq, D_W)
    rows = SB_T * SB_QBLOCKS
    assert seq % rows == 0 and seq % SB_NORM_ROWS == 0
    full = lambda j: pl.BlockSpec((None, seq, QK_W), lambda b, i: (b, 0, j))
    out = pl.pallas_call(
        _sb_kernel,
        grid=(bsz, seq // rows),
        in_specs=[pl.BlockSpec((None, rows, QK_W), lambda b, i: (b, i, 0)), full(1), full(2)],
        out_specs=pl.BlockSpec((None, rows, QK_W), lambda b, i: (b, i, 0)),
        out_shape=jax.ShapeDtypeStruct((bsz, seq, QK_W), BF16),
        scratch_shapes=[pltpu.VMEM((8, 128), F32)],
        compiler_params=_params("parallel", "arbitrary"),
        name="stick_breaking_mixer",
    )(d3, d3, d3)
    return out.reshape(bsz * seq, QK_W)


def _merge_kernel(x_ref, mod_ref, g_ref, oa_ref, ob_ref, oc_ref, od_ref,
                  wg_ref, wa_ref, wb_ref, wc_ref, wd_ref, wo_ref, out_ref):
    x = x_ref[...]
    h = _modulated_norm(x, g_ref[...], mod_ref[1:2, :], mod_ref[0:1, :]).astype(BF16)
    branches = ((oa_ref[...], wa_ref), (ob_ref[...], wb_ref), (oc_ref[...], wc_ref), (od_ref[...], wd_ref))
    merged = jnp.zeros(x.shape, F32)
    for j, (o_j, w_ref) in enumerate(branches):
        gate = _sigmoid(_dot(h, wg_ref[:, j * D_MODEL:(j + 1) * D_MODEL]))
        merged = merged + gate * _dot(o_j, w_ref[...])
    out_ref[...] = x + mod_ref[2:3, :] * _dot(merged.astype(BF16), wo_ref[...])


def _merge(x2d, mod, g_mix, oa, ob, oc, od, wg, wa, wb, wc, wd, wo, seq):
    t = x2d.shape[0]
    tm = TOKEN_TILE
    per_b = seq // tm
    row = lambda w: pl.BlockSpec((tm, w), lambda i: (i, 0))
    return pl.pallas_call(
        _merge_kernel,
        grid=(t // tm,),
        in_specs=[
            row(D_MODEL),
            pl.BlockSpec((None, 6, D_MODEL), lambda i: (i // per_b, 0, 0)),
            _const_spec((1, D_MODEL)),
            row(QK_W), row(CONV_CH), row(GLA_V_W), row(QK_W),
            _const_spec(wg.shape), _const_spec(wa.shape), _const_spec(wb.shape), _const_spec(wc.shape),
            _const_spec(wd.shape), _const_spec(wo.shape),
        ],
        out_specs=row(D_MODEL),
        out_shape=jax.ShapeDtypeStruct((t, D_MODEL), F32),
        compiler_params=_params("parallel"),
        name="gated_merge",
    )(x2d, mod, g_mix, oa, ob, oc, od, wg, wa, wb, wc, wd, wo)


def _mlp_kernel(x_ref, mod_ref, g_ref, wu_ref, wd_ref, out_ref):
    x = x_ref[...]
    h = _modulated_norm(x, g_ref[...], mod_ref[4:5, :], mod_ref[3:4, :]).astype(BF16)
    acc = jnp.zeros(x.shape, F32)
    for j in range(D_FF // D_MODEL):
        cols = slice(j * D_MODEL, (j + 1) * D_MODEL)
        u = jnp.maximum(_dot(h, wu_ref[:, cols]), 0.0)
        acc = acc + _dot((u * u).astype(BF16), wd_ref[cols, :])
    out_ref[...] = x + mod_ref[5:6, :] * acc


def _mlp(x2d, mod, g_mlp, wu, wd, seq):
    t = x2d.shape[0]
    tm = TOKEN_TILE
    per_b = seq // tm
    row = pl.BlockSpec((tm, D_MODEL), lambda i: (i, 0))
    return pl.pallas_call(
        _mlp_kernel,
        grid=(t // tm,),
        in_specs=[
            row,
            pl.BlockSpec((None, 6, D_MODEL), lambda i: (i // per_b, 0, 0)),
            _const_spec((1, D_MODEL)), _const_spec(wu.shape), _const_spec(wd.shape),
        ],
        out_specs=row,
        out_shape=jax.ShapeDtypeStruct((t, D_MODEL), F32),
        compiler_params=_params("parallel"),
        name="relu2_mlp",
    )(x2d, mod, g_mlp, wu, wd)


def _head_block_diag():
    idx = np.arange(QK_W) // HEAD_LANES
    return jnp.asarray((idx[:, None] == idx[None, :]).astype(np.float32) / HEAD_LANES, dtype=BF16)


def kernel(x, c, w_ada, b_ada, g_mix, w_in, q_gain, k_gain, conv_w, conv_b, conv_ln_g, conv_ln_b,
           gla_w_gate, gla_b_gate, gla_o_gain, w_br_a, w_br_b, w_br_c, w_br_d, w_out, g_mlp, w_up, w_down):
    bsz, seq, d = x.shape
    depth = w_ada.shape[0]
    assert d == D_MODEL and seq % (DSWA_BLOCK * DSWA_PATTERNS[-1][1]) == 0 and seq % TOKEN_TILE == 0
    mods = _modulation(c, w_ada, b_ada)
    bd = _head_block_diag()
    x2d = x.reshape(bsz * seq, d)
    qk_scale = HEAD_LANES ** -0.5
    for l in range(depth):
        w = w_in[l]
        o_conv = A_W
        o_c = o_conv + 2 * CONV_CH
        o_low = o_c + 2 * QK_W + 2 * GLA_V_W
        o_d = o_low + GLA_RANK
        o_gate = o_d + D_W
        assert w.shape == (d, o_gate + 4 * d) and w_up.shape[-1] == D_FF
        wa = w[:, :A_W].astype(BF16)
        wb = w[:, o_conv:o_c].astype(BF16)
        wc = jnp.concatenate([w[:, o_c:o_low + GLA_RANK], jnp.zeros((d, GLA_LOW_PAD - GLA_RANK), F32)], axis=1).astype(BF16)
        wd = w[:, o_d:o_gate].astype(BF16)
        wg = w[:, o_gate:].astype(BF16)
        qg = jnp.tile(q_gain[l], N_HEADS).reshape(1, QK_W) * qk_scale
        kg = jnp.tile(k_gain[l], N_HEADS).reshape(1, QK_W)
        mod = mods[l]
        gm = g_mix[l].reshape(1, d)

        a, y, cbuf, dbuf = _in_projection(x2d, mod, gm, wa, wb, wc, wd, qg, kg, bd, seq)
        oa = _dilated_attention(a, bsz, seq)
        ob = _conv_mixer(y, conv_w[l], conv_b[l], conv_ln_g[l], conv_ln_b[l], bsz, seq)
        oc = _gla_mixer(cbuf, gla_w_gate[l], gla_b_gate[l], gla_o_gain[l], bsz, seq)
        od = _stick_breaking_mixer(dbuf, bsz, seq)
        x2d = _merge(x2d, mod, gm, oa, ob, oc, od, wg, w_br_a[l].astype(BF16), w_br_b[l].astype(BF16),
                     w_br_c[l].astype(BF16), w_br_d[l].astype(BF16), w_out[l].astype(BF16), seq)
        x2d = _mlp(x2d, mod, g_mlp[l].reshape(1, d), w_up[l].astype(BF16), w_down[l].astype(BF16), seq)
    return x2d.reshape(bsz, seq, d)
```

```python
import functools

import numpy as np
import jax
import jax.numpy as jnp
from jax import lax
from jax.experimental import pallas as pl
from jax.experimental.pallas import tpu as pltpu

F32 = jnp.float32
BF16 = jnp.bfloat16

D_MODEL = 1024
D_FF = 4 * D_MODEL
NORM_EPS = 1e-6
LOG2_E = 1.4426950408889634
HEAD_LANES = 64
N_HEADS = 4
QK_W = N_HEADS * HEAD_LANES
DSWA_PATTERNS = ((128, 1), (512, 4), (2048, 16))
DSWA_BLOCK = 128
DSWA_SPAN = DSWA_BLOCK * DSWA_PATTERNS[-1][1]
DSWA_GROUP = 16
DSWA_PADDED_GROUP = 24
LANE_TILE = 128
SUBLANES = 8
CONV_CH = 256
CONV_WIDTH = 31
CONV_HALO = 32
GLA_DV = 128
GLA_V_W = N_HEADS * GLA_DV
GLA_RANK = 16
GLA_LOW_PAD = 128
GLA_TAU = 16.0
GLA_CHUNK = 64
GLA_SUB = 16
GLA_CHUNKS_PER_STEP = 16
GLA_CUM_ROWS = 256
SB_T = 256
SB_QBLOCKS = 4
SB_STRAIGHT_BLOCKS = 4
SB_NORM_ROWS = 512
SB_BOUND_SLACK = 1.02
SB_ZERO_EXPONENT = -104.5
TOKEN_TILE = 512
MASK_VALUE = -1e30
VMEM_LIMIT = 56 * 1024 * 1024

A_W = 3 * 3 * QK_W
C_W = 2 * QK_W + 2 * GLA_V_W + GLA_LOW_PAD
D_W = 3 * QK_W


def _dot(a, b):
    return jnp.dot(a, b, preferred_element_type=F32)


def _dot_nt(a, b):
    return lax.dot_general(a, b, (((1,), (1,)), ((), ())), preferred_element_type=F32)


def _dot_tn(a, b):
    return lax.dot_general(a, b, (((0,), (0,)), ((), ())), preferred_element_type=F32)


def _split_bf16(x):
    hi = x.astype(BF16)
    lo = (x - hi.astype(F32)).astype(BF16)
    return hi, lo


def _sigmoid(x):
    return 1.0 / (1.0 + jnp.exp(-x))


def _neg_softplus(x):
    return -(jnp.maximum(x, 0.0) + jnp.log(1.0 + jnp.exp(-jnp.abs(x))))


def _iota(shape, dim):
    return lax.broadcasted_iota(jnp.int32, shape, dim)


def _head_of_lane(width=QK_W, lanes_per_head=HEAD_LANES):
    return jnp.right_shift(_iota((1, width), 1), int(np.log2(lanes_per_head)))


def _ones_where(cond):
    return jnp.where(cond, 1.0, 0.0).astype(BF16)


def _const_spec(shape):
    nd = len(shape)
    return pl.BlockSpec(shape, lambda *_: (0,) * nd, pipeline_mode=pl.Buffered(1))


def _params(*sem):
    return pltpu.CompilerParams(dimension_semantics=sem, vmem_limit_bytes=VMEM_LIMIT)


def _modulated_norm(x, gain, scale, shift):
    ms = jnp.mean(x * x, axis=-1, keepdims=True)
    y = x * lax.rsqrt(ms + NORM_EPS) * gain
    return y * (1.0 + scale) + shift


def _mod_kernel(c_ref, w_ref, b_ref, o_ref):
    c_hi, c_lo = _split_bf16(c_ref[...])
    w_hi, w_lo = _split_bf16(w_ref[...])
    o_ref[...] = _dot(c_hi, w_hi) + _dot(c_hi, w_lo) + _dot(c_lo, w_hi) + b_ref[...]


def _modulation(c, w_ada, b_ada):
    depth, d, six_d = w_ada.shape
    bsz = c.shape[0]
    out = pl.pallas_call(
        _mod_kernel,
        grid=(depth, six_d // d),
        in_specs=[
            pl.BlockSpec((bsz, d), lambda l, j: (0, 0)),
            pl.BlockSpec((None, d, d), lambda l, j: (l, 0, j)),
            pl.BlockSpec((None, 1, d), lambda l, j: (l, 0, j)),
        ],
        out_specs=pl.BlockSpec((None, bsz, d), lambda l, j: (l, 0, j)),
        out_shape=jax.ShapeDtypeStruct((depth, bsz, six_d), F32),
        compiler_params=_params("parallel", "parallel"),
        name="adaln_modulation",
    )(c, w_ada, b_ada.reshape(depth, 1, six_d))
    return out.reshape(depth, bsz, six_d // d, d)


def _inproj_kernel(x_ref, mod_ref, g_ref, wa_ref, wb_ref, wc_ref, wd_ref, qg_ref, kg_ref, bd_ref,
                   a_ref, y_ref, c_ref, d_ref):
    h = _modulated_norm(x_ref[...], g_ref[...], mod_ref[1:2, :], mod_ref[0:1, :]).astype(BF16)
    bd = bd_ref[...]
    a_all = _dot(h, wa_ref[...])
    for j in range(A_W // QK_W):
        cols = slice(j * QK_W, (j + 1) * QK_W)
        acc = a_all[:, cols]
        if j < 6:
            ms = _dot((acc * acc).astype(BF16), bd)
            gain = qg_ref[...] if j < 3 else kg_ref[...]
            acc = acc * lax.rsqrt(ms + NORM_EPS) * gain
        a_ref[:, cols] = acc.astype(BF16)
    u = _dot(h, wb_ref[...])
    y_ref[...] = u[:, :CONV_CH] * _sigmoid(u[:, CONV_CH:])
    c_ref[...] = _dot(h, wc_ref[...])
    d_all = _dot(h, wd_ref[...])
    d_ref[:, :QK_W] = (d_all[:, :QK_W] * (HEAD_LANES ** -0.5)).astype(BF16)
    d_ref[:, QK_W:] = d_all[:, QK_W:].astype(BF16)


def _in_projection(x2d, mod, g_mix, wa, wb, wc, wd, qg, kg, bd, seq):
    t = x2d.shape[0]
    tm = TOKEN_TILE
    per_b = seq // tm
    row = lambda w: pl.BlockSpec((tm, w), lambda i: (i, 0))
    return pl.pallas_call(
        _inproj_kernel,
        grid=(t // tm,),
        in_specs=[
            row(D_MODEL),
            pl.BlockSpec((None, 6, D_MODEL), lambda i: (i // per_b, 0, 0)),
            _const_spec((1, D_MODEL)),
            _const_spec(wa.shape), _const_spec(wb.shape), _const_spec(wc.shape), _const_spec(wd.shape),
            _const_spec((1, QK_W)), _const_spec((1, QK_W)), _const_spec((QK_W, QK_W)),
        ],
        out_specs=[row(A_W), row(CONV_CH), row(C_W), row(D_W)],
        out_shape=[
            jax.ShapeDtypeStruct((t, A_W), BF16),
            jax.ShapeDtypeStruct((t, CONV_CH), F32),
            jax.ShapeDtypeStruct((t, C_W), F32),
            jax.ShapeDtypeStruct((t, D_W), BF16),
        ],
        compiler_params=_params("parallel"),
        name="in_projection",
    )(x2d, mod, g_mix, wa, wb, wc, wd, qg, kg, bd)


def _dswa_group(position, span_idx, q_ref, kp_ref, kc_ref, vp_ref, vc_ref, o_ref, q_st, k_st, v_st, m_st, den_st, acc_st,
                *, slopes, window, dilation):
    blk, span = DSWA_BLOCK, DSWA_SPAN
    halo = blk * dilation
    padded = dilation % DSWA_GROUP == 0
    pitch = DSWA_PADDED_GROUP if padded else DSWA_GROUP
    stride = dilation // DSWA_GROUP * pitch if padded else dilation

    def stage(dst, s, first_group, src, lo, hi, lanes):
        if not padded:
            dst[s, first_group * pitch:first_group * pitch + hi - lo, :] = src[lo:hi, lanes].astype(F32)
            return
        for g in range((hi - lo) // DSWA_GROUP):
            row = (first_group + g) * pitch
            dst[s, row:row + DSWA_GROUP, :] = src[lo + g * DSWA_GROUP:lo + (g + 1) * DSWA_GROUP, lanes].astype(F32)

    for s in range(QK_W // LANE_TILE):
        lanes = slice(s * LANE_TILE, (s + 1) * LANE_TILE)
        stage(q_st, s, 0, q_ref, 0, span, lanes)
        stage(k_st, s, 0, kp_ref, span - halo, span, lanes)
        stage(k_st, s, halo // DSWA_GROUP, kc_ref, 0, span, lanes)
        stage(v_st, s, 0, vp_ref, span - halo, span, lanes)
        stage(v_st, s, halo // DSWA_GROUP, vc_ref, 0, span, lanes)
    head = _head_of_lane()
    qi = _iota((blk, 2 * blk), 0)
    kj = _iota((blk, 2 * blk), 1)
    rel = qi - kj + blk
    in_window = (rel >= 0) & (rel <= window // dilation)
    dist = (dilation * rel).astype(F32)
    bias = jnp.concatenate([jnp.where(in_window, -slopes[h] * dist, MASK_VALUE) for h in range(N_HEADS)], axis=0)
    first_block_penalty = jnp.where(_iota((1, 2 * blk), 1) < blk, MASK_VALUE, 0.0)

    def rows_of(ref, start, n):
        return jnp.concatenate([ref[s, pl.ds(start, n, stride=dilation), :] for s in range(QK_W // LANE_TILE)], axis=1)

    def staged_rows(ref, start, n):
        row = start // DSWA_GROUP * pitch + start % DSWA_GROUP
        return jnp.concatenate([ref[s, pl.ds(row, n, stride=stride), :] for s in range(QK_W // LANE_TILE)], axis=1)

    def put_rows(ref, start, val):
        for s in range(QK_W // LANE_TILE):
            ref[s, pl.ds(start, blk, stride=dilation), :] = val[:, s * LANE_TILE:(s + 1) * LANE_TILE]

    def step(it):
        n = it // dilation
        start = n * halo + it % dilation
        q = staged_rows(q_st, start, blk).astype(BF16)
        k = staged_rows(k_st, start, 2 * blk).astype(BF16)
        v = staged_rows(v_st, start, 2 * blk).astype(BF16)
        q_stack = jnp.concatenate([jnp.where(head == h, q, jnp.zeros_like(q)) for h in range(N_HEADS)], axis=0)
        s = _dot_nt(q_stack, k) + bias
        if n == 0:
            s = s + jnp.where(span_idx == 0, first_block_penalty, 0.0)
        mx = jnp.max(s, axis=1, keepdims=True)
        p = jnp.exp(s - mx)
        den = jnp.sum(p, axis=1, keepdims=True)
        o_all = _dot(p.astype(BF16), v) * (1.0 / den)
        lse_all = mx + jnp.log(den)
        o_new = jnp.zeros((blk, QK_W), F32)
        lse = jnp.zeros((blk, QK_W), F32)
        for h in range(N_HEADS):
            in_head = head == h
            o_new = jnp.where(in_head, o_all[h * blk:(h + 1) * blk], o_new)
            lse = jnp.where(in_head, lse_all[h * blk:(h + 1) * blk], lse)
        if position == 0:
            m_new, acc_new = lse, o_new
        else:
            m_old = rows_of(m_st, start, blk)
            m_new = jnp.maximum(m_old, lse)
            w_old, w_new = jnp.exp(m_old - m_new), jnp.exp(lse - m_new)
            den_old = w_old if position == 1 else rows_of(den_st, start, blk) * w_old
            put_rows(den_st, start, den_old + w_new)
            acc_new = rows_of(acc_st, start, blk) * w_old + o_new * w_new
        if position < len(DSWA_PATTERNS) - 1:
            put_rows(m_st, start, m_new)
        put_rows(acc_st, start, acc_new)

    for it in range(span // blk):
        step(it)
    if position == len(DSWA_PATTERNS) - 1:
        for s in range(QK_W // LANE_TILE):
            o_ref[:, s * LANE_TILE:(s + 1) * LANE_TILE] = (acc_st[s, :, :] / den_st[s, :, :]).astype(BF16)


def _dswa_kernel(q_ref, kp_ref, kc_ref, vp_ref, vc_ref, o_ref, *scratch, slopes):
    span_idx, position = pl.program_id(1), pl.program_id(2)
    for pos in range(len(DSWA_PATTERNS)):
        g = _dswa_group_at(pos)
        window, dilation = DSWA_PATTERNS[g]
        pl.when(position == pos)(functools.partial(
            _dswa_group, pos, span_idx, q_ref, kp_ref, kc_ref, vp_ref, vc_ref, o_ref, *scratch,
            slopes=slopes[g * N_HEADS:(g + 1) * N_HEADS], window=window, dilation=dilation))


def _dswa_group_at(position):
    return len(DSWA_PATTERNS) - 1 - position


def _alibi_slopes(n):
    return [float(np.float32(2.0 ** (-8.0 * i / n))) for i in range(1, n + 1)]


def _dilated_attention(a, bsz, seq):
    span = DSWA_SPAN
    a3 = a.reshape(bsz, seq, A_W)
    n_groups = len(DSWA_PATTERNS)
    col = lambda part, pos: part * n_groups + _dswa_group_at(pos)
    cur = lambda part: pl.BlockSpec((None, span, QK_W), lambda b, i, p: (b, i, col(part, p)))
    prev = lambda part: pl.BlockSpec((None, span, QK_W), lambda b, i, p: (b, jnp.maximum(i - 1, 0), col(part, p)))
    slab = lambda rows: pltpu.VMEM((QK_W // LANE_TILE, rows, LANE_TILE), F32)
    padded = lambda rows: rows // DSWA_GROUP * DSWA_PADDED_GROUP
    out = pl.pallas_call(
        functools.partial(_dswa_kernel, slopes=_alibi_slopes(n_groups * N_HEADS)),
        grid=(bsz, seq // span, n_groups),
        in_specs=[cur(0), prev(1), cur(1), prev(2), cur(2)],
        out_specs=pl.BlockSpec((None, span, QK_W), lambda b, i, g: (b, i, 0)),
        out_shape=jax.ShapeDtypeStruct((bsz, seq, QK_W), BF16),
        scratch_shapes=[slab(padded(span)), slab(padded(2 * span)), slab(padded(2 * span)),
                        slab(span), slab(span), slab(span)],
        compiler_params=_params("parallel", "arbitrary", "arbitrary"),
        name="dilated_attention",
    )(a3, a3, a3, a3, a3)
    return out.reshape(bsz * seq, QK_W)


def _conv_kernel(prev_ref, cur_ref, w_ref, b_ref, lg_ref, lb_ref, o_ref, buf_ref):
    ts = cur_ref.shape[0]
    first = pl.program_id(1) == 0
    buf_ref[0, 0:CONV_HALO, :] = jnp.where(first, 0.0, prev_ref[...])
    buf_ref[0, CONV_HALO:, :] = cur_ref[...]
    rows = ts + CONV_HALO - SUBLANES
    for r in range(1, SUBLANES):
        buf_ref[r, 0:rows, :] = buf_ref[0, r:r + rows, :]
    off = CONV_HALO - (CONV_WIDTH - 1)
    acc = jnp.zeros((ts, CONV_CH), F32) + b_ref[...]
    for w in range(CONV_WIDTH):
        r, base = (off + w) % SUBLANES, (off + w) // SUBLANES * SUBLANES
        acc = acc + buf_ref[r, base:base + ts, :] * w_ref[w:w + 1, :]
    mu = jnp.mean(acc, axis=-1, keepdims=True)
    ctr = acc - mu
    var = jnp.mean(ctr * ctr, axis=-1, keepdims=True)
    y = ctr * lax.rsqrt(var + NORM_EPS) * lg_ref[...] + lb_ref[...]
    o_ref[...] = (y * _sigmoid(y)).astype(BF16)


def _conv_mixer(y, conv_w, conv_b, ln_g, ln_b, bsz, seq):
    ts = TOKEN_TILE
    per_halo = ts // CONV_HALO
    y3 = y.reshape(bsz, seq, CONV_CH)
    w_pad = jnp.zeros((CONV_HALO, CONV_CH), F32).at[:CONV_WIDTH].set(conv_w)
    out = pl.pallas_call(
        _conv_kernel,
        grid=(bsz, seq // ts),
        in_specs=[
            pl.BlockSpec((None, CONV_HALO, CONV_CH), lambda b, i: (b, jnp.maximum(i * per_halo - 1, 0), 0)),
            pl.BlockSpec((None, ts, CONV_CH), lambda b, i: (b, i, 0)),
            _const_spec((CONV_HALO, CONV_CH)), _const_spec((1, CONV_CH)), _const_spec((1, CONV_CH)),
            _const_spec((1, CONV_CH)),
        ],
        out_specs=pl.BlockSpec((None, ts, CONV_CH), lambda b, i: (b, i, 0)),
        out_shape=jax.ShapeDtypeStruct((bsz, seq, CONV_CH), BF16),
        scratch_shapes=[pltpu.VMEM((SUBLANES, ts + CONV_HALO, CONV_CH), F32)],
        compiler_params=_params("parallel", "parallel"),
        name="conv_mixer",
    )(y3, y3, w_pad, conv_b.reshape(1, -1), ln_g.reshape(1, -1), ln_b.reshape(1, -1))
    return out.reshape(bsz * seq, CONV_CH)


def _gla_kernel(q_ref, k_ref, v_ref, r_ref, low_ref, wg_ref, bg_ref, og_ref, tri_ref, am_ref, sm_ref, o_ref, st_ref):
    ch, sub = GLA_CHUNK, GLA_SUB
    tc = q_ref.shape[0]
    n_sub = ch // sub

    @pl.when(pl.program_id(1) == 0)
    def _():
        st_ref[...] = jnp.zeros_like(st_ref)

    head = _head_of_lane()
    key_row = _iota((ch, QK_W), 0)
    att_mask = am_ref[...]
    st_mask = sm_ref[...]

    logit = _dot(low_ref[...].astype(BF16), wg_ref[...]) + bg_ref[...]
    log_a = _neg_softplus(-logit) * (1.0 / GLA_TAU)
    la_hi, la_lo = _split_bf16(log_a)
    cr = GLA_CUM_ROWS
    cum_all = jnp.concatenate(
        [_dot(tri_ref[...], jnp.concatenate([la_hi[g * cr:(g + 1) * cr], la_lo[g * cr:(g + 1) * cr]], axis=0))
         for g in range(tc // cr)], axis=0)

    state = st_ref[...]
    for c in range(tc // ch):
        rows = slice(c * ch, (c + 1) * ch)
        cum = cum_all[rows, :]
        total = cum[ch - 1:ch, :]
        q = q_ref[rows, :] * (HEAD_LANES ** -0.5)
        k = k_ref[rows, :]
        v = v_ref[rows, :].astype(BF16)

        bases = [jnp.zeros((1, QK_W), F32)] + [cum[i * sub - 1:i * sub, :] for i in range(1, n_sub)]
        base_rows = jnp.concatenate([jnp.broadcast_to(b, (sub, QK_W)) for b in bases], axis=0)
        q_rel = q * jnp.exp(cum - base_rows)
        q_all = jnp.concatenate([jnp.where(head == h, q_rel, 0.0) for h in range(N_HEADS)], axis=0).astype(BF16)
        k_all = jnp.concatenate([jnp.where(key_row < (i + 1) * sub, k * jnp.exp(bases[i] - cum), 0.0)
                                 for i in range(n_sub)], axis=0).astype(BF16)
        att = (_dot_nt(q_all, k_all) * att_mask).astype(BF16)
        intra = _dot(att, jnp.concatenate([v] * n_sub, axis=0))
        o = jnp.concatenate([intra[h * ch:(h + 1) * ch, h * GLA_DV:(h + 1) * GLA_DV] for h in range(N_HEADS)], axis=1)

        o = o + _dot_nt((q * jnp.exp(cum)).astype(BF16), state.astype(BF16))
        k_dec = (k * jnp.exp(total - cum)).astype(BF16)
        state = state * jnp.exp(total) + _dot_tn(v, k_dec) * st_mask

        r = r_ref[rows, :]
        outs = []
        for h in range(N_HEADS):
            o_h = o[:, h * GLA_DV:(h + 1) * GLA_DV]
            ms = jnp.mean(o_h * o_h, axis=-1, keepdims=True)
            outs.append(o_h * lax.rsqrt(ms + NORM_EPS) * og_ref[...])
        o_ref[rows, :] = (r * _sigmoid(r) * jnp.concatenate(outs, axis=1)).astype(BF16)
    st_ref[...] = state


def _gla_masks(tc):
    ch, sub = GLA_CHUNK, GLA_SUB
    t = np.arange(tc)
    tri = (t[:, None] >= t[None, :]) & (t[:, None] // ch == t[None, :] // ch)
    a_t = np.arange(N_HEADS * ch) % ch
    a_col = np.arange((ch // sub) * ch)
    att = (a_col[None, :] // ch == a_t[:, None] // sub) & (a_col[None, :] % ch <= a_t[:, None])
    state = np.arange(GLA_V_W)[:, None] // GLA_DV == np.arange(QK_W)[None, :] // HEAD_LANES
    return (jnp.asarray(np.concatenate([tri, tri], axis=1), BF16), jnp.asarray(att, F32), jnp.asarray(state, F32))


def _gla_mixer(cbuf, w_gate, b_gate, o_gain, bsz, seq):
    tc = GLA_CHUNK * GLA_CHUNKS_PER_STEP
    assert seq % tc == 0 and tc % GLA_CUM_ROWS == 0 and GLA_CUM_ROWS % GLA_CHUNK == 0
    c3 = cbuf.reshape(bsz, seq, C_W)
    wg = jnp.zeros((GLA_LOW_PAD, QK_W), F32).at[:GLA_RANK].set(w_gate).astype(BF16)
    tri2, att_mask, st_mask = _gla_masks(GLA_CUM_ROWS)
    spec = lambda w, j: pl.BlockSpec((None, tc, w), lambda b, i: (b, i, j))
    out = pl.pallas_call(
        _gla_kernel,
        grid=(bsz, seq // tc),
        in_specs=[
            spec(QK_W, 0), spec(QK_W, 1), spec(GLA_V_W, 1), spec(GLA_V_W, 2),
            spec(GLA_LOW_PAD, (2 * QK_W + 2 * GLA_V_W) // GLA_LOW_PAD),
            _const_spec((GLA_LOW_PAD, QK_W)), _const_spec((1, QK_W)), _const_spec((1, GLA_DV)),
            _const_spec(tri2.shape), _const_spec(att_mask.shape), _const_spec(st_mask.shape),
        ],
        out_specs=pl.BlockSpec((None, tc, GLA_V_W), lambda b, i: (b, i, 0)),
        out_shape=jax.ShapeDtypeStruct((bsz, seq, GLA_V_W), BF16),
        scratch_shapes=[pltpu.VMEM((GLA_V_W, QK_W), F32)],
        compiler_params=_params("parallel", "arbitrary"),
        name="gla_mixer",
    )(c3, c3, c3, c3, c3, wg, b_gate.reshape(1, -1), o_gain.reshape(1, -1), tri2, att_mask, st_mask)
    return out.reshape(bsz * seq, GLA_V_W)


def _sb_kernel(q_ref, k_ref, v_ref, o_ref, kmax_ref):
    t = SB_T
    seq = k_ref.shape[0]
    i = pl.program_id(1)
    head = _head_of_lane()
    head_ones = _ones_where(jnp.right_shift(_iota((QK_W, QK_W), 0), int(np.log2(HEAD_LANES))) == head)

    @pl.when(i == 0)
    def _():
        best = jnp.zeros((SB_NORM_ROWS, QK_W), F32)
        for c in range(seq // SB_NORM_ROWS):
            kf = k_ref[c * SB_NORM_ROWS:(c + 1) * SB_NORM_ROWS, :].astype(F32)
            sq_hi, sq_lo = _split_bf16(kf * kf)
            best = jnp.maximum(best, _dot(sq_hi, head_ones) + _dot(sq_lo, head_ones))
        kmax_ref[...] = jnp.zeros_like(kmax_ref) + jnp.max(best)

    tri2 = _ones_where(jnp.bitwise_and(_iota((2 * t, t), 0), t - 1) >= _iota((2 * t, t), 1))
    keep = _iota((N_HEADS * t, t), 1) < jnp.bitwise_and(_iota((N_HEADS * t, t), 0), t - 1)

    def query_block(u):
        q = q_ref[u * t:(u + 1) * t, :]
        q_stack = jnp.concatenate([jnp.where(head == h, q, jnp.zeros_like(q)) for h in range(N_HEADS)], axis=0)
        qf = q_stack.astype(F32)
        z_bound = jnp.sqrt(jnp.sum(qf * qf, axis=1, keepdims=True) * kmax_ref[0:1, 0:1]) * SB_BOUND_SLACK
        return q_stack, z_bound

    def block(j, acc, run, masked, q_stack, z_bound):
        start = j * t if isinstance(j, int) else pl.multiple_of(j * t, t)
        k_blk = k_ref[pl.ds(start, t), :]
        v_blk = v_ref[pl.ds(start, t), :]
        z = _dot_nt(q_stack, k_blk)
        drop = jnp.maximum(z, 0.0) + jnp.log(1.0 + jnp.exp2(jnp.abs(z) * (-LOG2_E)))
        if masked:
            drop = jnp.where(keep, drop, 0.0)
        d_hi, d_lo = _split_bf16(drop)
        suffix = _dot(jnp.concatenate([d_hi, d_lo], axis=1), tri2)
        w = jnp.exp(z - suffix - run)
        if masked:
            w = jnp.where(keep, w, 0.0)
        w = w.astype(BF16)
        w_wide = jnp.concatenate([w[h * t:(h + 1) * t, :] for h in range(N_HEADS)], axis=1)
        v_stack = jnp.concatenate([jnp.where(head == h, v_blk, jnp.zeros_like(v_blk)) for h in range(N_HEADS)], axis=0)
        acc = acc + _dot(w_wide, v_stack)
        run = run + jnp.sum(drop, axis=1, keepdims=True)
        done = (jnp.max(z_bound - run) < SB_ZERO_EXPONENT).astype(jnp.int32)
        return acc, run, done

    def walk(first_blocks):
        states = []
        for u in range(SB_QBLOCKS):
            iu = i * SB_QBLOCKS + u
            qz = query_block(u)
            acc, run, done = block(iu, jnp.zeros((t, QK_W), F32), jnp.zeros((N_HEADS * t, 1), F32), True, *qz)
            for n in range(1, first_blocks):
                acc, run, done = block(iu - n, acc, run, False, *qz)
            states.append((iu - first_blocks, done, acc, run, qz))
        for u, (j, done, acc, run, qz) in enumerate(states):

            def body(c, qz=qz):
                j, _, acc, run = c
                acc, run, done = block(j, acc, run, False, *qz)
                return j - 1, done, acc, run

            carry = lax.while_loop(lambda c: jnp.logical_and(c[0] >= 0, c[1] == 0), body, (j, done, acc, run))
            o_ref[u * t:(u + 1) * t, :] = carry[2].astype(BF16)

    def first_step():
        for u in range(SB_QBLOCKS):
            qz = query_block(u)
            acc, run, _ = block(u, jnp.zeros((t, QK_W), F32), jnp.zeros((N_HEADS * t, 1), F32), True, *qz)
            for j in range(u - 1, -1, -1):
                acc, run, _ = block(j, acc, run, False, *qz)
            o_ref[u * t:(u + 1) * t, :] = acc.astype(BF16)

    assert SB_QBLOCKS >= SB_STRAIGHT_BLOCKS - 1
    pl.when(i > 0)(functools.partial(walk, SB_STRAIGHT_BLOCKS))
    pl.when(i == 0)(first_step)


def _stick_breaking_mixer(dbuf, bsz, seq):
    d3 = dbuf.reshape(bsz, seq, D_W)
    rows = SB_T * SB_QBLOCKS
    assert seq % rows == 0 and seq % SB_NORM_ROWS == 0
    full = lambda j: pl.BlockSpec((None, seq, QK_W), lambda b, i: (b, 0, j))
    out = pl.pallas_call(
        _sb_kernel,
        grid=(bsz, seq // rows),
        in_specs=[pl.BlockSpec((None, rows, QK_W), lambda b, i: (b, i, 0)), full(1), full(2)],
        out_specs=pl.BlockSpec((None, rows, QK_W), lambda b, i: (b, i, 0)),
        out_shape=jax.ShapeDtypeStruct((bsz, seq, QK_W), BF16),
        scratch_shapes=[pltpu.VMEM((8, 128), F32)],
        compiler_params=_params("parallel", "arbitrary"),
        name="stick_breaking_mixer",
    )(d3, d3, d3)
    return out.reshape(bsz * seq, QK_W)


def _merge_kernel(x_ref, mod_ref, g_ref, oa_ref, ob_ref, oc_ref, od_ref,
                  wg_ref, wa_ref, wb_ref, wc_ref, wd_ref, wo_ref, out_ref):
    x = x_ref[...]
    h = _modulated_norm(x, g_ref[...], mod_ref[1:2, :], mod_ref[0:1, :]).astype(BF16)
    branches = ((oa_ref[...], wa_ref), (ob_ref[...], wb_ref), (oc_ref[...], wc_ref), (od_ref[...], wd_ref))
    merged = jnp.zeros(x.shape, F32)
    for j, (o_j, w_ref) in enumerate(branches):
        gate = _sigmoid(_dot(h, wg_ref[:, j * D_MODEL:(j + 1) * D_MODEL]))
        merged = merged + gate * _dot(o_j, w_ref[...])
    out_ref[...] = x + mod_ref[2:3, :] * _dot(merged.astype(BF16), wo_ref[...])


def _merge(x2d, mod, g_mix, oa, ob, oc, od, wg, wa, wb, wc, wd, wo, seq):
    t = x2d.shape[0]
    tm = TOKEN_TILE
    per_b = seq // tm
    row = lambda w: pl.BlockSpec((tm, w), lambda i: (i, 0))
    return pl.pallas_call(
        _merge_kernel,
        grid=(t // tm,),
        in_specs=[
            row(D_MODEL),
            pl.BlockSpec((None, 6, D_MODEL), lambda i: (i // per_b, 0, 0)),
            _const_spec((1, D_MODEL)),
            row(QK_W), row(CONV_CH), row(GLA_V_W), row(QK_W),
            _const_spec(wg.shape), _const_spec(wa.shape), _const_spec(wb.shape), _const_spec(wc.shape),
            _const_spec(wd.shape), _const_spec(wo.shape),
        ],
        out_specs=row(D_MODEL),
        out_shape=jax.ShapeDtypeStruct((t, D_MODEL), F32),
        compiler_params=_params("parallel"),
        name="gated_merge",
    )(x2d, mod, g_mix, oa, ob, oc, od, wg, wa, wb, wc, wd, wo)


def _mlp_kernel(x_ref, mod_ref, g_ref, wu_ref, wd_ref, out_ref):
    x = x_ref[...]
    h = _modulated_norm(x, g_ref[...], mod_ref[4:5, :], mod_ref[3:4, :]).astype(BF16)
    acc = jnp.zeros(x.shape, F32)
    for j in range(D_FF // D_MODEL):
        cols = slice(j * D_MODEL, (j + 1) * D_MODEL)
        u = jnp.maximum(_dot(h, wu_ref[:, cols]), 0.0)
        acc = acc + _dot((u * u).astype(BF16), wd_ref[cols, :])
    out_ref[...] = x + mod_ref[5:6, :] * acc


def _mlp(x2d, mod, g_mlp, wu, wd, seq):
    t = x2d.shape[0]
    tm = TOKEN_TILE
    per_b = seq // tm
    row = pl.BlockSpec((tm, D_MODEL), lambda i: (i, 0))
    return pl.pallas_call(
        _mlp_kernel,
        grid=(t // tm,),
        in_specs=[
            row,
            pl.BlockSpec((None, 6, D_MODEL), lambda i: (i // per_b, 0, 0)),
            _const_spec((1, D_MODEL)), _const_spec(wu.shape), _const_spec(wd.shape),
        ],
        out_specs=row,
        out_shape=jax.ShapeDtypeStruct((t, D_MODEL), F32),
        compiler_params=_params("parallel"),
        name="relu2_mlp",
    )(x2d, mod, g_mlp, wu, wd)


def _head_block_diag():
    idx = np.arange(QK_W) // HEAD_LANES
    return jnp.asarray((idx[:, None] == idx[None, :]).astype(np.float32) / HEAD_LANES, dtype=BF16)


def kernel(x, c, w_ada, b_ada, g_mix, w_in, q_gain, k_gain, conv_w, conv_b, conv_ln_g, conv_ln_b,
           gla_w_gate, gla_b_gate, gla_o_gain, w_br_a, w_br_b, w_br_c, w_br_d, w_out, g_mlp, w_up, w_down):
    bsz, seq, d = x.shape
    depth = w_ada.shape[0]
    assert d == D_MODEL and seq % (DSWA_BLOCK * DSWA_PATTERNS[-1][1]) == 0 and seq % TOKEN_TILE == 0
    mods = _modulation(c, w_ada, b_ada)
    bd = _head_block_diag()
    x2d = x.reshape(bsz * seq, d)
    qk_scale = HEAD_LANES ** -0.5
    for l in range(depth):
        w = w_in[l]
        o_conv = A_W
        o_c = o_conv + 2 * CONV_CH
        o_low = o_c + 2 * QK_W + 2 * GLA_V_W
        o_d = o_low + GLA_RANK
        o_gate = o_d + D_W
        assert w.shape == (d, o_gate + 4 * d) and w_up.shape[-1] == D_FF
        wa = w[:, :A_W].astype(BF16)
        wb = w[:, o_conv:o_c].astype(BF16)
        wc = jnp.concatenate([w[:, o_c:o_low + GLA_RANK], jnp.zeros((d, GLA_LOW_PAD - GLA_RANK), F32)], axis=1).astype(BF16)
        wd = w[:, o_d:o_gate].astype(BF16)
        wg = w[:, o_gate:].astype(BF16)
        qg = jnp.tile(q_gain[l], N_HEADS).reshape(1, QK_W) * qk_scale
        kg = jnp.tile(k_gain[l], N_HEADS).reshape(1, QK_W)
        mod = mods[l]
        gm = g_mix[l].reshape(1, d)

        a, y, cbuf, dbuf = _in_projection(x2d, mod, gm, wa, wb, wc, wd, qg, kg, bd, seq)
        oa = _dilated_attention(a, bsz, seq)
        ob = _conv_mixer(y, conv_w[l], conv_b[l], conv_ln_g[l], conv_ln_b[l], bsz, seq)
        oc = _gla_mixer(cbuf, gla_w_gate[l], gla_b_gate[l], gla_o_gain[l], bsz, seq)
        od = _stick_breaking_mixer(dbuf, bsz, seq)
        x2d = _merge(x2d, mod, gm, oa, ob, oc, od, wg, w_br_a[l].astype(BF16), w_br_b[l].astype(BF16),
                     w_br_c[l].astype(BF16), w_br_d[l].astype(BF16), w_out[l].astype(BF16), seq)
        x2d = _mlp(x2d, mod, g_mlp[l].reshape(1, d), w_up[l].astype(BF16), w_down[l].astype(BF16), seq)
    return x2d.reshape(bsz, seq, d)
```

```python
import functools

import numpy as np
import jax
import jax.numpy as jnp
from jax import lax
from jax.experimental import pallas as pl
from jax.experimental.pallas import tpu as pltpu

F32 = jnp.float32
BF16 = jnp.bfloat16

D_MODEL = 1024
D_FF = 4 * D_MODEL
NORM_EPS = 1e-6
LOG2_E = 1.4426950408889634
HEAD_LANES = 64
N_HEADS = 4
QK_W = N_HEADS * HEAD_LANES
DSWA_PATTERNS = ((128, 1), (512, 4), (2048, 16))
DSWA_BLOCK = 128
DSWA_SPAN = DSWA_BLOCK * DSWA_PATTERNS[-1][1]
DSWA_GROUP = 16
DSWA_PADDED_GROUP = 24
LANE_TILE = 128
SUBLANES = 8
CONV_CH = 256
CONV_WIDTH = 31
CONV_HALO = 32
GLA_DV = 128
GLA_V_W = N_HEADS * GLA_DV
GLA_RANK = 16
GLA_LOW_PAD = 128
GLA_TAU = 16.0
GLA_CHUNK = 64
GLA_SUB = 16
GLA_CHUNKS_PER_STEP = 16
GLA_CUM_ROWS = 128
SB_T = 256
SB_QBLOCKS = 4
SB_STRAIGHT_BLOCKS = 4
SB_NORM_ROWS = 512
SB_BOUND_SLACK = 1.02
SB_ZERO_EXPONENT = -110.0
TOKEN_TILE = 512
MASK_VALUE = -1e30
VMEM_LIMIT = 56 * 1024 * 1024

A_W = 3 * 3 * QK_W
C_W = 2 * QK_W + 2 * GLA_V_W + GLA_LOW_PAD
D_W = 3 * QK_W


def _dot(a, b):
    return jnp.dot(a, b, preferred_element_type=F32)


def _dot_nt(a, b):
    return lax.dot_general(a, b, (((1,), (1,)), ((), ())), preferred_element_type=F32)


def _dot_tn(a, b):
    return lax.dot_general(a, b, (((0,), (0,)), ((), ())), preferred_element_type=F32)


def _split_bf16(x):
    hi = x.astype(BF16)
    lo = (x - hi.astype(F32)).astype(BF16)
    return hi, lo


def _sigmoid(x):
    return 1.0 / (1.0 + jnp.exp(-x))


def _neg_softplus(x):
    return -(jnp.maximum(x, 0.0) + jnp.log(1.0 + jnp.exp(-jnp.abs(x))))


def _iota(shape, dim):
    return lax.broadcasted_iota(jnp.int32, shape, dim)


def _head_of_lane(width=QK_W, lanes_per_head=HEAD_LANES):
    return jnp.right_shift(_iota((1, width), 1), int(np.log2(lanes_per_head)))


def _ones_where(cond):
    return jnp.where(cond, 1.0, 0.0).astype(BF16)


def _const_spec(shape):
    nd = len(shape)
    return pl.BlockSpec(shape, lambda *_: (0,) * nd, pipeline_mode=pl.Buffered(1))


def _params(*sem):
    return pltpu.CompilerParams(dimension_semantics=sem, vmem_limit_bytes=VMEM_LIMIT)


def _modulated_norm(x, gain, scale, shift):
    ms = jnp.mean(x * x, axis=-1, keepdims=True)
    y = x * lax.rsqrt(ms + NORM_EPS) * gain
    return y * (1.0 + scale) + shift


def _mod_kernel(c_ref, w_ref, b_ref, o_ref):
    c_hi, c_lo = _split_bf16(c_ref[...])
    w_hi, w_lo = _split_bf16(w_ref[...])
    o_ref[...] = _dot(c_hi, w_hi) + _dot(c_hi, w_lo) + _dot(c_lo, w_hi) + b_ref[...]


def _modulation(c, w_ada, b_ada):
    depth, d, six_d = w_ada.shape
    bsz = c.shape[0]
    out = pl.pallas_call(
        _mod_kernel,
        grid=(depth, six_d // d),
        in_specs=[
            pl.BlockSpec((bsz, d), lambda l, j: (0, 0)),
            pl.BlockSpec((None, d, d), lambda l, j: (l, 0, j)),
            pl.BlockSpec((None, 1, d), lambda l, j: (l, 0, j)),
        ],
        out_specs=pl.BlockSpec((None, bsz, d), lambda l, j: (l, 0, j)),
        out_shape=jax.ShapeDtypeStruct((depth, bsz, six_d), F32),
        compiler_params=_params("parallel", "parallel"),
        name="adaln_modulation",
    )(c, w_ada, b_ada.reshape(depth, 1, six_d))
    return out.reshape(depth, bsz, six_d // d, d)


def _inproj_kernel(x_ref, mod_ref, g_ref, wa_ref, wb_ref, wc_ref, wd_ref, qg_ref, kg_ref, bd_ref,
                   a_ref, y_ref, c_ref, d_ref):
    h = _modulated_norm(x_ref[...], g_ref[...], mod_ref[1:2, :], mod_ref[0:1, :]).astype(BF16)
    bd = bd_ref[...]
    a_all = _dot(h, wa_ref[...])
    for j in range(A_W // QK_W):
        cols = slice(j * QK_W, (j + 1) * QK_W)
        acc = a_all[:, cols]
        if j < 6:
            ms = _dot((acc * acc).astype(BF16), bd)
            gain = qg_ref[...] if j < 3 else kg_ref[...]
            acc = acc * lax.rsqrt(ms + NORM_EPS) * gain
        a_ref[:, cols] = acc.astype(BF16)
    u = _dot(h, wb_ref[...])
    y_ref[...] = u[:, :CONV_CH] * _sigmoid(u[:, CONV_CH:])
    c_ref[...] = _dot(h, wc_ref[...])
    d_all = _dot(h, wd_ref[...])
    d_ref[:, :QK_W] = (d_all[:, :QK_W] * (HEAD_LANES ** -0.5)).astype(BF16)
    d_ref[:, QK_W:] = d_all[:, QK_W:].astype(BF16)


def _in_projection(x2d, mod, g_mix, wa, wb, wc, wd, qg, kg, bd, seq):
    t = x2d.shape[0]
    tm = TOKEN_TILE
    per_b = seq // tm
    row = lambda w: pl.BlockSpec((tm, w), lambda i: (i, 0))
    return pl.pallas_call(
        _inproj_kernel,
        grid=(t // tm,),
        in_specs=[
            row(D_MODEL),
            pl.BlockSpec((None, 6, D_MODEL), lambda i: (i // per_b, 0, 0)),
            _const_spec((1, D_MODEL)),
            _const_spec(wa.shape), _const_spec(wb.shape), _const_spec(wc.shape), _const_spec(wd.shape),
            _const_spec((1, QK_W)), _const_spec((1, QK_W)), _const_spec((QK_W, QK_W)),
        ],
        out_specs=[row(A_W), row(CONV_CH), row(C_W), row(D_W)],
        out_shape=[
            jax.ShapeDtypeStruct((t, A_W), BF16),
            jax.ShapeDtypeStruct((t, CONV_CH), F32),
            jax.ShapeDtypeStruct((t, C_W), F32),
            jax.ShapeDtypeStruct((t, D_W), BF16),
        ],
        compiler_params=_params("parallel"),
        name="in_projection",
    )(x2d, mod, g_mix, wa, wb, wc, wd, qg, kg, bd)


def _dswa_group(position, span_idx, q_ref, kp_ref, kc_ref, vp_ref, vc_ref, o_ref, q_st, k_st, v_st, m_st, den_st, acc_st,
                *, slopes, window, dilation):
    blk, span = DSWA_BLOCK, DSWA_SPAN
    halo = blk * dilation
    padded = dilation % DSWA_GROUP == 0
    pitch = DSWA_PADDED_GROUP if padded else DSWA_GROUP
    stride = dilation // DSWA_GROUP * pitch if padded else dilation

    def stage(dst, s, first_group, src, lo, hi, lanes):
        if not padded:
            dst[s, first_group * pitch:first_group * pitch + hi - lo, :] = src[lo:hi, lanes].astype(F32)
            return
        for g in range((hi - lo) // DSWA_GROUP):
            row = (first_group + g) * pitch
            dst[s, row:row + DSWA_GROUP, :] = src[lo + g * DSWA_GROUP:lo + (g + 1) * DSWA_GROUP, lanes].astype(F32)

    for s in range(QK_W // LANE_TILE):
        lanes = slice(s * LANE_TILE, (s + 1) * LANE_TILE)
        stage(q_st, s, 0, q_ref, 0, span, lanes)
        stage(k_st, s, 0, kp_ref, span - halo, span, lanes)
        stage(k_st, s, halo // DSWA_GROUP, kc_ref, 0, span, lanes)
        stage(v_st, s, 0, vp_ref, span - halo, span, lanes)
        stage(v_st, s, halo // DSWA_GROUP, vc_ref, 0, span, lanes)
    head = _head_of_lane()
    qi = _iota((blk, 2 * blk), 0)
    kj = _iota((blk, 2 * blk), 1)
    rel = qi - kj + blk
    in_window = (rel >= 0) & (rel <= window // dilation)
    dist = (dilation * rel).astype(F32)
    bias = jnp.concatenate([jnp.where(in_window, -slopes[h] * dist, MASK_VALUE) for h in range(N_HEADS)], axis=0)
    first_block_penalty = jnp.where(_iota((1, 2 * blk), 1) < blk, MASK_VALUE, 0.0)

    def rows_of(ref, start, n):
        return jnp.concatenate([ref[s, pl.ds(start, n, stride=dilation), :] for s in range(QK_W // LANE_TILE)], axis=1)

    def staged_rows(ref, start, n):
        row = start // DSWA_GROUP * pitch + start % DSWA_GROUP
        return jnp.concatenate([ref[s, pl.ds(row, n, stride=stride), :] for s in range(QK_W // LANE_TILE)], axis=1)

    def put_rows(ref, start, val):
        for s in range(QK_W // LANE_TILE):
            ref[s, pl.ds(start, blk, stride=dilation), :] = val[:, s * LANE_TILE:(s + 1) * LANE_TILE]

    def step(it):
        n = it // dilation
        start = n * halo + it % dilation
        q = staged_rows(q_st, start, blk).astype(BF16)
        k = staged_rows(k_st, start, 2 * blk).astype(BF16)
        v = staged_rows(v_st, start, 2 * blk).astype(BF16)
        q_stack = jnp.concatenate([jnp.where(head == h, q, jnp.zeros_like(q)) for h in range(N_HEADS)], axis=0)
        s = _dot_nt(q_stack, k) + bias
        if n == 0:
            s = s + jnp.where(span_idx == 0, first_block_penalty, 0.0)
        mx = jnp.max(s, axis=1, keepdims=True)
        p = jnp.exp(s - mx)
        den = jnp.sum(p, axis=1, keepdims=True)
        o_all = _dot(p.astype(BF16), v) * (1.0 / den)
        lse_all = mx + jnp.log(den)
        o_new = jnp.zeros((blk, QK_W), F32)
        lse = jnp.zeros((blk, QK_W), F32)
        for h in range(N_HEADS):
            in_head = head == h
            o_new = jnp.where(in_head, o_all[h * blk:(h + 1) * blk], o_new)
            lse = jnp.where(in_head, lse_all[h * blk:(h + 1) * blk], lse)
        if position == 0:
            m_new, acc_new = lse, o_new
        else:
            m_old = rows_of(m_st, start, blk)
            m_new = jnp.maximum(m_old, lse)
            w_old, w_new = jnp.exp(m_old - m_new), jnp.exp(lse - m_new)
            den_old = w_old if position == 1 else rows_of(den_st, start, blk) * w_old
            put_rows(den_st, start, den_old + w_new)
            acc_new = rows_of(acc_st, start, blk) * w_old + o_new * w_new
        if position < len(DSWA_PATTERNS) - 1:
            put_rows(m_st, start, m_new)
        put_rows(acc_st, start, acc_new)

    for it in range(span // blk):
        step(it)
    if position == len(DSWA_PATTERNS) - 1:
        for s in range(QK_W // LANE_TILE):
            o_ref[:, s * LANE_TILE:(s + 1) * LANE_TILE] = (acc_st[s, :, :] / den_st[s, :, :]).astype(BF16)


def _dswa_kernel(q_ref, kp_ref, kc_ref, vp_ref, vc_ref, o_ref, *scratch, slopes):
    span_idx, position = pl.program_id(1), pl.program_id(2)
    for pos in range(len(DSWA_PATTERNS)):
        g = _dswa_group_at(pos)
        window, dilation = DSWA_PATTERNS[g]
        pl.when(position == pos)(functools.partial(
            _dswa_group, pos, span_idx, q_ref, kp_ref, kc_ref, vp_ref, vc_ref, o_ref, *scratch,
            slopes=slopes[g * N_HEADS:(g + 1) * N_HEADS], window=window, dilation=dilation))


def _dswa_group_at(position):
    return len(DSWA_PATTERNS) - 1 - position


def _alibi_slopes(n):
    return [float(np.float32(2.0 ** (-8.0 * i / n))) for i in range(1, n + 1)]


def _dilated_attention(a, bsz, seq):
    span = DSWA_SPAN
    a3 = a.reshape(bsz, seq, A_W)
    n_groups = len(DSWA_PATTERNS)
    col = lambda part, pos: part * n_groups + _dswa_group_at(pos)
    cur = lambda part: pl.BlockSpec((None, span, QK_W), lambda b, i, p: (b, i, col(part, p)))
    prev = lambda part: pl.BlockSpec((None, span, QK_W), lambda b, i, p: (b, jnp.maximum(i - 1, 0), col(part, p)))
    slab = lambda rows: pltpu.VMEM((QK_W // LANE_TILE, rows, LANE_TILE), F32)
    padded = lambda rows: rows // DSWA_GROUP * DSWA_PADDED_GROUP
    out = pl.pallas_call(
        functools.partial(_dswa_kernel, slopes=_alibi_slopes(n_groups * N_HEADS)),
        grid=(bsz, seq // span, n_groups),
        in_specs=[cur(0), prev(1), cur(1), prev(2), cur(2)],
        out_specs=pl.BlockSpec((None, span, QK_W), lambda b, i, g: (b, i, 0)),
        out_shape=jax.ShapeDtypeStruct((bsz, seq, QK_W), BF16),
        scratch_shapes=[slab(padded(span)), slab(padded(2 * span)), slab(padded(2 * span)),
                        slab(span), slab(span), slab(span)],
        compiler_params=_params("parallel", "arbitrary", "arbitrary"),
        name="dilated_attention",
    )(a3, a3, a3, a3, a3)
    return out.reshape(bsz * seq, QK_W)


def _conv_kernel(prev_ref, cur_ref, w_ref, b_ref, lg_ref, lb_ref, o_ref, buf_ref):
    ts = cur_ref.shape[0]
    first = pl.program_id(1) == 0
    buf_ref[0, 0:CONV_HALO, :] = jnp.where(first, 0.0, prev_ref[...])
    buf_ref[0, CONV_HALO:, :] = cur_ref[...]
    rows = ts + CONV_HALO - SUBLANES
    for r in range(1, SUBLANES):
        buf_ref[r, 0:rows, :] = buf_ref[0, r:r + rows, :]
    off = CONV_HALO - (CONV_WIDTH - 1)
    acc = jnp.zeros((ts, CONV_CH), F32) + b_ref[...]
    for w in range(CONV_WIDTH):
        r, base = (off + w) % SUBLANES, (off + w) // SUBLANES * SUBLANES
        acc = acc + buf_ref[r, base:base + ts, :] * w_ref[w:w + 1, :]
    mu = jnp.mean(acc, axis=-1, keepdims=True)
    ctr = acc - mu
    var = jnp.mean(ctr * ctr, axis=-1, keepdims=True)
    y = ctr * lax.rsqrt(var + NORM_EPS) * lg_ref[...] + lb_ref[...]
    o_ref[...] = (y * _sigmoid(y)).astype(BF16)


def _conv_mixer(y, conv_w, conv_b, ln_g, ln_b, bsz, seq):
    ts = TOKEN_TILE
    per_halo = ts // CONV_HALO
    y3 = y.reshape(bsz, seq, CONV_CH)
    w_pad = jnp.zeros((CONV_HALO, CONV_CH), F32).at[:CONV_WIDTH].set(conv_w)
    out = pl.pallas_call(
        _conv_kernel,
        grid=(bsz, seq // ts),
        in_specs=[
            pl.BlockSpec((None, CONV_HALO, CONV_CH), lambda b, i: (b, jnp.maximum(i * per_halo - 1, 0), 0)),
            pl.BlockSpec((None, ts, CONV_CH), lambda b, i: (b, i, 0)),
            _const_spec((CONV_HALO, CONV_CH)), _const_spec((1, CONV_CH)), _const_spec((1, CONV_CH)),
            _const_spec((1, CONV_CH)),
        ],
        out_specs=pl.BlockSpec((None, ts, CONV_CH), lambda b, i: (b, i, 0)),
        out_shape=jax.ShapeDtypeStruct((bsz, seq, CONV_CH), BF16),
        scratch_shapes=[pltpu.VMEM((SUBLANES, ts + CONV_HALO, CONV_CH), F32)],
        compiler_params=_params("parallel", "parallel"),
        name="conv_mixer",
    )(y3, y3, w_pad, conv_b.reshape(1, -1), ln_g.reshape(1, -1), ln_b.reshape(1, -1))
    return out.reshape(bsz * seq, CONV_CH)


def _gla_kernel(q_ref, k_ref, v_ref, r_ref, low_ref, wg_ref, bg_ref, og_ref, tri_ref, am_ref, sm_ref, o_ref, st_ref):
    ch, sub = GLA_CHUNK, GLA_SUB
    tc = q_ref.shape[0]
    n_sub = ch // sub

    @pl.when(pl.program_id(1) == 0)
    def _():
        st_ref[...] = jnp.zeros_like(st_ref)

    head = _head_of_lane()
    key_row = _iota((ch, QK_W), 0)
    att_mask = am_ref[...]
    st_mask = sm_ref[...]

    logit = _dot(low_ref[...].astype(BF16), wg_ref[...]) + bg_ref[...]
    log_a = _neg_softplus(-logit) * (1.0 / GLA_TAU)
    la_hi, la_lo = _split_bf16(log_a)
    cr = GLA_CUM_ROWS
    cum_all = jnp.concatenate(
        [_dot(tri_ref[...], jnp.concatenate([la_hi[g * cr:(g + 1) * cr], la_lo[g * cr:(g + 1) * cr]], axis=0))
         for g in range(tc // cr)], axis=0)

    state = st_ref[...]
    for c in range(tc // ch):
        rows = slice(c * ch, (c + 1) * ch)
        cum = cum_all[rows, :]
        total = cum[ch - 1:ch, :]
        q = q_ref[rows, :] * (HEAD_LANES ** -0.5)
        k = k_ref[rows, :]
        v = v_ref[rows, :].astype(BF16)

        bases = [jnp.zeros((1, QK_W), F32)] + [cum[i * sub - 1:i * sub, :] for i in range(1, n_sub)]
        base_rows = jnp.concatenate([jnp.broadcast_to(b, (sub, QK_W)) for b in bases], axis=0)
        q_rel = q * jnp.exp(cum - base_rows)
        q_all = jnp.concatenate([jnp.where(head == h, q_rel, 0.0) for h in range(N_HEADS)], axis=0).astype(BF16)
        k_all = jnp.concatenate([jnp.where(key_row < (i + 1) * sub, k * jnp.exp(bases[i] - cum), 0.0)
                                 for i in range(n_sub)], axis=0).astype(BF16)
        att = (_dot_nt(q_all, k_all) * att_mask).astype(BF16)
        intra = _dot(att, jnp.concatenate([v] * n_sub, axis=0))
        o = jnp.concatenate([intra[h * ch:(h + 1) * ch, h * GLA_DV:(h + 1) * GLA_DV] for h in range(N_HEADS)], axis=1)

        o = o + _dot_nt((q * jnp.exp(cum)).astype(BF16), state.astype(BF16))
        k_dec = (k * jnp.exp(total - cum)).astype(BF16)
        state = state * jnp.exp(total) + _dot_tn(v, k_dec) * st_mask

        r = r_ref[rows, :]
        outs = []
        for h in range(N_HEADS):
            o_h = o[:, h * GLA_DV:(h + 1) * GLA_DV]
            ms = jnp.mean(o_h * o_h, axis=-1, keepdims=True)
            outs.append(o_h * lax.rsqrt(ms + NORM_EPS) * og_ref[...])
        o_ref[rows, :] = (r * _sigmoid(r) * jnp.concatenate(outs, axis=1)).astype(BF16)
    st_ref[...] = state


def _gla_masks(tc):
    ch, sub = GLA_CHUNK, GLA_SUB
    t = np.arange(tc)
    tri = (t[:, None] >= t[None, :]) & (t[:, None] // ch == t[None, :] // ch)
    a_t = np.arange(N_HEADS * ch) % ch
    a_col = np.arange((ch // sub) * ch)
    att = (a_col[None, :] // ch == a_t[:, None] // sub) & (a_col[None, :] % ch <= a_t[:, None])
    state = np.arange(GLA_V_W)[:, None] // GLA_DV == np.arange(QK_W)[None, :] // HEAD_LANES
    return (jnp.asarray(np.concatenate([tri, tri], axis=1), BF16), jnp.asarray(att, F32), jnp.asarray(state, F32))


def _gla_mixer(cbuf, w_gate, b_gate, o_gain, bsz, seq):
    tc = GLA_CHUNK * GLA_CHUNKS_PER_STEP
    assert seq % tc == 0 and tc % GLA_CUM_ROWS == 0 and GLA_CUM_ROWS % GLA_CHUNK == 0
    c3 = cbuf.reshape(bsz, seq, C_W)
    wg = jnp.zeros((GLA_LOW_PAD, QK_W), F32).at[:GLA_RANK].set(w_gate).astype(BF16)
    tri2, att_mask, st_mask = _gla_masks(GLA_CUM_ROWS)
    spec = lambda w, j: pl.BlockSpec((None, tc, w), lambda b, i: (b, i, j))
    out = pl.pallas_call(
        _gla_kernel,
        grid=(bsz, seq // tc),
        in_specs=[
            spec(QK_W, 0), spec(QK_W, 1), spec(GLA_V_W, 1), spec(GLA_V_W, 2),
            spec(GLA_LOW_PAD, (2 * QK_W + 2 * GLA_V_W) // GLA_LOW_PAD),
            _const_spec((GLA_LOW_PAD, QK_W)), _const_spec((1, QK_W)), _const_spec((1, GLA_DV)),
            _const_spec(tri2.shape), _const_spec(att_mask.shape), _const_spec(st_mask.shape),
        ],
        out_specs=pl.BlockSpec((None, tc, GLA_V_W), lambda b, i: (b, i, 0)),
        out_shape=jax.ShapeDtypeStruct((bsz, seq, GLA_V_W), BF16),
        scratch_shapes=[pltpu.VMEM((GLA_V_W, QK_W), F32)],
        compiler_params=_params("parallel", "arbitrary"),
        name="gla_mixer",
    )(c3, c3, c3, c3, c3, wg, b_gate.reshape(1, -1), o_gain.reshape(1, -1), tri2, att_mask, st_mask)
    return out.reshape(bsz * seq, GLA_V_W)


def _sb_kernel(q_ref, k_ref, v_ref, o_ref, kmax_ref):
    t = SB_T
    seq = k_ref.shape[0]
    i = pl.program_id(1)
    head = _head_of_lane()
    head_ones = _ones_where(jnp.right_shift(_iota((QK_W, QK_W), 0), int(np.log2(HEAD_LANES))) == head)

    @pl.when(i == 0)
    def _():
        best = jnp.zeros((SB_NORM_ROWS, QK_W), F32)
        for c in range(seq // SB_NORM_ROWS):
            kf = k_ref[c * SB_NORM_ROWS:(c + 1) * SB_NORM_ROWS, :].astype(F32)
            sq_hi, sq_lo = _split_bf16(kf * kf)
            best = jnp.maximum(best, _dot(sq_hi, head_ones) + _dot(sq_lo, head_ones))
        kmax_ref[...] = jnp.zeros_like(kmax_ref) + jnp.max(best)

    tri2 = _ones_where(jnp.bitwise_and(_iota((2 * t, t), 0), t - 1) >= _iota((2 * t, t), 1))
    keep = _iota((N_HEADS * t, t), 1) < jnp.bitwise_and(_iota((N_HEADS * t, t), 0), t - 1)

    def query_block(u):
        q = q_ref[u * t:(u + 1) * t, :]
        q_stack = jnp.concatenate([jnp.where(head == h, q, jnp.zeros_like(q)) for h in range(N_HEADS)], axis=0)
        qf = q_stack.astype(F32)
        z_bound = jnp.sqrt(jnp.sum(qf * qf, axis=1, keepdims=True) * kmax_ref[0:1, 0:1]) * SB_BOUND_SLACK
        return q_stack, z_bound

    def block(j, acc, run, masked, q_stack, z_bound):
        start = j * t if isinstance(j, int) else pl.multiple_of(j * t, t)
        k_blk = k_ref[pl.ds(start, t), :]
        v_blk = v_ref[pl.ds(start, t), :]
        z = _dot_nt(q_stack, k_blk)
        drop = jnp.maximum(z, 0.0) + jnp.log(1.0 + jnp.exp2(jnp.abs(z) * (-LOG2_E)))
        if masked:
            drop = jnp.where(keep, drop, 0.0)
        d_hi, d_lo = _split_bf16(drop)
        suffix = _dot(jnp.concatenate([d_hi, d_lo], axis=1), tri2)
        w = jnp.exp(z - suffix - run)
        if masked:
            w = jnp.where(keep, w, 0.0)
        w = w.astype(BF16)
        w_wide = jnp.concatenate([w[h * t:(h + 1) * t, :] for h in range(N_HEADS)], axis=1)
        v_stack = jnp.concatenate([jnp.where(head == h, v_blk, jnp.zeros_like(v_blk)) for h in range(N_HEADS)], axis=0)
        acc = acc + _dot(w_wide, v_stack)
        run = run + jnp.sum(drop, axis=1, keepdims=True)
        done = (jnp.max(z_bound - run) < SB_ZERO_EXPONENT).astype(jnp.int32)
        return acc, run, done

    def walk(first_blocks):
        states = []
        for u in range(SB_QBLOCKS):
            iu = i * SB_QBLOCKS + u
            qz = query_block(u)
            acc, run, done = block(iu, jnp.zeros((t, QK_W), F32), jnp.zeros((N_HEADS * t, 1), F32), True, *qz)
            for n in range(1, first_blocks):
                acc, run, done = block(iu - n, acc, run, False, *qz)
            states.append((iu - first_blocks, done, acc, run, qz))
        for u, (j, done, acc, run, qz) in enumerate(states):

            def body(c, qz=qz):
                j, _, acc, run = c
                acc, run, done = block(j, acc, run, False, *qz)
                return j - 1, done, acc, run

            carry = lax.while_loop(lambda c: jnp.logical_and(c[0] >= 0, c[1] == 0), body, (j, done, acc, run))
            o_ref[u * t:(u + 1) * t, :] = carry[2].astype(BF16)

    def first_step():
        for u in range(SB_QBLOCKS):
            qz = query_block(u)
            acc, run, _ = block(u, jnp.zeros((t, QK_W), F32), jnp.zeros((N_HEADS * t, 1), F32), True, *qz)
            for j in range(u - 1, -1, -1):
                acc, run, _ = block(j, acc, run, False, *qz)
            o_ref[u * t:(u + 1) * t, :] = acc.astype(BF16)

    assert SB_QBLOCKS >= SB_STRAIGHT_BLOCKS - 1
    pl.when(i > 0)(functools.partial(walk, SB_STRAIGHT_BLOCKS))
    pl.when(i == 0)(first_step)


def _stick_breaking_mixer(dbuf, bsz, seq):
    d3 = dbuf.reshape(bsz, seq, D_W)
    rows = SB_T * SB_QBLOCKS
    assert seq % rows == 0 and seq % SB_NORM_ROWS == 0
    full = lambda j: pl.BlockSpec((None, seq, QK_W), lambda b, i: (b, 0, j))
    out = pl.pallas_call(
        _sb_kernel,
        grid=(bsz, seq // rows),
        in_specs=[pl.BlockSpec((None, rows, QK_W), lambda b, i: (b, i, 0)), full(1), full(2)],
        out_specs=pl.BlockSpec((None, rows, QK_W), lambda b, i: (b, i, 0)),
        out_shape=jax.ShapeDtypeStruct((bsz, seq, QK_W), BF16),
        scratch_shapes=[pltpu.VMEM((8, 128), F32)],
        compiler_params=_params("parallel", "arbitrary"),
        name="stick_breaking_mixer",
    )(d3, d3, d3)
    return out.reshape(bsz * seq, QK_W)


def _merge_kernel(x_ref, mod_ref, g_ref, oa_ref, ob_ref, oc_ref, od_ref,
                  wg_ref, wa_ref, wb_ref, wc_ref, wd_ref, wo_ref, out_ref):
    x = x_ref[...]
    h = _modulated_norm(x, g_ref[...], mod_ref[1:2, :], mod_ref[0:1, :]).astype(BF16)
    branches = ((oa_ref[...], wa_ref), (ob_ref[...], wb_ref), (oc_ref[...], wc_ref), (od_ref[...], wd_ref))
    merged = jnp.zeros(x.shape, F32)
    for j, (o_j, w_ref) in enumerate(branches):
        gate = _sigmoid(_dot(h, wg_ref[:, j * D_MODEL:(j + 1) * D_MODEL]))
        merged = merged + gate * _dot(o_j, w_ref[...])
    out_ref[...] = x + mod_ref[2:3, :] * _dot(merged.astype(BF16), wo_ref[...])


def _merge(x2d, mod, g_mix, oa, ob, oc, od, wg, wa, wb, wc, wd, wo, seq):
    t = x2d.shape[0]
    tm = TOKEN_TILE
    per_b = seq // tm
    row = lambda w: pl.BlockSpec((tm, w), lambda i: (i, 0))
    return pl.pallas_call(
        _merge_kernel,
        grid=(t // tm,),
        in_specs=[
            row(D_MODEL),
            pl.BlockSpec((None, 6, D_MODEL), lambda i: (i // per_b, 0, 0)),
            _const_spec((1, D_MODEL)),
            row(QK_W), row(CONV_CH), row(GLA_V_W), row(QK_W),
            _const_spec(wg.shape), _const_spec(wa.shape), _const_spec(wb.shape), _const_spec(wc.shape),
            _const_spec(wd.shape), _const_spec(wo.shape),
        ],
        out_specs=row(D_MODEL),
        out_shape=jax.ShapeDtypeStruct((t, D_MODEL), F32),
        compiler_params=_params("parallel"),
        name="gated_merge",
    )(x2d, mod, g_mix, oa, ob, oc, od, wg, wa, wb, wc, wd, wo)


def _mlp_kernel(x_ref, mod_ref, g_ref, wu_ref, wd_ref, out_ref):
    x = x_ref[...]
    h = _modulated_norm(x, g_ref[...], mod_ref[4:5, :], mod_ref[3:4, :]).astype(BF16)
    acc = jnp.zeros(x.shape, F32)
    for j in range(D_FF // D_MODEL):
        cols = slice(j * D_MODEL, (j + 1) * D_MODEL)
        u = jnp.maximum(_dot(h, wu_ref[:, cols]), 0.0)
        acc = acc + _dot((u * u).astype(BF16), wd_ref[cols, :])
    out_ref[...] = x + mod_ref[5:6, :] * acc


def _mlp(x2d, mod, g_mlp, wu, wd, seq):
    t = x2d.shape[0]
    tm = TOKEN_TILE
    per_b = seq // tm
    row = pl.BlockSpec((tm, D_MODEL), lambda i: (i, 0))
    return pl.pallas_call(
        _mlp_kernel,
        grid=(t // tm,),
        in_specs=[
            row,
            pl.BlockSpec((None, 6, D_MODEL), lambda i: (i // per_b, 0, 0)),
            _const_spec((1, D_MODEL)), _const_spec(wu.shape), _const_spec(wd.shape),
        ],
        out_specs=row,
        out_shape=jax.ShapeDtypeStruct((t, D_MODEL), F32),
        compiler_params=_params("parallel"),
        name="relu2_mlp",
    )(x2d, mod, g_mlp, wu, wd)


def _head_block_diag():
    idx = np.arange(QK_W) // HEAD_LANES
    return jnp.asarray((idx[:, None] == idx[None, :]).astype(np.float32) / HEAD_LANES, dtype=BF16)


def kernel(x, c, w_ada, b_ada, g_mix, w_in, q_gain, k_gain, conv_w, conv_b, conv_ln_g, conv_ln_b,
           gla_w_gate, gla_b_gate, gla_o_gain, w_br_a, w_br_b, w_br_c, w_br_d, w_out, g_mlp, w_up, w_down):
    bsz, seq, d = x.shape
    depth = w_ada.shape[0]
    assert d == D_MODEL and seq % (DSWA_BLOCK * DSWA_PATTERNS[-1][1]) == 0 and seq % TOKEN_TILE == 0
    mods = _modulation(c, w_ada, b_ada)
    bd = _head_block_diag()
    x2d = x.reshape(bsz * seq, d)
    qk_scale = HEAD_LANES ** -0.5
    for l in range(depth):
        w = w_in[l]
        o_conv = A_W
        o_c = o_conv + 2 * CONV_CH
        o_low = o_c + 2 * QK_W + 2 * GLA_V_W
        o_d = o_low + GLA_RANK
        o_gate = o_d + D_W
        assert w.shape == (d, o_gate + 4 * d) and w_up.shape[-1] == D_FF
        wa = w[:, :A_W].astype(BF16)
        wb = w[:, o_conv:o_c].astype(BF16)
        wc = jnp.concatenate([w[:, o_c:o_low + GLA_RANK], jnp.zeros((d, GLA_LOW_PAD - GLA_RANK), F32)], axis=1).astype(BF16)
        wd = w[:, o_d:o_gate].astype(BF16)
        wg = w[:, o_gate:].astype(BF16)
        qg = jnp.tile(q_gain[l], N_HEADS).reshape(1, QK_W) * qk_scale
        kg = jnp.tile(k_gain[l], N_HEADS).reshape(1, QK_W)
        mod = mods[l]
        gm = g_mix[l].reshape(1, d)

        a, y, cbuf, dbuf = _in_projection(x2d, mod, gm, wa, wb, wc, wd, qg, kg, bd, seq)
        oa = _dilated_attention(a, bsz, seq)
        ob = _conv_mixer(y, conv_w[l], conv_b[l], conv_ln_g[l], conv_ln_b[l], bsz, seq)
        oc = _gla_mixer(cbuf, gla_w_gate[l], gla_b_gate[l], gla_o_gain[l], bsz, seq)
        od = _stick_breaking_mixer(dbuf, bsz, seq)
        x2d = _merge(x2d, mod, gm, oa, ob, oc, od, wg, w_br_a[l].astype(BF16), w_br_b[l].astype(BF16),
                     w_br_c[l].astype(BF16), w_br_d[l].astype(BF16), w_out[l].astype(BF16), seq)
        x2d = _mlp(x2d, mod, g_mlp[l].reshape(1, d), w_up[l].astype(BF16), w_down[l].astype(BF16), seq)
    return x2d.reshape(bsz, seq, d)
```

```python
import functools

import numpy as np
import jax
import jax.numpy as jnp
from jax import lax
from jax.experimental import pallas as pl
from jax.experimental.pallas import tpu as pltpu

F32 = jnp.float32
BF16 = jnp.bfloat16

D_MODEL = 1024
D_FF = 4 * D_MODEL
NORM_EPS = 1e-6
LOG2_E = 1.4426950408889634
HEAD_LANES = 64
N_HEADS = 4
QK_W = N_HEADS * HEAD_LANES
DSWA_PATTERNS = ((128, 1), (512, 4), (2048, 16))
DSWA_BLOCK = 128
DSWA_SPAN = DSWA_BLOCK * DSWA_PATTERNS[-1][1]
DSWA_GROUP = 16
DSWA_PADDED_GROUP = 24
LANE_TILE = 128
SUBLANES = 8
CONV_CH = 256
CONV_WIDTH = 31
CONV_HALO = 32
GLA_DV = 128
GLA_V_W = N_HEADS * GLA_DV
GLA_RANK = 16
GLA_LOW_PAD = 128
GLA_TAU = 16.0
GLA_CHUNK = 64
GLA_SUB = 16
GLA_CHUNKS_PER_STEP = 16
GLA_CUM_ROWS = 256
SB_T = 256
SB_QBLOCKS = 4
SB_STRAIGHT_BLOCKS = 4
SB_NORM_ROWS = 512
SB_BOUND_SLACK = 1.02
SB_ZERO_EXPONENT = -110.0
TOKEN_TILE = 512
MASK_VALUE = -1e30
VMEM_LIMIT = 56 * 1024 * 1024

A_W = 3 * 3 * QK_W
C_W = 2 * QK_W + 2 * GLA_V_W + GLA_LOW_PAD
D_W = 3 * QK_W


def _dot(a, b):
    return jnp.dot(a, b, preferred_element_type=F32)


def _dot_nt(a, b):
    return lax.dot_general(a, b, (((1,), (1,)), ((), ())), preferred_element_type=F32)


def _dot_tn(a, b):
    return lax.dot_general(a, b, (((0,), (0,)), ((), ())), preferred_element_type=F32)


def _split_bf16(x):
    hi = x.astype(BF16)
    lo = (x - hi.astype(F32)).astype(BF16)
    return hi, lo


def _sigmoid(x):
    return 1.0 / (1.0 + jnp.exp(-x))


def _neg_softplus(x):
    return -(jnp.maximum(x, 0.0) + jnp.log(1.0 + jnp.exp(-jnp.abs(x))))


def _iota(shape, dim):
    return lax.broadcasted_iota(jnp.int32, shape, dim)


def _head_of_lane(width=QK_W, lanes_per_head=HEAD_LANES):
    return jnp.right_shift(_iota((1, width), 1), int(np.log2(lanes_per_head)))


def _ones_where(cond):
    return jnp.where(cond, 1.0, 0.0).astype(BF16)


def _const_spec(shape):
    nd = len(shape)
    return pl.BlockSpec(shape, lambda *_: (0,) * nd, pipeline_mode=pl.Buffered(1))


def _params(*sem):
    return pltpu.CompilerParams(dimension_semantics=sem, vmem_limit_bytes=VMEM_LIMIT)


def _modulated_norm(x, gain, scale, shift):
    ms = jnp.mean(x * x, axis=-1, keepdims=True)
    y = x * lax.rsqrt(ms + NORM_EPS) * gain
    return y * (1.0 + scale) + shift


def _mod_kernel(c_ref, w_ref, b_ref, o_ref):
    c_hi, c_lo = _split_bf16(c_ref[...])
    w_hi, w_lo = _split_bf16(w_ref[...])
    o_ref[...] = _dot(c_hi, w_hi) + _dot(c_hi, w_lo) + _dot(c_lo, w_hi) + b_ref[...]


def _modulation(c, w_ada, b_ada):
    depth, d, six_d = w_ada.shape
    bsz = c.shape[0]
    out = pl.pallas_call(
        _mod_kernel,
        grid=(depth, six_d // d),
        in_specs=[
            pl.BlockSpec((bsz, d), lambda l, j: (0, 0)),
            pl.BlockSpec((None, d, d), lambda l, j: (l, 0, j)),
            pl.BlockSpec((None, 1, d), lambda l, j: (l, 0, j)),
        ],
        out_specs=pl.BlockSpec((None, bsz, d), lambda l, j: (l, 0, j)),
        out_shape=jax.ShapeDtypeStruct((depth, bsz, six_d), F32),
        compiler_params=_params("parallel", "parallel"),
        name="adaln_modulation",
    )(c, w_ada, b_ada.reshape(depth, 1, six_d))
    return out.reshape(depth, bsz, six_d // d, d)


def _inproj_kernel(x_ref, mod_ref, g_ref, wa_ref, wb_ref, wc_ref, wd_ref, qg_ref, kg_ref, bd_ref,
                   a_ref, y_ref, c_ref, d_ref):
    h = _modulated_norm(x_ref[...], g_ref[...], mod_ref[1:2, :], mod_ref[0:1, :]).astype(BF16)
    bd = bd_ref[...]
    a_all = _dot(h, wa_ref[...])
    for j in range(A_W // QK_W):
        cols = slice(j * QK_W, (j + 1) * QK_W)
        acc = a_all[:, cols]
        if j < 6:
            ms = _dot((acc * acc).astype(BF16), bd)
            gain = qg_ref[...] if j < 3 else kg_ref[...]
            acc = acc * lax.rsqrt(ms + NORM_EPS) * gain
        a_ref[:, cols] = acc.astype(BF16)
    u = _dot(h, wb_ref[...])
    y_ref[...] = u[:, :CONV_CH] * _sigmoid(u[:, CONV_CH:])
    c_ref[...] = _dot(h, wc_ref[...])
    d_all = _dot(h, wd_ref[...])
    d_ref[:, :QK_W] = (d_all[:, :QK_W] * (HEAD_LANES ** -0.5)).astype(BF16)
    d_ref[:, QK_W:] = d_all[:, QK_W:].astype(BF16)


def _in_projection(x2d, mod, g_mix, wa, wb, wc, wd, qg, kg, bd, seq):
    t = x2d.shape[0]
    tm = TOKEN_TILE
    per_b = seq // tm
    row = lambda w: pl.BlockSpec((tm, w), lambda i: (i, 0))
    return pl.pallas_call(
        _inproj_kernel,
        grid=(t // tm,),
        in_specs=[
            row(D_MODEL),
            pl.BlockSpec((None, 6, D_MODEL), lambda i: (i // per_b, 0, 0)),
            _const_spec((1, D_MODEL)),
            _const_spec(wa.shape), _const_spec(wb.shape), _const_spec(wc.shape), _const_spec(wd.shape),
            _const_spec((1, QK_W)), _const_spec((1, QK_W)), _const_spec((QK_W, QK_W)),
        ],
        out_specs=[row(A_W), row(CONV_CH), row(C_W), row(D_W)],
        out_shape=[
            jax.ShapeDtypeStruct((t, A_W), BF16),
            jax.ShapeDtypeStruct((t, CONV_CH), F32),
            jax.ShapeDtypeStruct((t, C_W), F32),
            jax.ShapeDtypeStruct((t, D_W), BF16),
        ],
        compiler_params=_params("parallel"),
        name="in_projection",
    )(x2d, mod, g_mix, wa, wb, wc, wd, qg, kg, bd)


def _dswa_group(position, span_idx, q_ref, kp_ref, kc_ref, vp_ref, vc_ref, o_ref, q_st, k_st, v_st, m_st, den_st, acc_st,
                *, slopes, window, dilation):
    blk, span = DSWA_BLOCK, DSWA_SPAN
    halo = blk * dilation
    padded = dilation % DSWA_GROUP == 0
    pitch = DSWA_PADDED_GROUP if padded else DSWA_GROUP
    stride = dilation // DSWA_GROUP * pitch if padded else dilation

    def stage(dst, s, first_group, src, lo, hi, lanes):
        if not padded:
            dst[s, first_group * pitch:first_group * pitch + hi - lo, :] = src[lo:hi, lanes].astype(F32)
            return
        for g in range((hi - lo) // DSWA_GROUP):
            row = (first_group + g) * pitch
            dst[s, row:row + DSWA_GROUP, :] = src[lo + g * DSWA_GROUP:lo + (g + 1) * DSWA_GROUP, lanes].astype(F32)

    for s in range(QK_W // LANE_TILE):
        lanes = slice(s * LANE_TILE, (s + 1) * LANE_TILE)
        stage(q_st, s, 0, q_ref, 0, span, lanes)
        stage(k_st, s, 0, kp_ref, span - halo, span, lanes)
        stage(k_st, s, halo // DSWA_GROUP, kc_ref, 0, span, lanes)
        stage(v_st, s, 0, vp_ref, span - halo, span, lanes)
        stage(v_st, s, halo // DSWA_GROUP, vc_ref, 0, span, lanes)
    head = _head_of_lane()
    qi = _iota((blk, 2 * blk), 0)
    kj = _iota((blk, 2 * blk), 1)
    rel = qi - kj + blk
    in_window = (rel >= 0) & (rel <= window // dilation)
    dist = (dilation * rel).astype(F32)
    bias = jnp.concatenate([jnp.where(in_window, -slopes[h] * dist, MASK_VALUE) for h in range(N_HEADS)], axis=0)
    first_block_penalty = jnp.where(_iota((1, 2 * blk), 1) < blk, MASK_VALUE, 0.0)

    def rows_of(ref, start, n):
        return jnp.concatenate([ref[s, pl.ds(start, n, stride=dilation), :] for s in range(QK_W // LANE_TILE)], axis=1)

    def staged_rows(ref, start, n):
        row = start // DSWA_GROUP * pitch + start % DSWA_GROUP
        return jnp.concatenate([ref[s, pl.ds(row, n, stride=stride), :] for s in range(QK_W // LANE_TILE)], axis=1)

    def put_rows(ref, start, val):
        for s in range(QK_W // LANE_TILE):
            ref[s, pl.ds(start, blk, stride=dilation), :] = val[:, s * LANE_TILE:(s + 1) * LANE_TILE]

    def step(it):
        n = it // dilation
        start = n * halo + it % dilation
        q = staged_rows(q_st, start, blk).astype(BF16)
        k = staged_rows(k_st, start, 2 * blk).astype(BF16)
        v = staged_rows(v_st, start, 2 * blk).astype(BF16)
        q_stack = jnp.concatenate([jnp.where(head == h, q, jnp.zeros_like(q)) for h in range(N_HEADS)], axis=0)
        s = _dot_nt(q_stack, k) + bias
        if n == 0:
            s = s + jnp.where(span_idx == 0, first_block_penalty, 0.0)
        mx = jnp.max(s, axis=1, keepdims=True)
        p = jnp.exp(s - mx)
        den = jnp.sum(p, axis=1, keepdims=True)
        o_all = _dot(p.astype(BF16), v) * (1.0 / den)
        lse_all = mx + jnp.log(den)
        o_new = jnp.zeros((blk, QK_W), F32)
        lse = jnp.zeros((blk, QK_W), F32)
        for h in range(N_HEADS):
            in_head = head == h
            o_new = jnp.where(in_head, o_all[h * blk:(h + 1) * blk], o_new)
            lse = jnp.where(in_head, lse_all[h * blk:(h + 1) * blk], lse)
        if position == 0:
            m_new, acc_new = lse, o_new
        else:
            m_old = rows_of(m_st, start, blk)
            m_new = jnp.maximum(m_old, lse)
            w_old, w_new = jnp.exp(m_old - m_new), jnp.exp(lse - m_new)
            den_old = w_old if position == 1 else rows_of(den_st, start, blk) * w_old
            put_rows(den_st, start, den_old + w_new)
            acc_new = rows_of(acc_st, start, blk) * w_old + o_new * w_new
        if position < len(DSWA_PATTERNS) - 1:
            put_rows(m_st, start, m_new)
        put_rows(acc_st, start, acc_new)

    for it in range(span // blk):
        step(it)
    if position == len(DSWA_PATTERNS) - 1:
        for s in range(QK_W // LANE_TILE):
            o_ref[:, s * LANE_TILE:(s + 1) * LANE_TILE] = (acc_st[s, :, :] / den_st[s, :, :]).astype(BF16)


def _dswa_kernel(q_ref, kp_ref, kc_ref, vp_ref, vc_ref, o_ref, *scratch, slopes):
    span_idx, position = pl.program_id(1), pl.program_id(2)
    for pos in range(len(DSWA_PATTERNS)):
        g = _dswa_group_at(pos)
        window, dilation = DSWA_PATTERNS[g]
        pl.when(position == pos)(functools.partial(
            _dswa_group, pos, span_idx, q_ref, kp_ref, kc_ref, vp_ref, vc_ref, o_ref, *scratch,
            slopes=slopes[g * N_HEADS:(g + 1) * N_HEADS], window=window, dilation=dilation))


def _dswa_group_at(position):
    return len(DSWA_PATTERNS) - 1 - position


def _alibi_slopes(n):
    return [float(np.float32(2.0 ** (-8.0 * i / n))) for i in range(1, n + 1)]


def _dilated_attention(a, bsz, seq):
    span = DSWA_SPAN
    a3 = a.reshape(bsz, seq, A_W)
    n_groups = len(DSWA_PATTERNS)
    col = lambda part, pos: part * n_groups + _dswa_group_at(pos)
    cur = lambda part: pl.BlockSpec((None, span, QK_W), lambda b, i, p: (b, i, col(part, p)))
    prev = lambda part: pl.BlockSpec((None, span, QK_W), lambda b, i, p: (b, jnp.maximum(i - 1, 0), col(part, p)))
    slab = lambda rows: pltpu.VMEM((QK_W // LANE_TILE, rows, LANE_TILE), F32)
    padded = lambda rows: rows // DSWA_GROUP * DSWA_PADDED_GROUP
    out = pl.pallas_call(
        functools.partial(_dswa_kernel, slopes=_alibi_slopes(n_groups * N_HEADS)),
        grid=(bsz, seq // span, n_groups),
        in_specs=[cur(0), prev(1), cur(1), prev(2), cur(2)],
        out_specs=pl.BlockSpec((None, span, QK_W), lambda b, i, g: (b, i, 0)),
        out_shape=jax.ShapeDtypeStruct((bsz, seq, QK_W), BF16),
        scratch_shapes=[slab(padded(span)), slab(padded(2 * span)), slab(padded(2 * span)),
                        slab(span), slab(span), slab(span)],
        compiler_params=_params("parallel", "arbitrary", "arbitrary"),
        name="dilated_attention",
    )(a3, a3, a3, a3, a3)
    return out.reshape(bsz * seq, QK_W)


def _conv_kernel(prev_ref, cur_ref, w_ref, b_ref, lg_ref, lb_ref, o_ref, buf_ref):
    ts = cur_ref.shape[0]
    first = pl.program_id(1) == 0
    buf_ref[0, 0:CONV_HALO, :] = jnp.where(first, 0.0, prev_ref[...])
    buf_ref[0, CONV_HALO:, :] = cur_ref[...]
    rows = ts + CONV_HALO - SUBLANES
    for r in range(1, SUBLANES):
        buf_ref[r, 0:rows, :] = buf_ref[0, r:r + rows, :]
    off = CONV_HALO - (CONV_WIDTH - 1)
    acc = jnp.zeros((ts, CONV_CH), F32) + b_ref[...]
    for w in range(CONV_WIDTH):
        r, base = (off + w) % SUBLANES, (off + w) // SUBLANES * SUBLANES
        acc = acc + buf_ref[r, base:base + ts, :] * w_ref[w:w + 1, :]
    mu = jnp.mean(acc, axis=-1, keepdims=True)
    ctr = acc - mu
    var = jnp.mean(ctr * ctr, axis=-1, keepdims=True)
    y = ctr * lax.rsqrt(var + NORM_EPS) * lg_ref[...] + lb_ref[...]
    o_ref[...] = (y * _sigmoid(y)).astype(BF16)


def _conv_mixer(y, conv_w, conv_b, ln_g, ln_b, bsz, seq):
    ts = TOKEN_TILE
    per_halo = ts // CONV_HALO
    y3 = y.reshape(bsz, seq, CONV_CH)
    w_pad = jnp.zeros((CONV_HALO, CONV_CH), F32).at[:CONV_WIDTH].set(conv_w)
    out = pl.pallas_call(
        _conv_kernel,
        grid=(bsz, seq // ts),
        in_specs=[
            pl.BlockSpec((None, CONV_HALO, CONV_CH), lambda b, i: (b, jnp.maximum(i * per_halo - 1, 0), 0)),
            pl.BlockSpec((None, ts, CONV_CH), lambda b, i: (b, i, 0)),
            _const_spec((CONV_HALO, CONV_CH)), _const_spec((1, CONV_CH)), _const_spec((1, CONV_CH)),
            _const_spec((1, CONV_CH)),
        ],
        out_specs=pl.BlockSpec((None, ts, CONV_CH), lambda b, i: (b, i, 0)),
        out_shape=jax.ShapeDtypeStruct((bsz, seq, CONV_CH), BF16),
        scratch_shapes=[pltpu.VMEM((SUBLANES, ts + CONV_HALO, CONV_CH), F32)],
        compiler_params=_params("parallel", "parallel"),
        name="conv_mixer",
    )(y3, y3, w_pad, conv_b.reshape(1, -1), ln_g.reshape(1, -1), ln_b.reshape(1, -1))
    return out.reshape(bsz * seq, CONV_CH)


def _gla_kernel(q_ref, k_ref, v_ref, r_ref, low_ref, wg_ref, bg_ref, og_ref, tri_ref, am_ref, sm_ref, o_ref, st_ref):
    ch, sub = GLA_CHUNK, GLA_SUB
    tc = q_ref.shape[0]
    n_sub = ch // sub

    @pl.when(pl.program_id(1) == 0)
    def _():
        st_ref[...] = jnp.zeros_like(st_ref)

    head = _head_of_lane()
    key_row = _iota((ch, QK_W), 0)
    att_mask = am_ref[...]
    st_mask = sm_ref[...]

    logit = _dot(low_ref[...].astype(BF16), wg_ref[...]) + bg_ref[...]
    log_a = _neg_softplus(-logit) * (1.0 / GLA_TAU)
    la_hi, la_lo = _split_bf16(log_a)
    cr = GLA_CUM_ROWS
    cum_all = jnp.concatenate(
        [_dot(tri_ref[...], jnp.concatenate([la_hi[g * cr:(g + 1) * cr], la_lo[g * cr:(g + 1) * cr]], axis=0))
         for g in range(tc // cr)], axis=0)

    state = st_ref[...]
    for c in range(tc // ch):
        rows = slice(c * ch, (c + 1) * ch)
        cum = cum_all[rows, :]
        total = cum[ch - 1:ch, :]
        q = q_ref[rows, :] * (HEAD_LANES ** -0.5)
        k = k_ref[rows, :]
        v = v_ref[rows, :].astype(BF16)

        bases = [jnp.zeros((1, QK_W), F32)] + [cum[i * sub - 1:i * sub, :] for i in range(1, n_sub)]
        base_rows = jnp.concatenate([jnp.broadcast_to(b, (sub, QK_W)) for b in bases], axis=0)
        q_rel = q * jnp.exp(cum - base_rows)
        q_all = jnp.concatenate([jnp.where(head == h, q_rel, 0.0) for h in range(N_HEADS)], axis=0).astype(BF16)
        k_all = jnp.concatenate([jnp.where(key_row < (i + 1) * sub, k * jnp.exp(bases[i] - cum), 0.0)
                                 for i in range(n_sub)], axis=0).astype(BF16)
        att = (_dot_nt(q_all, k_all) * att_mask).astype(BF16)
        intra = _dot(att, jnp.concatenate([v] * n_sub, axis=0))
        o = jnp.concatenate([intra[h * ch:(h + 1) * ch, h * GLA_DV:(h + 1) * GLA_DV] for h in range(N_HEADS)], axis=1)

        o = o + _dot_nt((q * jnp.exp(cum)).astype(BF16), state.astype(BF16))
        k_dec = (k * jnp.exp(total - cum)).astype(BF16)
        state = state * jnp.exp(total) + _dot_tn(v, k_dec) * st_mask

        r = r_ref[rows, :]
        outs = []
        for h in range(N_HEADS):
            o_h = o[:, h * GLA_DV:(h + 1) * GLA_DV]
            ms = jnp.mean(o_h * o_h, axis=-1, keepdims=True)
            outs.append(o_h * lax.rsqrt(ms + NORM_EPS) * og_ref[...])
        o_ref[rows, :] = (r * _sigmoid(r) * jnp.concatenate(outs, axis=1)).astype(BF16)
    st_ref[...] = state


def _gla_masks(tc):
    ch, sub = GLA_CHUNK, GLA_SUB
    t = np.arange(tc)
    tri = (t[:, None] >= t[None, :]) & (t[:, None] // ch == t[None, :] // ch)
    a_t = np.arange(N_HEADS * ch) % ch
    a_col = np.arange((ch // sub) * ch)
    att = (a_col[None, :] // ch == a_t[:, None] // sub) & (a_col[None, :] % ch <= a_t[:, None])
    state = np.arange(GLA_V_W)[:, None] // GLA_DV == np.arange(QK_W)[None, :] // HEAD_LANES
    return (jnp.asarray(np.concatenate([tri, tri], axis=1), BF16), jnp.asarray(att, F32), jnp.asarray(state, F32))


def _gla_mixer(cbuf, w_gate, b_gate, o_gain, bsz, seq):
    tc = GLA_CHUNK * GLA_CHUNKS_PER_STEP
    assert seq % tc == 0 and tc % GLA_CUM_ROWS == 0 and GLA_CUM_ROWS % GLA_CHUNK == 0
    c3 = cbuf.reshape(bsz, seq, C_W)
    wg = jnp.zeros((GLA_LOW_PAD, QK_W), F32).at[:GLA_RANK].set(w_gate).astype(BF16)
    tri2, att_mask, st_mask = _gla_masks(GLA_CUM_ROWS)
    spec = lambda w, j: pl.BlockSpec((None, tc, w), lambda b, i: (b, i, j))
    out = pl.pallas_call(
        _gla_kernel,
        grid=(bsz, seq // tc),
        in_specs=[
            spec(QK_W, 0), spec(QK_W, 1), spec(GLA_V_W, 1), spec(GLA_V_W, 2),
            spec(GLA_LOW_PAD, (2 * QK_W + 2 * GLA_V_W) // GLA_LOW_PAD),
            _const_spec((GLA_LOW_PAD, QK_W)), _const_spec((1, QK_W)), _const_spec((1, GLA_DV)),
            _const_spec(tri2.shape), _const_spec(att_mask.shape), _const_spec(st_mask.shape),
        ],
        out_specs=pl.BlockSpec((None, tc, GLA_V_W), lambda b, i: (b, i, 0)),
        out_shape=jax.ShapeDtypeStruct((bsz, seq, GLA_V_W), BF16),
        scratch_shapes=[pltpu.VMEM((GLA_V_W, QK_W), F32)],
        compiler_params=_params("parallel", "arbitrary"),
        name="gla_mixer",
    )(c3, c3, c3, c3, c3, wg, b_gate.reshape(1, -1), o_gain.reshape(1, -1), tri2, att_mask, st_mask)
    return out.reshape(bsz * seq, GLA_V_W)


def _sb_kernel(q_ref, k_ref, v_ref, o_ref, kmax_ref):
    t = SB_T
    seq = k_ref.shape[0]
    i = pl.program_id(1)
    head = _head_of_lane()
    head_ones = _ones_where(jnp.right_shift(_iota((QK_W, QK_W), 0), int(np.log2(HEAD_LANES))) == head)

    @pl.when(i == 0)
    def _():
        best = jnp.zeros((SB_NORM_ROWS, QK_W), F32)
        for c in range(seq // SB_NORM_ROWS):
            kf = k_ref[c * SB_NORM_ROWS:(c + 1) * SB_NORM_ROWS, :].astype(F32)
            sq_hi, sq_lo = _split_bf16(kf * kf)
            best = jnp.maximum(best, _dot(sq_hi, head_ones) + _dot(sq_lo, head_ones))
        kmax_ref[...] = jnp.zeros_like(kmax_ref) + jnp.max(best)

    tri2 = _ones_where(jnp.bitwise_and(_iota((2 * t, t), 0), t - 1) >= _iota((2 * t, t), 1))
    keep = _iota((N_HEADS * t, t), 1) < jnp.bitwise_and(_iota((N_HEADS * t, t), 0), t - 1)

    def query_block(u):
        q = q_ref[u * t:(u + 1) * t, :]
        q_stack = jnp.concatenate([jnp.where(head == h, q, jnp.zeros_like(q)) for h in range(N_HEADS)], axis=0)
        qf = q_stack.astype(F32)
        z_bound = jnp.sqrt(jnp.sum(qf * qf, axis=1, keepdims=True) * kmax_ref[0:1, 0:1]) * SB_BOUND_SLACK
        return q_stack, z_bound

    def block(j, acc, run, masked, q_stack, z_bound):
        start = j * t if isinstance(j, int) else pl.multiple_of(j * t, t)
        k_blk = k_ref[pl.ds(start, t), :]
        v_blk = v_ref[pl.ds(start, t), :]
        z = _dot_nt(q_stack, k_blk)
        drop = jnp.maximum(z, 0.0) + jnp.log(1.0 + jnp.exp2(jnp.abs(z) * (-LOG2_E)))
        if masked:
            drop = jnp.where(keep, drop, 0.0)
        d_hi, d_lo = _split_bf16(drop)
        suffix = _dot(jnp.concatenate([d_hi, d_lo], axis=1), tri2)
        w = jnp.exp(z - suffix - run)
        if masked:
            w = jnp.where(keep, w, 0.0)
        w = w.astype(BF16)
        w_wide = jnp.concatenate([w[h * t:(h + 1) * t, :] for h in range(N_HEADS)], axis=1)
        v_stack = jnp.concatenate([jnp.where(head == h, v_blk, jnp.zeros_like(v_blk)) for h in range(N_HEADS)], axis=0)
        acc = acc + _dot(w_wide, v_stack)
        run = run + jnp.sum(drop, axis=1, keepdims=True)
        done = (jnp.max(z_bound - run) < SB_ZERO_EXPONENT).astype(jnp.int32)
        return acc, run, done

    def walk(first_blocks):
        states = []
        for u in range(SB_QBLOCKS):
            iu = i * SB_QBLOCKS + u
            qz = query_block(u)
            acc, run, done = block(iu, jnp.zeros((t, QK_W), F32), jnp.zeros((N_HEADS * t, 1), F32), True, *qz)
            for n in range(1, first_blocks):
                acc, run, done = block(iu - n, acc, run, False, *qz)
            states.append((iu - first_blocks, done, acc, run, qz))
        for u, (j, done, acc, run, qz) in enumerate(states):

            def body(c, qz=qz):
                j, _, acc, run = c
                acc, run, done = block(j, acc, run, False, *qz)
                return j - 1, done, acc, run

            carry = lax.while_loop(lambda c: jnp.logical_and(c[0] >= 0, c[1] == 0), body, (j, done, acc, run))
            o_ref[u * t:(u + 1) * t, :] = carry[2].astype(BF16)

    def first_step():
        for u in range(SB_QBLOCKS):
            qz = query_block(u)
            acc, run, _ = block(u, jnp.zeros((t, QK_W), F32), jnp.zeros((N_HEADS * t, 1), F32), True, *qz)
            for j in range(u - 1, -1, -1):
                acc, run, _ = block(j, acc, run, False, *qz)
            o_ref[u * t:(u + 1) * t, :] = acc.astype(BF16)

    assert SB_QBLOCKS >= SB_STRAIGHT_BLOCKS - 1
    pl.when(i > 0)(functools.partial(walk, SB_STRAIGHT_BLOCKS))
    pl.when(i == 0)(first_step)


def _stick_breaking_mixer(dbuf, bsz, seq):
    d3 = dbuf.reshape(bsz, seq, D_W)
    rows = SB_T * SB_QBLOCKS
    assert seq % rows == 0 and seq % SB_NORM_ROWS == 0
    full = lambda j: pl.BlockSpec((None, seq, QK_W), lambda b, i: (b, 0, j))
    out = pl.pallas_call(
        _sb_kernel,
        grid=(bsz, seq // rows),
        in_specs=[pl.BlockSpec((None, rows, QK_W), lambda b, i: (b, i, 0)), full(1), full(2)],
        out_specs=pl.BlockSpec((None, rows, QK_W), lambda b, i: (b, i, 0)),
        out_shape=jax.ShapeDtypeStruct((bsz, seq, QK_W), BF16),
        scratch_shapes=[pltpu.VMEM((8, 128), F32)],
        compiler_params=_params("parallel", "arbitrary"),
        name="stick_breaking_mixer",
    )(d3, d3, d3)
    return out.reshape(bsz * seq, QK_W)


def _merge_kernel(x_ref, mod_ref, g_ref, oa_ref, ob_ref, oc_ref, od_ref,
                  wg_ref, wa_ref, wb_ref, wc_ref, wd_ref, wo_ref, out_ref):
    x = x_ref[...]
    h = _modulated_norm(x, g_ref[...], mod_ref[1:2, :], mod_ref[0:1, :]).astype(BF16)
    branches = ((oa_ref[...], wa_ref), (ob_ref[...], wb_ref), (oc_ref[...], wc_ref), (od_ref[...], wd_ref))
    merged = jnp.zeros(x.shape, F32)
    for j, (o_j, w_ref) in enumerate(branches):
        gate = _sigmoid(_dot(h, wg_ref[:, j * D_MODEL:(j + 1) * D_MODEL]))
        merged = merged + gate * _dot(o_j, w_ref[...])
    out_ref[...] = x + mod_ref[2:3, :] * _dot(merged.astype(BF16), wo_ref[...])


def _merge(x2d, mod, g_mix, oa, ob, oc, od, wg, wa, wb, wc, wd, wo, seq):
    t = x2d.shape[0]
    tm = TOKEN_TILE
    per_b = seq // tm
    row = lambda w: pl.BlockSpec((tm, w), lambda i: (i, 0))
    return pl.pallas_call(
        _merge_kernel,
        grid=(t // tm,),
        in_specs=[
            row(D_MODEL),
            pl.BlockSpec((None, 6, D_MODEL), lambda i: (i // per_b, 0, 0)),
            _const_spec((1, D_MODEL)),
            row(QK_W), row(CONV_CH), row(GLA_V_W), row(QK_W),
            _const_spec(wg.shape), _const_spec(wa.shape), _const_spec(wb.shape), _const_spec(wc.shape),
            _const_spec(wd.shape), _const_spec(wo.shape),
        ],
        out_specs=row(D_MODEL),
        out_shape=jax.ShapeDtypeStruct((t, D_MODEL), F32),
        compiler_params=_params("parallel"),
        name="gated_merge",
    )(x2d, mod, g_mix, oa, ob, oc, od, wg, wa, wb, wc, wd, wo)


def _mlp_kernel(x_ref, mod_ref, g_ref, wu_ref, wd_ref, out_ref):
    x = x_ref[...]
    h = _modulated_norm(x, g_ref[...], mod_ref[4:5, :], mod_ref[3:4, :]).astype(BF16)
    acc = jnp.zeros(x.shape, F32)
    for j in range(D_FF // D_MODEL):
        cols = slice(j * D_MODEL, (j + 1) * D_MODEL)
        u = jnp.maximum(_dot(h, wu_ref[:, cols]), 0.0)
        acc = acc + _dot((u * u).astype(BF16), wd_ref[cols, :])
    out_ref[...] = x + mod_ref[5:6, :] * acc


def _mlp(x2d, mod, g_mlp, wu, wd, seq):
    t = x2d.shape[0]
    tm = TOKEN_TILE
    per_b = seq // tm
    row = pl.BlockSpec((tm, D_MODEL), lambda i: (i, 0))
    return pl.pallas_call(
        _mlp_kernel,
        grid=(t // tm,),
        in_specs=[
            row,
            pl.BlockSpec((None, 6, D_MODEL), lambda i: (i // per_b, 0, 0)),
            _const_spec((1, D_MODEL)), _const_spec(wu.shape), _const_spec(wd.shape),
        ],
        out_specs=row,
        out_shape=jax.ShapeDtypeStruct((t, D_MODEL), F32),
        compiler_params=_params("parallel"),
        name="relu2_mlp",
    )(x2d, mod, g_mlp, wu, wd)


def _merge_mlp_kernel(x_ref, mod_ref, g_ref, oa_ref, ob_ref, oc_ref, od_ref, wg_ref, wa_ref, wb_ref, wc_ref, wd_ref,
                      wo_ref, g2_ref, wu_ref, wdn_ref, out_ref):
    x = x_ref[...]
    h = _modulated_norm(x, g_ref[...], mod_ref[1:2, :], mod_ref[0:1, :]).astype(BF16)
    branches = ((oa_ref[...], wa_ref), (ob_ref[...], wb_ref), (oc_ref[...], wc_ref), (od_ref[...], wd_ref))
    merged = jnp.zeros(x.shape, F32)
    for j, (o_j, w_ref) in enumerate(branches):
        gate = _sigmoid(_dot(h, wg_ref[:, j * D_MODEL:(j + 1) * D_MODEL]))
        merged = merged + gate * _dot(o_j, w_ref[...])
    x = x + mod_ref[2:3, :] * _dot(merged.astype(BF16), wo_ref[...])
    h = _modulated_norm(x, g2_ref[...], mod_ref[4:5, :], mod_ref[3:4, :]).astype(BF16)
    acc = jnp.zeros(x.shape, F32)
    for j in range(D_FF // D_MODEL):
        cols = slice(j * D_MODEL, (j + 1) * D_MODEL)
        u = jnp.maximum(_dot(h, wu_ref[:, cols]), 0.0)
        acc = acc + _dot((u * u).astype(BF16), wdn_ref[cols, :])
    out_ref[...] = x + mod_ref[5:6, :] * acc


def _merge_mlp(x2d, mod, g_mix, oa, ob, oc, od, wg, wa, wb, wc, wd, wo, g_mlp, wu, wdn, seq):
    t = x2d.shape[0]
    tm = TOKEN_TILE
    per_b = seq // tm
    row = lambda w: pl.BlockSpec((tm, w), lambda i: (i, 0))
    weights = (wg, wa, wb, wc, wd, wo)
    return pl.pallas_call(
        _merge_mlp_kernel,
        grid=(t // tm,),
        in_specs=[
            row(D_MODEL),
            pl.BlockSpec((None, 6, D_MODEL), lambda i: (i // per_b, 0, 0)),
            _const_spec((1, D_MODEL)),
            row(QK_W), row(CONV_CH), row(GLA_V_W), row(QK_W),
            *[_const_spec(w.shape) for w in weights],
            _const_spec((1, D_MODEL)), _const_spec(wu.shape), _const_spec(wdn.shape),
        ],
        out_specs=row(D_MODEL),
        out_shape=jax.ShapeDtypeStruct((t, D_MODEL), F32),
        compiler_params=_params("parallel"),
        name="merge_and_mlp",
    )(x2d, mod, g_mix, oa, ob, oc, od, *weights, g_mlp, wu, wdn)


def _head_block_diag():
    idx = np.arange(QK_W) // HEAD_LANES
    return jnp.asarray((idx[:, None] == idx[None, :]).astype(np.float32) / HEAD_LANES, dtype=BF16)


def kernel(x, c, w_ada, b_ada, g_mix, w_in, q_gain, k_gain, conv_w, conv_b, conv_ln_g, conv_ln_b,
           gla_w_gate, gla_b_gate, gla_o_gain, w_br_a, w_br_b, w_br_c, w_br_d, w_out, g_mlp, w_up, w_down):
    bsz, seq, d = x.shape
    depth = w_ada.shape[0]
    assert d == D_MODEL and seq % (DSWA_BLOCK * DSWA_PATTERNS[-1][1]) == 0 and seq % TOKEN_TILE == 0
    mods = _modulation(c, w_ada, b_ada)
    bd = _head_block_diag()
    x2d = x.reshape(bsz * seq, d)
    qk_scale = HEAD_LANES ** -0.5
    for l in range(depth):
        w = w_in[l]
        o_conv = A_W
        o_c = o_conv + 2 * CONV_CH
        o_low = o_c + 2 * QK_W + 2 * GLA_V_W
        o_d = o_low + GLA_RANK
        o_gate = o_d + D_W
        assert w.shape == (d, o_gate + 4 * d) and w_up.shape[-1] == D_FF
        wa = w[:, :A_W].astype(BF16)
        wb = w[:, o_conv:o_c].astype(BF16)
        wc = jnp.concatenate([w[:, o_c:o_low + GLA_RANK], jnp.zeros((d, GLA_LOW_PAD - GLA_RANK), F32)], axis=1).astype(BF16)
        wd = w[:, o_d:o_gate].astype(BF16)
        wg = w[:, o_gate:].astype(BF16)
        qg = jnp.tile(q_gain[l], N_HEADS).reshape(1, QK_W) * qk_scale
        kg = jnp.tile(k_gain[l], N_HEADS).reshape(1, QK_W)
        mod = mods[l]
        gm = g_mix[l].reshape(1, d)

        a, y, cbuf, dbuf = _in_projection(x2d, mod, gm, wa, wb, wc, wd, qg, kg, bd, seq)
        oa = _dilated_attention(a, bsz, seq)
        ob = _conv_mixer(y, conv_w[l], conv_b[l], conv_ln_g[l], conv_ln_b[l], bsz, seq)
        oc = _gla_mixer(cbuf, gla_w_gate[l], gla_b_gate[l], gla_o_gain[l], bsz, seq)
        od = _stick_breaking_mixer(dbuf, bsz, seq)
        x2d = _merge_mlp(x2d, mod, gm, oa, ob, oc, od, wg, w_br_a[l].astype(BF16), w_br_b[l].astype(BF16),
                         w_br_c[l].astype(BF16), w_br_d[l].astype(BF16), w_out[l].astype(BF16),
                         g_mlp[l].reshape(1, d), w_up[l].astype(BF16), w_down[l].astype(BF16), seq)
    return x2d.reshape(bsz, seq, d)
```
